```python
import jax, jax.numpy as jnp
from jax import lax
import numpy as np

D_MODEL = 2048
BATCH = 2
SEQ = 4096
DEPTH = 1
DEC_BATCH = 32
DEC_SEQ = 1
PAST_LEN = 16384
PAGE_SIZE = 128

ATT_HEADS = 8
ATT_KV_HEADS = 2
ATT_HEAD_DIM = 128
ATT_WIDTH = ATT_HEADS * ATT_HEAD_DIM
ATT_KV_WIDTH = ATT_KV_HEADS * ATT_HEAD_DIM
ATT_SCALE = ATT_HEAD_DIM ** -0.5
IDX_HEADS = 16
IDX_DIM = 64
IDX_SCALE = (IDX_HEADS * IDX_DIM) ** -0.5
TOPK_MAX = 256
Q_BLOCK = 128
ML_HEADS = 4
ML_QK_DIM = 128
ML_V_DIM = 256
ML_WIDTH = ML_HEADS * ML_V_DIM
ML_CHUNK = 64
MIX_WIDTH = ATT_WIDTH + ML_WIDTH
RMS_EPS = 1e-6
IN_SIZES = (ATT_WIDTH, ATT_KV_WIDTH, ATT_KV_WIDTH, IDX_HEADS * IDX_DIM, IDX_DIM, IDX_HEADS, ATT_WIDTH,
            ML_HEADS * ML_QK_DIM, ML_HEADS * ML_QK_DIM, ML_WIDTH, ML_HEADS, ML_HEADS, ML_WIDTH, ML_WIDTH)
IN_WIDTH = sum(IN_SIZES)

kernel_name = "hymba_dsa_mlstm_decoder_step"


def rmsnorm(x, w):
    xf = x.astype(jnp.float32)
    y = xf * lax.rsqrt(jnp.mean(xf * xf, axis=-1, keepdims=True) + RMS_EPS) * w.astype(jnp.float32)
    return y.astype(x.dtype)


def project(h, w_in, b_i, b_f):
    B, T, _ = h.shape
    split_points = [int(s) for s in np.cumsum(IN_SIZES)[:-1]]
    (aq, ak, av, iq, ik, iw, az, mq, mk, mv, mi, mf, mo, mz) = jnp.split(h @ w_in, split_points, axis=-1)
    f32 = jnp.float32
    return dict(
        q=aq.reshape(B, T, ATT_HEADS, ATT_HEAD_DIM),
        k=ak.reshape(B, T, ATT_KV_HEADS, ATT_HEAD_DIM),
        v=av.reshape(B, T, ATT_KV_HEADS, ATT_HEAD_DIM),
        qi=iq.reshape(B, T, IDX_HEADS, IDX_DIM),
        ki=ik,
        wi=iw,
        az=az,
        mq=mq.reshape(B, T, ML_HEADS, ML_QK_DIM),
        mk=mk.reshape(B, T, ML_HEADS, ML_QK_DIM) * (ML_QK_DIM ** -0.5),
        mv=mv.reshape(B, T, ML_HEADS, ML_V_DIM),
        ig=mi.astype(f32) + b_i.astype(f32),
        lf=jax.nn.log_sigmoid(mf.astype(f32) + b_f.astype(f32)),
        mo=mo, mz=mz)


def indexer_scores(qi, ki, wi):
    s = jax.nn.relu(jnp.einsum('bqhd,bld->bqlh', qi.astype(jnp.float32), ki.astype(jnp.float32)))
    return jnp.einsum('bqlh,bqh->bql', s, wi.astype(jnp.float32)) * IDX_SCALE


def sparse_attend(q, kg, vg, valid):
    B, Q = q.shape[:2]
    G = ATT_HEADS // ATT_KV_HEADS
    qg = q.reshape(B, Q, ATT_KV_HEADS, G, ATT_HEAD_DIM).astype(jnp.float32)
    logits = jnp.einsum('bqhgd,bqjhd->bqhgj', qg, kg.astype(jnp.float32)) * ATT_SCALE
    logits = jnp.where(valid[:, :, None, None, :], logits, -jnp.inf)
    p = jax.nn.softmax(logits, axis=-1)
    o = jnp.einsum('bqhgj,bqjhd->bqhgd', p, vg.astype(jnp.float32))
    return o.reshape(B, Q, ATT_WIDTH)


def gather_rows(a, idx):
    return jax.vmap(lambda aa, ii: aa[ii])(a, idx)


def dsa_prompt(q, k, v, qi, ki, wi):
    B, S = q.shape[:2]
    topk = min(TOPK_MAX, S // 4)
    nb = S // Q_BLOCK
    key_pos = jnp.arange(S)

    def block(args):
        i, qb, qib, wib = args
        pos = i * Q_BLOCK + jnp.arange(Q_BLOCK)
        scores = indexer_scores(qib, ki, wib)
        scores = jnp.where((key_pos[None, :] <= pos[:, None])[None], scores, -jnp.inf)
        _, idx = lax.top_k(scores, topk)
        valid = idx <= pos[None, :, None]
        return sparse_attend(qb, gather_rows(k, idx), gather_rows(v, idx), valid)

    qb = q.reshape(B, nb, Q_BLOCK, ATT_HEADS, ATT_HEAD_DIM).swapaxes(0, 1)
    qib = qi.reshape(B, nb, Q_BLOCK, IDX_HEADS, IDX_DIM).swapaxes(0, 1)
    wib = wi.reshape(B, nb, Q_BLOCK, IDX_HEADS).swapaxes(0, 1)
    out = lax.map(block, (jnp.arange(nb), qb, qib, wib))
    return out.swapaxes(0, 1).reshape(B, S, ATT_WIDTH)


def dsa_sample(q, k_new, v_new, qi, ki_new, wi, cache_k, cache_v, cache_ik, page_table):
    Bd, T = q.shape[:2]
    past = page_table.shape[1] * PAGE_SIZE
    L = past + T
    topk = min(TOPK_MAX, L // 4)
    ki_past = cache_ik[page_table].reshape(Bd, past, IDX_DIM)
    ki_all = jnp.concatenate([ki_past, ki_new.astype(ki_past.dtype)], axis=1)
    scores = indexer_scores(qi, ki_all, wi)
    pos = past + jnp.arange(T)
    scores = jnp.where((jnp.arange(L)[None, :] <= pos[:, None])[None], scores, -jnp.inf)
    _, idx = lax.top_k(scores, topk)
    valid = idx <= pos[None, :, None]
    in_past = idx < past
    ip = jnp.minimum(idx, past - 1)
    phys = jax.vmap(lambda pt, ii: pt[ii])(page_table, ip // PAGE_SIZE)
    off = ip % PAGE_SIZE
    inew = jnp.clip(idx - past, 0, T - 1)
    kg = jnp.where(in_past[..., None, None], cache_k[phys, off], gather_rows(k_new.astype(cache_k.dtype), inew))
    vg = jnp.where(in_past[..., None, None], cache_v[phys, off], gather_rows(v_new.astype(cache_v.dtype), inew))
    return sparse_attend(q, kg, vg, valid)


def mlstm_chunk(carry, xs):
    C, n, m = carry
    q, k, v, ig, lf = xs
    L = q.shape[2]
    b = jnp.cumsum(lf, axis=-1)
    causal = jnp.tril(jnp.ones((L, L), dtype=bool))
    log_d = jnp.where(causal, b[..., :, None] - b[..., None, :] + ig[..., None, :], -jnp.inf)
    log_a = b + m[..., None]
    m_t = jnp.maximum(log_a, jnp.max(log_d, axis=-1))
    d = jnp.exp(log_d - m_t[..., None])
    a = jnp.exp(log_a - m_t)
    s = jnp.einsum('bhtd,bhsd->bhts', q, k) * d
    num = a[..., None] * jnp.einsum('bhvd,bhtd->bhtv', C, q) + jnp.einsum('bhts,bhsv->bhtv', s, v)
    den = a * jnp.einsum('bhd,bhtd->bht', n, q) + jnp.sum(s, axis=-1)
    h = num / jnp.maximum(jnp.abs(den), jnp.exp(-m_t))[..., None]
    m_new = m_t[..., -1]
    w = jnp.exp(b[..., -1:] - b + ig - m_new[..., None])
    a_end = a[..., -1]
    C_new = a_end[..., None, None] * C + jnp.einsum('bhs,bhsv,bhsd->bhvd', w, v, k)
    n_new = a_end[..., None] * n + jnp.einsum('bhs,bhsd->bhd', w, k)
    return (C_new, n_new, m_new), h


def mlstm_run(q, k, v, ig, lf, C0, n0, m0, chunk):
    B, T, H, _ = q.shape
    nc = T // chunk
    f32 = jnp.float32
    seq4 = lambda a: a.astype(f32).reshape(B, nc, chunk, H, a.shape[-1]).transpose(1, 0, 3, 2, 4)
    seq3 = lambda a: a.astype(f32).reshape(B, nc, chunk, H).transpose(1, 0, 3, 2)
    (C, n, m), h = lax.scan(mlstm_chunk, (C0.astype(f32), n0.astype(f32), m0.astype(f32)),
                            (seq4(q), seq4(k), seq4(v), seq3(ig), seq3(lf)))
    h = h.transpose(1, 0, 3, 2, 4).reshape(B, T, H, ML_V_DIM)
    return h, C, n, m


def merge(x, att_o, p, h_ml, ml_norm_w, w_out):
    B, T, _ = x.shape
    a = att_o.astype(x.dtype) * jax.nn.silu(p['az'])
    hn = h_ml * lax.rsqrt(jnp.mean(h_ml * h_ml, axis=-1, keepdims=True) + RMS_EPS)
    hn = hn * ml_norm_w.astype(jnp.float32).reshape(ML_HEADS, ML_V_DIM)
    mo = hn.reshape(B, T, ML_WIDTH).astype(x.dtype) * jax.nn.sigmoid(p['mo']) * jax.nn.silu(p['mz'])
    return x + jnp.concatenate([a, mo], axis=-1) @ w_out


def setup_inputs(seed: int = 0) -> dict:
    key = jax.random.key(seed)
    ks = jax.random.split(key, 20)
    n_pages = PAST_LEN // PAGE_SIZE
    used = DEC_BATCH * n_pages
    n_pool = used + max(1, used // 4)
    f32 = jnp.float32
    page_table = jax.random.permutation(ks[0], n_pool)[:used].reshape(DEC_BATCH, n_pages).astype(jnp.int32)
    return {
        'x_prompt': jax.random.normal(ks[1], (BATCH, SEQ, D_MODEL), f32),
        'x_sample': jax.random.normal(ks[2], (DEC_BATCH, DEC_SEQ, D_MODEL), f32),
        'cache_k': jax.random.normal(ks[3], (DEPTH, n_pool, PAGE_SIZE, ATT_KV_HEADS, ATT_HEAD_DIM), f32),
        'cache_v': jax.random.normal(ks[4], (DEPTH, n_pool, PAGE_SIZE, ATT_KV_HEADS, ATT_HEAD_DIM), f32),
        'cache_idx_k': jax.random.normal(ks[5], (DEPTH, n_pool, PAGE_SIZE, IDX_DIM), f32),
        'state_C': 0.1 * jax.random.normal(ks[6], (DEPTH, DEC_BATCH, ML_HEADS, ML_V_DIM, ML_QK_DIM), f32),
        'state_n': 0.5 * jax.random.normal(ks[7], (DEPTH, DEC_BATCH, ML_HEADS, ML_QK_DIM), f32),
        'state_m': jax.random.normal(ks[8], (DEPTH, DEC_BATCH, ML_HEADS), f32),
        'page_table': page_table,
        'norm_w': 1.0 + 0.01 * jax.random.normal(ks[9], (DEPTH, D_MODEL), f32),
        'w_in': jax.random.normal(ks[10], (DEPTH, D_MODEL, IN_WIDTH), f32) * D_MODEL ** -0.5,
        'b_igate': 0.1 * jax.random.normal(ks[11], (DEPTH, ML_HEADS), f32),
        'b_fgate': 3.0 + 0.1 * jax.random.normal(ks[12], (DEPTH, ML_HEADS), f32),
        'ml_norm_w': 1.0 + 0.01 * jax.random.normal(ks[13], (DEPTH, ML_WIDTH), f32),
        'w_out': jax.random.normal(ks[14], (DEPTH, MIX_WIDTH, D_MODEL), f32) * MIX_WIDTH ** -0.5,
        'final_norm_w': 1.0 + 0.01 * jax.random.normal(ks[15], (D_MODEL,), f32),
    }


def reference(x_prompt, x_sample, cache_k, cache_v, cache_idx_k, state_C, state_n, state_m, page_table,
              norm_w, w_in, b_igate, b_fgate, ml_norm_w, w_out, final_norm_w):
    xp, xs = x_prompt, x_sample
    Bp = xp.shape[0]
    kp_l, vp_l, ikp_l, Cp_l, np_l, mp_l = [], [], [], [], [], []
    ks_l, vs_l, iks_l, Cs_l, ns_l, ms_l = [], [], [], [], [], []
    for l in range(DEPTH):
        p = project(rmsnorm(xp, norm_w[l]), w_in[l], b_igate[l], b_fgate[l])
        att = dsa_prompt(p['q'], p['k'], p['v'], p['qi'], p['ki'], p['wi'])
        zC = jnp.zeros((Bp, ML_HEADS, ML_V_DIM, ML_QK_DIM), jnp.float32)
        zn = jnp.zeros((Bp, ML_HEADS, ML_QK_DIM), jnp.float32)
        zm = jnp.zeros((Bp, ML_HEADS), jnp.float32)
        h_ml, C1, n1, m1 = mlstm_run(p['mq'], p['mk'], p['mv'], p['ig'], p['lf'], zC, zn, zm, ML_CHUNK)
        xp = merge(xp, att, p, h_ml, ml_norm_w[l], w_out[l])
        kp_l.append(p['k']); vp_l.append(p['v']); ikp_l.append(p['ki'])
        Cp_l.append(C1); np_l.append(n1); mp_l.append(m1)
        s = project(rmsnorm(xs, norm_w[l]), w_in[l], b_igate[l], b_fgate[l])
        att_s = dsa_sample(s['q'], s['k'], s['v'], s['qi'], s['ki'], s['wi'],
                           cache_k[l], cache_v[l], cache_idx_k[l], page_table)
        h_s, C2, n2, m2 = mlstm_run(s['mq'], s['mk'], s['mv'], s['ig'], s['lf'],
                                    state_C[l], state_n[l], state_m[l], xs.shape[1])
        xs = merge(xs, att_s, s, h_s, ml_norm_w[l], w_out[l])
        ks_l.append(s['k']); vs_l.append(s['v']); iks_l.append(s['ki'])
        Cs_l.append(C2); ns_l.append(n2); ms_l.append(m2)
    y_prompt = rmsnorm(xp, final_norm_w)
    y_sample = rmsnorm(xs, final_norm_w)
    return (y_prompt, y_sample,
            jnp.stack(kp_l), jnp.stack(vp_l), jnp.stack(ikp_l), jnp.stack(Cp_l), jnp.stack(np_l), jnp.stack(mp_l),
            jnp.stack(ks_l), jnp.stack(vs_l), jnp.stack(iks_l), jnp.stack(Cs_l), jnp.stack(ns_l), jnp.stack(ms_l))
```

```python
import functools

import jax
import jax.numpy as jnp
import numpy as np
from jax import lax
from jax.experimental import pallas as pl
from jax.experimental.pallas import tpu as pltpu

F32 = jnp.float32
BF16 = jnp.bfloat16
I32 = jnp.int32

D_MODEL = 2048
PAGE = 128
ATT_HEADS = 8
KV_HEADS = 2
HEAD_DIM = 128
GROUP = ATT_HEADS // KV_HEADS
ATT_WIDTH = ATT_HEADS * HEAD_DIM
KV_WIDTH = KV_HEADS * HEAD_DIM
ATT_SCALE = HEAD_DIM ** -0.5
IDX_HEADS = 16
IDX_DIM = 64
IDX_SCALE = (IDX_HEADS * IDX_DIM) ** -0.5
TOPK_MAX = 256
ML_HEADS = 4
ML_QK = 128
ML_V = 256
ML_WIDTH = ML_HEADS * ML_V
RMS_EPS = 1e-6
IN_SIZES = (ATT_WIDTH, KV_WIDTH, KV_WIDTH, IDX_HEADS * IDX_DIM, IDX_DIM, IDX_HEADS, ATT_WIDTH,
            ML_HEADS * ML_QK, ML_HEADS * ML_QK, ML_WIDTH, ML_HEADS, ML_HEADS, ML_WIDTH, ML_WIDTH)

LANES = 128
NEG_BIG = -1e30
VMEM_LIMIT = 56 * 1024 * 1024

MAIN_W = 7168
SMALL_W = 640
MISC_WI = IDX_DIM
MISC_IG = IDX_DIM + IDX_HEADS
MISC_FG = MISC_IG + ML_HEADS


def _dot(a, b):
    return jnp.dot(a, b, preferred_element_type=F32)


def _dot_nt(a, b):
    return lax.dot_general(a, b, (((1,), (1,)), ((), ())), preferred_element_type=F32)


def _proj_kernel(x_ref, nw_ref, w_ref, o_ref, h_scr):
    @pl.when(pl.program_id(1) == 0)
    def _():
        x = x_ref[...]
        ms = jnp.mean(x * x, axis=-1, keepdims=True)
        h_scr[...] = (x * lax.rsqrt(ms + RMS_EPS) * nw_ref[...]).astype(BF16)

    o_ref[...] = _dot(h_scr[...], w_ref[...]).astype(o_ref.dtype)


def _project(x2d, norm_w, w, out_dtype, tm, tn, name):
    m, d = x2d.shape
    n = w.shape[1]
    return pl.pallas_call(
        _proj_kernel,
        out_shape=jax.ShapeDtypeStruct((m, n), out_dtype),
        grid=(m // tm, n // tn),
        in_specs=[pl.BlockSpec((tm, d), lambda i, j: (i, 0)),
                  pl.BlockSpec((1, d), lambda i, j: (0, 0)),
                  pl.BlockSpec((d, tn), lambda i, j: (0, j))],
        out_specs=pl.BlockSpec((tm, tn), lambda i, j: (i, j)),
        scratch_shapes=[pltpu.VMEM((tm, d), BF16)],
        compiler_params=pltpu.CompilerParams(dimension_semantics=("parallel", "arbitrary"),
                                             vmem_limit_bytes=VMEM_LIMIT),
        name=name,
    )(x2d, norm_w.reshape(1, d), w)


def _out_kernel(x_ref, a_ref, m_ref, wa_ref, wm_ref, fw_ref, o_ref):
    y = x_ref[...] + _dot(a_ref[...], wa_ref[...]) + _dot(m_ref[...], wm_ref[...])
    ms = jnp.mean(y * y, axis=-1, keepdims=True)
    o_ref[...] = y * lax.rsqrt(ms + RMS_EPS) * fw_ref[...]


def _out_project(x2d, a, mo, w_att, w_ml, final_w, tm, name):
    m, d = x2d.shape
    return pl.pallas_call(
        _out_kernel,
        out_shape=jax.ShapeDtypeStruct((m, d), F32),
        grid=(m // tm,),
        in_specs=[pl.BlockSpec((tm, d), lambda i: (i, 0)),
                  pl.BlockSpec((tm, ATT_WIDTH), lambda i: (i, 0)),
                  pl.BlockSpec((tm, ML_WIDTH), lambda i: (i, 0)),
                  pl.BlockSpec((ATT_WIDTH, d), lambda i: (0, 0)),
                  pl.BlockSpec((ML_WIDTH, d), lambda i: (0, 0)),
                  pl.BlockSpec((1, d), lambda i: (0, 0))],
        out_specs=pl.BlockSpec((tm, d), lambda i: (i, 0)),
        compiler_params=pltpu.CompilerParams(dimension_semantics=("parallel",),
                                             vmem_limit_bytes=VMEM_LIMIT),
        name=name,
    )(x2d, a, mo, w_att, w_ml, final_w.reshape(1, d))


def _log_sigmoid(x):
    return jnp.minimum(x, 0.0) - jnp.log(1.0 + jnp.exp(-jnp.abs(x)))


def _sigmoid(x):
    return 1.0 / (1.0 + jnp.exp(-x))


def _silu(x):
    return x * _sigmoid(x)


ML_CHUNK = 256
ML_AUG = ML_V + LANES


def _mlstm_kernel(bi_ref, bf_ref, q_ref, k_ref, v_ref, mo_ref, mz_ref, misc_ref, nw_ref,
                  out_ref, ct_ref, m_ref):
    L = ML_CHUNK

    @pl.when(pl.program_id(1) == 0)
    def _():
        ct_ref[...] = jnp.zeros_like(ct_ref)
        m_ref[...] = jnp.zeros_like(m_ref)

    misc = misc_ref[...]
    misc_t = misc.T
    t_idx = lax.broadcasted_iota(I32, (L, L), 0)
    s_idx = lax.broadcasted_iota(I32, (L, L), 1)
    causal = s_idx <= t_idx
    ones_col = jnp.where(lax.broadcasted_iota(I32, (L, LANES), 1) == 0, 1.0, 0.0).astype(BF16)

    for h in range(ML_HEADS):
        ig_row = misc_t[MISC_IG + h:MISC_IG + h + 1, :] + bi_ref[h]
        lf_row = _log_sigmoid(misc_t[MISC_FG + h:MISC_FG + h + 1, :] + bf_ref[h])
        lf_col = _log_sigmoid(misc[:, MISC_FG + h:MISC_FG + h + 1] + bf_ref[h])
        b_col = jnp.sum(jnp.where(causal, lf_row, 0.0), axis=1, keepdims=True)
        b_row = jnp.sum(jnp.where(t_idx <= s_idx, lf_col, 0.0), axis=0, keepdims=True)
        m_prev = m_ref[0, h][0:1, 0:1]
        log_d = jnp.where(causal, b_col - b_row + ig_row, -jnp.inf)
        log_a = b_col + m_prev
        m_t = jnp.maximum(log_a, jnp.max(log_d, axis=1, keepdims=True))
        d = jnp.exp(log_d - m_t)
        a = jnp.exp(log_a - m_t)

        q = q_ref[:, h * ML_QK:(h + 1) * ML_QK]
        k = k_ref[:, h * ML_QK:(h + 1) * ML_QK]
        v_aug = jnp.concatenate([v_ref[:, h * ML_V:(h + 1) * ML_V], ones_col], axis=1)
        s = (_dot_nt(q, k) * d).astype(BF16)
        ct = ct_ref[0, h]
        num_aug = a * _dot(q, ct.astype(BF16)) + _dot(s, v_aug)
        den = num_aug[:, ML_V:ML_V + 1]
        hh = num_aug[:, :ML_V] / jnp.maximum(jnp.abs(den), jnp.exp(-m_t))

        m_new = m_t[L - 1:L, :]
        a_end = a[L - 1:L, :]
        w_row = jnp.exp(b_row[:, L - 1:L] - b_row + ig_row - m_new)
        ktw = (k.astype(F32).T * w_row).astype(BF16)
        ct_new = a_end * ct + _dot(ktw, v_aug)
        ct_ref[0, h] = ct_new
        m_ref[0, h] = jnp.broadcast_to(m_new, (8, LANES))

        ms = jnp.mean(hh * hh, axis=1, keepdims=True)
        hn = hh * lax.rsqrt(ms + RMS_EPS) * nw_ref[:, h * ML_V:(h + 1) * ML_V]
        gate = _sigmoid(mo_ref[:, h * ML_V:(h + 1) * ML_V].astype(F32)) * _silu(mz_ref[:, h * ML_V:(h + 1) * ML_V].astype(F32))
        out_ref[:, h * ML_V:(h + 1) * ML_V] = (hn * gate).astype(out_ref.dtype)


def _mlstm_prompt(p_main, p_small, b_i, b_f, ml_norm_w, batch, seq):
    L = ML_CHUNK
    nc = seq // L
    row = lambda b, c: b * nc + c
    return pl.pallas_call(
        _mlstm_kernel,
        out_shape=(jax.ShapeDtypeStruct((batch * seq, ML_WIDTH), BF16),
                   jax.ShapeDtypeStruct((batch, ML_HEADS, ML_QK, ML_AUG), F32),
                   jax.ShapeDtypeStruct((batch, ML_HEADS, 8, LANES), F32)),
        grid=(batch, nc),
        in_specs=[pl.BlockSpec(memory_space=pltpu.SMEM),
                  pl.BlockSpec(memory_space=pltpu.SMEM),
                  pl.BlockSpec((L, 512), lambda b, c: (row(b, c), 12)),
                  pl.BlockSpec((L, 512), lambda b, c: (row(b, c), 13)),
                  pl.BlockSpec((L, 1024), lambda b, c: (row(b, c), 3)),
                  pl.BlockSpec((L, 1024), lambda b, c: (row(b, c), 4)),
                  pl.BlockSpec((L, 1024), lambda b, c: (row(b, c), 5)),
                  pl.BlockSpec((L, LANES), lambda b, c: (row(b, c), 4)),
                  pl.BlockSpec((1, ML_WIDTH), lambda b, c: (0, 0))],
        out_specs=(pl.BlockSpec((L, ML_WIDTH), lambda b, c: (row(b, c), 0)),
                   pl.BlockSpec((1, ML_HEADS, ML_QK, ML_AUG), lambda b, c: (b, 0, 0, 0)),
                   pl.BlockSpec((1, ML_HEADS, 8, LANES), lambda b, c: (b, 0, 0, 0))),
        compiler_params=pltpu.CompilerParams(dimension_semantics=("parallel", "arbitrary"),
                                             vmem_limit_bytes=VMEM_LIMIT),
        name="mlstm_prompt",
    )(b_i, b_f, p_main, p_main, p_main, p_main, p_main, p_small, ml_norm_w.reshape(1, ML_WIDTH))


INT_MIN = -2 ** 31


def _key_to_float(key):
    bits = jnp.where(key >= 0, key, key ^ jnp.int32(0x7FFFFFFF))
    return lax.bitcast_convert_type(bits, F32)


KEY_NEG_INF = INT_MIN + 0x7FFFFF


def _kth_largest_key(count_ge, shape, k):
    def bit_body(it, prefix):
        cand = prefix + lax.shift_left(jnp.int32(1), 31 - it)
        cand_f = _key_to_float(jnp.maximum(cand, KEY_NEG_INF))
        return jnp.where(count_ge(cand_f) >= float(k), cand, prefix)

    return lax.fori_loop(0, 32, bit_body, jnp.full(shape, INT_MIN, I32))


def _tie_cutoff(count_tie_le, need, shape, index_bits):
    def bit_body(it, lo):
        cand = lo + lax.shift_left(jnp.int32(1), index_bits - 1 - it).astype(F32)
        return jnp.where(count_tie_le(cand) < need, cand, lo)

    return lax.fori_loop(0, index_bits, bit_body, jnp.full(shape, -1.0, F32)) + 1.0


DSA_QB = 128
DSA_TK = 256


def _dsa_kernel(q_ref, qi_ref, az_ref, miscq_ref, k_ref, v_ref, misck_ref, out_ref,
                k_bf, v_bf, ki_lo, ki_hi, sc_scr, bias_scr, j_scr, *, topk):
    i = pl.program_id(1)
    seq = k_ref.shape[0]
    QB, TK = DSA_QB, DSA_TK
    nt = (i * QB + QB + TK - 1) // TK

    @pl.when(i == 0)
    def _():
        k_bf[...] = k_ref[...].astype(BF16)
        v_bf[...] = v_ref[...].astype(BF16)
        lane = lax.broadcasted_iota(I32, (seq, LANES), 1)
        lo = jnp.where(lane < IDX_DIM, misck_ref[...], 0.0)
        ki_lo[...] = lo.astype(BF16)
        ki_hi[...] = pltpu.roll(lo, IDX_DIM, axis=1).astype(BF16)

    k_iota = lax.broadcasted_iota(I32, (TK, QB), 0)
    q_pos = i * QB + lax.broadcasted_iota(I32, (TK, QB), 1)
    w_t = miscq_ref[...].T

    def tile_rows(t):
        return pl.ds(pl.multiple_of(t * TK, TK), TK)

    def score_tile(t, carry):
        klo = ki_lo[tile_rows(t), :]
        khi = ki_hi[tile_rows(t), :]
        acc = jnp.zeros((TK, QB), F32)
        for p in range(IDX_HEADS // 2):
            qp = qi_ref[:, p * LANES:(p + 1) * LANES]
            w0 = w_t[MISC_WI + 2 * p:MISC_WI + 2 * p + 1, :]
            w1 = w_t[MISC_WI + 2 * p + 1:MISC_WI + 2 * p + 2, :]
            acc = acc + w0 * jnp.maximum(_dot_nt(klo, qp), 0.0) + w1 * jnp.maximum(_dot_nt(khi, qp), 0.0)
        k_pos = t * TK + k_iota
        sc_scr[tile_rows(t), :] = jnp.where(k_pos <= q_pos, acc * IDX_SCALE, -jnp.inf)
        return carry

    lax.fori_loop(0, nt, score_tile, 0)

    def count_ge(cand):
        def body(t, cnt):
            x = sc_scr[tile_rows(t), :]
            return cnt + jnp.sum(jnp.where(x >= cand, 1.0, 0.0).reshape(TK // 8, 8, QB), axis=0)

        cnt = lax.fori_loop(0, nt, body, jnp.zeros((8, QB), F32))
        return jnp.sum(cnt, axis=0, keepdims=True)

    key = _kth_largest_key(count_ge, (1, QB), topk)
    thr = _key_to_float(key)
    thr_next = _key_to_float(key + 1)
    need = float(topk) - count_ge(thr_next)

    j_scr[...] = jnp.full((1, QB), float(seq), F32)

    @pl.when(jnp.max(count_ge(thr)) > float(topk))
    def _():
        def count_tie_le(cut):
            def body(t, cnt):
                x = sc_scr[tile_rows(t), :]
                k_pos = (t * TK + k_iota).astype(F32)
                tie = jnp.where(x >= thr_next, 0.0, jnp.where(x >= thr, jnp.where(k_pos <= cut, 1.0, 0.0), 0.0))
                return cnt + jnp.sum(tie.reshape(TK // 8, 8, QB), axis=0)

            cnt = lax.fori_loop(0, nt, body, jnp.zeros((8, QB), F32))
            return jnp.sum(cnt, axis=0, keepdims=True)

        j_scr[...] = _tie_cutoff(count_tie_le, need, (1, QB), int(np.log2(seq)))

    cut = j_scr[...]

    def bias_tile(t, carry):
        x = sc_scr[tile_rows(t), :]
        k_pos = t * TK + k_iota
        taken = jnp.where(x >= thr_next, 0.0,
                          jnp.where(x >= thr, jnp.where(k_pos.astype(F32) <= cut, 0.0, NEG_BIG), NEG_BIG))
        bias_scr[t] = jnp.where(k_pos <= q_pos, taken, NEG_BIG).T
        return carry

    lax.fori_loop(0, nt, bias_tile, 0)

    az = az_ref[...].astype(F32)
    for g in range(KV_HEADS):
        qg = jnp.concatenate([q_ref[:, (g * GROUP + hh) * HEAD_DIM:(g * GROUP + hh + 1) * HEAD_DIM]
                              for hh in range(GROUP)], axis=0)

        def att_tile(t, carry):
            m, l, acc = carry
            kt = k_bf[tile_rows(t), g * HEAD_DIM:(g + 1) * HEAD_DIM]
            vt = v_bf[tile_rows(t), g * HEAD_DIM:(g + 1) * HEAD_DIM]
            bias = bias_scr[t]
            lg = _dot_nt(qg, kt) + jnp.concatenate([bias] * GROUP, axis=0)
            m_new = jnp.maximum(m, jnp.max(lg, axis=1, keepdims=True))
            alpha = jnp.exp(m - m_new)
            p = jnp.exp(lg - m_new)
            l = alpha * l + jnp.sum(p, axis=1, keepdims=True)
            acc = alpha * acc + _dot(p.astype(BF16), vt)
            return m_new, l, acc

        init = (jnp.full((GROUP * QB, 1), NEG_BIG, F32), jnp.zeros((GROUP * QB, 1), F32),
                jnp.zeros((GROUP * QB, HEAD_DIM), F32))
        _, l, acc = lax.fori_loop(0, nt, att_tile, init)
        o = acc / l
        for hh in range(GROUP):
            c0 = (g * GROUP + hh) * HEAD_DIM
            out_ref[:, c0:c0 + HEAD_DIM] = (o[hh * QB:(hh + 1) * QB] * _silu(az[:, c0:c0 + HEAD_DIM])).astype(out_ref.dtype)


def _dsa_prompt(p_main, p_small, batch, seq):
    QB, TK = DSA_QB, DSA_TK
    nb = seq // QB
    topk = min(TOPK_MAX, seq // 4)
    assert TK >= topk and seq % TK == 0 and TK % QB == 0
    qrow = lambda b, i: b * nb + i
    return pl.pallas_call(
        functools.partial(_dsa_kernel, topk=topk),
        out_shape=jax.ShapeDtypeStruct((batch * seq, ATT_WIDTH), BF16),
        grid=(batch, nb),
        in_specs=[pl.BlockSpec((QB, ATT_WIDTH), lambda b, i: (qrow(b, i), 0)),
                  pl.BlockSpec((QB, 1024), lambda b, i: (qrow(b, i), 1)),
                  pl.BlockSpec((QB, ATT_WIDTH), lambda b, i: (qrow(b, i), 2)),
                  pl.BlockSpec((QB, LANES), lambda b, i: (qrow(b, i), 4)),
                  pl.BlockSpec((seq, KV_WIDTH), lambda b, i: (b, 0)),
                  pl.BlockSpec((seq, KV_WIDTH), lambda b, i: (b, 1)),
                  pl.BlockSpec((seq, LANES), lambda b, i: (b, 4))],
        out_specs=pl.BlockSpec((QB, ATT_WIDTH), lambda b, i: (qrow(b, i), 0)),
        scratch_shapes=[pltpu.VMEM((seq, KV_WIDTH), BF16),
                        pltpu.VMEM((seq, KV_WIDTH), BF16),
                        pltpu.VMEM((seq, LANES), BF16),
                        pltpu.VMEM((seq, LANES), BF16),
                        pltpu.VMEM((seq, QB), F32),
                        pltpu.VMEM((seq // TK, QB, TK), F32),
                        pltpu.VMEM((1, QB), F32)],
        compiler_params=pltpu.CompilerParams(dimension_semantics=("parallel", "arbitrary"),
                                             vmem_limit_bytes=VMEM_LIMIT),
        name="dsa_prompt",
    )(p_main, p_main, p_main, p_small, p_small, p_small, p_small)


IDX_PAGES_PER_STEP = 32
KV_PAGES_PER_STEP = 16


def _page_spec(block, group, j):
    return pl.BlockSpec(block, lambda b, g, pt: (pt[b, g * group + j],) + (0,) * (len(block) - 1))


def _sample_scores_kernel(pt_ref, qi_ref, wb_ref, kin_ref, *refs):
    pages = refs[:IDX_PAGES_PER_STEP]
    sc_ref, snew_ref = refs[IDX_PAGES_PER_STEP:]
    qi = qi_ref[0]
    wb = wb_ref[0]
    rows = []
    for page in pages:
        s = _dot_nt(qi, page[0].astype(BF16))
        rows.append(jnp.sum(jnp.maximum(s, 0.0) * wb, axis=0, keepdims=True) * IDX_SCALE)
    sc_ref[0] = jnp.concatenate(rows, axis=0)

    @pl.when(pl.program_id(1) == 0)
    def _():
        ki_new = kin_ref[0].astype(BF16).astype(F32)
        s = jnp.sum(qi.astype(F32) * ki_new, axis=1, keepdims=True)
        s_new = jnp.sum(jnp.maximum(s, 0.0) * wb[:, 0:1], axis=0, keepdims=True) * IDX_SCALE
        snew_ref[0] = jnp.broadcast_to(s_new, (8, LANES))


def _sample_scores(page_table, qi16, wb, ki_new, cache_ik):
    bd, n_pages = page_table.shape
    pg = IDX_PAGES_PER_STEP
    bmap = lambda b, g, pt: (b, 0, 0)
    return pl.pallas_call(
        _sample_scores_kernel,
        out_shape=(jax.ShapeDtypeStruct((bd, n_pages, PAGE), F32),
                   jax.ShapeDtypeStruct((bd, 8, LANES), F32)),
        grid_spec=pltpu.PrefetchScalarGridSpec(
            num_scalar_prefetch=1,
            grid=(bd, n_pages // pg),
            in_specs=[pl.BlockSpec((1, IDX_HEADS, IDX_DIM), bmap),
                      pl.BlockSpec((1, IDX_HEADS, LANES), bmap),
                      pl.BlockSpec((1, 1, IDX_DIM), bmap)]
                     + [_page_spec((1, PAGE, IDX_DIM), pg, j) for j in range(pg)],
            out_specs=(pl.BlockSpec((1, pg, PAGE), lambda b, g, pt: (b, g, 0)),
                       pl.BlockSpec((1, 8, LANES), bmap))),
        compiler_params=pltpu.CompilerParams(dimension_semantics=("parallel", "arbitrary"),
                                             vmem_limit_bytes=VMEM_LIMIT),
        name="sample_scores",
    )(page_table, qi16, wb, ki_new, *([cache_ik] * pg))


SEL_THR, SEL_NEXT, SEL_CUT = 0, 1, 2


def _sample_select_kernel(sc_ref, snew_ref, sel_ref, *, topk):
    x = sc_ref[...]
    bd, n_pages, _ = x.shape
    past = n_pages * PAGE
    s_new = snew_ref[:, 0:1, 0:1]

    def total(v):
        return jnp.sum(jnp.sum(v, axis=1, keepdims=True), axis=2, keepdims=True)

    def count_ge(cand):
        return total(jnp.where(x >= cand, 1.0, 0.0)) + jnp.where(s_new >= cand, 1.0, 0.0)

    key = _kth_largest_key(count_ge, (bd, 1, 1), topk)
    thr = _key_to_float(key)
    thr_next = _key_to_float(key + 1)
    need = float(topk) - count_ge(thr_next)
    pos = (lax.broadcasted_iota(I32, (1, n_pages, PAGE), 1) * PAGE
           + lax.broadcasted_iota(I32, (1, n_pages, PAGE), 2)).astype(F32)

    def count_tie_le(cut):
        tie = jnp.where(x >= thr_next, 0.0, jnp.where(x >= thr, jnp.where(pos <= cut, 1.0, 0.0), 0.0))
        tie_new = jnp.where(s_new >= thr_next, 0.0, jnp.where(s_new >= thr, jnp.where(float(past) <= cut, 1.0, 0.0), 0.0))
        return total(tie) + tie_new

    cut = _tie_cutoff(count_tie_le, need, (bd, 1, 1), int(np.log2(past)) + 1)
    row = lax.broadcasted_iota(I32, (bd, 8, LANES), 1)
    sel_ref[...] = jnp.where(row == SEL_THR, thr, jnp.where(row == SEL_NEXT, thr_next, cut))


def _sample_select(scores, s_new, topk):
    bd = scores.shape[0]
    return pl.pallas_call(
        functools.partial(_sample_select_kernel, topk=topk),
        out_shape=jax.ShapeDtypeStruct((bd, 8, LANES), F32),
        compiler_params=pltpu.CompilerParams(vmem_limit_bytes=VMEM_LIMIT),
        name="sample_select",
    )(scores, s_new)


def _taken_bias(score, pos, sel):
    thr, thr_next, cut = sel[SEL_THR:SEL_THR + 1, 0:1], sel[SEL_NEXT:SEL_NEXT + 1, 0:1], sel[SEL_CUT:SEL_CUT + 1, 0:1]
    return jnp.where(score >= thr_next, 0.0,
                     jnp.where(score >= thr, jnp.where(pos <= cut, 0.0, NEG_BIG), NEG_BIG))


def _sample_attend_kernel(pt_ref, q_ref, az_ref, kn_ref, vn_ref, sc_ref, snew_ref, sel_ref, *refs, n_pages):
    pg = KV_PAGES_PER_STEP
    k_pages, v_pages = refs[:pg], refs[pg:2 * pg]
    out_ref, m_scr, l_scr, acc_scr = refs[2 * pg:]
    g = pl.program_id(1)
    n_groups = n_pages // pg

    @pl.when(g == 0)
    def _():
        m_scr[...] = jnp.full_like(m_scr, NEG_BIG)
        l_scr[...] = jnp.zeros_like(l_scr)
        acc_scr[...] = jnp.zeros_like(acc_scr)

    q = q_ref[0]
    sel = sel_ref[0]
    sc = sc_ref[0]
    head_kv = lax.broadcasted_iota(I32, (ATT_HEADS, PAGE), 0) // GROUP
    lane = lax.broadcasted_iota(I32, (1, PAGE), 1)

    logits = []
    for j in range(pg):
        lg = jnp.where(head_kv == 0,
                       _dot_nt(q, k_pages[j][0, :, 0, :].astype(BF16)),
                       _dot_nt(q, k_pages[j][0, :, 1, :].astype(BF16)))
        pos = ((g * pg + j) * PAGE + lane).astype(F32)
        logits.append(lg + _taken_bias(sc[j:j + 1, :], pos, sel))
    m_old = m_scr[...]
    tile_max = functools.reduce(jnp.maximum, logits)
    m_new = jnp.maximum(m_old, jnp.max(tile_max, axis=1, keepdims=True))
    alpha = jnp.exp(m_old - m_new)
    l_add = jnp.zeros((ATT_HEADS, 1), F32)
    acc_add = jnp.zeros((ATT_HEADS, HEAD_DIM), F32)
    for j in range(pg):
        p = jnp.exp(logits[j] - m_new)
        l_add = l_add + jnp.sum(p, axis=1, keepdims=True)
        pb = p.astype(BF16)
        acc_add = acc_add + jnp.where(head_kv == 0,
                                      _dot(pb, v_pages[j][0, :, 0, :].astype(BF16)),
                                      _dot(pb, v_pages[j][0, :, 1, :].astype(BF16)))
    m_scr[...] = m_new
    l_scr[...] = alpha * l_scr[...] + l_add
    acc_scr[...] = alpha * acc_scr[...] + acc_add

    @pl.when(g == n_groups - 1)
    def _():
        past = float(n_pages * PAGE)
        lg_new = (jnp.sum(q.astype(F32) * kn_ref[0].astype(BF16).astype(F32), axis=1, keepdims=True)
                  + _taken_bias(snew_ref[0][0:1, 0:1], past, sel))
        m_old = m_scr[...]
        m_fin = jnp.maximum(m_old, lg_new)
        alpha = jnp.exp(m_old - m_fin)
        p_new = jnp.exp(lg_new - m_fin)
        l_fin = alpha * l_scr[...] + p_new
        acc = alpha * acc_scr[...] + p_new * vn_ref[0].astype(BF16).astype(F32)
        out_ref[0] = (acc / l_fin) * _silu(az_ref[0])


def _sample_attend(page_table, q8, az8, k_new8, v_new8, scores, s_new, sel, cache_k, cache_v):
    bd, n_pages = page_table.shape
    pg = KV_PAGES_PER_STEP
    bmap = lambda b, g, pt: (b, 0, 0)
    head_tile = pl.BlockSpec((1, ATT_HEADS, HEAD_DIM), bmap)
    return pl.pallas_call(
        functools.partial(_sample_attend_kernel, n_pages=n_pages),
        out_shape=jax.ShapeDtypeStruct((bd, ATT_HEADS, HEAD_DIM), F32),
        grid_spec=pltpu.PrefetchScalarGridSpec(
            num_scalar_prefetch=1,
            grid=(bd, n_pages // pg),
            in_specs=[head_tile, head_tile, head_tile, head_tile,
                      pl.BlockSpec((1, pg, PAGE), lambda b, g, pt: (b, g, 0)),
                      pl.BlockSpec((1, 8, LANES), bmap),
                      pl.BlockSpec((1, 8, LANES), bmap)]
                     + [_page_spec((1, PAGE, KV_HEADS, HEAD_DIM), pg, j) for j in range(pg)] * 2,
            out_specs=head_tile,
            scratch_shapes=[pltpu.VMEM((ATT_HEADS, 1), F32),
                            pltpu.VMEM((ATT_HEADS, 1), F32),
                            pltpu.VMEM((ATT_HEADS, HEAD_DIM), F32)]),
        compiler_params=pltpu.CompilerParams(dimension_semantics=("parallel", "arbitrary"),
                                             vmem_limit_bytes=VMEM_LIMIT),
        name="sample_attend",
    )(page_table, q8, az8, k_new8, v_new8, scores, s_new, sel, *([cache_k] * pg), *([cache_v] * pg))


def _mlstm_step_kernel(bi_ref, bf_ref, q_ref, k_ref, v_ref, mo_ref, mz_ref, misc_ref, nw_ref,
                       c_ref, n_ref, m_ref, out_ref, c_out, n_out, m_out):
    misc = misc_ref[0]
    eye = lax.broadcasted_iota(I32, (ML_V, ML_V), 0) == lax.broadcasted_iota(I32, (ML_V, ML_V), 1)
    for h in range(ML_HEADS):
        ig = misc[:, MISC_IG + h:MISC_IG + h + 1] + bi_ref[h]
        lf = _log_sigmoid(misc[:, MISC_FG + h:MISC_FG + h + 1] + bf_ref[h])
        m_prev = m_ref[0, h][:, 0:1]
        log_a = lf + m_prev
        m_t = jnp.maximum(log_a, ig)
        d = jnp.exp(ig - m_t)
        a = jnp.exp(log_a - m_t)
        q = q_ref[0][:, h * ML_QK:(h + 1) * ML_QK]
        k = k_ref[0][:, h * ML_QK:(h + 1) * ML_QK]
        v = v_ref[0][:, h * ML_V:(h + 1) * ML_V]
        v_col = jnp.sum(jnp.where(eye, v, 0.0), axis=1, keepdims=True)
        c = c_ref[0, h]
        n = n_ref[0, h]
        s = jnp.sum(q * k, axis=1, keepdims=True) * d
        num = a * jnp.sum(c * q, axis=1, keepdims=True) + s * v_col
        den = a * jnp.sum(n * q, axis=1, keepdims=True) + s
        h_col = num / jnp.maximum(jnp.abs(den), jnp.exp(-m_t))
        c_out[0, h] = a * c + (d * v_col) * k
        n_out[0, h] = a * n + d * k
        m_out[0, h] = jnp.broadcast_to(m_t, (1, LANES))

        h_row = jnp.sum(jnp.where(eye, h_col, 0.0), axis=0, keepdims=True)
        ms = jnp.mean(h_row * h_row, axis=1, keepdims=True)
        hn = h_row * lax.rsqrt(ms + RMS_EPS) * nw_ref[:, h * ML_V:(h + 1) * ML_V]
        gate = _sigmoid(mo_ref[0][:, h * ML_V:(h + 1) * ML_V]) * _silu(mz_ref[0][:, h * ML_V:(h + 1) * ML_V])
        out_ref[0, :, h * ML_V:(h + 1) * ML_V] = (hn * gate).astype(out_ref.dtype)


def _mlstm_step(ps_main, ps_small, b_i, b_f, ml_norm_w, state_c, state_n, state_m):
    bd = ps_main.shape[0]
    col = lambda j: (lambda b: (b, 0, j))
    st4 = lambda b: (b, 0, 0, 0)
    return pl.pallas_call(
        _mlstm_step_kernel,
        out_shape=(jax.ShapeDtypeStruct((bd, 1, ML_WIDTH), BF16),
                   jax.ShapeDtypeStruct(state_c.shape, F32),
                   jax.ShapeDtypeStruct(state_n.shape, F32),
                   jax.ShapeDtypeStruct(state_m.shape, F32)),
        grid=(bd,),
        in_specs=[pl.BlockSpec(memory_space=pltpu.SMEM),
                  pl.BlockSpec(memory_space=pltpu.SMEM),
                  pl.BlockSpec((1, 1, 512), col(12)),
                  pl.BlockSpec((1, 1, 512), col(13)),
                  pl.BlockSpec((1, 1, 1024), col(3)),
                  pl.BlockSpec((1, 1, 1024), col(4)),
                  pl.BlockSpec((1, 1, 1024), col(5)),
                  pl.BlockSpec((1, 1, LANES), col(4)),
                  pl.BlockSpec((1, ML_WIDTH), lambda b: (0, 0)),
                  pl.BlockSpec((1, ML_HEADS, ML_V, ML_QK), st4),
                  pl.BlockSpec((1, ML_HEADS, 1, ML_QK), st4),
                  pl.BlockSpec((1, ML_HEADS, 1, LANES), st4)],
        out_specs=(pl.BlockSpec((1, 1, ML_WIDTH), lambda b: (b, 0, 0)),
                   pl.BlockSpec((1, ML_HEADS, ML_V, ML_QK), st4),
                   pl.BlockSpec((1, ML_HEADS, 1, ML_QK), st4),
                   pl.BlockSpec((1, ML_HEADS, 1, LANES), st4)),
        compiler_params=pltpu.CompilerParams(dimension_semantics=("parallel",),
                                             vmem_limit_bytes=VMEM_LIMIT),
        name="mlstm_step",
    )(b_i, b_f, ps_main, ps_main, ps_main, ps_main, ps_main, ps_small, ml_norm_w.reshape(1, ML_WIDTH),
      state_c, state_n, state_m)


def _split_weights(w_in):
    offs = np.cumsum((0,) + IN_SIZES)
    aq, ak, av, iq, ik, iw, az, mq, mk, mv, mi, mf, mo, mz = (w_in[:, offs[j]:offs[j + 1]] for j in range(len(IN_SIZES)))
    w_main = jnp.concatenate([aq * ATT_SCALE, iq, az, mv, mo, mz, mq, mk * ML_QK ** -0.5], axis=1)
    pad = jnp.zeros((w_in.shape[0], LANES - IDX_DIM - IDX_HEADS - 2 * ML_HEADS), w_in.dtype)
    w_small = jnp.concatenate([ak, av, ik, iw, mi, mf, pad], axis=1)
    assert w_main.shape[1] == MAIN_W and w_small.shape[1] == SMALL_W
    return w_main.astype(BF16), w_small.astype(BF16)


def kernel(x_prompt, x_sample, cache_k, cache_v, cache_idx_k, state_C, state_n, state_m, page_table,
           norm_w, w_in, b_igate, b_fgate, ml_norm_w, w_out, final_norm_w):
    depth = w_in.shape[0]
    batch, seq, d = x_prompt.shape
    bd, dec_seq, _ = x_sample.shape
    assert depth == 1 and dec_seq == 1 and d == D_MODEL
    n_pages = page_table.shape[1]

    w_main, w_small = _split_weights(w_in[0])
    w_o = w_out[0].astype(BF16)
    w_o_att, w_o_ml = w_o[:ATT_WIDTH], w_o[ATT_WIDTH:]

    xp = x_prompt.reshape(batch * seq, d)
    p_main = _project(xp, norm_w[0], w_main, BF16, 1024, 512, "proj_main")
    p_small = _project(xp, norm_w[0], w_small, F32, 1024, SMALL_W, "proj_small")
    att = _dsa_prompt(p_main, p_small, batch, seq)
    ml, ct, m_p = _mlstm_prompt(p_main, p_small, b_igate[0], b_fgate[0], ml_norm_w[0], batch, seq)
    y_prompt = _out_project(xp, att, ml, w_o_att, w_o_ml, final_norm_w, 256, "out_prompt").reshape(batch, seq, d)
    k_prompt = p_small[:, :KV_WIDTH].reshape(1, batch, seq, KV_HEADS, HEAD_DIM)
    v_prompt = p_small[:, KV_WIDTH:2 * KV_WIDTH].reshape(1, batch, seq, KV_HEADS, HEAD_DIM)
    ik_prompt = p_small[:, 2 * KV_WIDTH:2 * KV_WIDTH + IDX_DIM].reshape(1, batch, seq, IDX_DIM)
    c_prompt = jnp.swapaxes(ct[..., :ML_V], -1, -2)[None]
    n_prompt = ct[..., ML_V][None]
    m_prompt = m_p[:, :, 0, 0][None]

    xs = x_sample.reshape(bd, d)
    ps_main = _project(xs, norm_w[0], w_main, F32, bd, 512, "proj_main_s")
    ps_small = _project(xs, norm_w[0], w_small, F32, bd, SMALL_W, "proj_small_s")
    q8 = ps_main[:, :ATT_WIDTH].reshape(bd, ATT_HEADS, HEAD_DIM).astype(BF16)
    qi16 = ps_main[:, 1024:2048].reshape(bd, IDX_HEADS, IDX_DIM).astype(BF16)
    az8 = ps_main[:, 2048:3072].reshape(bd, ATT_HEADS, HEAD_DIM)
    misc_s = ps_small[:, 2 * KV_WIDTH:]
    ki_new = misc_s[:, :IDX_DIM].reshape(bd, 1, IDX_DIM)
    wb = jnp.broadcast_to(misc_s[:, MISC_WI:MISC_WI + IDX_HEADS, None], (bd, IDX_HEADS, LANES))
    k_new8 = jnp.repeat(ps_small[:, :KV_WIDTH].reshape(bd, KV_HEADS, HEAD_DIM), GROUP, axis=1)
    v_new8 = jnp.repeat(ps_small[:, KV_WIDTH:2 * KV_WIDTH].reshape(bd, KV_HEADS, HEAD_DIM), GROUP, axis=1)
    scores, s_new = _sample_scores(page_table, qi16, wb, ki_new, cache_idx_k[0])
    sel = _sample_select(scores, s_new, min(TOPK_MAX, (n_pages * PAGE + 1) // 4))
    att_s = _sample_attend(page_table, q8, az8, k_new8, v_new8, scores, s_new, sel,
                           cache_k[0], cache_v[0])
    ml_s, c_s, n_s, m_s = _mlstm_step(
        ps_main.reshape(bd, 1, MAIN_W), ps_small.reshape(bd, 1, SMALL_W), b_igate[0], b_fgate[0], ml_norm_w[0],
        state_C[0], state_n[0].reshape(bd, ML_HEADS, 1, ML_QK),
        jnp.broadcast_to(state_m[0][:, :, None, None], (bd, ML_HEADS, 1, LANES)))
    y_sample = _out_project(xs, att_s.reshape(bd, ATT_WIDTH).astype(BF16), ml_s.reshape(bd, ML_WIDTH),
                            w_o_att, w_o_ml, final_norm_w, bd, "out_sample").reshape(bd, 1, d)
    k_sample = ps_small[:, :KV_WIDTH].reshape(1, bd, 1, KV_HEADS, HEAD_DIM)
    v_sample = ps_small[:, KV_WIDTH:2 * KV_WIDTH].reshape(1, bd, 1, KV_HEADS, HEAD_DIM)
    ik_sample = misc_s[:, :IDX_DIM].reshape(1, bd, 1, IDX_DIM)

    return (y_prompt, y_sample, k_prompt, v_prompt, ik_prompt, c_prompt, n_prompt, m_prompt,
            k_sample, v_sample, ik_sample, c_s[None], n_s.reshape(1, bd, ML_HEADS, ML_QK), m_s[:, :, 0, 0][None])
```

```python
import functools

import jax
import jax.numpy as jnp
import numpy as np
from jax import lax
from jax.experimental import pallas as pl
from jax.experimental.pallas import tpu as pltpu

F32 = jnp.float32
BF16 = jnp.bfloat16
I32 = jnp.int32

D_MODEL = 2048
PAGE = 128
ATT_HEADS = 8
KV_HEADS = 2
HEAD_DIM = 128
GROUP = ATT_HEADS // KV_HEADS
ATT_WIDTH = ATT_HEADS * HEAD_DIM
KV_WIDTH = KV_HEADS * HEAD_DIM
ATT_SCALE = HEAD_DIM ** -0.5
IDX_HEADS = 16
IDX_DIM = 64
IDX_SCALE = (IDX_HEADS * IDX_DIM) ** -0.5
TOPK_MAX = 256
ML_HEADS = 4
ML_QK = 128
ML_V = 256
ML_WIDTH = ML_HEADS * ML_V
RMS_EPS = 1e-6
IN_SIZES = (ATT_WIDTH, KV_WIDTH, KV_WIDTH, IDX_HEADS * IDX_DIM, IDX_DIM, IDX_HEADS, ATT_WIDTH,
            ML_HEADS * ML_QK, ML_HEADS * ML_QK, ML_WIDTH, ML_HEADS, ML_HEADS, ML_WIDTH, ML_WIDTH)

LANES = 128
NEG_BIG = -1e30
VMEM_LIMIT = 56 * 1024 * 1024

MAIN_W = 7168
SMALL_W = 640
MISC_WI = IDX_DIM
MISC_IG = IDX_DIM + IDX_HEADS
MISC_FG = MISC_IG + ML_HEADS


def _dot(a, b):
    return jnp.dot(a, b, preferred_element_type=F32)


def _dot_nt(a, b):
    return lax.dot_general(a, b, (((1,), (1,)), ((), ())), preferred_element_type=F32)


def _tree_reduce(op, parts):
    parts = list(parts)
    while len(parts) > 1:
        paired = [op(parts[j], parts[j + 1]) for j in range(0, len(parts) - 1, 2)]
        parts = paired + parts[len(parts) - len(parts) % 2:]
    return parts[0]


def _fold_rows(op, x):
    return _tree_reduce(op, [x[r:r + 8] for r in range(0, x.shape[0], 8)])


def _proj_kernel(x_ref, nw_ref, w_ref, o_ref, h_scr):
    @pl.when(pl.program_id(1) == 0)
    def _():
        x = x_ref[...]
        ms = jnp.mean(x * x, axis=-1, keepdims=True)
        h_scr[...] = (x * lax.rsqrt(ms + RMS_EPS) * nw_ref[...]).astype(BF16)

    o_ref[...] = _dot(h_scr[...], w_ref[...]).astype(o_ref.dtype)


def _project(x2d, norm_w, w, out_dtype, tm, tn, name):
    m, d = x2d.shape
    n = w.shape[1]
    return pl.pallas_call(
        _proj_kernel,
        out_shape=jax.ShapeDtypeStruct((m, n), out_dtype),
        grid=(m // tm, n // tn),
        in_specs=[pl.BlockSpec((tm, d), lambda i, j: (i, 0)),
                  pl.BlockSpec((1, d), lambda i, j: (0, 0)),
                  pl.BlockSpec((d, tn), lambda i, j: (0, j))],
        out_specs=pl.BlockSpec((tm, tn), lambda i, j: (i, j)),
        scratch_shapes=[pltpu.VMEM((tm, d), BF16)],
        compiler_params=pltpu.CompilerParams(dimension_semantics=("parallel", "arbitrary"),
                                             vmem_limit_bytes=VMEM_LIMIT),
        name=name,
    )(x2d, norm_w.reshape(1, d), w)


def _out_kernel(x_ref, a_ref, m_ref, wa_ref, wm_ref, fw_ref, o_ref):
    y = x_ref[...] + _dot(a_ref[...], wa_ref[...]) + _dot(m_ref[...], wm_ref[...])
    ms = jnp.mean(y * y, axis=-1, keepdims=True)
    o_ref[...] = y * lax.rsqrt(ms + RMS_EPS) * fw_ref[...]


def _out_project(x2d, a, mo, w_att, w_ml, final_w, tm, name):
    m, d = x2d.shape
    return pl.pallas_call(
        _out_kernel,
        out_shape=jax.ShapeDtypeStruct((m, d), F32),
        grid=(m // tm,),
        in_specs=[pl.BlockSpec((tm, d), lambda i: (i, 0)),
                  pl.BlockSpec((tm, ATT_WIDTH), lambda i: (i, 0)),
                  pl.BlockSpec((tm, ML_WIDTH), lambda i: (i, 0)),
                  pl.BlockSpec((ATT_WIDTH, d), lambda i: (0, 0)),
                  pl.BlockSpec((ML_WIDTH, d), lambda i: (0, 0)),
                  pl.BlockSpec((1, d), lambda i: (0, 0))],
        out_specs=pl.BlockSpec((tm, d), lambda i: (i, 0)),
        compiler_params=pltpu.CompilerParams(dimension_semantics=("parallel",),
                                             vmem_limit_bytes=VMEM_LIMIT),
        name=name,
    )(x2d, a, mo, w_att, w_ml, final_w.reshape(1, d))


def _log_sigmoid(x):
    return jnp.minimum(x, 0.0) - jnp.log(1.0 + jnp.exp(-jnp.abs(x)))


def _sigmoid(x):
    return 1.0 / (1.0 + jnp.exp(-x))


def _silu(x):
    return x * _sigmoid(x)


ML_CHUNK = 256
ML_AUG = ML_V + LANES


def _mlstm_kernel(bi_ref, bf_ref, q_ref, k_ref, v_ref, mo_ref, mz_ref, misc_ref, nw_ref,
                  out_ref, ct_ref, m_ref):
    L = ML_CHUNK

    @pl.when(pl.program_id(1) == 0)
    def _():
        ct_ref[...] = jnp.zeros_like(ct_ref)
        m_ref[...] = jnp.zeros_like(m_ref)

    misc = misc_ref[...]
    misc_t = misc.T
    t_idx = lax.broadcasted_iota(I32, (L, L), 0)
    s_idx = lax.broadcasted_iota(I32, (L, L), 1)
    causal = s_idx <= t_idx
    ones_col = jnp.where(lax.broadcasted_iota(I32, (L, LANES), 1) == 0, 1.0, 0.0).astype(BF16)

    for h in range(ML_HEADS):
        ig_row = misc_t[MISC_IG + h:MISC_IG + h + 1, :] + bi_ref[h]
        lf_row = _log_sigmoid(misc_t[MISC_FG + h:MISC_FG + h + 1, :] + bf_ref[h])
        lf_col = _log_sigmoid(misc[:, MISC_FG + h:MISC_FG + h + 1] + bf_ref[h])
        b_col = jnp.sum(jnp.where(causal, lf_row, 0.0), axis=1, keepdims=True)
        b_row = jnp.sum(jnp.where(t_idx <= s_idx, lf_col, 0.0), axis=0, keepdims=True)
        m_prev = m_ref[0, h][0:1, 0:1]
        log_d = jnp.where(causal, b_col - b_row + ig_row, -jnp.inf)
        log_a = b_col + m_prev
        m_t = jnp.maximum(log_a, jnp.max(log_d, axis=1, keepdims=True))
        d = jnp.exp(log_d - m_t)
        a = jnp.exp(log_a - m_t)

        q = q_ref[:, h * ML_QK:(h + 1) * ML_QK]
        k = k_ref[:, h * ML_QK:(h + 1) * ML_QK]
        v_aug = jnp.concatenate([v_ref[:, h * ML_V:(h + 1) * ML_V], ones_col], axis=1)
        s = (_dot_nt(q, k) * d).astype(BF16)
        ct = ct_ref[0, h]
        num_aug = a * _dot(q, ct.astype(BF16)) + _dot(s, v_aug)
        den = num_aug[:, ML_V:ML_V + 1]
        hh = num_aug[:, :ML_V] / jnp.maximum(jnp.abs(den), jnp.exp(-m_t))

        m_new = m_t[L - 1:L, :]
        a_end = a[L - 1:L, :]
        w_row = jnp.exp(b_row[:, L - 1:L] - b_row + ig_row - m_new)
        ktw = (k.astype(F32).T * w_row).astype(BF16)
        ct_new = a_end * ct + _dot(ktw, v_aug)
        ct_ref[0, h] = ct_new
        m_ref[0, h] = jnp.broadcast_to(m_new, (8, LANES))

        ms = jnp.mean(hh * hh, axis=1, keepdims=True)
        hn = hh * lax.rsqrt(ms + RMS_EPS) * nw_ref[:, h * ML_V:(h + 1) * ML_V]
        gate = _sigmoid(mo_ref[:, h * ML_V:(h + 1) * ML_V].astype(F32)) * _silu(mz_ref[:, h * ML_V:(h + 1) * ML_V].astype(F32))
        out_ref[:, h * ML_V:(h + 1) * ML_V] = (hn * gate).astype(out_ref.dtype)


def _mlstm_prompt(p_main, p_small, b_i, b_f, ml_norm_w, batch, seq):
    L = ML_CHUNK
    nc = seq // L
    row = lambda b, c: b * nc + c
    return pl.pallas_call(
        _mlstm_kernel,
        out_shape=(jax.ShapeDtypeStruct((batch * seq, ML_WIDTH), BF16),
                   jax.ShapeDtypeStruct((batch, ML_HEADS, ML_QK, ML_AUG), F32),
                   jax.ShapeDtypeStruct((batch, ML_HEADS, 8, LANES), F32)),
        grid=(batch, nc),
        in_specs=[pl.BlockSpec(memory_space=pltpu.SMEM),
                  pl.BlockSpec(memory_space=pltpu.SMEM),
                  pl.BlockSpec((L, 512), lambda b, c: (row(b, c), 12)),
                  pl.BlockSpec((L, 512), lambda b, c: (row(b, c), 13)),
                  pl.BlockSpec((L, 1024), lambda b, c: (row(b, c), 3)),
                  pl.BlockSpec((L, 1024), lambda b, c: (row(b, c), 4)),
                  pl.BlockSpec((L, 1024), lambda b, c: (row(b, c), 5)),
                  pl.BlockSpec((L, LANES), lambda b, c: (row(b, c), 4)),
                  pl.BlockSpec((1, ML_WIDTH), lambda b, c: (0, 0))],
        out_specs=(pl.BlockSpec((L, ML_WIDTH), lambda b, c: (row(b, c), 0)),
                   pl.BlockSpec((1, ML_HEADS, ML_QK, ML_AUG), lambda b, c: (b, 0, 0, 0)),
                   pl.BlockSpec((1, ML_HEADS, 8, LANES), lambda b, c: (b, 0, 0, 0))),
        compiler_params=pltpu.CompilerParams(dimension_semantics=("parallel", "arbitrary"),
                                             vmem_limit_bytes=VMEM_LIMIT),
        name="mlstm_prompt",
    )(b_i, b_f, p_main, p_main, p_main, p_main, p_main, p_small, ml_norm_w.reshape(1, ML_WIDTH))


INT_MIN = -2 ** 31


def _key_to_float(key):
    bits = jnp.where(key >= 0, key, key ^ jnp.int32(0x7FFFFFFF))
    return lax.bitcast_convert_type(bits, F32)


KEY_NEG_INF = INT_MIN + 0x7FFFFF


def _kth_largest_key(count_ge, shape, k):
    def bit_body(it, prefix):
        cand = prefix + lax.shift_left(jnp.int32(1), 31 - it)
        cand_f = _key_to_float(jnp.maximum(cand, KEY_NEG_INF))
        return jnp.where(count_ge(cand_f) >= float(k), cand, prefix)

    return lax.fori_loop(0, 32, bit_body, jnp.full(shape, INT_MIN, I32))


def _tie_cutoff(count_tie_le, need, shape, index_bits):
    def bit_body(it, lo):
        cand = lo + lax.shift_left(jnp.int32(1), index_bits - 1 - it).astype(F32)
        return jnp.where(count_tie_le(cand) < need, cand, lo)

    return lax.fori_loop(0, index_bits, bit_body, jnp.full(shape, -1.0, F32)) + 1.0


DSA_QB = 128
DSA_TK = 256


def _dsa_kernel(q_ref, qi_ref, az_ref, miscq_ref, k_ref, v_ref, misck_ref, out_ref,
                k_bf, vt_scr, ki_lo, ki_hi, sc_scr, acc_scr, j_scr, *, topk):
    i = pl.program_id(1)
    seq = k_ref.shape[0]
    QB, TK = DSA_QB, DSA_TK
    nt = (i * QB + QB + TK - 1) // TK

    @pl.when(i == 0)
    def _():
        k_bf[...] = k_ref[...].astype(BF16)

        def v_tile(t, carry):
            vt_scr[t] = v_ref[pl.ds(pl.multiple_of(t * TK, TK), TK), :].T.astype(BF16)
            return carry

        lax.fori_loop(0, seq // TK, v_tile, 0)
        lane = lax.broadcasted_iota(I32, (seq, LANES), 1)
        lo = jnp.where(lane < IDX_DIM, misck_ref[...], 0.0)
        ki_lo[...] = lo.astype(BF16)
        ki_hi[...] = pltpu.roll(lo, IDX_DIM, axis=1).astype(BF16)

    k_iota = lax.broadcasted_iota(I32, (TK, QB), 0)
    q_pos = i * QB + lax.broadcasted_iota(I32, (TK, QB), 1)
    w_t = miscq_ref[...].T

    def tile_rows(t):
        return pl.ds(pl.multiple_of(t * TK, TK), TK)

    def score_tile(t, carry):
        klo = ki_lo[tile_rows(t), :]
        khi = ki_hi[tile_rows(t), :]
        acc = jnp.zeros((TK, QB), F32)
        for p in range(IDX_HEADS // 2):
            qp = qi_ref[:, p * LANES:(p + 1) * LANES]
            w0 = w_t[MISC_WI + 2 * p:MISC_WI + 2 * p + 1, :]
            w1 = w_t[MISC_WI + 2 * p + 1:MISC_WI + 2 * p + 2, :]
            acc = acc + w0 * jnp.maximum(_dot_nt(klo, qp), 0.0) + w1 * jnp.maximum(_dot_nt(khi, qp), 0.0)
        k_pos = t * TK + k_iota
        sc_scr[tile_rows(t), :] = jnp.where(k_pos <= q_pos, acc * IDX_SCALE, -jnp.inf)
        return carry

    lax.fori_loop(0, nt, score_tile, 0)

    def count_ge(cand):
        def body(t, cnt):
            x = sc_scr[tile_rows(t), :]
            return cnt + _fold_rows(jnp.add, jnp.where(x >= cand, 1.0, 0.0))

        cnt = lax.fori_loop(0, nt, body, jnp.zeros((8, QB), F32))
        return jnp.sum(cnt, axis=0, keepdims=True)

    key = _kth_largest_key(count_ge, (1, QB), topk)
    thr = _key_to_float(key)
    thr_next = _key_to_float(key + 1)
    need = float(topk) - count_ge(thr_next)

    j_scr[...] = jnp.full((1, QB), float(seq), F32)

    @pl.when(jnp.max(count_ge(thr)) > float(topk))
    def _():
        def count_tie_le(cut):
            def body(t, cnt):
                x = sc_scr[tile_rows(t), :]
                k_pos = (t * TK + k_iota).astype(F32)
                tie = jnp.where(x >= thr_next, 0.0, jnp.where(x >= thr, jnp.where(k_pos <= cut, 1.0, 0.0), 0.0))
                return cnt + _fold_rows(jnp.add, tie)

            cnt = lax.fori_loop(0, nt, body, jnp.zeros((8, QB), F32))
            return jnp.sum(cnt, axis=0, keepdims=True)

        j_scr[...] = _tie_cutoff(count_tie_le, need, (1, QB), int(np.log2(seq)))

    cut = j_scr[...]

    def bias_tile(t, carry):
        x = sc_scr[tile_rows(t), :]
        k_pos = t * TK + k_iota
        taken = jnp.where(x >= thr_next, 0.0,
                          jnp.where(x >= thr, jnp.where(k_pos.astype(F32) <= cut, 0.0, NEG_BIG), NEG_BIG))
        sc_scr[tile_rows(t), :] = jnp.where(k_pos <= q_pos, taken, NEG_BIG)
        return carry

    lax.fori_loop(0, nt, bias_tile, 0)

    az = az_ref[...].astype(F32)
    GQ = GROUP * QB
    for g in range(KV_HEADS):
        qg = jnp.concatenate([q_ref[:, (g * GROUP + hh) * HEAD_DIM:(g * GROUP + hh + 1) * HEAD_DIM]
                              for hh in range(GROUP)], axis=0)
        acc_scr[...] = jnp.zeros_like(acc_scr)

        def att_tile(t, carry):
            m, l8 = carry
            kt = k_bf[tile_rows(t), g * HEAD_DIM:(g + 1) * HEAD_DIM]
            bias = sc_scr[tile_rows(t), :]
            lg = _dot_nt(kt, qg) + jnp.concatenate([bias] * GROUP, axis=1)
            m_new = jnp.maximum(m, jnp.max(_fold_rows(jnp.maximum, lg), axis=0, keepdims=True))
            alpha = jnp.exp(m - m_new)
            p = jnp.exp(lg - m_new)
            l8 = alpha * l8 + _fold_rows(jnp.add, p)
            vt = vt_scr[t][g * HEAD_DIM:(g + 1) * HEAD_DIM, :]
            acc_scr[...] = alpha * acc_scr[...] + _dot(vt, p.astype(BF16))
            return m_new, l8

        init = (jnp.full((1, GQ), NEG_BIG, F32), jnp.zeros((8, GQ), F32))
        _, l8 = lax.fori_loop(0, nt, att_tile, init)
        o_t = acc_scr[...] / jnp.sum(l8, axis=0, keepdims=True)
        for hh in range(GROUP):
            c0 = (g * GROUP + hh) * HEAD_DIM
            o = o_t[:, hh * QB:(hh + 1) * QB].T
            out_ref[:, c0:c0 + HEAD_DIM] = (o * _silu(az[:, c0:c0 + HEAD_DIM])).astype(out_ref.dtype)


def _dsa_prompt(p_main, p_small, batch, seq):
    QB, TK = DSA_QB, DSA_TK
    nb = seq // QB
    topk = min(TOPK_MAX, seq // 4)
    assert TK >= topk and seq % TK == 0 and TK % QB == 0
    qrow = lambda b, i: b * nb + i
    return pl.pallas_call(
        functools.partial(_dsa_kernel, topk=topk),
        out_shape=jax.ShapeDtypeStruct((batch * seq, ATT_WIDTH), BF16),
        grid=(batch, nb),
        in_specs=[pl.BlockSpec((QB, ATT_WIDTH), lambda b, i: (qrow(b, i), 0)),
                  pl.BlockSpec((QB, 1024), lambda b, i: (qrow(b, i), 1)),
                  pl.BlockSpec((QB, ATT_WIDTH), lambda b, i: (qrow(b, i), 2)),
                  pl.BlockSpec((QB, LANES), lambda b, i: (qrow(b, i), 4)),
                  pl.BlockSpec((seq, KV_WIDTH), lambda b, i: (b, 0)),
                  pl.BlockSpec((seq, KV_WIDTH), lambda b, i: (b, 1)),
                  pl.BlockSpec((seq, LANES), lambda b, i: (b, 4))],
        out_specs=pl.BlockSpec((QB, ATT_WIDTH), lambda b, i: (qrow(b, i), 0)),
        scratch_shapes=[pltpu.VMEM((seq, KV_WIDTH), BF16),
                        pltpu.VMEM((seq // TK, KV_WIDTH, TK), BF16),
                        pltpu.VMEM((seq, LANES), BF16),
                        pltpu.VMEM((seq, LANES), BF16),
                        pltpu.VMEM((seq, QB), F32),
                        pltpu.VMEM((HEAD_DIM, GROUP * QB), F32),
                        pltpu.VMEM((1, QB), F32)],
        compiler_params=pltpu.CompilerParams(dimension_semantics=("parallel", "arbitrary"),
                                             vmem_limit_bytes=VMEM_LIMIT),
        name="dsa_prompt",
    )(p_main, p_main, p_main, p_small, p_small, p_small, p_small)


IDX_PAGES_PER_STEP = 32
KV_PAGES_PER_STEP = 16


def _page_spec(block, group, j):
    return pl.BlockSpec(block, lambda b, g, pt: (pt[b, g * group + j],) + (0,) * (len(block) - 1))


def _sample_scores_kernel(pt_ref, qi_ref, wb_ref, kin_ref, *refs):
    pages = refs[:IDX_PAGES_PER_STEP]
    sc_ref, snew_ref = refs[IDX_PAGES_PER_STEP:]
    qi = qi_ref[0]
    wb = wb_ref[0]
    rows = []
    for page in pages:
        s = _dot(qi, page[0].astype(BF16))
        rows.append(jnp.sum(jnp.maximum(s, 0.0) * wb, axis=0, keepdims=True) * IDX_SCALE)
    sc_ref[0] = jnp.concatenate(rows, axis=0)

    @pl.when(pl.program_id(1) == 0)
    def _():
        ki_new = kin_ref[0].astype(BF16).astype(F32)
        s = jnp.sum(qi.astype(F32) * ki_new, axis=1, keepdims=True)
        s_new = jnp.sum(jnp.maximum(s, 0.0) * wb[:, 0:1], axis=0, keepdims=True) * IDX_SCALE
        snew_ref[0] = jnp.broadcast_to(s_new, (8, LANES))


def _sample_scores(page_table, qi16, wb, ki_new, cache_ik_t):
    bd, n_pages = page_table.shape
    pg = IDX_PAGES_PER_STEP
    bmap = lambda b, g, pt: (b, 0, 0)
    return pl.pallas_call(
        _sample_scores_kernel,
        out_shape=(jax.ShapeDtypeStruct((bd, n_pages, PAGE), F32),
                   jax.ShapeDtypeStruct((bd, 8, LANES), F32)),
        grid_spec=pltpu.PrefetchScalarGridSpec(
            num_scalar_prefetch=1,
            grid=(bd, n_pages // pg),
            in_specs=[pl.BlockSpec((1, IDX_HEADS, IDX_DIM), bmap),
                      pl.BlockSpec((1, IDX_HEADS, LANES), bmap),
                      pl.BlockSpec((1, 1, IDX_DIM), bmap)]
                     + [_page_spec((1, IDX_DIM, PAGE), pg, j) for j in range(pg)],
            out_specs=(pl.BlockSpec((1, pg, PAGE), lambda b, g, pt: (b, g, 0)),
                       pl.BlockSpec((1, 8, LANES), bmap))),
        compiler_params=pltpu.CompilerParams(dimension_semantics=("parallel", "arbitrary"),
                                             vmem_limit_bytes=VMEM_LIMIT),
        name="sample_scores",
    )(page_table, qi16, wb, ki_new, *([cache_ik_t] * pg))


SEL_THR, SEL_NEXT, SEL_CUT = 0, 1, 2


def _sample_select_kernel(sc_ref, snew_ref, sel_ref, *, topk):
    x = sc_ref[...]
    bd, n_pages, _ = x.shape
    past = n_pages * PAGE
    s_new = snew_ref[:, 0:1, 0:1]

    def total(v):
        return jnp.sum(jnp.sum(v, axis=1, keepdims=True), axis=2, keepdims=True)

    def count_ge(cand):
        return total(jnp.where(x >= cand, 1.0, 0.0)) + jnp.where(s_new >= cand, 1.0, 0.0)

    key = _kth_largest_key(count_ge, (bd, 1, 1), topk)
    thr = _key_to_float(key)
    thr_next = _key_to_float(key + 1)
    need = float(topk) - count_ge(thr_next)
    pos = (lax.broadcasted_iota(I32, (1, n_pages, PAGE), 1) * PAGE
           + lax.broadcasted_iota(I32, (1, n_pages, PAGE), 2)).astype(F32)

    def count_tie_le(cut):
        tie = jnp.where(x >= thr_next, 0.0, jnp.where(x >= thr, jnp.where(pos <= cut, 1.0, 0.0), 0.0))
        tie_new = jnp.where(s_new >= thr_next, 0.0, jnp.where(s_new >= thr, jnp.where(float(past) <= cut, 1.0, 0.0), 0.0))
        return total(tie) + tie_new

    cut = _tie_cutoff(count_tie_le, need, (bd, 1, 1), int(np.log2(past)) + 1)
    row = lax.broadcasted_iota(I32, (bd, 8, LANES), 1)
    sel_ref[...] = jnp.where(row == SEL_THR, thr, jnp.where(row == SEL_NEXT, thr_next, cut))


def _sample_select(scores, s_new, topk):
    bd = scores.shape[0]
    return pl.pallas_call(
        functools.partial(_sample_select_kernel, topk=topk),
        out_shape=jax.ShapeDtypeStruct((bd, 8, LANES), F32),
        compiler_params=pltpu.CompilerParams(vmem_limit_bytes=VMEM_LIMIT),
        name="sample_select",
    )(scores, s_new)


def _taken_bias(score, pos, sel):
    thr, thr_next, cut = sel[SEL_THR:SEL_THR + 1, 0:1], sel[SEL_NEXT:SEL_NEXT + 1, 0:1], sel[SEL_CUT:SEL_CUT + 1, 0:1]
    return jnp.where(score >= thr_next, 0.0,
                     jnp.where(score >= thr, jnp.where(pos <= cut, 0.0, NEG_BIG), NEG_BIG))


def _sample_attend_kernel(pt_ref, q_ref, az_ref, kn_ref, vn_ref, sc_ref, snew_ref, sel_ref, *refs, n_pages):
    pg = KV_PAGES_PER_STEP
    k_pages, v_pages = refs[:pg], refs[pg:2 * pg]
    out_ref, m_scr, l_scr, acc_scr = refs[2 * pg:]
    g = pl.program_id(1)
    n_groups = n_pages // pg

    @pl.when(g == 0)
    def _():
        m_scr[...] = jnp.full_like(m_scr, NEG_BIG)
        l_scr[...] = jnp.zeros_like(l_scr)
        acc_scr[...] = jnp.zeros_like(acc_scr)

    pw = PAGE * KV_HEADS
    q = q_ref[0]
    sel = sel_ref[0]
    pos = ((g * pg + lax.broadcasted_iota(I32, (pg, PAGE), 0)) * PAGE
           + lax.broadcasted_iota(I32, (pg, PAGE), 1)).astype(F32)
    key_bias = _taken_bias(sc_ref[0], pos, sel)
    spread = (lax.broadcasted_iota(I32, (PAGE, pw), 1) // KV_HEADS
              == lax.broadcasted_iota(I32, (PAGE, pw), 0)).astype(BF16)
    col_bias = _dot(key_bias.astype(BF16), spread)
    own_head = (lax.broadcasted_iota(I32, (ATT_HEADS, pw), 1) % KV_HEADS
                == lax.broadcasted_iota(I32, (ATT_HEADS, pw), 0) // GROUP)
    head_bias = jnp.where(own_head, 0.0, NEG_BIG)

    logits = [_dot_nt(q, k_pages[j][...].astype(BF16)) + col_bias[j:j + 1, :] + head_bias for j in range(pg)]
    m_old = m_scr[...]
    m_new = jnp.maximum(m_old, jnp.max(_tree_reduce(jnp.maximum, logits), axis=1, keepdims=True))
    alpha = jnp.exp(m_old - m_new)
    probs = [jnp.exp(lg - m_new) for lg in logits]
    l_add = jnp.sum(_tree_reduce(jnp.add, probs), axis=1, keepdims=True)
    acc_add = _tree_reduce(jnp.add, [_dot(probs[j].astype(BF16), v_pages[j][...].astype(BF16)) for j in range(pg)])
    m_scr[...] = m_new
    l_scr[...] = alpha * l_scr[...] + l_add
    acc_scr[...] = alpha * acc_scr[...] + acc_add

    @pl.when(g == n_groups - 1)
    def _():
        past = float(n_pages * PAGE)
        lg_new = (jnp.sum(q.astype(F32) * kn_ref[0].astype(BF16).astype(F32), axis=1, keepdims=True)
                  + _taken_bias(snew_ref[0][0:1, 0:1], past, sel))
        m_old = m_scr[...]
        m_fin = jnp.maximum(m_old, lg_new)
        alpha = jnp.exp(m_old - m_fin)
        p_new = jnp.exp(lg_new - m_fin)
        l_fin = alpha * l_scr[...] + p_new
        acc = alpha * acc_scr[...] + p_new * vn_ref[0].astype(BF16).astype(F32)
        out_ref[0] = (acc / l_fin) * _silu(az_ref[0])


def _sample_attend(page_table, q8, az8, k_new8, v_new8, scores, s_new, sel, cache_k, cache_v):
    bd, n_pages = page_table.shape
    pg = KV_PAGES_PER_STEP
    bmap = lambda b, g, pt: (b, 0, 0)
    head_tile = pl.BlockSpec((1, ATT_HEADS, HEAD_DIM), bmap)
    return pl.pallas_call(
        functools.partial(_sample_attend_kernel, n_pages=n_pages),
        out_shape=jax.ShapeDtypeStruct((bd, ATT_HEADS, HEAD_DIM), F32),
        grid_spec=pltpu.PrefetchScalarGridSpec(
            num_scalar_prefetch=1,
            grid=(bd, n_pages // pg),
            in_specs=[head_tile, head_tile, head_tile, head_tile,
                      pl.BlockSpec((1, pg, PAGE), lambda b, g, pt: (b, g, 0)),
                      pl.BlockSpec((1, 8, LANES), bmap),
                      pl.BlockSpec((1, 8, LANES), bmap)]
                     + [_page_spec((PAGE * KV_HEADS, HEAD_DIM), pg, j) for j in range(pg)] * 2,
            out_specs=head_tile,
            scratch_shapes=[pltpu.VMEM((ATT_HEADS, 1), F32),
                            pltpu.VMEM((ATT_HEADS, 1), F32),
                            pltpu.VMEM((ATT_HEADS, HEAD_DIM), F32)]),
        compiler_params=pltpu.CompilerParams(dimension_semantics=("parallel", "arbitrary"),
                                             vmem_limit_bytes=VMEM_LIMIT),
        name="sample_attend",
    )(page_table, q8, az8, k_new8, v_new8, scores, s_new, sel, *([cache_k] * pg), *([cache_v] * pg))


def _mlstm_step_kernel(bi_ref, bf_ref, q_ref, k_ref, v_ref, mo_ref, mz_ref, misc_ref, nw_ref,
                       c_ref, n_ref, m_ref, out_ref, c_out, n_out, m_out):
    misc = misc_ref[0]
    eye = lax.broadcasted_iota(I32, (ML_V, ML_V), 0) == lax.broadcasted_iota(I32, (ML_V, ML_V), 1)
    for h in range(ML_HEADS):
        ig = misc[:, MISC_IG + h:MISC_IG + h + 1] + bi_ref[h]
        lf = _log_sigmoid(misc[:, MISC_FG + h:MISC_FG + h + 1] + bf_ref[h])
        m_prev = m_ref[0, h][:, 0:1]
        log_a = lf + m_prev
        m_t = jnp.maximum(log_a, ig)
        d = jnp.exp(ig - m_t)
        a = jnp.exp(log_a - m_t)
        q = q_ref[0][:, h * ML_QK:(h + 1) * ML_QK]
        k = k_ref[0][:, h * ML_QK:(h + 1) * ML_QK]
        v = v_ref[0][:, h * ML_V:(h + 1) * ML_V]
        v_col = jnp.sum(jnp.where(eye, v, 0.0), axis=1, keepdims=True)
        c = c_ref[0, h]
        n = n_ref[0, h]
        s = jnp.sum(q * k, axis=1, keepdims=True) * d
        num = a * jnp.sum(c * q, axis=1, keepdims=True) + s * v_col
        den = a * jnp.sum(n * q, axis=1, keepdims=True) + s
        h_col = num / jnp.maximum(jnp.abs(den), jnp.exp(-m_t))
        c_out[0, h] = a * c + (d * v_col) * k
        n_out[0, h] = a * n + d * k
        m_out[0, h] = jnp.broadcast_to(m_t, (1, LANES))

        h_row = jnp.sum(jnp.where(eye, h_col, 0.0), axis=0, keepdims=True)
        ms = jnp.mean(h_row * h_row, axis=1, keepdims=True)
        hn = h_row * lax.rsqrt(ms + RMS_EPS) * nw_ref[:, h * ML_V:(h + 1) * ML_V]
        gate = _sigmoid(mo_ref[0][:, h * ML_V:(h + 1) * ML_V]) * _silu(mz_ref[0][:, h * ML_V:(h + 1) * ML_V])
        out_ref[0, :, h * ML_V:(h + 1) * ML_V] = (hn * gate).astype(out_ref.dtype)


def _mlstm_step(ps_main, ps_small, b_i, b_f, ml_norm_w, state_c, state_n, state_m):
    bd = ps_main.shape[0]
    col = lambda j: (lambda b: (b, 0, j))
    st4 = lambda b: (b, 0, 0, 0)
    return pl.pallas_call(
        _mlstm_step_kernel,
        out_shape=(jax.ShapeDtypeStruct((bd, 1, ML_WIDTH), BF16),
                   jax.ShapeDtypeStruct(state_c.shape, F32),
                   jax.ShapeDtypeStruct(state_n.shape, F32),
                   jax.ShapeDtypeStruct(state_m.shape, F32)),
        grid=(bd,),
        in_specs=[pl.BlockSpec(memory_space=pltpu.SMEM),
                  pl.BlockSpec(memory_space=pltpu.SMEM),
                  pl.BlockSpec((1, 1, 512), col(12)),
                  pl.BlockSpec((1, 1, 512), col(13)),
                  pl.BlockSpec((1, 1, 1024), col(3)),
                  pl.BlockSpec((1, 1, 1024), col(4)),
                  pl.BlockSpec((1, 1, 1024), col(5)),
                  pl.BlockSpec((1, 1, LANES), col(4)),
                  pl.BlockSpec((1, ML_WIDTH), lambda b: (0, 0)),
                  pl.BlockSpec((1, ML_HEADS, ML_V, ML_QK), st4),
                  pl.BlockSpec((1, ML_HEADS, 1, ML_QK), st4),
                  pl.BlockSpec((1, ML_HEADS, 1, LANES), st4)],
        out_specs=(pl.BlockSpec((1, 1, ML_WIDTH), lambda b: (b, 0, 0)),
                   pl.BlockSpec((1, ML_HEADS, ML_V, ML_QK), st4),
                   pl.BlockSpec((1, ML_HEADS, 1, ML_QK), st4),
                   pl.BlockSpec((1, ML_HEADS, 1, LANES), st4)),
        compiler_params=pltpu.CompilerParams(dimension_semantics=("parallel",),
                                             vmem_limit_bytes=VMEM_LIMIT),
        name="mlstm_step",
    )(b_i, b_f, ps_main, ps_main, ps_main, ps_main, ps_main, ps_small, ml_norm_w.reshape(1, ML_WIDTH),
      state_c, state_n, state_m)


def _split_weights(w_in):
    offs = np.cumsum((0,) + IN_SIZES)
    aq, ak, av, iq, ik, iw, az, mq, mk, mv, mi, mf, mo, mz = (w_in[:, offs[j]:offs[j + 1]] for j in range(len(IN_SIZES)))
    w_main = jnp.concatenate([aq * ATT_SCALE, iq, az, mv, mo, mz, mq, mk * ML_QK ** -0.5], axis=1)
    pad = jnp.zeros((w_in.shape[0], LANES - IDX_DIM - IDX_HEADS - 2 * ML_HEADS), w_in.dtype)
    w_small = jnp.concatenate([ak, av, ik, iw, mi, mf, pad], axis=1)
    assert w_main.shape[1] == MAIN_W and w_small.shape[1] == SMALL_W
    return w_main.astype(BF16), w_small.astype(BF16)


def kernel(x_prompt, x_sample, cache_k, cache_v, cache_idx_k, state_C, state_n, state_m, page_table,
           norm_w, w_in, b_igate, b_fgate, ml_norm_w, w_out, final_norm_w):
    depth = w_in.shape[0]
    batch, seq, d = x_prompt.shape
    bd, dec_seq, _ = x_sample.shape
    assert depth == 1 and dec_seq == 1 and d == D_MODEL
    n_pages = page_table.shape[1]

    w_main, w_small = _split_weights(w_in[0])
    w_o = w_out[0].astype(BF16)
    w_o_att, w_o_ml = w_o[:ATT_WIDTH], w_o[ATT_WIDTH:]

    xp = x_prompt.reshape(batch * seq, d)
    p_main = _project(xp, norm_w[0], w_main, BF16, 1024, 512, "proj_main")
    p_small = _project(xp, norm_w[0], w_small, F32, 1024, SMALL_W, "proj_small")
    att = _dsa_prompt(p_main, p_small, batch, seq)
    ml, ct, m_p = _mlstm_prompt(p_main, p_small, b_igate[0], b_fgate[0], ml_norm_w[0], batch, seq)
    y_prompt = _out_project(xp, att, ml, w_o_att, w_o_ml, final_norm_w, 256, "out_prompt").reshape(batch, seq, d)
    k_prompt = p_small[:, :KV_WIDTH].reshape(1, batch, seq, KV_HEADS, HEAD_DIM)
    v_prompt = p_small[:, KV_WIDTH:2 * KV_WIDTH].reshape(1, batch, seq, KV_HEADS, HEAD_DIM)
    ik_prompt = p_small[:, 2 * KV_WIDTH:2 * KV_WIDTH + IDX_DIM].reshape(1, batch, seq, IDX_DIM)
    c_prompt = jnp.swapaxes(ct[..., :ML_V], -1, -2)[None]
    n_prompt = ct[..., ML_V][None]
    m_prompt = m_p[:, :, 0, 0][None]

    xs = x_sample.reshape(bd, d)
    ps_main = _project(xs, norm_w[0], w_main, F32, bd, 512, "proj_main_s")
    ps_small = _project(xs, norm_w[0], w_small, F32, bd, SMALL_W, "proj_small_s")
    q8 = ps_main[:, :ATT_WIDTH].reshape(bd, ATT_HEADS, HEAD_DIM).astype(BF16)
    qi16 = ps_main[:, 1024:2048].reshape(bd, IDX_HEADS, IDX_DIM).astype(BF16)
    az8 = ps_main[:, 2048:3072].reshape(bd, ATT_HEADS, HEAD_DIM)
    misc_s = ps_small[:, 2 * KV_WIDTH:]
    ki_new = misc_s[:, :IDX_DIM].reshape(bd, 1, IDX_DIM)
    wb = jnp.broadcast_to(misc_s[:, MISC_WI:MISC_WI + IDX_HEADS, None], (bd, IDX_HEADS, LANES))
    k_new8 = jnp.repeat(ps_small[:, :KV_WIDTH].reshape(bd, KV_HEADS, HEAD_DIM), GROUP, axis=1)
    v_new8 = jnp.repeat(ps_small[:, KV_WIDTH:2 * KV_WIDTH].reshape(bd, KV_HEADS, HEAD_DIM), GROUP, axis=1)
    scores, s_new = _sample_scores(page_table, qi16, wb, ki_new, jnp.swapaxes(cache_idx_k[0], 1, 2))
    sel = _sample_select(scores, s_new, min(TOPK_MAX, (n_pages * PAGE + 1) // 4))
    att_s = _sample_attend(page_table, q8, az8, k_new8, v_new8, scores, s_new, sel,
                           cache_k.reshape(-1, HEAD_DIM), cache_v.reshape(-1, HEAD_DIM))
    ml_s, c_s, n_s, m_s = _mlstm_step(
        ps_main.reshape(bd, 1, MAIN_W), ps_small.reshape(bd, 1, SMALL_W), b_igate[0], b_fgate[0], ml_norm_w[0],
        state_C[0], state_n[0].reshape(bd, ML_HEADS, 1, ML_QK),
        jnp.broadcast_to(state_m[0][:, :, None, None], (bd, ML_HEADS, 1, LANES)))
    y_sample = _out_project(xs, att_s.reshape(bd, ATT_WIDTH).astype(BF16), ml_s.reshape(bd, ML_WIDTH),
                            w_o_att, w_o_ml, final_norm_w, bd, "out_sample").reshape(bd, 1, d)
    k_sample = ps_small[:, :KV_WIDTH].reshape(1, bd, 1, KV_HEADS, HEAD_DIM)
    v_sample = ps_small[:, KV_WIDTH:2 * KV_WIDTH].reshape(1, bd, 1, KV_HEADS, HEAD_DIM)
    ik_sample = misc_s[:, :IDX_DIM].reshape(1, bd, 1, IDX_DIM)

    return (y_prompt, y_sample, k_prompt, v_prompt, ik_prompt, c_prompt, n_prompt, m_prompt,
            k_sample, v_sample, ik_sample, c_s[None], n_s.reshape(1, bd, ML_HEADS, ML_QK), m_s[:, :, 0, 0][None])
```

```python
import functools

import jax
import jax.numpy as jnp
import numpy as np
from jax import lax
from jax.experimental import pallas as pl
from jax.experimental.pallas import tpu as pltpu

F32 = jnp.float32
BF16 = jnp.bfloat16
I32 = jnp.int32

D_MODEL = 2048
PAGE = 128
ATT_HEADS = 8
KV_HEADS = 2
HEAD_DIM = 128
GROUP = ATT_HEADS // KV_HEADS
ATT_WIDTH = ATT_HEADS * HEAD_DIM
KV_WIDTH = KV_HEADS * HEAD_DIM
ATT_SCALE = HEAD_DIM ** -0.5
IDX_HEADS = 16
IDX_DIM = 64
IDX_SCALE = (IDX_HEADS * IDX_DIM) ** -0.5
TOPK_MAX = 256
ML_HEADS = 4
ML_QK = 128
ML_V = 256
ML_WIDTH = ML_HEADS * ML_V
RMS_EPS = 1e-6
IN_SIZES = (ATT_WIDTH, KV_WIDTH, KV_WIDTH, IDX_HEADS * IDX_DIM, IDX_DIM, IDX_HEADS, ATT_WIDTH,
            ML_HEADS * ML_QK, ML_HEADS * ML_QK, ML_WIDTH, ML_HEADS, ML_HEADS, ML_WIDTH, ML_WIDTH)

LANES = 128
NEG_BIG = -1e30
VMEM_LIMIT = 56 * 1024 * 1024

MAIN_W = 7168
SMALL_W = 640
MISC_WI = IDX_DIM
MISC_IG = IDX_DIM + IDX_HEADS
MISC_FG = MISC_IG + ML_HEADS


def _dot(a, b):
    return jnp.dot(a, b, preferred_element_type=F32)


def _dot_nt(a, b):
    return lax.dot_general(a, b, (((1,), (1,)), ((), ())), preferred_element_type=F32)


def _tree_reduce(op, parts):
    parts = list(parts)
    while len(parts) > 1:
        paired = [op(parts[j], parts[j + 1]) for j in range(0, len(parts) - 1, 2)]
        parts = paired + parts[len(parts) - len(parts) % 2:]
    return parts[0]


def _fold_rows(op, x):
    return _tree_reduce(op, [x[r:r + 8] for r in range(0, x.shape[0], 8)])


def _proj_kernel(x_ref, nw_ref, w_ref, o_ref, h_scr):
    @pl.when(pl.program_id(1) == 0)
    def _():
        x = x_ref[...]
        ms = jnp.mean(x * x, axis=-1, keepdims=True)
        h_scr[...] = (x * lax.rsqrt(ms + RMS_EPS) * nw_ref[...]).astype(BF16)

    o_ref[...] = _dot(h_scr[...], w_ref[...]).astype(o_ref.dtype)


def _project(x2d, norm_w, w, out_dtype, tm, tn, name):
    m, d = x2d.shape
    n = w.shape[1]
    return pl.pallas_call(
        _proj_kernel,
        out_shape=jax.ShapeDtypeStruct((m, n), out_dtype),
        grid=(m // tm, n // tn),
        in_specs=[pl.BlockSpec((tm, d), lambda i, j: (i, 0)),
                  pl.BlockSpec((1, d), lambda i, j: (0, 0)),
                  pl.BlockSpec((d, tn), lambda i, j: (0, j))],
        out_specs=pl.BlockSpec((tm, tn), lambda i, j: (i, j)),
        scratch_shapes=[pltpu.VMEM((tm, d), BF16)],
        compiler_params=pltpu.CompilerParams(dimension_semantics=("parallel", "arbitrary"),
                                             vmem_limit_bytes=VMEM_LIMIT),
        name=name,
    )(x2d, norm_w.reshape(1, d), w)


def _out_kernel(x_ref, a_ref, m_ref, wa_ref, wm_ref, fw_ref, o_ref):
    y = x_ref[...] + _dot(a_ref[...], wa_ref[...]) + _dot(m_ref[...], wm_ref[...])
    ms = jnp.mean(y * y, axis=-1, keepdims=True)
    o_ref[...] = y * lax.rsqrt(ms + RMS_EPS) * fw_ref[...]


def _out_project(x2d, a, mo, w_att, w_ml, final_w, tm, name):
    m, d = x2d.shape
    return pl.pallas_call(
        _out_kernel,
        out_shape=jax.ShapeDtypeStruct((m, d), F32),
        grid=(m // tm,),
        in_specs=[pl.BlockSpec((tm, d), lambda i: (i, 0)),
                  pl.BlockSpec((tm, ATT_WIDTH), lambda i: (i, 0)),
                  pl.BlockSpec((tm, ML_WIDTH), lambda i: (i, 0)),
                  pl.BlockSpec((ATT_WIDTH, d), lambda i: (0, 0)),
                  pl.BlockSpec((ML_WIDTH, d), lambda i: (0, 0)),
                  pl.BlockSpec((1, d), lambda i: (0, 0))],
        out_specs=pl.BlockSpec((tm, d), lambda i: (i, 0)),
        compiler_params=pltpu.CompilerParams(dimension_semantics=("parallel",),
                                             vmem_limit_bytes=VMEM_LIMIT),
        name=name,
    )(x2d, a, mo, w_att, w_ml, final_w.reshape(1, d))


def _log_sigmoid(x):
    return jnp.minimum(x, 0.0) - jnp.log(1.0 + jnp.exp(-jnp.abs(x)))


def _sigmoid(x):
    return 1.0 / (1.0 + jnp.exp(-x))


def _silu(x):
    return x * _sigmoid(x)


ML_CHUNK = 256
ML_AUG = ML_V + LANES


def _mlstm_kernel(bi_ref, bf_ref, q_ref, k_ref, v_ref, mo_ref, mz_ref, misc_ref, nw_ref,
                  out_ref, ct_ref, m_ref):
    L = ML_CHUNK

    @pl.when(pl.program_id(1) == 0)
    def _():
        ct_ref[...] = jnp.zeros_like(ct_ref)
        m_ref[...] = jnp.zeros_like(m_ref)

    misc = misc_ref[...]
    misc_t = misc.T
    t_idx = lax.broadcasted_iota(I32, (L, L), 0)
    s_idx = lax.broadcasted_iota(I32, (L, L), 1)
    causal = s_idx <= t_idx
    ones_col = jnp.where(lax.broadcasted_iota(I32, (L, LANES), 1) == 0, 1.0, 0.0).astype(BF16)

    for h in range(ML_HEADS):
        ig_row = misc_t[MISC_IG + h:MISC_IG + h + 1, :] + bi_ref[h]
        lf_row = _log_sigmoid(misc_t[MISC_FG + h:MISC_FG + h + 1, :] + bf_ref[h])
        lf_col = _log_sigmoid(misc[:, MISC_FG + h:MISC_FG + h + 1] + bf_ref[h])
        b_col = jnp.sum(jnp.where(causal, lf_row, 0.0), axis=1, keepdims=True)
        b_row = jnp.sum(jnp.where(t_idx <= s_idx, lf_col, 0.0), axis=0, keepdims=True)
        m_prev = m_ref[0, h][0:1, 0:1]
        log_d = jnp.where(causal, b_col - b_row + ig_row, -jnp.inf)
        log_a = b_col + m_prev
        m_t = jnp.maximum(log_a, jnp.max(log_d, axis=1, keepdims=True))
        d = jnp.exp(log_d - m_t)
        a = jnp.exp(log_a - m_t)

        q = q_ref[:, h * ML_QK:(h + 1) * ML_QK]
        k = k_ref[:, h * ML_QK:(h + 1) * ML_QK]
        v_aug = jnp.concatenate([v_ref[:, h * ML_V:(h + 1) * ML_V], ones_col], axis=1)
        s = (_dot_nt(q, k) * d).astype(BF16)
        ct = ct_ref[0, h]
        num_aug = a * _dot(q, ct.astype(BF16)) + _dot(s, v_aug)
        den = num_aug[:, ML_V:ML_V + 1]
        hh = num_aug[:, :ML_V] / jnp.maximum(jnp.abs(den), jnp.exp(-m_t))

        m_new = m_t[L - 1:L, :]
        a_end = a[L - 1:L, :]
        w_row = jnp.exp(b_row[:, L - 1:L] - b_row + ig_row - m_new)
        ktw = (k.astype(F32).T * w_row).astype(BF16)
        ct_new = a_end * ct + _dot(ktw, v_aug)
        ct_ref[0, h] = ct_new
        m_ref[0, h] = jnp.broadcast_to(m_new, (8, LANES))

        ms = jnp.mean(hh * hh, axis=1, keepdims=True)
        hn = hh * lax.rsqrt(ms + RMS_EPS) * nw_ref[:, h * ML_V:(h + 1) * ML_V]
        gate = _sigmoid(mo_ref[:, h * ML_V:(h + 1) * ML_V].astype(F32)) * _silu(mz_ref[:, h * ML_V:(h + 1) * ML_V].astype(F32))
        out_ref[:, h * ML_V:(h + 1) * ML_V] = (hn * gate).astype(out_ref.dtype)


def _mlstm_prompt(p_main, p_small, b_i, b_f, ml_norm_w, batch, seq):
    L = ML_CHUNK
    nc = seq // L
    row = lambda b, c: b * nc + c
    return pl.pallas_call(
        _mlstm_kernel,
        out_shape=(jax.ShapeDtypeStruct((batch * seq, ML_WIDTH), BF16),
                   jax.ShapeDtypeStruct((batch, ML_HEADS, ML_QK, ML_AUG), F32),
                   jax.ShapeDtypeStruct((batch, ML_HEADS, 8, LANES), F32)),
        grid=(batch, nc),
        in_specs=[pl.BlockSpec(memory_space=pltpu.SMEM),
                  pl.BlockSpec(memory_space=pltpu.SMEM),
                  pl.BlockSpec((L, 512), lambda b, c: (row(b, c), 12)),
                  pl.BlockSpec((L, 512), lambda b, c: (row(b, c), 13)),
                  pl.BlockSpec((L, 1024), lambda b, c: (row(b, c), 3)),
                  pl.BlockSpec((L, 1024), lambda b, c: (row(b, c), 4)),
                  pl.BlockSpec((L, 1024), lambda b, c: (row(b, c), 5)),
                  pl.BlockSpec((L, LANES), lambda b, c: (row(b, c), 4)),
                  pl.BlockSpec((1, ML_WIDTH), lambda b, c: (0, 0))],
        out_specs=(pl.BlockSpec((L, ML_WIDTH), lambda b, c: (row(b, c), 0)),
                   pl.BlockSpec((1, ML_HEADS, ML_QK, ML_AUG), lambda b, c: (b, 0, 0, 0)),
                   pl.BlockSpec((1, ML_HEADS, 8, LANES), lambda b, c: (b, 0, 0, 0))),
        compiler_params=pltpu.CompilerParams(dimension_semantics=("parallel", "arbitrary"),
                                             vmem_limit_bytes=VMEM_LIMIT),
        name="mlstm_prompt",
    )(b_i, b_f, p_main, p_main, p_main, p_main, p_main, p_small, ml_norm_w.reshape(1, ML_WIDTH))


INT_MIN = -2 ** 31


def _key_to_float(key):
    bits = jnp.where(key >= 0, key, key ^ jnp.int32(0x7FFFFFFF))
    return lax.bitcast_convert_type(bits, F32)


KEY_NEG_INF = INT_MIN + 0x7FFFFF


def _kth_largest_key(count_ge, shape, k):
    def bit_body(it, prefix):
        cand = prefix + lax.shift_left(jnp.int32(1), 31 - it)
        cand_f = _key_to_float(jnp.maximum(cand, KEY_NEG_INF))
        return jnp.where(count_ge(cand_f) >= float(k), cand, prefix)

    return lax.fori_loop(0, 32, bit_body, jnp.full(shape, INT_MIN, I32))


def _tie_cutoff(count_tie_le, need, shape, index_bits):
    def bit_body(it, lo):
        cand = lo + lax.shift_left(jnp.int32(1), index_bits - 1 - it).astype(F32)
        return jnp.where(count_tie_le(cand) < need, cand, lo)

    return lax.fori_loop(0, index_bits, bit_body, jnp.full(shape, -1.0, F32)) + 1.0


DSA_QB = 128
DSA_TK = 256
DSA_TS = 256


def _dsa_kernel(q_ref, qi_ref, az_ref, miscq_ref, k_ref, v_ref, misck_ref, out_ref,
                k_bf, vt_scr, ki_lo, ki_hi, sc_scr, acc_scr, j_scr, *head_scr, topk):
    i = pl.program_id(1)
    seq = k_ref.shape[0]
    QB, TK, TS = DSA_QB, DSA_TK, DSA_TS
    nt = (i * QB + QB + TK - 1) // TK

    @pl.when(i == 0)
    def _():
        k_bf[...] = k_ref[...].astype(BF16)

        def v_tile(t, carry):
            vt_scr[t] = v_ref[pl.ds(pl.multiple_of(t * TK, TK), TK), :].T.astype(BF16)
            return carry

        lax.fori_loop(0, seq // TK, v_tile, 0)
        lane = lax.broadcasted_iota(I32, (seq, LANES), 1)
        lo = jnp.where(lane < IDX_DIM, misck_ref[...], 0.0)
        ki_lo[...] = lo.astype(BF16)
        ki_hi[...] = pltpu.roll(lo, IDX_DIM, axis=1).astype(BF16)

    k_iota = lax.broadcasted_iota(I32, (TK, QB), 0)
    q_pos = i * QB + lax.broadcasted_iota(I32, (TK, QB), 1)
    w_t = miscq_ref[...].T

    def tile_rows(t):
        return pl.ds(pl.multiple_of(t * TK, TK), TK)

    def score_tile(t, carry):
        rows = pl.ds(pl.multiple_of(t * TS, TS), TS)
        klo = ki_lo[rows, :]
        khi = ki_hi[rows, :]
        acc = jnp.zeros((TS, QB), F32)
        for p in range(IDX_HEADS // 2):
            qp = qi_ref[:, p * LANES:(p + 1) * LANES]
            w0 = w_t[MISC_WI + 2 * p:MISC_WI + 2 * p + 1, :]
            w1 = w_t[MISC_WI + 2 * p + 1:MISC_WI + 2 * p + 2, :]
            acc = acc + w0 * jnp.maximum(_dot_nt(klo, qp), 0.0) + w1 * jnp.maximum(_dot_nt(khi, qp), 0.0)
        k_pos = t * TS + lax.broadcasted_iota(I32, (TS, QB), 0)
        q_pos_s = i * QB + lax.broadcasted_iota(I32, (TS, QB), 1)
        sc_scr[rows, :] = jnp.where(k_pos <= q_pos_s, acc * IDX_SCALE, -jnp.inf)
        return carry

    lax.fori_loop(0, nt * (TK // TS), score_tile, 0)

    def count_ge(cand):
        def body(t, cnt):
            x = sc_scr[tile_rows(t), :]
            return cnt + _fold_rows(jnp.add, jnp.where(x >= cand, 1.0, 0.0))

        cnt = lax.fori_loop(0, nt, body, jnp.zeros((8, QB), F32))
        return jnp.sum(cnt, axis=0, keepdims=True)

    key = _kth_largest_key(count_ge, (1, QB), topk)
    thr = _key_to_float(key)
    thr_next = _key_to_float(key + 1)
    need = float(topk) - count_ge(thr_next)

    j_scr[...] = jnp.full((1, QB), float(seq), F32)

    @pl.when(jnp.max(count_ge(thr)) > float(topk))
    def _():
        def count_tie_le(cut):
            def body(t, cnt):
                x = sc_scr[tile_rows(t), :]
                k_pos = (t * TK + k_iota).astype(F32)
                tie = jnp.where(x >= thr_next, 0.0, jnp.where(x >= thr, jnp.where(k_pos <= cut, 1.0, 0.0), 0.0))
                return cnt + _fold_rows(jnp.add, tie)

            cnt = lax.fori_loop(0, nt, body, jnp.zeros((8, QB), F32))
            return jnp.sum(cnt, axis=0, keepdims=True)

        j_scr[...] = _tie_cutoff(count_tie_le, need, (1, QB), int(np.log2(seq)))

    cut = j_scr[...]

    def bias_tile(t, carry):
        x = sc_scr[tile_rows(t), :]
        k_pos = t * TK + k_iota
        taken = jnp.where(x >= thr_next, 0.0,
                          jnp.where(x >= thr, jnp.where(k_pos.astype(F32) <= cut, 0.0, NEG_BIG), NEG_BIG))
        sc_scr[tile_rows(t), :] = jnp.where(k_pos <= q_pos, taken, NEG_BIG)
        return carry

    lax.fori_loop(0, nt, bias_tile, 0)

    az = az_ref[...].astype(F32)
    lg_scr, p_scr = head_scr[:ATT_HEADS], head_scr[ATT_HEADS:]
    acc_scr[...] = jnp.zeros_like(acc_scr)

    def logits_stage(h, t, slot):
        g = h // GROUP
        kt = k_bf[tile_rows(t), g * HEAD_DIM:(g + 1) * HEAD_DIM]
        lg_scr[h][slot] = _dot_nt(kt, q_ref[:, h * HEAD_DIM:(h + 1) * HEAD_DIM])

    def softmax_stage(h, bias, slot, state):
        m, l8, _ = state
        lg = lg_scr[h][slot] + bias
        m_new = jnp.maximum(m, jnp.max(_fold_rows(jnp.maximum, lg), axis=0, keepdims=True))
        alpha = jnp.exp(m - m_new)
        p = jnp.exp(lg - m_new)
        p_scr[h][slot] = p.astype(BF16)
        return m_new, alpha * l8 + _fold_rows(jnp.add, p), alpha

    def value_stage(h, t, slot, state):
        g = h // GROUP
        vt = vt_scr[t][g * HEAD_DIM:(g + 1) * HEAD_DIM, :]
        acc_scr[h] = state[2] * acc_scr[h] + _dot(vt, p_scr[h][slot])

    def att_tile(t, state):
        slot = lax.rem(t, 2)
        bias = sc_scr[tile_rows(t), :]
        out = []
        for h in range(ATT_HEADS):
            value_stage(h, jnp.maximum(t - 1, 0), 1 - slot, state[h])
            out.append(softmax_stage(h, bias, slot, state[h]))
            logits_stage(h, jnp.minimum(t + 1, nt - 1), 1 - slot)
        return tuple(out)

    init = tuple((jnp.full((1, QB), NEG_BIG, F32), jnp.zeros((8, QB), F32), jnp.ones((1, QB), F32))
                 for _ in range(ATT_HEADS))
    for h in range(ATT_HEADS):
        p_scr[h][1] = jnp.zeros((TK, QB), BF16)
        logits_stage(h, 0, 0)
    fin = lax.fori_loop(0, nt, att_tile, init)
    for h in range(ATT_HEADS):
        value_stage(h, nt - 1, lax.rem(nt - 1, 2), fin[h])
    for h in range(ATT_HEADS):
        o = (acc_scr[h] / jnp.sum(fin[h][1], axis=0, keepdims=True)).T
        c0 = h * HEAD_DIM
        out_ref[:, c0:c0 + HEAD_DIM] = (o * _silu(az[:, c0:c0 + HEAD_DIM])).astype(out_ref.dtype)


def _dsa_prompt(p_main, p_small, batch, seq):
    QB, TK = DSA_QB, DSA_TK
    nb = seq // QB
    topk = min(TOPK_MAX, seq // 4)
    assert TK >= topk and seq % TK == 0 and seq % QB == 0 and TK % DSA_TS == 0
    qrow = lambda b, i: b * nb + i
    return pl.pallas_call(
        functools.partial(_dsa_kernel, topk=topk),
        out_shape=jax.ShapeDtypeStruct((batch * seq, ATT_WIDTH), BF16),
        grid=(batch, nb),
        in_specs=[pl.BlockSpec((QB, ATT_WIDTH), lambda b, i: (qrow(b, i), 0)),
                  pl.BlockSpec((QB, 1024), lambda b, i: (qrow(b, i), 1)),
                  pl.BlockSpec((QB, ATT_WIDTH), lambda b, i: (qrow(b, i), 2)),
                  pl.BlockSpec((QB, LANES), lambda b, i: (qrow(b, i), 4)),
                  pl.BlockSpec((seq, KV_WIDTH), lambda b, i: (b, 0)),
                  pl.BlockSpec((seq, KV_WIDTH), lambda b, i: (b, 1)),
                  pl.BlockSpec((seq, LANES), lambda b, i: (b, 4))],
        out_specs=pl.BlockSpec((QB, ATT_WIDTH), lambda b, i: (qrow(b, i), 0)),
        scratch_shapes=[pltpu.VMEM((seq, KV_WIDTH), BF16),
                        pltpu.VMEM((seq // TK, KV_WIDTH, TK), BF16),
                        pltpu.VMEM((seq, LANES), BF16),
                        pltpu.VMEM((seq, LANES), BF16),
                        pltpu.VMEM((seq, QB), F32),
                        pltpu.VMEM((ATT_HEADS, HEAD_DIM, QB), F32),
                        pltpu.VMEM((1, QB), F32)]
                       + [pltpu.VMEM((2, TK, QB), F32)] * ATT_HEADS
                       + [pltpu.VMEM((2, TK, QB), BF16)] * ATT_HEADS,
        compiler_params=pltpu.CompilerParams(dimension_semantics=("parallel", "arbitrary"),
                                             vmem_limit_bytes=VMEM_LIMIT),
        name="dsa_prompt",
    )(p_main, p_main, p_main, p_small, p_small, p_small, p_small)


IDX_PAGES_PER_STEP = 32
KV_PAGES_PER_STEP = 16


def _page_spec(block, group, j):
    return pl.BlockSpec(block, lambda b, g, pt: (pt[b, g * group + j],) + (0,) * (len(block) - 1))


def _sample_scores_kernel(pt_ref, qi_ref, wb_ref, kin_ref, *refs):
    pages = refs[:IDX_PAGES_PER_STEP]
    sc_ref, snew_ref = refs[IDX_PAGES_PER_STEP:]
    qi = qi_ref[0]
    wb = wb_ref[0]
    rows = []
    for page in pages:
        s = _dot(qi, page[0].astype(BF16))
        rows.append(jnp.sum(jnp.maximum(s, 0.0) * wb, axis=0, keepdims=True) * IDX_SCALE)
    sc_ref[0] = jnp.concatenate(rows, axis=0)

    @pl.when(pl.program_id(1) == 0)
    def _():
        ki_new = kin_ref[0].astype(BF16).astype(F32)
        s = jnp.sum(qi.astype(F32) * ki_new, axis=1, keepdims=True)
        s_new = jnp.sum(jnp.maximum(s, 0.0) * wb[:, 0:1], axis=0, keepdims=True) * IDX_SCALE
        snew_ref[0] = jnp.broadcast_to(s_new, (8, LANES))


def _sample_scores(page_table, qi16, wb, ki_new, cache_ik_t):
    bd, n_pages = page_table.shape
    pg = IDX_PAGES_PER_STEP
    bmap = lambda b, g, pt: (b, 0, 0)
    return pl.pallas_call(
        _sample_scores_kernel,
        out_shape=(jax.ShapeDtypeStruct((bd, n_pages, PAGE), F32),
                   jax.ShapeDtypeStruct((bd, 8, LANES), F32)),
        grid_spec=pltpu.PrefetchScalarGridSpec(
            num_scalar_prefetch=1,
            grid=(bd, n_pages // pg),
            in_specs=[pl.BlockSpec((1, IDX_HEADS, IDX_DIM), bmap),
                      pl.BlockSpec((1, IDX_HEADS, LANES), bmap),
                      pl.BlockSpec((1, 1, IDX_DIM), bmap)]
                     + [_page_spec((1, IDX_DIM, PAGE), pg, j) for j in range(pg)],
            out_specs=(pl.BlockSpec((1, pg, PAGE), lambda b, g, pt: (b, g, 0)),
                       pl.BlockSpec((1, 8, LANES), bmap))),
        compiler_params=pltpu.CompilerParams(dimension_semantics=("parallel", "arbitrary"),
                                             vmem_limit_bytes=VMEM_LIMIT),
        name="sample_scores",
    )(page_table, qi16, wb, ki_new, *([cache_ik_t] * pg))


SEL_THR, SEL_NEXT, SEL_CUT = 0, 1, 2


def _sample_select_kernel(sc_ref, snew_ref, sel_ref, *, topk):
    x = sc_ref[...]
    bd, n_pages, _ = x.shape
    past = n_pages * PAGE
    s_new = snew_ref[:, 0:1, 0:1]

    def total(v):
        return jnp.sum(jnp.sum(v, axis=1, keepdims=True), axis=2, keepdims=True)

    def count_ge(cand):
        return total(jnp.where(x >= cand, 1.0, 0.0)) + jnp.where(s_new >= cand, 1.0, 0.0)

    key = _kth_largest_key(count_ge, (bd, 1, 1), topk)
    thr = _key_to_float(key)
    thr_next = _key_to_float(key + 1)
    need = float(topk) - count_ge(thr_next)
    pos = (lax.broadcasted_iota(I32, (1, n_pages, PAGE), 1) * PAGE
           + lax.broadcasted_iota(I32, (1, n_pages, PAGE), 2)).astype(F32)

    def count_tie_le(cut):
        tie = jnp.where(x >= thr_next, 0.0, jnp.where(x >= thr, jnp.where(pos <= cut, 1.0, 0.0), 0.0))
        tie_new = jnp.where(s_new >= thr_next, 0.0, jnp.where(s_new >= thr, jnp.where(float(past) <= cut, 1.0, 0.0), 0.0))
        return total(tie) + tie_new

    cut = _tie_cutoff(count_tie_le, need, (bd, 1, 1), int(np.log2(past)) + 1)
    row = lax.broadcasted_iota(I32, (bd, 8, LANES), 1)
    sel_ref[...] = jnp.where(row == SEL_THR, thr, jnp.where(row == SEL_NEXT, thr_next, cut))


def _sample_select(scores, s_new, topk):
    bd = scores.shape[0]
    return pl.pallas_call(
        functools.partial(_sample_select_kernel, topk=topk),
        out_shape=jax.ShapeDtypeStruct((bd, 8, LANES), F32),
        compiler_params=pltpu.CompilerParams(vmem_limit_bytes=VMEM_LIMIT),
        name="sample_select",
    )(scores, s_new)


def _taken_bias(score, pos, sel):
    thr, thr_next, cut = sel[SEL_THR:SEL_THR + 1, 0:1], sel[SEL_NEXT:SEL_NEXT + 1, 0:1], sel[SEL_CUT:SEL_CUT + 1, 0:1]
    return jnp.where(score >= thr_next, 0.0,
                     jnp.where(score >= thr, jnp.where(pos <= cut, 0.0, NEG_BIG), NEG_BIG))


def _sample_attend_kernel(pt_ref, q_ref, az_ref, kn_ref, vn_ref, sc_ref, snew_ref, sel_ref, *refs, n_pages):
    pg = KV_PAGES_PER_STEP
    k_pages, v_pages = refs[:pg], refs[pg:2 * pg]
    out_ref, m_scr, l_scr, acc_scr = refs[2 * pg:]
    g = pl.program_id(1)
    n_groups = n_pages // pg

    @pl.when(g == 0)
    def _():
        m_scr[...] = jnp.full_like(m_scr, NEG_BIG)
        l_scr[...] = jnp.zeros_like(l_scr)
        acc_scr[...] = jnp.zeros_like(acc_scr)

    pw = PAGE * KV_HEADS
    q = q_ref[0]
    sel = sel_ref[0]
    pos = ((g * pg + lax.broadcasted_iota(I32, (pg, PAGE), 0)) * PAGE
           + lax.broadcasted_iota(I32, (pg, PAGE), 1)).astype(F32)
    key_bias = _taken_bias(sc_ref[0], pos, sel)
    spread = (lax.broadcasted_iota(I32, (PAGE, pw), 1) // KV_HEADS
              == lax.broadcasted_iota(I32, (PAGE, pw), 0)).astype(BF16)
    col_bias = _dot(key_bias.astype(BF16), spread)
    own_head = (lax.broadcasted_iota(I32, (ATT_HEADS, pw), 1) % KV_HEADS
                == lax.broadcasted_iota(I32, (ATT_HEADS, pw), 0) // GROUP)
    head_bias = jnp.where(own_head, 0.0, NEG_BIG)

    logits = [_dot_nt(q, k_pages[j][...].astype(BF16)) + col_bias[j:j + 1, :] + head_bias for j in range(pg)]
    m_old = m_scr[...]
    m_new = jnp.maximum(m_old, jnp.max(_tree_reduce(jnp.maximum, logits), axis=1, keepdims=True))
    alpha = jnp.exp(m_old - m_new)
    probs = [jnp.exp(lg - m_new) for lg in logits]
    l_add = jnp.sum(_tree_reduce(jnp.add, probs), axis=1, keepdims=True)
    acc_add = _tree_reduce(jnp.add, [_dot(probs[j].astype(BF16), v_pages[j][...].astype(BF16)) for j in range(pg)])
    m_scr[...] = m_new
    l_scr[...] = alpha * l_scr[...] + l_add
    acc_scr[...] = alpha * acc_scr[...] + acc_add

    @pl.when(g == n_groups - 1)
    def _():
        past = float(n_pages * PAGE)
        lg_new = (jnp.sum(q.astype(F32) * kn_ref[0].astype(BF16).astype(F32), axis=1, keepdims=True)
                  + _taken_bias(snew_ref[0][0:1, 0:1], past, sel))
        m_old = m_scr[...]
        m_fin = jnp.maximum(m_old, lg_new)
        alpha = jnp.exp(m_old - m_fin)
        p_new = jnp.exp(lg_new - m_fin)
        l_fin = alpha * l_scr[...] + p_new
        acc = alpha * acc_scr[...] + p_new * vn_ref[0].astype(BF16).astype(F32)
        out_ref[0] = (acc / l_fin) * _silu(az_ref[0])


def _sample_attend(page_table, q8, az8, k_new8, v_new8, scores, s_new, sel, cache_k, cache_v):
    bd, n_pages = page_table.shape
    pg = KV_PAGES_PER_STEP
    bmap = lambda b, g, pt: (b, 0, 0)
    head_tile = pl.BlockSpec((1, ATT_HEADS, HEAD_DIM), bmap)
    return pl.pallas_call(
        functools.partial(_sample_attend_kernel, n_pages=n_pages),
        out_shape=jax.ShapeDtypeStruct((bd, ATT_HEADS, HEAD_DIM), F32),
        grid_spec=pltpu.PrefetchScalarGridSpec(
            num_scalar_prefetch=1,
            grid=(bd, n_pages // pg),
            in_specs=[head_tile, head_tile, head_tile, head_tile,
                      pl.BlockSpec((1, pg, PAGE), lambda b, g, pt: (b, g, 0)),
                      pl.BlockSpec((1, 8, LANES), bmap),
                      pl.BlockSpec((1, 8, LANES), bmap)]
                     + [_page_spec((PAGE * KV_HEADS, HEAD_DIM), pg, j) for j in range(pg)] * 2,
            out_specs=head_tile,
            scratch_shapes=[pltpu.VMEM((ATT_HEADS, 1), F32),
                            pltpu.VMEM((ATT_HEADS, 1), F32),
                            pltpu.VMEM((ATT_HEADS, HEAD_DIM), F32)]),
        compiler_params=pltpu.CompilerParams(dimension_semantics=("parallel", "arbitrary"),
                                             vmem_limit_bytes=VMEM_LIMIT),
        name="sample_attend",
    )(page_table, q8, az8, k_new8, v_new8, scores, s_new, sel, *([cache_k] * pg), *([cache_v] * pg))


def _mlstm_step_kernel(bi_ref, bf_ref, q_ref, k_ref, v_ref, mo_ref, mz_ref, misc_ref, nw_ref,
                       c_ref, n_ref, m_ref, out_ref, c_out, n_out, m_out):
    misc = misc_ref[0]
    eye = lax.broadcasted_iota(I32, (ML_V, ML_V), 0) == lax.broadcasted_iota(I32, (ML_V, ML_V), 1)
    for h in range(ML_HEADS):
        ig = misc[:, MISC_IG + h:MISC_IG + h + 1] + bi_ref[h]
        lf = _log_sigmoid(misc[:, MISC_FG + h:MISC_FG + h + 1] + bf_ref[h])
        m_prev = m_ref[0, h][:, 0:1]
        log_a = lf + m_prev
        m_t = jnp.maximum(log_a, ig)
        d = jnp.exp(ig - m_t)
        a = jnp.exp(log_a - m_t)
        q = q_ref[0][:, h * ML_QK:(h + 1) * ML_QK]
        k = k_ref[0][:, h * ML_QK:(h + 1) * ML_QK]
        v = v_ref[0][:, h * ML_V:(h + 1) * ML_V]
        v_col = jnp.sum(jnp.where(eye, v, 0.0), axis=1, keepdims=True)
        c = c_ref[0, h]
        n = n_ref[0, h]
        s = jnp.sum(q * k, axis=1, keepdims=True) * d
        num = a * jnp.sum(c * q, axis=1, keepdims=True) + s * v_col
        den = a * jnp.sum(n * q, axis=1, keepdims=True) + s
        h_col = num / jnp.maximum(jnp.abs(den), jnp.exp(-m_t))
        c_out[0, h] = a * c + (d * v_col) * k
        n_out[0, h] = a * n + d * k
        m_out[0, h] = jnp.broadcast_to(m_t, (1, LANES))

        h_row = jnp.sum(jnp.where(eye, h_col, 0.0), axis=0, keepdims=True)
        ms = jnp.mean(h_row * h_row, axis=1, keepdims=True)
        hn = h_row * lax.rsqrt(ms + RMS_EPS) * nw_ref[:, h * ML_V:(h + 1) * ML_V]
        gate = _sigmoid(mo_ref[0][:, h * ML_V:(h + 1) * ML_V]) * _silu(mz_ref[0][:, h * ML_V:(h + 1) * ML_V])
        out_ref[0, :, h * ML_V:(h + 1) * ML_V] = (hn * gate).astype(out_ref.dtype)


def _mlstm_step(ps_main, ps_small, b_i, b_f, ml_norm_w, state_c, state_n, state_m):
    bd = ps_main.shape[0]
    col = lambda j: (lambda b: (b, 0, j))
    st4 = lambda b: (b, 0, 0, 0)
    return pl.pallas_call(
        _mlstm_step_kernel,
        out_shape=(jax.ShapeDtypeStruct((bd, 1, ML_WIDTH), BF16),
                   jax.ShapeDtypeStruct(state_c.shape, F32),
                   jax.ShapeDtypeStruct(state_n.shape, F32),
                   jax.ShapeDtypeStruct(state_m.shape, F32)),
        grid=(bd,),
        in_specs=[pl.BlockSpec(memory_space=pltpu.SMEM),
                  pl.BlockSpec(memory_space=pltpu.SMEM),
                  pl.BlockSpec((1, 1, 512), col(12)),
                  pl.BlockSpec((1, 1, 512), col(13)),
                  pl.BlockSpec((1, 1, 1024), col(3)),
                  pl.BlockSpec((1, 1, 1024), col(4)),
                  pl.BlockSpec((1, 1, 1024), col(5)),
                  pl.BlockSpec((1, 1, LANES), col(4)),
                  pl.BlockSpec((1, ML_WIDTH), lambda b: (0, 0)),
                  pl.BlockSpec((1, ML_HEADS, ML_V, ML_QK), st4),
                  pl.BlockSpec((1, ML_HEADS, 1, ML_QK), st4),
                  pl.BlockSpec((1, ML_HEADS, 1, LANES), st4)],
        out_specs=(pl.BlockSpec((1, 1, ML_WIDTH), lambda b: (b, 0, 0)),
                   pl.BlockSpec((1, ML_HEADS, ML_V, ML_QK), st4),
                   pl.BlockSpec((1, ML_HEADS, 1, ML_QK), st4),
                   pl.BlockSpec((1, ML_HEADS, 1, LANES), st4)),
        compiler_params=pltpu.CompilerParams(dimension_semantics=("parallel",),
                                             vmem_limit_bytes=VMEM_LIMIT),
        name="mlstm_step",
    )(b_i, b_f, ps_main, ps_main, ps_main, ps_main, ps_main, ps_small, ml_norm_w.reshape(1, ML_WIDTH),
      state_c, state_n, state_m)


def _split_weights(w_in):
    offs = np.cumsum((0,) + IN_SIZES)
    aq, ak, av, iq, ik, iw, az, mq, mk, mv, mi, mf, mo, mz = (w_in[:, offs[j]:offs[j + 1]] for j in range(len(IN_SIZES)))
    w_main = jnp.concatenate([aq * ATT_SCALE, iq, az, mv, mo, mz, mq, mk * ML_QK ** -0.5], axis=1)
    pad = jnp.zeros((w_in.shape[0], LANES - IDX_DIM - IDX_HEADS - 2 * ML_HEADS), w_in.dtype)
    w_small = jnp.concatenate([ak, av, ik, iw, mi, mf, pad], axis=1)
    assert w_main.shape[1] == MAIN_W and w_small.shape[1] == SMALL_W
    return w_main.astype(BF16), w_small.astype(BF16)


def kernel(x_prompt, x_sample, cache_k, cache_v, cache_idx_k, state_C, state_n, state_m, page_table,
           norm_w, w_in, b_igate, b_fgate, ml_norm_w, w_out, final_norm_w):
    depth = w_in.shape[0]
    batch, seq, d = x_prompt.shape
    bd, dec_seq, _ = x_sample.shape
    assert depth == 1 and dec_seq == 1 and d == D_MODEL
    n_pages = page_table.shape[1]

    w_main, w_small = _split_weights(w_in[0])
    w_o = w_out[0].astype(BF16)
    w_o_att, w_o_ml = w_o[:ATT_WIDTH], w_o[ATT_WIDTH:]

    xp = x_prompt.reshape(batch * seq, d)
    p_main = _project(xp, norm_w[0], w_main, BF16, 1024, 512, "proj_main")
    p_small = _project(xp, norm_w[0], w_small, F32, 1024, SMALL_W, "proj_small")
    att = _dsa_prompt(p_main, p_small, batch, seq)
    ml, ct, m_p = _mlstm_prompt(p_main, p_small, b_igate[0], b_fgate[0], ml_norm_w[0], batch, seq)
    y_prompt = _out_project(xp, att, ml, w_o_att, w_o_ml, final_norm_w, 256, "out_prompt").reshape(batch, seq, d)
    k_prompt = p_small[:, :KV_WIDTH].reshape(1, batch, seq, KV_HEADS, HEAD_DIM)
    v_prompt = p_small[:, KV_WIDTH:2 * KV_WIDTH].reshape(1, batch, seq, KV_HEADS, HEAD_DIM)
    ik_prompt = p_small[:, 2 * KV_WIDTH:2 * KV_WIDTH + IDX_DIM].reshape(1, batch, seq, IDX_DIM)
    c_prompt = jnp.swapaxes(ct[..., :ML_V], -1, -2)[None]
    n_prompt = ct[..., ML_V][None]
    m_prompt = m_p[:, :, 0, 0][None]

    xs = x_sample.reshape(bd, d)
    ps_main = _project(xs, norm_w[0], w_main, F32, bd, 512, "proj_main_s")
    ps_small = _project(xs, norm_w[0], w_small, F32, bd, SMALL_W, "proj_small_s")
    q8 = ps_main[:, :ATT_WIDTH].reshape(bd, ATT_HEADS, HEAD_DIM).astype(BF16)
    qi16 = ps_main[:, 1024:2048].reshape(bd, IDX_HEADS, IDX_DIM).astype(BF16)
    az8 = ps_main[:, 2048:3072].reshape(bd, ATT_HEADS, HEAD_DIM)
    misc_s = ps_small[:, 2 * KV_WIDTH:]
    ki_new = misc_s[:, :IDX_DIM].reshape(bd, 1, IDX_DIM)
    wb = jnp.broadcast_to(misc_s[:, MISC_WI:MISC_WI + IDX_HEADS, None], (bd, IDX_HEADS, LANES))
    k_new8 = jnp.repeat(ps_small[:, :KV_WIDTH].reshape(bd, KV_HEADS, HEAD_DIM), GROUP, axis=1)
    v_new8 = jnp.repeat(ps_small[:, KV_WIDTH:2 * KV_WIDTH].reshape(bd, KV_HEADS, HEAD_DIM), GROUP, axis=1)
    scores, s_new = _sample_scores(page_table, qi16, wb, ki_new, jnp.swapaxes(cache_idx_k[0], 1, 2))
    sel = _sample_select(scores, s_new, min(TOPK_MAX, (n_pages * PAGE + 1) // 4))
    att_s = _sample_attend(page_table, q8, az8, k_new8, v_new8, scores, s_new, sel,
                           cache_k.reshape(-1, HEAD_DIM), cache_v.reshape(-1, HEAD_DIM))
    ml_s, c_s, n_s, m_s = _mlstm_step(
        ps_main.reshape(bd, 1, MAIN_W), ps_small.reshape(bd, 1, SMALL_W), b_igate[0], b_fgate[0], ml_norm_w[0],
        state_C[0], state_n[0].reshape(bd, ML_HEADS, 1, ML_QK),
        jnp.broadcast_to(state_m[0][:, :, None, None], (bd, ML_HEADS, 1, LANES)))
    y_sample = _out_project(xs, att_s.reshape(bd, ATT_WIDTH).astype(BF16), ml_s.reshape(bd, ML_WIDTH),
                            w_o_att, w_o_ml, final_norm_w, bd, "out_sample").reshape(bd, 1, d)
    k_sample = ps_small[:, :KV_WIDTH].reshape(1, bd, 1, KV_HEADS, HEAD_DIM)
    v_sample = ps_small[:, KV_WIDTH:2 * KV_WIDTH].reshape(1, bd, 1, KV_HEADS, HEAD_DIM)
    ik_sample = misc_s[:, :IDX_DIM].reshape(1, bd, 1, IDX_DIM)

    return (y_prompt, y_sample, k_prompt, v_prompt, ik_prompt, c_prompt, n_prompt, m_prompt,
            k_sample, v_sample, ik_sample, c_s[None], n_s.reshape(1, bd, ML_HEADS, ML_QK), m_s[:, :, 0, 0][None])
```

```python
import functools

import jax
import jax.numpy as jnp
import numpy as np
from jax import lax
from jax.experimental import pallas as pl
from jax.experimental.pallas import tpu as pltpu
from jax.experimental.pallas import tpu_sc as plsc

F32 = jnp.float32
BF16 = jnp.bfloat16
I32 = jnp.int32
I16 = jnp.int16

D_MODEL = 2048
PAGE = 128
ATT_HEADS = 8
KV_HEADS = 2
HEAD_DIM = 128
GROUP = ATT_HEADS // KV_HEADS
ATT_WIDTH = ATT_HEADS * HEAD_DIM
KV_WIDTH = KV_HEADS * HEAD_DIM
ATT_SCALE = HEAD_DIM ** -0.5
IDX_HEADS = 16
IDX_DIM = 64
IDX_SCALE = (IDX_HEADS * IDX_DIM) ** -0.5
TOPK_MAX = 256
ML_HEADS = 4
ML_QK = 128
ML_V = 256
ML_WIDTH = ML_HEADS * ML_V
RMS_EPS = 1e-6
IN_SIZES = (ATT_WIDTH, KV_WIDTH, KV_WIDTH, IDX_HEADS * IDX_DIM, IDX_DIM, IDX_HEADS, ATT_WIDTH,
            ML_HEADS * ML_QK, ML_HEADS * ML_QK, ML_WIDTH, ML_HEADS, ML_HEADS, ML_WIDTH, ML_WIDTH)

LANES = 128
PACK16 = 16
NEG_BIG = -1e30
VMEM_LIMIT = 56 * 1024 * 1024

MAIN_W = 7168
SMALL_W = 640
MISC_WI = IDX_DIM
MISC_IG = IDX_DIM + IDX_HEADS
MISC_FG = MISC_IG + ML_HEADS


def _dot(a, b):
    return jnp.dot(a, b, preferred_element_type=F32)


def _dot_nt(a, b):
    return lax.dot_general(a, b, (((1,), (1,)), ((), ())), preferred_element_type=F32)


def _tree_reduce(op, parts):
    parts = list(parts)
    while len(parts) > 1:
        paired = [op(parts[j], parts[j + 1]) for j in range(0, len(parts) - 1, 2)]
        parts = paired + parts[len(parts) - len(parts) % 2:]
    return parts[0]


def _fold_rows(op, x):
    return _tree_reduce(op, [x[r:r + 8] for r in range(0, x.shape[0], 8)])


def _proj_kernel(x_ref, nw_ref, w_ref, o_ref, h_scr):
    @pl.when(pl.program_id(1) == 0)
    def _():
        x = x_ref[...]
        ms = jnp.mean(x * x, axis=-1, keepdims=True)
        h_scr[...] = (x * lax.rsqrt(ms + RMS_EPS) * nw_ref[...]).astype(BF16)

    o_ref[...] = _dot(h_scr[...], w_ref[...]).astype(o_ref.dtype)


def _project(x2d, norm_w, w, out_dtype, tm, tn, name):
    m, d = x2d.shape
    n = w.shape[1]
    return pl.pallas_call(
        _proj_kernel,
        out_shape=jax.ShapeDtypeStruct((m, n), out_dtype),
        grid=(m // tm, n // tn),
        in_specs=[pl.BlockSpec((tm, d), lambda i, j: (i, 0)),
                  pl.BlockSpec((1, d), lambda i, j: (0, 0)),
                  pl.BlockSpec((d, tn), lambda i, j: (0, j))],
        out_specs=pl.BlockSpec((tm, tn), lambda i, j: (i, j)),
        scratch_shapes=[pltpu.VMEM((tm, d), BF16)],
        compiler_params=pltpu.CompilerParams(dimension_semantics=("parallel", "arbitrary"),
                                             vmem_limit_bytes=VMEM_LIMIT),
        name=name,
    )(x2d, norm_w.reshape(1, d), w)


def _out_kernel(x_ref, a_ref, m_ref, wa_ref, wm_ref, fw_ref, o_ref):
    y = x_ref[...] + _dot(a_ref[...], wa_ref[...]) + _dot(m_ref[...], wm_ref[...])
    ms = jnp.mean(y * y, axis=-1, keepdims=True)
    o_ref[...] = y * lax.rsqrt(ms + RMS_EPS) * fw_ref[...]


def _out_project(x2d, a, mo, w_att, w_ml, final_w, tm, name):
    m, d = x2d.shape
    return pl.pallas_call(
        _out_kernel,
        out_shape=jax.ShapeDtypeStruct((m, d), F32),
        grid=(m // tm,),
        in_specs=[pl.BlockSpec((tm, d), lambda i: (i, 0)),
                  pl.BlockSpec((tm, ATT_WIDTH), lambda i: (i, 0)),
                  pl.BlockSpec((tm, ML_WIDTH), lambda i: (i, 0)),
                  pl.BlockSpec((ATT_WIDTH, d), lambda i: (0, 0)),
                  pl.BlockSpec((ML_WIDTH, d), lambda i: (0, 0)),
                  pl.BlockSpec((1, d), lambda i: (0, 0))],
        out_specs=pl.BlockSpec((tm, d), lambda i: (i, 0)),
        compiler_params=pltpu.CompilerParams(dimension_semantics=("parallel",),
                                             vmem_limit_bytes=VMEM_LIMIT),
        name=name,
    )(x2d, a, mo, w_att, w_ml, final_w.reshape(1, d))


def _log_sigmoid(x):
    return jnp.minimum(x, 0.0) - jnp.log(1.0 + jnp.exp(-jnp.abs(x)))


def _sigmoid(x):
    return 1.0 / (1.0 + jnp.exp(-x))


def _silu(x):
    return x * _sigmoid(x)


ML_CHUNK = 256
ML_AUG = ML_V + LANES


def _mlstm_kernel(bi_ref, bf_ref, q_ref, k_ref, v_ref, mo_ref, mz_ref, misc_ref, nw_ref,
                  out_ref, ct_ref, m_ref):
    L = ML_CHUNK

    @pl.when(pl.program_id(1) == 0)
    def _():
        ct_ref[...] = jnp.zeros_like(ct_ref)
        m_ref[...] = jnp.zeros_like(m_ref)

    misc = misc_ref[...]
    misc_t = misc.T
    t_idx = lax.broadcasted_iota(I32, (L, L), 0)
    s_idx = lax.broadcasted_iota(I32, (L, L), 1)
    causal = s_idx <= t_idx
    ones_col = jnp.where(lax.broadcasted_iota(I32, (L, LANES), 1) == 0, 1.0, 0.0).astype(BF16)

    for h in range(ML_HEADS):
        ig_row = misc_t[MISC_IG + h:MISC_IG + h + 1, :] + bi_ref[h]
        lf_row = _log_sigmoid(misc_t[MISC_FG + h:MISC_FG + h + 1, :] + bf_ref[h])
        lf_col = _log_sigmoid(misc[:, MISC_FG + h:MISC_FG + h + 1] + bf_ref[h])
        b_col = jnp.sum(jnp.where(causal, lf_row, 0.0), axis=1, keepdims=True)
        b_row = jnp.sum(jnp.where(t_idx <= s_idx, lf_col, 0.0), axis=0, keepdims=True)
        m_prev = m_ref[0, h][0:1, 0:1]
        log_d = jnp.where(causal, b_col - b_row + ig_row, -jnp.inf)
        log_a = b_col + m_prev
        m_t = jnp.maximum(log_a, jnp.max(log_d, axis=1, keepdims=True))
        d = jnp.exp(log_d - m_t)
        a = jnp.exp(log_a - m_t)

        q = q_ref[:, h * ML_QK:(h + 1) * ML_QK]
        k = k_ref[:, h * ML_QK:(h + 1) * ML_QK]
        v_aug = jnp.concatenate([v_ref[:, h * ML_V:(h + 1) * ML_V], ones_col], axis=1)
        s = (_dot_nt(q, k) * d).astype(BF16)
        ct = ct_ref[0, h]
        num_aug = a * _dot(q, ct.astype(BF16)) + _dot(s, v_aug)
        den = num_aug[:, ML_V:ML_V + 1]
        hh = num_aug[:, :ML_V] / jnp.maximum(jnp.abs(den), jnp.exp(-m_t))

        m_new = m_t[L - 1:L, :]
        a_end = a[L - 1:L, :]
        w_row = jnp.exp(b_row[:, L - 1:L] - b_row + ig_row - m_new)
        ktw = (k.astype(F32).T * w_row).astype(BF16)
        ct_new = a_end * ct + _dot(ktw, v_aug)
        ct_ref[0, h] = ct_new
        m_ref[0, h] = jnp.broadcast_to(m_new, (8, LANES))

        ms = jnp.mean(hh * hh, axis=1, keepdims=True)
        hn = hh * lax.rsqrt(ms + RMS_EPS) * nw_ref[:, h * ML_V:(h + 1) * ML_V]
        gate = _sigmoid(mo_ref[:, h * ML_V:(h + 1) * ML_V].astype(F32)) * _silu(mz_ref[:, h * ML_V:(h + 1) * ML_V].astype(F32))
        out_ref[:, h * ML_V:(h + 1) * ML_V] = (hn * gate).astype(out_ref.dtype)


def _mlstm_prompt(p_main, p_small, b_i, b_f, ml_norm_w, batch, seq):
    L = ML_CHUNK
    nc = seq // L
    row = lambda b, c: b * nc + c
    return pl.pallas_call(
        _mlstm_kernel,
        out_shape=(jax.ShapeDtypeStruct((batch * seq, ML_WIDTH), BF16),
                   jax.ShapeDtypeStruct((batch, ML_HEADS, ML_QK, ML_AUG), F32),
                   jax.ShapeDtypeStruct((batch, ML_HEADS, 8, LANES), F32)),
        grid=(batch, nc),
        in_specs=[pl.BlockSpec(memory_space=pltpu.SMEM),
                  pl.BlockSpec(memory_space=pltpu.SMEM),
                  pl.BlockSpec((L, 512), lambda b, c: (row(b, c), 12)),
                  pl.BlockSpec((L, 512), lambda b, c: (row(b, c), 13)),
                  pl.BlockSpec((L, 1024), lambda b, c: (row(b, c), 3)),
                  pl.BlockSpec((L, 1024), lambda b, c: (row(b, c), 4)),
                  pl.BlockSpec((L, 1024), lambda b, c: (row(b, c), 5)),
                  pl.BlockSpec((L, LANES), lambda b, c: (row(b, c), 4)),
                  pl.BlockSpec((1, ML_WIDTH), lambda b, c: (0, 0))],
        out_specs=(pl.BlockSpec((L, ML_WIDTH), lambda b, c: (row(b, c), 0)),
                   pl.BlockSpec((1, ML_HEADS, ML_QK, ML_AUG), lambda b, c: (b, 0, 0, 0)),
                   pl.BlockSpec((1, ML_HEADS, 8, LANES), lambda b, c: (b, 0, 0, 0))),
        compiler_params=pltpu.CompilerParams(dimension_semantics=("parallel", "arbitrary"),
                                             vmem_limit_bytes=VMEM_LIMIT),
        name="mlstm_prompt",
    )(b_i, b_f, p_main, p_main, p_main, p_main, p_main, p_small, ml_norm_w.reshape(1, ML_WIDTH))


INT_MIN = -2 ** 31


def _float_to_key(x):
    bits = lax.bitcast_convert_type(x, I32)
    return jnp.where(bits >= 0, bits, bits ^ jnp.int32(0x7FFFFFFF))


def _key_to_float(key):
    bits = jnp.where(key >= 0, key, key ^ jnp.int32(0x7FFFFFFF))
    return lax.bitcast_convert_type(bits, F32)


KEY_NEG_INF = INT_MIN + 0x7FFFFF


def _kth_largest_key(count_ge, shape, k):
    def bit_body(it, prefix):
        cand = prefix + lax.shift_left(jnp.int32(1), 31 - it)
        cand_f = _key_to_float(jnp.maximum(cand, KEY_NEG_INF))
        return jnp.where(count_ge(cand_f) >= float(k), cand, prefix)

    return lax.fori_loop(0, 32, bit_body, jnp.full(shape, INT_MIN, I32))


def _tie_cutoff(count_tie_le, need, shape, index_bits):
    def bit_body(it, lo):
        cand = lo + lax.shift_left(jnp.int32(1), index_bits - 1 - it).astype(F32)
        return jnp.where(count_tie_le(cand) < need, cand, lo)

    return lax.fori_loop(0, index_bits, bit_body, jnp.full(shape, -1.0, F32)) + 1.0


DSA_QB = 128
DSA_TK = 256
DSA_TS = 256


def _dsa_kernel(q_ref, qi_ref, az_ref, miscq_ref, k_ref, v_ref, misck_ref, out_ref,
                k_bf, vt_scr, ki_lo, ki_hi, sc_scr, digit_scr, acc_scr, j_scr, *head_scr, topk):
    i = pl.program_id(1)
    seq = k_ref.shape[0]
    QB, TK, TS = DSA_QB, DSA_TK, DSA_TS
    nt = (i * QB + QB + TK - 1) // TK

    @pl.when(i == 0)
    def _():
        k_bf[...] = k_ref[...].astype(BF16)

        def v_tile(t, carry):
            vt_scr[t] = v_ref[pl.ds(pl.multiple_of(t * TK, TK), TK), :].T.astype(BF16)
            return carry

        lax.fori_loop(0, seq // TK, v_tile, 0)
        lane = lax.broadcasted_iota(I32, (seq, LANES), 1)
        lo = jnp.where(lane < IDX_DIM, misck_ref[...], 0.0)
        ki_lo[...] = lo.astype(BF16)
        ki_hi[...] = pltpu.roll(lo, IDX_DIM, axis=1).astype(BF16)

    k_iota = lax.broadcasted_iota(I32, (TK, QB), 0)
    q_pos = i * QB + lax.broadcasted_iota(I32, (TK, QB), 1)
    w_t = miscq_ref[...].T

    def tile_rows(t):
        return pl.ds(pl.multiple_of(t * TK, TK), TK)

    def score_tile(t, carry):
        rows = pl.ds(pl.multiple_of(t * TS, TS), TS)
        klo = ki_lo[rows, :]
        khi = ki_hi[rows, :]
        acc = jnp.zeros((TS, QB), F32)
        for p in range(IDX_HEADS // 2):
            qp = qi_ref[:, p * LANES:(p + 1) * LANES]
            w0 = w_t[MISC_WI + 2 * p:MISC_WI + 2 * p + 1, :]
            w1 = w_t[MISC_WI + 2 * p + 1:MISC_WI + 2 * p + 2, :]
            acc = acc + w0 * jnp.maximum(_dot_nt(klo, qp), 0.0) + w1 * jnp.maximum(_dot_nt(khi, qp), 0.0)
        k_pos = t * TS + lax.broadcasted_iota(I32, (TS, QB), 0)
        q_pos_s = i * QB + lax.broadcasted_iota(I32, (TS, QB), 1)
        score = jnp.where(k_pos <= q_pos_s, acc * IDX_SCALE + 0.0, -jnp.inf)
        key = _float_to_key(score)
        sc_scr[rows, :] = key
        digit_scr[rows, :] = lax.shift_right_arithmetic(key, 16).astype(I16)
        return carry

    lax.fori_loop(0, nt * (TK // TS), score_tile, 0)

    CT = 2 * TK

    def pad_digits():
        @pl.when(jnp.logical_and(nt % 2 == 1, nt * TK < seq))
        def _():
            digit_scr[tile_rows(nt), :] = jnp.full((TK, QB), -2 ** 15, I16)

    def count_digit_ge(cand):
        cand_b = jnp.broadcast_to(cand, (PACK16, QB)).astype(I16)

        def body(t, cnt):
            d = digit_scr[pl.ds(pl.multiple_of(t * CT, CT), CT), :]
            ones = [jnp.where(d[r:r + PACK16] >= cand_b, jnp.int16(1), jnp.int16(0)) for r in range(0, CT, PACK16)]
            return cnt + _tree_reduce(jnp.add, ones)

        cnt = lax.fori_loop(0, (nt + 1) // 2, body, jnp.zeros((PACK16, QB), I16))
        return jnp.sum(cnt.astype(I32), axis=0, keepdims=True)

    def digit_search(count0):
        def bit_body(it, carry):
            prefix, cnt_at = carry
            cand = prefix + lax.shift_left(jnp.int32(1), 15 - it)
            cnt = count_digit_ge(cand)
            ok = cnt >= topk
            return jnp.where(ok, cand, prefix), jnp.where(ok, cnt, cnt_at)

        return lax.fori_loop(0, 16, bit_body, (jnp.full((1, QB), -2 ** 15, I32), count0))

    pad_digits()
    hi, cnt_hi = digit_search(jnp.full((1, QB), TK, I32) * nt)

    def low_digit_tile(t, carry):
        key = sc_scr[tile_rows(t), :]
        key_hi = lax.shift_right_arithmetic(key, 16)
        low = (key & 0xFFFF) - 2 ** 15
        digit_scr[tile_rows(t), :] = jnp.where(key_hi > hi, 2 ** 15 - 1,
                                               jnp.where(key_hi < hi, -2 ** 15, low)).astype(I16)
        return carry

    lax.fori_loop(0, nt, low_digit_tile, 0)
    pad_digits()
    lo, cnt_thr = digit_search(cnt_hi)
    thr = hi * 2 ** 16 + (lo + 2 ** 15)

    j_scr[...] = jnp.full((1, QB), float(seq), F32)

    @pl.when(jnp.max(cnt_thr) > topk)
    def _():
        def count_keys(indicator):
            def body(t, cnt):
                k_pos = (t * TK + k_iota).astype(F32)
                return cnt + _fold_rows(jnp.add, indicator(sc_scr[tile_rows(t), :], k_pos))

            return jnp.sum(lax.fori_loop(0, nt, body, jnp.zeros((8, QB), F32)), axis=0, keepdims=True)

        need = float(topk) - count_keys(lambda key, k_pos: jnp.where(key > thr, 1.0, 0.0))
        count_tie_le = lambda cut: count_keys(
            lambda key, k_pos: jnp.where(key == thr, jnp.where(k_pos <= cut, 1.0, 0.0), 0.0))
        j_scr[...] = _tie_cutoff(count_tie_le, need, (1, QB), int(np.log2(seq)))

    cut = j_scr[...]

    def bias_tile(t, carry):
        key = sc_scr[tile_rows(t), :]
        k_pos = t * TK + k_iota
        taken = jnp.where(key > thr, 0.0,
                          jnp.where(key == thr, jnp.where(k_pos.astype(F32) <= cut, 0.0, NEG_BIG), NEG_BIG))
        sc_scr[tile_rows(t), :] = lax.bitcast_convert_type(jnp.where(k_pos <= q_pos, taken, NEG_BIG), I32)
        return carry

    lax.fori_loop(0, nt, bias_tile, 0)

    az = az_ref[...].astype(F32)
    lg_scr, p_scr = head_scr[:ATT_HEADS], head_scr[ATT_HEADS:]
    acc_scr[...] = jnp.zeros_like(acc_scr)

    def logits_stage(h, t, slot):
        g = h // GROUP
        kt = k_bf[tile_rows(t), g * HEAD_DIM:(g + 1) * HEAD_DIM]
        lg_scr[h][slot] = _dot_nt(kt, q_ref[:, h * HEAD_DIM:(h + 1) * HEAD_DIM])

    def softmax_stage(h, bias, slot, state):
        m, l8, _ = state
        lg = lg_scr[h][slot] + bias
        m_new = jnp.maximum(m, jnp.max(_fold_rows(jnp.maximum, lg), axis=0, keepdims=True))
        alpha = jnp.exp(m - m_new)
        p = jnp.exp(lg - m_new)
        p_scr[h][slot] = p.astype(BF16)
        return m_new, alpha * l8 + _fold_rows(jnp.add, p), alpha

    def value_stage(h, t, slot, state):
        g = h // GROUP
        vt = vt_scr[t][g * HEAD_DIM:(g + 1) * HEAD_DIM, :]
        acc_scr[h] = state[2] * acc_scr[h] + _dot(vt, p_scr[h][slot])

    def att_tile(t, state):
        slot = lax.rem(t, 2)
        bias = lax.bitcast_convert_type(sc_scr[tile_rows(t), :], F32)
        out = []
        for h in range(ATT_HEADS):
            value_stage(h, jnp.maximum(t - 1, 0), 1 - slot, state[h])
            out.append(softmax_stage(h, bias, slot, state[h]))
            logits_stage(h, jnp.minimum(t + 1, nt - 1), 1 - slot)
        return tuple(out)

    init = tuple((jnp.full((1, QB), NEG_BIG, F32), jnp.zeros((8, QB), F32), jnp.ones((1, QB), F32))
                 for _ in range(ATT_HEADS))
    for h in range(ATT_HEADS):
        p_scr[h][1] = jnp.zeros((TK, QB), BF16)
        logits_stage(h, 0, 0)
    fin = lax.fori_loop(0, nt, att_tile, init)
    for h in range(ATT_HEADS):
        value_stage(h, nt - 1, lax.rem(nt - 1, 2), fin[h])
    for h in range(ATT_HEADS):
        o = (acc_scr[h] / jnp.sum(fin[h][1], axis=0, keepdims=True)).T
        c0 = h * HEAD_DIM
        out_ref[:, c0:c0 + HEAD_DIM] = (o * _silu(az[:, c0:c0 + HEAD_DIM])).astype(out_ref.dtype)


def _dsa_prompt(p_main, p_small, batch, seq):
    QB, TK = DSA_QB, DSA_TK
    nb = seq // QB
    topk = min(TOPK_MAX, seq // 4)
    assert TK >= topk and seq % (2 * TK) == 0 and seq % QB == 0 and TK % DSA_TS == 0
    qrow = lambda b, i: b * nb + i
    return pl.pallas_call(
        functools.partial(_dsa_kernel, topk=topk),
        out_shape=jax.ShapeDtypeStruct((batch * seq, ATT_WIDTH), BF16),
        grid=(batch, nb),
        in_specs=[pl.BlockSpec((QB, ATT_WIDTH), lambda b, i: (qrow(b, i), 0)),
                  pl.BlockSpec((QB, 1024), lambda b, i: (qrow(b, i), 1)),
                  pl.BlockSpec((QB, ATT_WIDTH), lambda b, i: (qrow(b, i), 2)),
                  pl.BlockSpec((QB, LANES), lambda b, i: (qrow(b, i), 4)),
                  pl.BlockSpec((seq, KV_WIDTH), lambda b, i: (b, 0)),
                  pl.BlockSpec((seq, KV_WIDTH), lambda b, i: (b, 1)),
                  pl.BlockSpec((seq, LANES), lambda b, i: (b, 4))],
        out_specs=pl.BlockSpec((QB, ATT_WIDTH), lambda b, i: (qrow(b, i), 0)),
        scratch_shapes=[pltpu.VMEM((seq, KV_WIDTH), BF16),
                        pltpu.VMEM((seq // TK, KV_WIDTH, TK), BF16),
                        pltpu.VMEM((seq, LANES), BF16),
                        pltpu.VMEM((seq, LANES), BF16),
                        pltpu.VMEM((seq, QB), I32),
                        pltpu.VMEM((seq, QB), I16),
                        pltpu.VMEM((ATT_HEADS, HEAD_DIM, QB), F32),
                        pltpu.VMEM((1, QB), F32)]
                       + [pltpu.VMEM((2, TK, QB), F32)] * ATT_HEADS
                       + [pltpu.VMEM((2, TK, QB), BF16)] * ATT_HEADS,
        compiler_params=pltpu.CompilerParams(dimension_semantics=("parallel", "arbitrary"),
                                             vmem_limit_bytes=VMEM_LIMIT),
        name="dsa_prompt",
    )(p_main, p_main, p_main, p_small, p_small, p_small, p_small)


IDX_PAGES_PER_STEP = 32
KV_PAGES_PER_STEP = 16


def _page_spec(block, group, j):
    return pl.BlockSpec(block, lambda b, g, pt: (pt[b, g * group + j],) + (0,) * (len(block) - 1))


def _sample_scores_kernel(pt_ref, qi_ref, wb_ref, kin_ref, *refs):
    pages = refs[:IDX_PAGES_PER_STEP]
    sc_ref, snew_ref = refs[IDX_PAGES_PER_STEP:]
    qi = qi_ref[0]
    wb = wb_ref[0]
    rows = []
    for page in pages:
        s = _dot(qi, page[0].astype(BF16))
        rows.append(jnp.sum(jnp.maximum(s, 0.0) * wb, axis=0, keepdims=True) * IDX_SCALE)
    sc_ref[0] = jnp.concatenate(rows, axis=0)

    @pl.when(pl.program_id(1) == 0)
    def _():
        ki_new = kin_ref[0].astype(BF16).astype(F32)
        s = jnp.sum(qi.astype(F32) * ki_new, axis=1, keepdims=True)
        s_new = jnp.sum(jnp.maximum(s, 0.0) * wb[:, 0:1], axis=0, keepdims=True) * IDX_SCALE
        snew_ref[0] = jnp.broadcast_to(s_new, (8, LANES))


def _sample_scores(page_table, qi16, wb, ki_new, cache_ik_t):
    bd, n_pages = page_table.shape
    pg = IDX_PAGES_PER_STEP
    bmap = lambda b, g, pt: (b, 0, 0)
    return pl.pallas_call(
        _sample_scores_kernel,
        out_shape=(jax.ShapeDtypeStruct((bd, n_pages, PAGE), F32),
                   jax.ShapeDtypeStruct((bd, 8, LANES), F32)),
        grid_spec=pltpu.PrefetchScalarGridSpec(
            num_scalar_prefetch=1,
            grid=(bd, n_pages // pg),
            in_specs=[pl.BlockSpec((1, IDX_HEADS, IDX_DIM), bmap),
                      pl.BlockSpec((1, IDX_HEADS, LANES), bmap),
                      pl.BlockSpec((1, 1, IDX_DIM), bmap)]
                     + [_page_spec((1, IDX_DIM, PAGE), pg, j) for j in range(pg)],
            out_specs=(pl.BlockSpec((1, pg, PAGE), lambda b, g, pt: (b, g, 0)),
                       pl.BlockSpec((1, 8, LANES), bmap))),
        compiler_params=pltpu.CompilerParams(dimension_semantics=("parallel", "arbitrary"),
                                             vmem_limit_bytes=VMEM_LIMIT),
        name="sample_scores",
    )(page_table, qi16, wb, ki_new, *([cache_ik_t] * pg))


SEL_THR, SEL_NEXT, SEL_CUT = 0, 1, 2


def _sample_select_kernel(sc_ref, snew_ref, sel_ref, *, topk):
    x = sc_ref[...]
    bd, n_pages, _ = x.shape
    past = n_pages * PAGE
    s_new = snew_ref[:, 0:1, 0:1]

    def total(v):
        return jnp.sum(jnp.sum(v, axis=1, keepdims=True), axis=2, keepdims=True)

    def count_ge(cand):
        return total(jnp.where(x >= cand, 1.0, 0.0)) + jnp.where(s_new >= cand, 1.0, 0.0)

    key = _kth_largest_key(count_ge, (bd, 1, 1), topk)
    thr = _key_to_float(key)
    thr_next = _key_to_float(key + 1)
    need = float(topk) - count_ge(thr_next)
    pos = (lax.broadcasted_iota(I32, (1, n_pages, PAGE), 1) * PAGE
           + lax.broadcasted_iota(I32, (1, n_pages, PAGE), 2)).astype(F32)

    def count_tie_le(cut):
        tie = jnp.where(x >= thr_next, 0.0, jnp.where(x >= thr, jnp.where(pos <= cut, 1.0, 0.0), 0.0))
        tie_new = jnp.where(s_new >= thr_next, 0.0, jnp.where(s_new >= thr, jnp.where(float(past) <= cut, 1.0, 0.0), 0.0))
        return total(tie) + tie_new

    cut = _tie_cutoff(count_tie_le, need, (bd, 1, 1), int(np.log2(past)) + 1)
    row = lax.broadcasted_iota(I32, (bd, 8, LANES), 1)
    sel_ref[...] = jnp.where(row == SEL_THR, thr, jnp.where(row == SEL_NEXT, thr_next, cut))


def _sample_select(scores, s_new, topk):
    bd = scores.shape[0]
    return pl.pallas_call(
        functools.partial(_sample_select_kernel, topk=topk),
        out_shape=jax.ShapeDtypeStruct((bd, 8, LANES), F32),
        compiler_params=pltpu.CompilerParams(vmem_limit_bytes=VMEM_LIMIT),
        name="sample_select",
    )(scores, s_new)


def _taken_bias(score, pos, sel):
    thr, thr_next, cut = sel[SEL_THR:SEL_THR + 1, 0:1], sel[SEL_NEXT:SEL_NEXT + 1, 0:1], sel[SEL_CUT:SEL_CUT + 1, 0:1]
    return jnp.where(score >= thr_next, 0.0,
                     jnp.where(score >= thr, jnp.where(pos <= cut, 0.0, NEG_BIG), NEG_BIG))


def _sample_compact_kernel(pt_ref, sc_ref, sel_ref, rows_ref, nsel_ref, rank_scr, *, slots):
    b = pl.program_id(0)
    n_pages = sc_ref.shape[1]
    pos = (lax.broadcasted_iota(I32, (n_pages, PAGE), 0) * PAGE
           + lax.broadcasted_iota(I32, (n_pages, PAGE), 1)).astype(F32)
    taken = jnp.where(_taken_bias(sc_ref[0], pos, sel_ref[0]) == 0.0, 1.0, 0.0)
    before = lax.broadcasted_iota(I32, (PAGE, PAGE), 0) < lax.broadcasted_iota(I32, (PAGE, PAGE), 1)
    in_page = _dot(taken.astype(BF16), jnp.where(before, 1.0, 0.0).astype(BF16))
    page_tot = jnp.broadcast_to(jnp.sum(taken, axis=1, keepdims=True), (n_pages, PAGE))
    earlier = lax.broadcasted_iota(I32, (n_pages, n_pages), 1) < lax.broadcasted_iota(I32, (n_pages, n_pages), 0)
    in_pages = _dot(jnp.where(earlier, 1.0, 0.0).astype(BF16), page_tot.astype(BF16))
    rank_scr[...] = jnp.where(taken > 0.0, in_page + in_pages, -1.0)
    nsel_ref[0] = jnp.broadcast_to(jnp.sum(jnp.sum(taken, axis=1, keepdims=True), axis=0, keepdims=True),
                                   (8, LANES))

    slot = lax.broadcasted_iota(I32, (slots, PAGE), 0).astype(F32)
    off = lax.broadcasted_iota(I32, (1, PAGE), 1)

    def page_body(p, acc):
        phys = (pt_ref[b, p] * PAGE + off).astype(F32)
        return acc + jnp.where(slot == rank_scr[pl.ds(p, 1), :], phys, 0.0)

    acc = lax.fori_loop(0, n_pages, page_body, jnp.zeros((slots, PAGE), F32))
    rows_ref[0] = jnp.broadcast_to(jnp.sum(acc, axis=1, keepdims=True), (slots, LANES))


def _sample_compact(page_table, scores, sel, slots):
    bd, n_pages = page_table.shape
    assert n_pages * PAGE * (page_table.shape[0] + 1) < 2 ** 24
    bmap = lambda b, pt: (b, 0, 0)
    return pl.pallas_call(
        functools.partial(_sample_compact_kernel, slots=slots),
        out_shape=(jax.ShapeDtypeStruct((bd, slots, LANES), F32), jax.ShapeDtypeStruct((bd, 8, LANES), F32)),
        grid_spec=pltpu.PrefetchScalarGridSpec(
            num_scalar_prefetch=1,
            grid=(bd,),
            in_specs=[pl.BlockSpec((1, n_pages, PAGE), bmap), pl.BlockSpec((1, 8, LANES), bmap)],
            out_specs=(pl.BlockSpec((1, slots, LANES), bmap), pl.BlockSpec((1, 8, LANES), bmap)),
            scratch_shapes=[pltpu.VMEM((n_pages, PAGE), F32)]),
        compiler_params=pltpu.CompilerParams(dimension_semantics=("parallel",), vmem_limit_bytes=VMEM_LIMIT),
        name="sample_compact",
    )(page_table, scores, sel)


SC_GATHER_CHUNK = 128


def _gather_rows(table_k, table_v, idx):
    info = plsc.get_sparse_core_info()
    n_workers = info.num_cores * info.num_subcores
    n_idx = idx.shape[0]
    per_worker = n_idx // n_workers
    assert per_worker * n_workers == n_idx and per_worker % SC_GATHER_CHUNK == 0
    mesh = plsc.VectorSubcoreMesh(core_axis_name="c", subcore_axis_name="s")
    out = jax.ShapeDtypeStruct((n_idx, table_k.shape[1]), table_k.dtype)

    @functools.partial(
        pl.kernel, mesh=mesh, out_type=(out, out),
        scratch_types=[pltpu.VMEM((SC_GATHER_CHUNK,), I32),
                       pltpu.VMEM((SC_GATHER_CHUNK, table_k.shape[1]), table_k.dtype),
                       pltpu.VMEM((SC_GATHER_CHUNK, table_v.shape[1]), table_v.dtype),
                       pltpu.SemaphoreType.DMA, pltpu.SemaphoreType.DMA])
    def gather(tk_hbm, tv_hbm, idx_hbm, ok_hbm, ov_hbm, idx_v, rk_v, rv_v, sem_k, sem_v):
        worker = lax.axis_index("s") * info.num_cores + lax.axis_index("c")

        @pl.loop(0, per_worker // SC_GATHER_CHUNK)
        def _(j):
            base = worker * per_worker + j * SC_GATHER_CHUNK
            pltpu.sync_copy(idx_hbm.at[pl.ds(base, SC_GATHER_CHUNK)], idx_v)
            copy_k = pltpu.async_copy(tk_hbm.at[idx_v], rk_v, sem_k)
            copy_v = pltpu.async_copy(tv_hbm.at[idx_v], rv_v, sem_v)
            copy_k.wait()
            copy_v.wait()
            pltpu.sync_copy(rk_v, ok_hbm.at[pl.ds(base, SC_GATHER_CHUNK)])
            pltpu.sync_copy(rv_v, ov_hbm.at[pl.ds(base, SC_GATHER_CHUNK)])

    return gather(table_k, table_v, idx)


def _sample_attend_kernel(q_ref, az_ref, kn_ref, vn_ref, snew_ref, sel_ref, nsel_ref, k_ref, v_ref, out_ref, *, past):
    width = k_ref.shape[1]
    q = q_ref[0]
    col = lax.broadcasted_iota(I32, (ATT_HEADS, width), 1)
    head = lax.broadcasted_iota(I32, (ATT_HEADS, width), 0)
    own_head = col % KV_HEADS == head // GROUP
    filled = (col // KV_HEADS).astype(F32) < nsel_ref[0][0:1, 0:1]
    lg = _dot_nt(q, k_ref[0].astype(BF16))
    lg = jnp.where(own_head, jnp.where(filled, lg, NEG_BIG), NEG_BIG)
    lg_new = (jnp.sum(q.astype(F32) * kn_ref[0].astype(BF16).astype(F32), axis=1, keepdims=True)
              + _taken_bias(snew_ref[0][0:1, 0:1], float(past), sel_ref[0]))
    m = jnp.maximum(jnp.max(lg, axis=1, keepdims=True), lg_new)
    p = jnp.exp(lg - m)
    p_new = jnp.exp(lg_new - m)
    l = jnp.sum(p, axis=1, keepdims=True) + p_new
    acc = _dot(p.astype(BF16), v_ref[0].astype(BF16)) + p_new * vn_ref[0].astype(BF16).astype(F32)
    out_ref[0] = (acc / l) * _silu(az_ref[0])


def _sample_attend(q8, az8, k_new8, v_new8, s_new, sel, n_sel, k_sel, v_sel, past):
    bd, width, _ = k_sel.shape
    bmap = lambda b: (b, 0, 0)
    head_tile = pl.BlockSpec((1, ATT_HEADS, HEAD_DIM), bmap)
    par_tile = pl.BlockSpec((1, 8, LANES), bmap)
    rows_tile = pl.BlockSpec((1, width, HEAD_DIM), bmap)
    return pl.pallas_call(
        functools.partial(_sample_attend_kernel, past=past),
        out_shape=jax.ShapeDtypeStruct((bd, ATT_HEADS, HEAD_DIM), F32),
        grid=(bd,),
        in_specs=[head_tile, head_tile, head_tile, head_tile, par_tile, par_tile, par_tile, rows_tile, rows_tile],
        out_specs=head_tile,
        compiler_params=pltpu.CompilerParams(dimension_semantics=("parallel",), vmem_limit_bytes=VMEM_LIMIT),
        name="sample_attend",
    )(q8, az8, k_new8, v_new8, s_new, sel, n_sel, k_sel, v_sel)


def _mlstm_step_kernel(bi_ref, bf_ref, q_ref, k_ref, v_ref, mo_ref, mz_ref, misc_ref, nw_ref,
                       c_ref, n_ref, m_ref, out_ref, c_out, n_out, m_out):
    misc = misc_ref[0]
    eye = lax.broadcasted_iota(I32, (ML_V, ML_V), 0) == lax.broadcasted_iota(I32, (ML_V, ML_V), 1)
    for h in range(ML_HEADS):
        ig = misc[:, MISC_IG + h:MISC_IG + h + 1] + bi_ref[h]
        lf = _log_sigmoid(misc[:, MISC_FG + h:MISC_FG + h + 1] + bf_ref[h])
        m_prev = m_ref[0, h][:, 0:1]
        log_a = lf + m_prev
        m_t = jnp.maximum(log_a, ig)
        d = jnp.exp(ig - m_t)
        a = jnp.exp(log_a - m_t)
        q = q_ref[0][:, h * ML_QK:(h + 1) * ML_QK]
        k = k_ref[0][:, h * ML_QK:(h + 1) * ML_QK]
        v = v_ref[0][:, h * ML_V:(h + 1) * ML_V]
        v_col = jnp.sum(jnp.where(eye, v, 0.0), axis=1, keepdims=True)
        c = c_ref[0, h]
        n = n_ref[0, h]
        s = jnp.sum(q * k, axis=1, keepdims=True) * d
        num = a * jnp.sum(c * q, axis=1, keepdims=True) + s * v_col
        den = a * jnp.sum(n * q, axis=1, keepdims=True) + s
        h_col = num / jnp.maximum(jnp.abs(den), jnp.exp(-m_t))
        c_out[0, h] = a * c + (d * v_col) * k
        n_out[0, h] = a * n + d * k
        m_out[0, h] = jnp.broadcast_to(m_t, (1, LANES))

        h_row = jnp.sum(jnp.where(eye, h_col, 0.0), axis=0, keepdims=True)
        ms = jnp.mean(h_row * h_row, axis=1, keepdims=True)
        hn = h_row * lax.rsqrt(ms + RMS_EPS) * nw_ref[:, h * ML_V:(h + 1) * ML_V]
        gate = _sigmoid(mo_ref[0][:, h * ML_V:(h + 1) * ML_V]) * _silu(mz_ref[0][:, h * ML_V:(h + 1) * ML_V])
        out_ref[0, :, h * ML_V:(h + 1) * ML_V] = (hn * gate).astype(out_ref.dtype)


def _mlstm_step(ps_main, ps_small, b_i, b_f, ml_norm_w, state_c, state_n, state_m):
    bd = ps_main.shape[0]
    col = lambda j: (lambda b: (b, 0, j))
    st4 = lambda b: (b, 0, 0, 0)
    return pl.pallas_call(
        _mlstm_step_kernel,
        out_shape=(jax.ShapeDtypeStruct((bd, 1, ML_WIDTH), BF16),
                   jax.ShapeDtypeStruct(state_c.shape, F32),
                   jax.ShapeDtypeStruct(state_n.shape, F32),
                   jax.ShapeDtypeStruct(state_m.shape, F32)),
        grid=(bd,),
        in_specs=[pl.BlockSpec(memory_space=pltpu.SMEM),
                  pl.BlockSpec(memory_space=pltpu.SMEM),
                  pl.BlockSpec((1, 1, 512), col(12)),
                  pl.BlockSpec((1, 1, 512), col(13)),
                  pl.BlockSpec((1, 1, 1024), col(3)),
                  pl.BlockSpec((1, 1, 1024), col(4)),
                  pl.BlockSpec((1, 1, 1024), col(5)),
                  pl.BlockSpec((1, 1, LANES), col(4)),
                  pl.BlockSpec((1, ML_WIDTH), lambda b: (0, 0)),
                  pl.BlockSpec((1, ML_HEADS, ML_V, ML_QK), st4),
                  pl.BlockSpec((1, ML_HEADS, 1, ML_QK), st4),
                  pl.BlockSpec((1, ML_HEADS, 1, LANES), st4)],
        out_specs=(pl.BlockSpec((1, 1, ML_WIDTH), lambda b: (b, 0, 0)),
                   pl.BlockSpec((1, ML_HEADS, ML_V, ML_QK), st4),
                   pl.BlockSpec((1, ML_HEADS, 1, ML_QK), st4),
                   pl.BlockSpec((1, ML_HEADS, 1, LANES), st4)),
        compiler_params=pltpu.CompilerParams(dimension_semantics=("parallel",),
                                             vmem_limit_bytes=VMEM_LIMIT),
        name="mlstm_step",
    )(b_i, b_f, ps_main, ps_main, ps_main, ps_main, ps_main, ps_small, ml_norm_w.reshape(1, ML_WIDTH),
      state_c, state_n, state_m)


def _split_weights(w_in):
    offs = np.cumsum((0,) + IN_SIZES)
    aq, ak, av, iq, ik, iw, az, mq, mk, mv, mi, mf, mo, mz = (w_in[:, offs[j]:offs[j + 1]] for j in range(len(IN_SIZES)))
    w_main = jnp.concatenate([aq * ATT_SCALE, iq, az, mv, mo, mz, mq, mk * ML_QK ** -0.5], axis=1)
    pad = jnp.zeros((w_in.shape[0], LANES - IDX_DIM - IDX_HEADS - 2 * ML_HEADS), w_in.dtype)
    w_small = jnp.concatenate([ak, av, ik, iw, mi, mf, pad], axis=1)
    assert w_main.shape[1] == MAIN_W and w_small.shape[1] == SMALL_W
    return w_main.astype(BF16), w_small.astype(BF16)


def kernel(x_prompt, x_sample, cache_k, cache_v, cache_idx_k, state_C, state_n, state_m, page_table,
           norm_w, w_in, b_igate, b_fgate, ml_norm_w, w_out, final_norm_w):
    depth = w_in.shape[0]
    batch, seq, d = x_prompt.shape
    bd, dec_seq, _ = x_sample.shape
    assert depth == 1 and dec_seq == 1 and d == D_MODEL
    n_pages = page_table.shape[1]

    w_main, w_small = _split_weights(w_in[0])
    w_o = w_out[0].astype(BF16)
    w_o_att, w_o_ml = w_o[:ATT_WIDTH], w_o[ATT_WIDTH:]

    xp = x_prompt.reshape(batch * seq, d)
    p_main = _project(xp, norm_w[0], w_main, BF16, 1024, 512, "proj_main")
    p_small = _project(xp, norm_w[0], w_small, F32, 1024, SMALL_W, "proj_small")
    att = _dsa_prompt(p_main, p_small, batch, seq)
    ml, ct, m_p = _mlstm_prompt(p_main, p_small, b_igate[0], b_fgate[0], ml_norm_w[0], batch, seq)
    y_prompt = _out_project(xp, att, ml, w_o_att, w_o_ml, final_norm_w, 256, "out_prompt").reshape(batch, seq, d)
    k_prompt = p_small[:, :KV_WIDTH].reshape(1, batch, seq, KV_HEADS, HEAD_DIM)
    v_prompt = p_small[:, KV_WIDTH:2 * KV_WIDTH].reshape(1, batch, seq, KV_HEADS, HEAD_DIM)
    ik_prompt = p_small[:, 2 * KV_WIDTH:2 * KV_WIDTH + IDX_DIM].reshape(1, batch, seq, IDX_DIM)
    c_prompt = jnp.swapaxes(ct[..., :ML_V], -1, -2)[None]
    n_prompt = ct[..., ML_V][None]
    m_prompt = m_p[:, :, 0, 0][None]

    xs = x_sample.reshape(bd, d)
    ps_main = _project(xs, norm_w[0], w_main, F32, bd, 512, "proj_main_s")
    ps_small = _project(xs, norm_w[0], w_small, F32, bd, SMALL_W, "proj_small_s")
    q8 = ps_main[:, :ATT_WIDTH].reshape(bd, ATT_HEADS, HEAD_DIM).astype(BF16)
    qi16 = ps_main[:, 1024:2048].reshape(bd, IDX_HEADS, IDX_DIM).astype(BF16)
    az8 = ps_main[:, 2048:3072].reshape(bd, ATT_HEADS, HEAD_DIM)
    misc_s = ps_small[:, 2 * KV_WIDTH:]
    ki_new = misc_s[:, :IDX_DIM].reshape(bd, 1, IDX_DIM)
    wb = jnp.broadcast_to(misc_s[:, MISC_WI:MISC_WI + IDX_HEADS, None], (bd, IDX_HEADS, LANES))
    k_new8 = jnp.repeat(ps_small[:, :KV_WIDTH].reshape(bd, KV_HEADS, HEAD_DIM), GROUP, axis=1)
    v_new8 = jnp.repeat(ps_small[:, KV_WIDTH:2 * KV_WIDTH].reshape(bd, KV_HEADS, HEAD_DIM), GROUP, axis=1)
    scores, s_new = _sample_scores(page_table, qi16, wb, ki_new, jnp.swapaxes(cache_idx_k[0], 1, 2))
    topk_s = min(TOPK_MAX, (n_pages * PAGE + 1) // 4)
    sel = _sample_select(scores, s_new, topk_s)
    key_rows, n_sel = _sample_compact(page_table, scores, sel, topk_s)
    idx = (key_rows[:, :, :1].astype(I32) * KV_HEADS + jnp.arange(KV_HEADS, dtype=I32)).reshape(-1)
    k_sel, v_sel = _gather_rows(cache_k.reshape(-1, HEAD_DIM), cache_v.reshape(-1, HEAD_DIM), idx)
    att_s = _sample_attend(q8, az8, k_new8, v_new8, s_new, sel, n_sel,
                           k_sel.reshape(bd, topk_s * KV_HEADS, HEAD_DIM),
                           v_sel.reshape(bd, topk_s * KV_HEADS, HEAD_DIM), n_pages * PAGE)
    ml_s, c_s, n_s, m_s = _mlstm_step(
        ps_main.reshape(bd, 1, MAIN_W), ps_small.reshape(bd, 1, SMALL_W), b_igate[0], b_fgate[0], ml_norm_w[0],
        state_C[0], state_n[0].reshape(bd, ML_HEADS, 1, ML_QK),
        jnp.broadcast_to(state_m[0][:, :, None, None], (bd, ML_HEADS, 1, LANES)))
    y_sample = _out_project(xs, att_s.reshape(bd, ATT_WIDTH).astype(BF16), ml_s.reshape(bd, ML_WIDTH),
                            w_o_att, w_o_ml, final_norm_w, bd, "out_sample").reshape(bd, 1, d)
    k_sample = ps_small[:, :KV_WIDTH].reshape(1, bd, 1, KV_HEADS, HEAD_DIM)
    v_sample = ps_small[:, KV_WIDTH:2 * KV_WIDTH].reshape(1, bd, 1, KV_HEADS, HEAD_DIM)
    ik_sample = misc_s[:, :IDX_DIM].reshape(1, bd, 1, IDX_DIM)

    return (y_prompt, y_sample, k_prompt, v_prompt, ik_prompt, c_prompt, n_prompt, m_prompt,
            k_sample, v_sample, ik_sample, c_s[None], n_s.reshape(1, bd, ML_HEADS, ML_QK), m_s[:, :, 0, 0][None])
```

```python
import functools

import jax
import jax.numpy as jnp
import numpy as np
from jax import lax
from jax.experimental import pallas as pl
from jax.experimental.pallas import tpu as pltpu
from jax.experimental.pallas import tpu_sc as plsc

F32 = jnp.float32
BF16 = jnp.bfloat16
I32 = jnp.int32

D_MODEL = 2048
PAGE = 128
ATT_HEADS = 8
KV_HEADS = 2
HEAD_DIM = 128
GROUP = ATT_HEADS // KV_HEADS
ATT_WIDTH = ATT_HEADS * HEAD_DIM
KV_WIDTH = KV_HEADS * HEAD_DIM
ATT_SCALE = HEAD_DIM ** -0.5
IDX_HEADS = 16
IDX_DIM = 64
IDX_SCALE = (IDX_HEADS * IDX_DIM) ** -0.5
TOPK_MAX = 256
ML_HEADS = 4
ML_QK = 128
ML_V = 256
ML_WIDTH = ML_HEADS * ML_V
RMS_EPS = 1e-6
IN_SIZES = (ATT_WIDTH, KV_WIDTH, KV_WIDTH, IDX_HEADS * IDX_DIM, IDX_DIM, IDX_HEADS, ATT_WIDTH,
            ML_HEADS * ML_QK, ML_HEADS * ML_QK, ML_WIDTH, ML_HEADS, ML_HEADS, ML_WIDTH, ML_WIDTH)

LANES = 128
NEG_BIG = -1e30
VMEM_LIMIT = 56 * 1024 * 1024

MAIN_W = 7168
SMALL_W = 640
MISC_WI = IDX_DIM
MISC_IG = IDX_DIM + IDX_HEADS
MISC_FG = MISC_IG + ML_HEADS


def _dot(a, b):
    return jnp.dot(a, b, preferred_element_type=F32)


def _dot_nt(a, b):
    return lax.dot_general(a, b, (((1,), (1,)), ((), ())), preferred_element_type=F32)


def _tree_reduce(op, parts):
    parts = list(parts)
    while len(parts) > 1:
        paired = [op(parts[j], parts[j + 1]) for j in range(0, len(parts) - 1, 2)]
        parts = paired + parts[len(parts) - len(parts) % 2:]
    return parts[0]


def _fold_rows(op, x):
    return _tree_reduce(op, [x[r:r + 8] for r in range(0, x.shape[0], 8)])


def _proj_kernel(x_ref, nw_ref, w_ref, o_ref, h_scr):
    @pl.when(pl.program_id(1) == 0)
    def _():
        x = x_ref[...]
        ms = jnp.mean(x * x, axis=-1, keepdims=True)
        h_scr[...] = (x * lax.rsqrt(ms + RMS_EPS) * nw_ref[...]).astype(BF16)

    o_ref[...] = _dot_nt(h_scr[...], w_ref[...]).astype(o_ref.dtype)


def _project(x2d, norm_w, w_t, out_dtype, tm, tn, name):
    m, d = x2d.shape
    n = w_t.shape[0]
    return pl.pallas_call(
        _proj_kernel,
        out_shape=jax.ShapeDtypeStruct((m, n), out_dtype),
        grid=(m // tm, n // tn),
        in_specs=[pl.BlockSpec((tm, d), lambda i, j: (i, 0)),
                  pl.BlockSpec((1, d), lambda i, j: (0, 0)),
                  pl.BlockSpec((tn, d), lambda i, j: (j, 0))],
        out_specs=pl.BlockSpec((tm, tn), lambda i, j: (i, j)),
        scratch_shapes=[pltpu.VMEM((tm, d), BF16)],
        compiler_params=pltpu.CompilerParams(dimension_semantics=("parallel", "arbitrary"),
                                             vmem_limit_bytes=VMEM_LIMIT),
        name=name,
    )(x2d, norm_w.reshape(1, d), w_t)


def _out_kernel(x_ref, a_ref, m_ref, wa_ref, wm_ref, fw_ref, o_ref):
    y = x_ref[...] + _dot(a_ref[...], wa_ref[...]) + _dot(m_ref[...], wm_ref[...])
    ms = jnp.mean(y * y, axis=-1, keepdims=True)
    o_ref[...] = y * lax.rsqrt(ms + RMS_EPS) * fw_ref[...]


def _out_project(x2d, a, mo, w_att, w_ml, final_w, tm, name):
    m, d = x2d.shape
    return pl.pallas_call(
        _out_kernel,
        out_shape=jax.ShapeDtypeStruct((m, d), F32),
        grid=(m // tm,),
        in_specs=[pl.BlockSpec((tm, d), lambda i: (i, 0)),
                  pl.BlockSpec((tm, ATT_WIDTH), lambda i: (i, 0)),
                  pl.BlockSpec((tm, ML_WIDTH), lambda i: (i, 0)),
                  pl.BlockSpec((ATT_WIDTH, d), lambda i: (0, 0)),
                  pl.BlockSpec((ML_WIDTH, d), lambda i: (0, 0)),
                  pl.BlockSpec((1, d), lambda i: (0, 0))],
        out_specs=pl.BlockSpec((tm, d), lambda i: (i, 0)),
        compiler_params=pltpu.CompilerParams(dimension_semantics=("parallel",),
                                             vmem_limit_bytes=VMEM_LIMIT),
        name=name,
    )(x2d, a, mo, w_att, w_ml, final_w.reshape(1, d))


def _log_sigmoid(x):
    return jnp.minimum(x, 0.0) - jnp.log(1.0 + jnp.exp(-jnp.abs(x)))


def _sigmoid(x):
    return 1.0 / (1.0 + jnp.exp(-x))


def _silu(x):
    return x * _sigmoid(x)


ML_CHUNK = 256
ML_AUG = ML_V + LANES


def _mlstm_kernel(bi_ref, bf_ref, q_ref, k_ref, v_ref, mo_ref, mz_ref, misc_ref, nw_ref,
                  out_ref, ct_ref, m_ref):
    L = ML_CHUNK

    @pl.when(pl.program_id(1) == 0)
    def _():
        ct_ref[...] = jnp.zeros_like(ct_ref)
        m_ref[...] = jnp.zeros_like(m_ref)

    misc = misc_ref[...]
    misc_t = misc.T
    t_idx = lax.broadcasted_iota(I32, (L, L), 0)
    s_idx = lax.broadcasted_iota(I32, (L, L), 1)
    causal = s_idx <= t_idx
    ones_col = jnp.where(lax.broadcasted_iota(I32, (L, LANES), 1) == 0, 1.0, 0.0).astype(BF16)

    for h in range(ML_HEADS):
        ig_row = misc_t[MISC_IG + h:MISC_IG + h + 1, :] + bi_ref[h]
        lf_row = _log_sigmoid(misc_t[MISC_FG + h:MISC_FG + h + 1, :] + bf_ref[h])
        lf_col = _log_sigmoid(misc[:, MISC_FG + h:MISC_FG + h + 1] + bf_ref[h])
        b_col = jnp.sum(jnp.where(causal, lf_row, 0.0), axis=1, keepdims=True)
        b_row = jnp.sum(jnp.where(t_idx <= s_idx, lf_col, 0.0), axis=0, keepdims=True)
        m_prev = m_ref[0, h][0:1, 0:1]
        log_d = jnp.where(causal, b_col - b_row + ig_row, -jnp.inf)
        log_a = b_col + m_prev
        m_t = jnp.maximum(log_a, jnp.max(log_d, axis=1, keepdims=True))
        d = jnp.exp(log_d - m_t)
        a = jnp.exp(log_a - m_t)

        q = q_ref[:, h * ML_QK:(h + 1) * ML_QK]
        k = k_ref[:, h * ML_QK:(h + 1) * ML_QK]
        v_aug = jnp.concatenate([v_ref[:, h * ML_V:(h + 1) * ML_V], ones_col], axis=1)
        s = (_dot_nt(q, k) * d).astype(BF16)
        ct = ct_ref[0, h]
        num_aug = a * _dot(q, ct.astype(BF16)) + _dot(s, v_aug)
        den = num_aug[:, ML_V:ML_V + 1]
        hh = num_aug[:, :ML_V] / jnp.maximum(jnp.abs(den), jnp.exp(-m_t))

        m_new = m_t[L - 1:L, :]
        a_end = a[L - 1:L, :]
        w_row = jnp.exp(b_row[:, L - 1:L] - b_row + ig_row - m_new)
        ktw = (k.astype(F32).T * w_row).astype(BF16)
        ct_new = a_end * ct + _dot(ktw, v_aug)
        ct_ref[0, h] = ct_new
        m_ref[0, h] = jnp.broadcast_to(m_new, (8, LANES))

        ms = jnp.mean(hh * hh, axis=1, keepdims=True)
        hn = hh * lax.rsqrt(ms + RMS_EPS) * nw_ref[:, h * ML_V:(h + 1) * ML_V]
        gate = _sigmoid(mo_ref[:, h * ML_V:(h + 1) * ML_V].astype(F32)) * _silu(mz_ref[:, h * ML_V:(h + 1) * ML_V].astype(F32))
        out_ref[:, h * ML_V:(h + 1) * ML_V] = (hn * gate).astype(out_ref.dtype)


def _mlstm_prompt(p_main, p_small, b_i, b_f, ml_norm_w, batch, seq):
    L = ML_CHUNK
    nc = seq // L
    row = lambda b, c: b * nc + c
    return pl.pallas_call(
        _mlstm_kernel,
        out_shape=(jax.ShapeDtypeStruct((batch * seq, ML_WIDTH), BF16),
                   jax.ShapeDtypeStruct((batch, ML_HEADS, ML_QK, ML_AUG), F32),
                   jax.ShapeDtypeStruct((batch, ML_HEADS, 8, LANES), F32)),
        grid=(batch, nc),
        in_specs=[pl.BlockSpec(memory_space=pltpu.SMEM),
                  pl.BlockSpec(memory_space=pltpu.SMEM),
                  pl.BlockSpec((L, 512), lambda b, c: (row(b, c), 12)),
                  pl.BlockSpec((L, 512), lambda b, c: (row(b, c), 13)),
                  pl.BlockSpec((L, 1024), lambda b, c: (row(b, c), 3)),
                  pl.BlockSpec((L, 1024), lambda b, c: (row(b, c), 4)),
                  pl.BlockSpec((L, 1024), lambda b, c: (row(b, c), 5)),
                  pl.BlockSpec((L, LANES), lambda b, c: (row(b, c), 4)),
                  pl.BlockSpec((1, ML_WIDTH), lambda b, c: (0, 0))],
        out_specs=(pl.BlockSpec((L, ML_WIDTH), lambda b, c: (row(b, c), 0)),
                   pl.BlockSpec((1, ML_HEADS, ML_QK, ML_AUG), lambda b, c: (b, 0, 0, 0)),
                   pl.BlockSpec((1, ML_HEADS, 8, LANES), lambda b, c: (b, 0, 0, 0))),
        compiler_params=pltpu.CompilerParams(dimension_semantics=("parallel", "arbitrary"),
                                             vmem_limit_bytes=VMEM_LIMIT),
        name="mlstm_prompt",
    )(b_i, b_f, p_main, p_main, p_main, p_main, p_main, p_small, ml_norm_w.reshape(1, ML_WIDTH))


INT_MIN = -2 ** 31


def _float_to_key(x):
    bits = lax.bitcast_convert_type(x, I32)
    return jnp.where(bits >= 0, bits, bits ^ jnp.int32(0x7FFFFFFF))


def _key_to_float(key):
    bits = jnp.where(key >= 0, key, key ^ jnp.int32(0x7FFFFFFF))
    return lax.bitcast_convert_type(bits, F32)


KEY_NEG_INF = INT_MIN + 0x7FFFFF


def _kth_largest_key(count_ge, shape, k):
    def bit_body(it, prefix):
        cand = prefix + lax.shift_left(jnp.int32(1), 31 - it)
        cand_f = _key_to_float(jnp.maximum(cand, KEY_NEG_INF))
        return jnp.where(count_ge(cand_f) >= float(k), cand, prefix)

    return lax.fori_loop(0, 32, bit_body, jnp.full(shape, INT_MIN, I32))


def _tie_cutoff(count_tie_le, need, shape, index_bits):
    def bit_body(it, lo):
        cand = lo + lax.shift_left(jnp.int32(1), index_bits - 1 - it).astype(F32)
        return jnp.where(count_tie_le(cand) < need, cand, lo)

    return lax.fori_loop(0, index_bits, bit_body, jnp.full(shape, -1.0, F32)) + 1.0


DSA_QB = 128
DSA_TK = 256
DSA_TS = 256


def _dsa_kernel(q_ref, qi_ref, az_ref, miscq_ref, k_ref, v_ref, misck_ref, out_ref,
                k_bf, vt_scr, ki_lo, ki_hi, sc_scr, acc_scr, j_scr, *head_scr, topk):
    i = pl.program_id(1)
    seq = k_ref.shape[0]
    QB, TK, TS = DSA_QB, DSA_TK, DSA_TS
    nt = (i * QB + QB + TK - 1) // TK

    @pl.when(i == 0)
    def _():
        k_bf[...] = k_ref[...].astype(BF16)

        def v_tile(t, carry):
            vt_scr[t] = v_ref[pl.ds(pl.multiple_of(t * TK, TK), TK), :].T.astype(BF16)
            return carry

        lax.fori_loop(0, seq // TK, v_tile, 0)
        lane = lax.broadcasted_iota(I32, (seq, LANES), 1)
        lo = jnp.where(lane < IDX_DIM, misck_ref[...], 0.0)
        ki_lo[...] = lo.astype(BF16)
        ki_hi[...] = pltpu.roll(lo, IDX_DIM, axis=1).astype(BF16)

    k_iota = lax.broadcasted_iota(I32, (TK, QB), 0)
    q_pos = i * QB + lax.broadcasted_iota(I32, (TK, QB), 1)
    w_t = miscq_ref[...].T

    def tile_rows(t):
        return pl.ds(pl.multiple_of(t * TK, TK), TK)

    def score_tile(t, carry):
        rows = pl.ds(pl.multiple_of(t * TS, TS), TS)
        klo = ki_lo[rows, :]
        khi = ki_hi[rows, :]
        acc = jnp.zeros((TS, QB), F32)
        for p in range(IDX_HEADS // 2):
            qp = qi_ref[:, p * LANES:(p + 1) * LANES]
            w0 = w_t[MISC_WI + 2 * p:MISC_WI + 2 * p + 1, :]
            w1 = w_t[MISC_WI + 2 * p + 1:MISC_WI + 2 * p + 2, :]
            acc = acc + w0 * jnp.maximum(_dot_nt(klo, qp), 0.0) + w1 * jnp.maximum(_dot_nt(khi, qp), 0.0)
        k_pos = t * TS + lax.broadcasted_iota(I32, (TS, QB), 0)
        q_pos_s = i * QB + lax.broadcasted_iota(I32, (TS, QB), 1)
        score = jnp.where(k_pos <= q_pos_s, acc * IDX_SCALE + 0.0, -jnp.inf)
        sc_scr[rows, :] = _float_to_key(score)
        return carry

    lax.fori_loop(0, nt * (TK // TS), score_tile, 0)

    CT = 2 * TK

    @pl.when(jnp.logical_and(nt % 2 == 1, nt * TK < seq))
    def _():
        sc_scr[tile_rows(nt), :] = jnp.full((TK, QB), INT_MIN, I32)

    def count_key_ge(cand):
        def body(t, cnt):
            key = sc_scr[pl.ds(pl.multiple_of(t * CT, CT), CT), :]
            return cnt + _fold_rows(jnp.add, jnp.where(key >= cand, 1, 0))

        cnt = lax.fori_loop(0, (nt + 1) // 2, body, jnp.zeros((8, QB), I32))
        return jnp.sum(cnt, axis=0, keepdims=True)

    def bit_body(it, carry):
        prefix, cnt_at = carry
        cand = prefix + lax.shift_left(jnp.int32(1), 31 - it)
        cnt = count_key_ge(cand)
        ok = cnt >= topk
        return jnp.where(ok, cand, prefix), jnp.where(ok, cnt, cnt_at)

    thr, cnt_thr = lax.fori_loop(0, 32, bit_body, (jnp.full((1, QB), INT_MIN, I32), jnp.full((1, QB), TK, I32) * nt))

    j_scr[...] = jnp.full((1, QB), float(seq), F32)

    @pl.when(jnp.max(cnt_thr) > topk)
    def _():
        def count_keys(indicator):
            def body(t, cnt):
                k_pos = (t * TK + k_iota).astype(F32)
                return cnt + _fold_rows(jnp.add, indicator(sc_scr[tile_rows(t), :], k_pos))

            return jnp.sum(lax.fori_loop(0, nt, body, jnp.zeros((8, QB), F32)), axis=0, keepdims=True)

        need = float(topk) - count_keys(lambda key, k_pos: jnp.where(key > thr, 1.0, 0.0))
        count_tie_le = lambda cut: count_keys(
            lambda key, k_pos: jnp.where(key == thr, jnp.where(k_pos <= cut, 1.0, 0.0), 0.0))
        j_scr[...] = _tie_cutoff(count_tie_le, need, (1, QB), int(np.log2(seq)))

    cut = j_scr[...]

    def bias_tile(t, carry):
        key = sc_scr[tile_rows(t), :]
        k_pos = t * TK + k_iota
        taken = jnp.where(key > thr, 0.0,
                          jnp.where(key == thr, jnp.where(k_pos.astype(F32) <= cut, 0.0, NEG_BIG), NEG_BIG))
        sc_scr[tile_rows(t), :] = lax.bitcast_convert_type(jnp.where(k_pos <= q_pos, taken, NEG_BIG), I32)
        return carry

    lax.fori_loop(0, nt, bias_tile, 0)

    az = az_ref[...].astype(F32)
    lg_scr, p_scr = head_scr[:ATT_HEADS], head_scr[ATT_HEADS:]
    acc_scr[...] = jnp.zeros_like(acc_scr)

    def logits_stage(h, t, slot):
        g = h // GROUP
        kt = k_bf[tile_rows(t), g * HEAD_DIM:(g + 1) * HEAD_DIM]
        lg_scr[h][slot] = _dot_nt(kt, q_ref[:, h * HEAD_DIM:(h + 1) * HEAD_DIM])

    def softmax_stage(h, bias, slot, state):
        m, l8, _ = state
        lg = lg_scr[h][slot] + bias
        m_new = jnp.maximum(m, jnp.max(_fold_rows(jnp.maximum, lg), axis=0, keepdims=True))
        alpha = jnp.exp(m - m_new)
        p = jnp.exp(lg - m_new)
        p_scr[h][slot] = p.astype(BF16)
        return m_new, alpha * l8 + _fold_rows(jnp.add, p), alpha

    def value_stage(h, t, slot, state):
        g = h // GROUP
        vt = vt_scr[t][g * HEAD_DIM:(g + 1) * HEAD_DIM, :]
        acc_scr[h] = state[2] * acc_scr[h] + _dot(vt, p_scr[h][slot])

    def att_tile(t, state):
        slot = lax.rem(t, 2)
        bias = lax.bitcast_convert_type(sc_scr[tile_rows(t), :], F32)
        out = []
        for h in range(ATT_HEADS):
            value_stage(h, jnp.maximum(t - 1, 0), 1 - slot, state[h])
            out.append(softmax_stage(h, bias, slot, state[h]))
            logits_stage(h, jnp.minimum(t + 1, nt - 1), 1 - slot)
        return tuple(out)

    init = tuple((jnp.full((1, QB), NEG_BIG, F32), jnp.zeros((8, QB), F32), jnp.ones((1, QB), F32))
                 for _ in range(ATT_HEADS))
    for h in range(ATT_HEADS):
        p_scr[h][1] = jnp.zeros((TK, QB), BF16)
        logits_stage(h, 0, 0)
    fin = lax.fori_loop(0, nt, att_tile, init)
    for h in range(ATT_HEADS):
        value_stage(h, nt - 1, lax.rem(nt - 1, 2), fin[h])
    for h in range(ATT_HEADS):
        o = (acc_scr[h] / jnp.sum(fin[h][1], axis=0, keepdims=True)).T
        c0 = h * HEAD_DIM
        out_ref[:, c0:c0 + HEAD_DIM] = (o * _silu(az[:, c0:c0 + HEAD_DIM])).astype(out_ref.dtype)


def _dsa_prompt(p_main, p_small, batch, seq):
    QB, TK = DSA_QB, DSA_TK
    nb = seq // QB
    topk = min(TOPK_MAX, seq // 4)
    assert TK >= topk and seq % (2 * TK) == 0 and seq % QB == 0 and TK % DSA_TS == 0
    qrow = lambda b, i: b * nb + i
    return pl.pallas_call(
        functools.partial(_dsa_kernel, topk=topk),
        out_shape=jax.ShapeDtypeStruct((batch * seq, ATT_WIDTH), BF16),
        grid=(batch, nb),
        in_specs=[pl.BlockSpec((QB, ATT_WIDTH), lambda b, i: (qrow(b, i), 0)),
                  pl.BlockSpec((QB, 1024), lambda b, i: (qrow(b, i), 1)),
                  pl.BlockSpec((QB, ATT_WIDTH), lambda b, i: (qrow(b, i), 2)),
                  pl.BlockSpec((QB, LANES), lambda b, i: (qrow(b, i), 4)),
                  pl.BlockSpec((seq, KV_WIDTH), lambda b, i: (b, 0)),
                  pl.BlockSpec((seq, KV_WIDTH), lambda b, i: (b, 1)),
                  pl.BlockSpec((seq, LANES), lambda b, i: (b, 4))],
        out_specs=pl.BlockSpec((QB, ATT_WIDTH), lambda b, i: (qrow(b, i), 0)),
        scratch_shapes=[pltpu.VMEM((seq, KV_WIDTH), BF16),
                        pltpu.VMEM((seq // TK, KV_WIDTH, TK), BF16),
                        pltpu.VMEM((seq, LANES), BF16),
                        pltpu.VMEM((seq, LANES), BF16),
                        pltpu.VMEM((seq, QB), I32),
                        pltpu.VMEM((ATT_HEADS, HEAD_DIM, QB), F32),
                        pltpu.VMEM((1, QB), F32)]
                       + [pltpu.VMEM((2, TK, QB), F32)] * ATT_HEADS
                       + [pltpu.VMEM((2, TK, QB), BF16)] * ATT_HEADS,
        compiler_params=pltpu.CompilerParams(dimension_semantics=("parallel", "arbitrary"),
                                             vmem_limit_bytes=VMEM_LIMIT),
        name="dsa_prompt",
    )(p_main, p_main, p_main, p_small, p_small, p_small, p_small)


IDX_PAGES_PER_STEP = 32
KV_PAGES_PER_STEP = 16


def _page_spec(block, group, j):
    return pl.BlockSpec(block, lambda b, g, pt: (pt[b, g * group + j],) + (0,) * (len(block) - 1))


def _sample_scores_kernel(pt_ref, qi_ref, wb_ref, kin_ref, *refs):
    pages = refs[:IDX_PAGES_PER_STEP]
    sc_ref, snew_ref = refs[IDX_PAGES_PER_STEP:]
    qi = qi_ref[0]
    wb = wb_ref[0]
    rows = []
    for page in pages:
        s = _dot(qi, page[0].astype(BF16))
        rows.append(jnp.sum(jnp.maximum(s, 0.0) * wb, axis=0, keepdims=True) * IDX_SCALE)
    sc_ref[0] = jnp.concatenate(rows, axis=0)

    @pl.when(pl.program_id(1) == 0)
    def _():
        ki_new = kin_ref[0].astype(BF16).astype(F32)
        s = jnp.sum(qi.astype(F32) * ki_new, axis=1, keepdims=True)
        s_new = jnp.sum(jnp.maximum(s, 0.0) * wb[:, 0:1], axis=0, keepdims=True) * IDX_SCALE
        snew_ref[0] = jnp.broadcast_to(s_new, (8, LANES))


def _sample_scores(page_table, qi16, wb, ki_new, cache_ik_t):
    bd, n_pages = page_table.shape
    pg = IDX_PAGES_PER_STEP
    bmap = lambda b, g, pt: (b, 0, 0)
    return pl.pallas_call(
        _sample_scores_kernel,
        out_shape=(jax.ShapeDtypeStruct((bd, n_pages, PAGE), F32),
                   jax.ShapeDtypeStruct((bd, 8, LANES), F32)),
        grid_spec=pltpu.PrefetchScalarGridSpec(
            num_scalar_prefetch=1,
            grid=(bd, n_pages // pg),
            in_specs=[pl.BlockSpec((1, IDX_HEADS, IDX_DIM), bmap),
                      pl.BlockSpec((1, IDX_HEADS, LANES), bmap),
                      pl.BlockSpec((1, 1, IDX_DIM), bmap)]
                     + [_page_spec((1, IDX_DIM, PAGE), pg, j) for j in range(pg)],
            out_specs=(pl.BlockSpec((1, pg, PAGE), lambda b, g, pt: (b, g, 0)),
                       pl.BlockSpec((1, 8, LANES), bmap))),
        compiler_params=pltpu.CompilerParams(dimension_semantics=("parallel", "arbitrary"),
                                             vmem_limit_bytes=VMEM_LIMIT),
        name="sample_scores",
    )(page_table, qi16, wb, ki_new, *([cache_ik_t] * pg))


SEL_THR, SEL_NEXT, SEL_CUT = 0, 1, 2


def _sample_select_kernel(sc_ref, snew_ref, sel_ref, *, topk):
    x = sc_ref[...]
    bd, n_pages, _ = x.shape
    past = n_pages * PAGE
    s_new = snew_ref[:, 0:1, 0:1]

    def total(v):
        return jnp.sum(jnp.sum(v, axis=1, keepdims=True), axis=2, keepdims=True)

    def count_ge(cand):
        return total(jnp.where(x >= cand, 1.0, 0.0)) + jnp.where(s_new >= cand, 1.0, 0.0)

    key = _kth_largest_key(count_ge, (bd, 1, 1), topk)
    thr = _key_to_float(key)
    thr_next = _key_to_float(key + 1)
    need = float(topk) - count_ge(thr_next)
    pos = (lax.broadcasted_iota(I32, (1, n_pages, PAGE), 1) * PAGE
           + lax.broadcasted_iota(I32, (1, n_pages, PAGE), 2)).astype(F32)

    def count_tie_le(cut):
        tie = jnp.where(x >= thr_next, 0.0, jnp.where(x >= thr, jnp.where(pos <= cut, 1.0, 0.0), 0.0))
        tie_new = jnp.where(s_new >= thr_next, 0.0, jnp.where(s_new >= thr, jnp.where(float(past) <= cut, 1.0, 0.0), 0.0))
        return total(tie) + tie_new

    cut = _tie_cutoff(count_tie_le, need, (bd, 1, 1), int(np.log2(past)) + 1)
    row = lax.broadcasted_iota(I32, (bd, 8, LANES), 1)
    sel_ref[...] = jnp.where(row == SEL_THR, thr, jnp.where(row == SEL_NEXT, thr_next, cut))


def _sample_select(scores, s_new, topk):
    bd = scores.shape[0]
    return pl.pallas_call(
        functools.partial(_sample_select_kernel, topk=topk),
        out_shape=jax.ShapeDtypeStruct((bd, 8, LANES), F32),
        compiler_params=pltpu.CompilerParams(vmem_limit_bytes=VMEM_LIMIT),
        name="sample_select",
    )(scores, s_new)


def _taken_bias(score, pos, sel):
    thr, thr_next, cut = sel[SEL_THR:SEL_THR + 1, 0:1], sel[SEL_NEXT:SEL_NEXT + 1, 0:1], sel[SEL_CUT:SEL_CUT + 1, 0:1]
    return jnp.where(score >= thr_next, 0.0,
                     jnp.where(score >= thr, jnp.where(pos <= cut, 0.0, NEG_BIG), NEG_BIG))


def _sample_compact_kernel(pt_ref, sc_ref, sel_ref, rows_ref, nsel_ref, *, slots):
    n_pages = sc_ref.shape[1]
    pos = (lax.broadcasted_iota(I32, (n_pages, PAGE), 0) * PAGE
           + lax.broadcasted_iota(I32, (n_pages, PAGE), 1)).astype(F32)
    taken = jnp.where(_taken_bias(sc_ref[0], pos, sel_ref[0]) == 0.0, 1.0, 0.0)
    before = lax.broadcasted_iota(I32, (PAGE, PAGE), 0) < lax.broadcasted_iota(I32, (PAGE, PAGE), 1)
    in_page = _dot(taken, jnp.where(before, 1.0, 0.0))
    taken_t = taken.T
    rank_t = jnp.where(taken_t > 0.0, in_page.T, -1.0)
    page_tot = jnp.sum(taken_t, axis=0, keepdims=True)
    earlier = lax.broadcasted_iota(I32, (n_pages, n_pages), 0) < lax.broadcasted_iota(I32, (n_pages, n_pages), 1)
    first_slot = _dot(jnp.broadcast_to(page_tot, (8, n_pages)),
                      jnp.where(earlier, 1.0, 0.0))[0:1]
    phys_t = (pt_ref[0] * PAGE + lax.broadcasted_iota(I32, (PAGE, n_pages), 0)).astype(F32)
    slot = lax.broadcasted_iota(I32, (slots, n_pages), 0).astype(F32)

    def rank_body(r, acc):
        r = r.astype(F32)
        row = jnp.sum(jnp.where(rank_t == r, phys_t, 0.0), axis=0, keepdims=True)
        target = jnp.where(r < page_tot, first_slot + r, -1.0)
        return acc + jnp.where(slot == target, row, 0.0)

    most = jnp.max(page_tot).astype(I32)
    acc = lax.fori_loop(0, most, rank_body, jnp.zeros((slots, n_pages), F32))
    rows_ref[0] = jnp.broadcast_to(jnp.sum(acc, axis=1, keepdims=True), (slots, LANES))
    nsel_ref[0] = jnp.broadcast_to(jnp.sum(page_tot, axis=1, keepdims=True), (8, LANES))


def _sample_compact(page_table, scores, sel, slots):
    bd, n_pages = page_table.shape
    assert n_pages % LANES == 0
    bmap = lambda b: (b, 0, 0)
    return pl.pallas_call(
        functools.partial(_sample_compact_kernel, slots=slots),
        out_shape=(jax.ShapeDtypeStruct((bd, slots, LANES), F32), jax.ShapeDtypeStruct((bd, 8, LANES), F32)),
        grid=(bd,),
        in_specs=[pl.BlockSpec((1, 1, n_pages), bmap), pl.BlockSpec((1, n_pages, PAGE), bmap),
                  pl.BlockSpec((1, 8, LANES), bmap)],
        out_specs=(pl.BlockSpec((1, slots, LANES), bmap), pl.BlockSpec((1, 8, LANES), bmap)),
        compiler_params=pltpu.CompilerParams(dimension_semantics=("parallel",), vmem_limit_bytes=VMEM_LIMIT),
        name="sample_compact",
    )(page_table.reshape(bd, 1, n_pages), scores, sel)


SC_GATHER_CHUNK = 128


def _gather_rows(table_k, table_v, idx):
    info = plsc.get_sparse_core_info()
    n_workers = info.num_cores * info.num_subcores
    n_idx = idx.shape[0]
    per_worker = n_idx // n_workers
    assert per_worker * n_workers == n_idx and per_worker % SC_GATHER_CHUNK == 0
    mesh = plsc.VectorSubcoreMesh(core_axis_name="c", subcore_axis_name="s")
    out = jax.ShapeDtypeStruct((n_idx, table_k.shape[1]), table_k.dtype)

    @functools.partial(
        pl.kernel, mesh=mesh, out_type=(out, out),
        scratch_types=[pltpu.VMEM((SC_GATHER_CHUNK,), I32),
                       pltpu.VMEM((SC_GATHER_CHUNK, table_k.shape[1]), table_k.dtype),
                       pltpu.VMEM((SC_GATHER_CHUNK, table_v.shape[1]), table_v.dtype),
                       pltpu.SemaphoreType.DMA, pltpu.SemaphoreType.DMA])
    def gather(tk_hbm, tv_hbm, idx_hbm, ok_hbm, ov_hbm, idx_v, rk_v, rv_v, sem_k, sem_v):
        worker = lax.axis_index("s") * info.num_cores + lax.axis_index("c")

        @pl.loop(0, per_worker // SC_GATHER_CHUNK)
        def _(j):
            base = worker * per_worker + j * SC_GATHER_CHUNK
            pltpu.sync_copy(idx_hbm.at[pl.ds(base, SC_GATHER_CHUNK)], idx_v)
            copy_k = pltpu.async_copy(tk_hbm.at[idx_v], rk_v, sem_k)
            copy_v = pltpu.async_copy(tv_hbm.at[idx_v], rv_v, sem_v)
            copy_k.wait()
            copy_v.wait()
            pltpu.sync_copy(rk_v, ok_hbm.at[pl.ds(base, SC_GATHER_CHUNK)])
            pltpu.sync_copy(rv_v, ov_hbm.at[pl.ds(base, SC_GATHER_CHUNK)])

    return gather(table_k, table_v, idx)


def _sample_attend_kernel(q_ref, az_ref, kn_ref, vn_ref, snew_ref, sel_ref, nsel_ref, k_ref, v_ref, out_ref, *, past):
    width = k_ref.shape[1]
    q = q_ref[0]
    col = lax.broadcasted_iota(I32, (ATT_HEADS, width), 1)
    head = lax.broadcasted_iota(I32, (ATT_HEADS, width), 0)
    own_head = col % KV_HEADS == head // GROUP
    filled = (col // KV_HEADS).astype(F32) < nsel_ref[0][0:1, 0:1]
    lg = _dot_nt(q, k_ref[0].astype(BF16))
    lg = jnp.where(own_head, jnp.where(filled, lg, NEG_BIG), NEG_BIG)
    lg_new = (jnp.sum(q.astype(F32) * kn_ref[0].astype(BF16).astype(F32), axis=1, keepdims=True)
              + _taken_bias(snew_ref[0][0:1, 0:1], float(past), sel_ref[0]))
    m = jnp.maximum(jnp.max(lg, axis=1, keepdims=True), lg_new)
    p = jnp.exp(lg - m)
    p_new = jnp.exp(lg_new - m)
    l = jnp.sum(p, axis=1, keepdims=True) + p_new
    acc = _dot(p.astype(BF16), v_ref[0].astype(BF16)) + p_new * vn_ref[0].astype(BF16).astype(F32)
    out_ref[0] = (acc / l) * _silu(az_ref[0])


def _sample_attend(q8, az8, k_new8, v_new8, s_new, sel, n_sel, k_sel, v_sel, past):
    bd, width, _ = k_sel.shape
    bmap = lambda b: (b, 0, 0)
    head_tile = pl.BlockSpec((1, ATT_HEADS, HEAD_DIM), bmap)
    par_tile = pl.BlockSpec((1, 8, LANES), bmap)
    rows_tile = pl.BlockSpec((1, width, HEAD_DIM), bmap)
    return pl.pallas_call(
        functools.partial(_sample_attend_kernel, past=past),
        out_shape=jax.ShapeDtypeStruct((bd, ATT_HEADS, HEAD_DIM), F32),
        grid=(bd,),
        in_specs=[head_tile, head_tile, head_tile, head_tile, par_tile, par_tile, par_tile, rows_tile, rows_tile],
        out_specs=head_tile,
        compiler_params=pltpu.CompilerParams(dimension_semantics=("parallel",), vmem_limit_bytes=VMEM_LIMIT),
        name="sample_attend",
    )(q8, az8, k_new8, v_new8, s_new, sel, n_sel, k_sel, v_sel)


def _mlstm_step_kernel(bi_ref, bf_ref, q_ref, k_ref, v_ref, mo_ref, mz_ref, misc_ref, nw_ref,
                       c_ref, n_ref, m_ref, out_ref, c_out, n_out, m_out):
    misc = misc_ref[0]
    eye = lax.broadcasted_iota(I32, (ML_V, ML_V), 0) == lax.broadcasted_iota(I32, (ML_V, ML_V), 1)
    for h in range(ML_HEADS):
        ig = misc[:, MISC_IG + h:MISC_IG + h + 1] + bi_ref[h]
        lf = _log_sigmoid(misc[:, MISC_FG + h:MISC_FG + h + 1] + bf_ref[h])
        m_prev = m_ref[0, h][:, 0:1]
        log_a = lf + m_prev
        m_t = jnp.maximum(log_a, ig)
        d = jnp.exp(ig - m_t)
        a = jnp.exp(log_a - m_t)
        q = q_ref[0][:, h * ML_QK:(h + 1) * ML_QK]
        k = k_ref[0][:, h * ML_QK:(h + 1) * ML_QK]
        v = v_ref[0][:, h * ML_V:(h + 1) * ML_V]
        v_col = jnp.sum(jnp.where(eye, v, 0.0), axis=1, keepdims=True)
        c = c_ref[0, h]
        n = n_ref[0, h]
        s = jnp.sum(q * k, axis=1, keepdims=True) * d
        num = a * jnp.sum(c * q, axis=1, keepdims=True) + s * v_col
        den = a * jnp.sum(n * q, axis=1, keepdims=True) + s
        h_col = num / jnp.maximum(jnp.abs(den), jnp.exp(-m_t))
        c_out[0, h] = a * c + (d * v_col) * k
        n_out[0, h] = a * n + d * k
        m_out[0, h] = jnp.broadcast_to(m_t, (1, LANES))

        h_row = jnp.sum(jnp.where(eye, h_col, 0.0), axis=0, keepdims=True)
        ms = jnp.mean(h_row * h_row, axis=1, keepdims=True)
        hn = h_row * lax.rsqrt(ms + RMS_EPS) * nw_ref[:, h * ML_V:(h + 1) * ML_V]
        gate = _sigmoid(mo_ref[0][:, h * ML_V:(h + 1) * ML_V]) * _silu(mz_ref[0][:, h * ML_V:(h + 1) * ML_V])
        out_ref[0, :, h * ML_V:(h + 1) * ML_V] = (hn * gate).astype(out_ref.dtype)


def _mlstm_step(ps_main, ps_small, b_i, b_f, ml_norm_w, state_c, state_n, state_m):
    bd = ps_main.shape[0]
    col = lambda j: (lambda b: (b, 0, j))
    st4 = lambda b: (b, 0, 0, 0)
    return pl.pallas_call(
        _mlstm_step_kernel,
        out_shape=(jax.ShapeDtypeStruct((bd, 1, ML_WIDTH), BF16),
                   jax.ShapeDtypeStruct(state_c.shape, F32),
                   jax.ShapeDtypeStruct(state_n.shape, F32),
                   jax.ShapeDtypeStruct(state_m.shape, F32)),
        grid=(bd,),
        in_specs=[pl.BlockSpec(memory_space=pltpu.SMEM),
                  pl.BlockSpec(memory_space=pltpu.SMEM),
                  pl.BlockSpec((1, 1, 512), col(12)),
                  pl.BlockSpec((1, 1, 512), col(13)),
                  pl.BlockSpec((1, 1, 1024), col(3)),
                  pl.BlockSpec((1, 1, 1024), col(4)),
                  pl.BlockSpec((1, 1, 1024), col(5)),
                  pl.BlockSpec((1, 1, LANES), col(4)),
                  pl.BlockSpec((1, ML_WIDTH), lambda b: (0, 0)),
                  pl.BlockSpec((1, ML_HEADS, ML_V, ML_QK), st4),
                  pl.BlockSpec((1, ML_HEADS, 1, ML_QK), st4),
                  pl.BlockSpec((1, ML_HEADS, 1, LANES), st4)],
        out_specs=(pl.BlockSpec((1, 1, ML_WIDTH), lambda b: (b, 0, 0)),
                   pl.BlockSpec((1, ML_HEADS, ML_V, ML_QK), st4),
                   pl.BlockSpec((1, ML_HEADS, 1, ML_QK), st4),
                   pl.BlockSpec((1, ML_HEADS, 1, LANES), st4)),
        compiler_params=pltpu.CompilerParams(dimension_semantics=("parallel",),
                                             vmem_limit_bytes=VMEM_LIMIT),
        name="mlstm_step",
    )(b_i, b_f, ps_main, ps_main, ps_main, ps_main, ps_main, ps_small, ml_norm_w.reshape(1, ML_WIDTH),
      state_c, state_n, state_m)


def _split_weights(w_in):
    w_t = jnp.swapaxes(w_in, 0, 1)
    offs = np.cumsum((0,) + IN_SIZES)
    aq, ak, av, iq, ik, iw, az, mq, mk, mv, mi, mf, mo, mz = (w_t[offs[j]:offs[j + 1]] for j in range(len(IN_SIZES)))
    w_main = jnp.concatenate([aq * ATT_SCALE, iq, az, mv, mo, mz, mq, mk * ML_QK ** -0.5], axis=0)
    pad = jnp.zeros((LANES - IDX_DIM - IDX_HEADS - 2 * ML_HEADS, w_t.shape[1]), w_t.dtype)
    w_small = jnp.concatenate([ak, av, ik, iw, mi, mf, pad], axis=0)
    assert w_main.shape[0] == MAIN_W and w_small.shape[0] == SMALL_W
    return w_main.astype(BF16), w_small.astype(BF16)


def kernel(x_prompt, x_sample, cache_k, cache_v, cache_idx_k, state_C, state_n, state_m, page_table,
           norm_w, w_in, b_igate, b_fgate, ml_norm_w, w_out, final_norm_w):
    depth = w_in.shape[0]
    batch, seq, d = x_prompt.shape
    bd, dec_seq, _ = x_sample.shape
    assert depth == 1 and dec_seq == 1 and d == D_MODEL
    n_pages = page_table.shape[1]

    w_main, w_small = _split_weights(w_in[0])
    w_o = w_out[0].astype(BF16)
    w_o_att, w_o_ml = w_o[:ATT_WIDTH], w_o[ATT_WIDTH:]

    xp = x_prompt.reshape(batch * seq, d)
    p_main = _project(xp, norm_w[0], w_main, BF16, 1024, 512, "proj_main")
    p_small = _project(xp, norm_w[0], w_small, F32, 1024, SMALL_W, "proj_small")
    att = _dsa_prompt(p_main, p_small, batch, seq)
    ml, ct, m_p = _mlstm_prompt(p_main, p_small, b_igate[0], b_fgate[0], ml_norm_w[0], batch, seq)
    y_prompt = _out_project(xp, att, ml, w_o_att, w_o_ml, final_norm_w, 256, "out_prompt").reshape(batch, seq, d)
    k_prompt = p_small[:, :KV_WIDTH].reshape(1, batch, seq, KV_HEADS, HEAD_DIM)
    v_prompt = p_small[:, KV_WIDTH:2 * KV_WIDTH].reshape(1, batch, seq, KV_HEADS, HEAD_DIM)
    ik_prompt = p_small[:, 2 * KV_WIDTH:2 * KV_WIDTH + IDX_DIM].reshape(1, batch, seq, IDX_DIM)
    c_prompt = jnp.swapaxes(ct[..., :ML_V], -1, -2)[None]
    n_prompt = ct[..., ML_V][None]
    m_prompt = m_p[:, :, 0, 0][None]

    xs = x_sample.reshape(bd, d)
    ps_main = _project(xs, norm_w[0], w_main, F32, bd, 512, "proj_main_s")
    ps_small = _project(xs, norm_w[0], w_small, F32, bd, SMALL_W, "proj_small_s")
    q8 = ps_main[:, :ATT_WIDTH].reshape(bd, ATT_HEADS, HEAD_DIM).astype(BF16)
    qi16 = ps_main[:, 1024:2048].reshape(bd, IDX_HEADS, IDX_DIM).astype(BF16)
    az8 = ps_main[:, 2048:3072].reshape(bd, ATT_HEADS, HEAD_DIM)
    misc_s = ps_small[:, 2 * KV_WIDTH:]
    ki_new = misc_s[:, :IDX_DIM].reshape(bd, 1, IDX_DIM)
    wb = jnp.broadcast_to(misc_s[:, MISC_WI:MISC_WI + IDX_HEADS, None], (bd, IDX_HEADS, LANES))
    k_new8 = jnp.repeat(ps_small[:, :KV_WIDTH].reshape(bd, KV_HEADS, HEAD_DIM), GROUP, axis=1)
    v_new8 = jnp.repeat(ps_small[:, KV_WIDTH:2 * KV_WIDTH].reshape(bd, KV_HEADS, HEAD_DIM), GROUP, axis=1)
    scores, s_new = _sample_scores(page_table, qi16, wb, ki_new, jnp.swapaxes(cache_idx_k[0], 1, 2))
    assert cache_k.shape[1] * PAGE < 2 ** 24
    topk_s = min(TOPK_MAX, (n_pages * PAGE + 1) // 4)
    sel = _sample_select(scores, s_new, topk_s)
    key_rows, n_sel = _sample_compact(page_table, scores, sel, topk_s)
    idx = (key_rows[:, :, :1].astype(I32) * KV_HEADS + jnp.arange(KV_HEADS, dtype=I32)).reshape(-1)
    k_sel, v_sel = _gather_rows(cache_k.reshape(-1, HEAD_DIM), cache_v.reshape(-1, HEAD_DIM), idx)
    att_s = _sample_attend(q8, az8, k_new8, v_new8, s_new, sel, n_sel,
                           k_sel.reshape(bd, topk_s * KV_HEADS, HEAD_DIM),
                           v_sel.reshape(bd, topk_s * KV_HEADS, HEAD_DIM), n_pages * PAGE)
    ml_s, c_s, n_s, m_s = _mlstm_step(
        ps_main.reshape(bd, 1, MAIN_W), ps_small.reshape(bd, 1, SMALL_W), b_igate[0], b_fgate[0], ml_norm_w[0],
        state_C[0], state_n[0].reshape(bd, ML_HEADS, 1, ML_QK),
        jnp.broadcast_to(state_m[0][:, :, None, None], (bd, ML_HEADS, 1, LANES)))
    y_sample = _out_project(xs, att_s.reshape(bd, ATT_WIDTH).astype(BF16), ml_s.reshape(bd, ML_WIDTH),
                            w_o_att, w_o_ml, final_norm_w, bd, "out_sample").reshape(bd, 1, d)
    k_sample = ps_small[:, :KV_WIDTH].reshape(1, bd, 1, KV_HEADS, HEAD_DIM)
    v_sample = ps_small[:, KV_WIDTH:2 * KV_WIDTH].reshape(1, bd, 1, KV_HEADS, HEAD_DIM)
    ik_sample = misc_s[:, :IDX_DIM].reshape(1, bd, 1, IDX_DIM)

    return (y_prompt, y_sample, k_prompt, v_prompt, ik_prompt, c_prompt, n_prompt, m_prompt,
            k_sample, v_sample, ik_sample, c_s[None], n_s.reshape(1, bd, ML_HEADS, ML_QK), m_s[:, :, 0, 0][None])
```

```python
import functools

import jax
import jax.numpy as jnp
import numpy as np
from jax import lax
from jax.experimental import pallas as pl
from jax.experimental.pallas import tpu as pltpu
from jax.experimental.pallas import tpu_sc as plsc

F32 = jnp.float32
BF16 = jnp.bfloat16
I32 = jnp.int32

D_MODEL = 2048
PAGE = 128
ATT_HEADS = 8
KV_HEADS = 2
HEAD_DIM = 128
GROUP = ATT_HEADS // KV_HEADS
ATT_WIDTH = ATT_HEADS * HEAD_DIM
KV_WIDTH = KV_HEADS * HEAD_DIM
ATT_SCALE = HEAD_DIM ** -0.5
IDX_HEADS = 16
IDX_DIM = 64
IDX_SCALE = (IDX_HEADS * IDX_DIM) ** -0.5
TOPK_MAX = 256
ML_HEADS = 4
ML_QK = 128
ML_V = 256
ML_WIDTH = ML_HEADS * ML_V
RMS_EPS = 1e-6
IN_SIZES = (ATT_WIDTH, KV_WIDTH, KV_WIDTH, IDX_HEADS * IDX_DIM, IDX_DIM, IDX_HEADS, ATT_WIDTH,
            ML_HEADS * ML_QK, ML_HEADS * ML_QK, ML_WIDTH, ML_HEADS, ML_HEADS, ML_WIDTH, ML_WIDTH)

LANES = 128
NEG_BIG = -1e30
VMEM_LIMIT = 56 * 1024 * 1024

MAIN_W = 7168
SMALL_W = 640
MISC_WI = IDX_DIM
MISC_IG = IDX_DIM + IDX_HEADS
MISC_FG = MISC_IG + ML_HEADS


def _dot(a, b):
    return jnp.dot(a, b, preferred_element_type=F32)


def _dot_nt(a, b):
    return lax.dot_general(a, b, (((1,), (1,)), ((), ())), preferred_element_type=F32)


def _tree_reduce(op, parts):
    parts = list(parts)
    while len(parts) > 1:
        paired = [op(parts[j], parts[j + 1]) for j in range(0, len(parts) - 1, 2)]
        parts = paired + parts[len(parts) - len(parts) % 2:]
    return parts[0]


def _fold_rows(op, x):
    return _tree_reduce(op, [x[r:r + 8] for r in range(0, x.shape[0], 8)])


def _proj_kernel(x_ref, nw_ref, w_ref, o_ref, h_scr):
    @pl.when(pl.program_id(1) == 0)
    def _():
        x = x_ref[...]
        ms = jnp.mean(x * x, axis=-1, keepdims=True)
        h_scr[...] = (x * lax.rsqrt(ms + RMS_EPS) * nw_ref[...]).astype(BF16)

    o_ref[...] = _dot_nt(h_scr[...], w_ref[...]).astype(o_ref.dtype)


def _project(x2d, norm_w, w_t, out_dtype, tm, tn, name):
    m, d = x2d.shape
    n = w_t.shape[0]
    return pl.pallas_call(
        _proj_kernel,
        out_shape=jax.ShapeDtypeStruct((m, n), out_dtype),
        grid=(m // tm, n // tn),
        in_specs=[pl.BlockSpec((tm, d), lambda i, j: (i, 0)),
                  pl.BlockSpec((1, d), lambda i, j: (0, 0)),
                  pl.BlockSpec((tn, d), lambda i, j: (j, 0))],
        out_specs=pl.BlockSpec((tm, tn), lambda i, j: (i, j)),
        scratch_shapes=[pltpu.VMEM((tm, d), BF16)],
        compiler_params=pltpu.CompilerParams(dimension_semantics=("parallel", "arbitrary"),
                                             vmem_limit_bytes=VMEM_LIMIT),
        name=name,
    )(x2d, norm_w.reshape(1, d), w_t)


def _out_kernel(x_ref, a_ref, m_ref, wa_ref, wm_ref, fw_ref, o_ref):
    y = x_ref[...] + _dot(a_ref[...], wa_ref[...]) + _dot(m_ref[...], wm_ref[...])
    ms = jnp.mean(y * y, axis=-1, keepdims=True)
    o_ref[...] = y * lax.rsqrt(ms + RMS_EPS) * fw_ref[...]


def _out_project(x2d, a, mo, w_att, w_ml, final_w, tm, name):
    m, d = x2d.shape
    return pl.pallas_call(
        _out_kernel,
        out_shape=jax.ShapeDtypeStruct((m, d), F32),
        grid=(m // tm,),
        in_specs=[pl.BlockSpec((tm, d), lambda i: (i, 0)),
                  pl.BlockSpec((tm, ATT_WIDTH), lambda i: (i, 0)),
                  pl.BlockSpec((tm, ML_WIDTH), lambda i: (i, 0)),
                  pl.BlockSpec((ATT_WIDTH, d), lambda i: (0, 0)),
                  pl.BlockSpec((ML_WIDTH, d), lambda i: (0, 0)),
                  pl.BlockSpec((1, d), lambda i: (0, 0))],
        out_specs=pl.BlockSpec((tm, d), lambda i: (i, 0)),
        compiler_params=pltpu.CompilerParams(dimension_semantics=("parallel",),
                                             vmem_limit_bytes=VMEM_LIMIT),
        name=name,
    )(x2d, a, mo, w_att, w_ml, final_w.reshape(1, d))


def _log_sigmoid(x):
    return jnp.minimum(x, 0.0) - jnp.log(1.0 + jnp.exp(-jnp.abs(x)))


def _sigmoid(x):
    return 1.0 / (1.0 + jnp.exp(-x))


def _silu(x):
    return x * _sigmoid(x)


ML_CHUNK = 256
ML_AUG = ML_V + LANES


def _mlstm_kernel(bi_ref, bf_ref, q_ref, k_ref, v_ref, mo_ref, mz_ref, misc_ref, nw_ref,
                  out_ref, ct_ref, m_ref):
    L = ML_CHUNK

    @pl.when(pl.program_id(1) == 0)
    def _():
        ct_ref[...] = jnp.zeros_like(ct_ref)
        m_ref[...] = jnp.zeros_like(m_ref)

    misc = misc_ref[...]
    misc_t = misc.T
    t_idx = lax.broadcasted_iota(I32, (L, L), 0)
    s_idx = lax.broadcasted_iota(I32, (L, L), 1)
    causal = s_idx <= t_idx
    ones_col = jnp.where(lax.broadcasted_iota(I32, (L, LANES), 1) == 0, 1.0, 0.0).astype(BF16)

    for h in range(ML_HEADS):
        ig_row = misc_t[MISC_IG + h:MISC_IG + h + 1, :] + bi_ref[h]
        lf_row = _log_sigmoid(misc_t[MISC_FG + h:MISC_FG + h + 1, :] + bf_ref[h])
        lf_col = _log_sigmoid(misc[:, MISC_FG + h:MISC_FG + h + 1] + bf_ref[h])
        b_col = jnp.sum(jnp.where(causal, lf_row, 0.0), axis=1, keepdims=True)
        b_row = jnp.sum(jnp.where(t_idx <= s_idx, lf_col, 0.0), axis=0, keepdims=True)
        m_prev = m_ref[0, h][0:1, 0:1]
        log_d = jnp.where(causal, b_col - b_row + ig_row, -jnp.inf)
        log_a = b_col + m_prev
        m_t = jnp.maximum(log_a, jnp.max(log_d, axis=1, keepdims=True))
        d = jnp.exp(log_d - m_t)
        a = jnp.exp(log_a - m_t)

        q = q_ref[:, h * ML_QK:(h + 1) * ML_QK]
        k = k_ref[:, h * ML_QK:(h + 1) * ML_QK]
        v_aug = jnp.concatenate([v_ref[:, h * ML_V:(h + 1) * ML_V], ones_col], axis=1)
        s = (_dot_nt(q, k) * d).astype(BF16)
        ct = ct_ref[0, h]
        num_aug = a * _dot(q, ct.astype(BF16)) + _dot(s, v_aug)
        den = num_aug[:, ML_V:ML_V + 1]
        hh = num_aug[:, :ML_V] / jnp.maximum(jnp.abs(den), jnp.exp(-m_t))

        m_new = m_t[L - 1:L, :]
        a_end = a[L - 1:L, :]
        w_row = jnp.exp(b_row[:, L - 1:L] - b_row + ig_row - m_new)
        ktw = (k.astype(F32).T * w_row).astype(BF16)
        ct_new = a_end * ct + _dot(ktw, v_aug)
        ct_ref[0, h] = ct_new
        m_ref[0, h] = jnp.broadcast_to(m_new, (8, LANES))

        ms = jnp.mean(hh * hh, axis=1, keepdims=True)
        hn = hh * lax.rsqrt(ms + RMS_EPS) * nw_ref[:, h * ML_V:(h + 1) * ML_V]
        gate = _sigmoid(mo_ref[:, h * ML_V:(h + 1) * ML_V].astype(F32)) * _silu(mz_ref[:, h * ML_V:(h + 1) * ML_V].astype(F32))
        out_ref[:, h * ML_V:(h + 1) * ML_V] = (hn * gate).astype(out_ref.dtype)


def _mlstm_prompt(p_main, p_small, b_i, b_f, ml_norm_w, batch, seq):
    L = ML_CHUNK
    nc = seq // L
    row = lambda b, c: b * nc + c
    return pl.pallas_call(
        _mlstm_kernel,
        out_shape=(jax.ShapeDtypeStruct((batch * seq, ML_WIDTH), BF16),
                   jax.ShapeDtypeStruct((batch, ML_HEADS, ML_QK, ML_AUG), F32),
                   jax.ShapeDtypeStruct((batch, ML_HEADS, 8, LANES), F32)),
        grid=(batch, nc),
        in_specs=[pl.BlockSpec(memory_space=pltpu.SMEM),
                  pl.BlockSpec(memory_space=pltpu.SMEM),
                  pl.BlockSpec((L, 512), lambda b, c: (row(b, c), 12)),
                  pl.BlockSpec((L, 512), lambda b, c: (row(b, c), 13)),
                  pl.BlockSpec((L, 1024), lambda b, c: (row(b, c), 3)),
                  pl.BlockSpec((L, 1024), lambda b, c: (row(b, c), 4)),
                  pl.BlockSpec((L, 1024), lambda b, c: (row(b, c), 5)),
                  pl.BlockSpec((L, LANES), lambda b, c: (row(b, c), 4)),
                  pl.BlockSpec((1, ML_WIDTH), lambda b, c: (0, 0))],
        out_specs=(pl.BlockSpec((L, ML_WIDTH), lambda b, c: (row(b, c), 0)),
                   pl.BlockSpec((1, ML_HEADS, ML_QK, ML_AUG), lambda b, c: (b, 0, 0, 0)),
                   pl.BlockSpec((1, ML_HEADS, 8, LANES), lambda b, c: (b, 0, 0, 0))),
        compiler_params=pltpu.CompilerParams(dimension_semantics=("parallel", "arbitrary"),
                                             vmem_limit_bytes=VMEM_LIMIT),
        name="mlstm_prompt",
    )(b_i, b_f, p_main, p_main, p_main, p_main, p_main, p_small, ml_norm_w.reshape(1, ML_WIDTH))


INT_MIN = -2 ** 31


def _float_to_key(x):
    bits = lax.bitcast_convert_type(x, I32)
    return jnp.where(bits >= 0, bits, bits ^ jnp.int32(0x7FFFFFFF))


def _key_to_float(key):
    bits = jnp.where(key >= 0, key, key ^ jnp.int32(0x7FFFFFFF))
    return lax.bitcast_convert_type(bits, F32)


KEY_NEG_INF = INT_MIN + 0x7FFFFF


def _kth_largest_key(count_ge, shape, k):
    def bit_body(it, prefix):
        cand = prefix + lax.shift_left(jnp.int32(1), 31 - it)
        cand_f = _key_to_float(jnp.maximum(cand, KEY_NEG_INF))
        return jnp.where(count_ge(cand_f) >= float(k), cand, prefix)

    return lax.fori_loop(0, 32, bit_body, jnp.full(shape, INT_MIN, I32))


def _tie_cutoff(count_tie_le, need, shape, index_bits):
    def bit_body(it, lo):
        cand = lo + lax.shift_left(jnp.int32(1), index_bits - 1 - it).astype(F32)
        return jnp.where(count_tie_le(cand) < need, cand, lo)

    return lax.fori_loop(0, index_bits, bit_body, jnp.full(shape, -1.0, F32)) + 1.0


DSA_QB = 128
DSA_TK = 512
DSA_TS = 256
DSA_TA = 256


def _dsa_kernel(q_ref, qi_ref, az_ref, miscq_ref, k_ref, v_ref, misck_ref, out_ref,
                k_bf, vt_scr, ki_lo, ki_hi, sc_scr, acc_scr, j_scr, *head_scr, topk):
    i = pl.program_id(1)
    seq = k_ref.shape[0]
    QB, TK, TS, TA = DSA_QB, DSA_TK, DSA_TS, DSA_TA
    nt = (i * QB + QB + TK - 1) // TK

    @pl.when(i == 0)
    def _():
        k_bf[...] = k_ref[...].astype(BF16)

        def v_tile(a, carry):
            vt_scr[a] = v_ref[pl.ds(pl.multiple_of(a * TA, TA), TA), :].T.astype(BF16)
            return carry

        lax.fori_loop(0, seq // TA, v_tile, 0)
        lane = lax.broadcasted_iota(I32, (seq, LANES), 1)
        lo = jnp.where(lane < IDX_DIM, misck_ref[...], 0.0)
        ki_lo[...] = lo.astype(BF16)
        ki_hi[...] = pltpu.roll(lo, IDX_DIM, axis=1).astype(BF16)

    k_iota = lax.broadcasted_iota(I32, (TK, QB), 0)
    q_pos = i * QB + lax.broadcasted_iota(I32, (TK, QB), 1)
    w_t = miscq_ref[...].T

    def tile_rows(t):
        return pl.ds(pl.multiple_of(t * TK, TK), TK)

    def score_tile(t, carry):
        rows = pl.ds(pl.multiple_of(t * TS, TS), TS)
        klo = ki_lo[rows, :]
        khi = ki_hi[rows, :]
        acc = jnp.zeros((TS, QB), F32)
        for p in range(IDX_HEADS // 2):
            qp = qi_ref[:, p * LANES:(p + 1) * LANES]
            w0 = w_t[MISC_WI + 2 * p:MISC_WI + 2 * p + 1, :]
            w1 = w_t[MISC_WI + 2 * p + 1:MISC_WI + 2 * p + 2, :]
            acc = acc + w0 * jnp.maximum(_dot_nt(klo, qp), 0.0) + w1 * jnp.maximum(_dot_nt(khi, qp), 0.0)
        k_pos = t * TS + lax.broadcasted_iota(I32, (TS, QB), 0)
        q_pos_s = i * QB + lax.broadcasted_iota(I32, (TS, QB), 1)
        score = jnp.where(k_pos <= q_pos_s, acc * IDX_SCALE + 0.0, -jnp.inf)
        sc_scr[rows, :] = _float_to_key(score)
        return carry

    def score_tiles(t, carry):
        for sub in range(TK // TS):
            score_tile(t * (TK // TS) + sub, carry)
        return carry

    lax.fori_loop(0, nt, score_tiles, 0)

    def count_key_ge(cand):
        def body(t, cnt):
            return cnt + _fold_rows(jnp.add, jnp.where(sc_scr[tile_rows(t), :] >= cand, 1, 0))

        cnt = lax.fori_loop(0, nt, body, jnp.zeros((8, QB), I32))
        return jnp.sum(cnt, axis=0, keepdims=True)

    def bit_body(it, carry):
        prefix, cnt_at = carry
        cand = prefix + lax.shift_left(jnp.int32(1), 31 - it)
        cnt = count_key_ge(cand)
        ok = cnt >= topk
        return jnp.where(ok, cand, prefix), jnp.where(ok, cnt, cnt_at)

    thr, cnt_thr = lax.fori_loop(0, 32, bit_body, (jnp.full((1, QB), INT_MIN, I32), jnp.full((1, QB), TK, I32) * nt))

    j_scr[...] = jnp.full((1, QB), float(seq), F32)

    @pl.when(jnp.max(cnt_thr) > topk)
    def _():
        def count_keys(indicator):
            def body(t, cnt):
                k_pos = (t * TK + k_iota).astype(F32)
                return cnt + _fold_rows(jnp.add, indicator(sc_scr[tile_rows(t), :], k_pos))

            return jnp.sum(lax.fori_loop(0, nt, body, jnp.zeros((8, QB), F32)), axis=0, keepdims=True)

        need = float(topk) - count_keys(lambda key, k_pos: jnp.where(key > thr, 1.0, 0.0))
        count_tie_le = lambda cut: count_keys(
            lambda key, k_pos: jnp.where(key == thr, jnp.where(k_pos <= cut, 1.0, 0.0), 0.0))
        j_scr[...] = _tie_cutoff(count_tie_le, need, (1, QB), int(np.log2(seq)))

    cut = j_scr[...]

    def bias_tile(t, carry):
        key = sc_scr[tile_rows(t), :]
        k_pos = t * TK + k_iota
        taken = jnp.where(key > thr, 0.0,
                          jnp.where(key == thr, jnp.where(k_pos.astype(F32) <= cut, 0.0, NEG_BIG), NEG_BIG))
        sc_scr[tile_rows(t), :] = lax.bitcast_convert_type(jnp.where(k_pos <= q_pos, taken, NEG_BIG), I32)
        return carry

    lax.fori_loop(0, nt, bias_tile, 0)

    assert TK == 2 * TA
    az = az_ref[...].astype(F32)
    lg_scr, p_scr = head_scr[:ATT_HEADS], head_scr[ATT_HEADS:]
    acc_scr[...] = jnp.zeros_like(acc_scr)
    last = 2 * nt - 1

    def sub_rows(a):
        return pl.ds(pl.multiple_of(a * TA, TA), TA)

    def logits_stage(h, a, slot):
        g = h // GROUP
        kt = k_bf[sub_rows(a), g * HEAD_DIM:(g + 1) * HEAD_DIM]
        lg_scr[h][slot] = _dot_nt(kt, q_ref[:, h * HEAD_DIM:(h + 1) * HEAD_DIM])

    def softmax_stage(h, bias, slot, state):
        m, l8, _ = state
        lg = lg_scr[h][slot] + bias
        m_new = jnp.maximum(m, jnp.max(_fold_rows(jnp.maximum, lg), axis=0, keepdims=True))
        alpha = jnp.exp(m - m_new)
        p = jnp.exp(lg - m_new)
        p_scr[h][slot] = p.astype(BF16)
        return m_new, alpha * l8 + _fold_rows(jnp.add, p), alpha

    def value_stage(h, a, slot, state):
        g = h // GROUP
        vt = vt_scr[a][g * HEAD_DIM:(g + 1) * HEAD_DIM, :]
        acc_scr[h] = state[2] * acc_scr[h] + _dot(vt, p_scr[h][slot])

    def att_tile(t, state):
        for slot in range(2):
            a = 2 * t + slot
            bias = lax.bitcast_convert_type(sc_scr[sub_rows(a), :], F32)
            out = []
            for h in range(ATT_HEADS):
                value_stage(h, jnp.maximum(a - 1, 0), 1 - slot, state[h])
                out.append(softmax_stage(h, bias, slot, state[h]))
                logits_stage(h, jnp.minimum(a + 1, last), 1 - slot)
            state = tuple(out)
        return state

    init = tuple((jnp.full((1, QB), NEG_BIG, F32), jnp.zeros((8, QB), F32), jnp.ones((1, QB), F32))
                 for _ in range(ATT_HEADS))
    for h in range(ATT_HEADS):
        p_scr[h][1] = jnp.zeros((TA, QB), BF16)
        logits_stage(h, 0, 0)
    fin = lax.fori_loop(0, nt, att_tile, init)
    for h in range(ATT_HEADS):
        value_stage(h, last, 1, fin[h])
    for h in range(ATT_HEADS):
        o = (acc_scr[h] / jnp.sum(fin[h][1], axis=0, keepdims=True)).T
        c0 = h * HEAD_DIM
        out_ref[:, c0:c0 + HEAD_DIM] = (o * _silu(az[:, c0:c0 + HEAD_DIM])).astype(out_ref.dtype)


def _dsa_prompt(p_main, p_small, batch, seq):
    QB, TK = DSA_QB, DSA_TK
    nb = seq // QB
    topk = min(TOPK_MAX, seq // 4)
    TA = DSA_TA
    assert TK >= topk and seq % TK == 0 and seq % QB == 0 and TK % DSA_TS == 0
    qrow = lambda b, i: b * nb + i
    return pl.pallas_call(
        functools.partial(_dsa_kernel, topk=topk),
        out_shape=jax.ShapeDtypeStruct((batch * seq, ATT_WIDTH), BF16),
        grid=(batch, nb),
        in_specs=[pl.BlockSpec((QB, ATT_WIDTH), lambda b, i: (qrow(b, i), 0)),
                  pl.BlockSpec((QB, 1024), lambda b, i: (qrow(b, i), 1)),
                  pl.BlockSpec((QB, ATT_WIDTH), lambda b, i: (qrow(b, i), 2)),
                  pl.BlockSpec((QB, LANES), lambda b, i: (qrow(b, i), 4)),
                  pl.BlockSpec((seq, KV_WIDTH), lambda b, i: (b, 0)),
                  pl.BlockSpec((seq, KV_WIDTH), lambda b, i: (b, 1)),
                  pl.BlockSpec((seq, LANES), lambda b, i: (b, 4))],
        out_specs=pl.BlockSpec((QB, ATT_WIDTH), lambda b, i: (qrow(b, i), 0)),
        scratch_shapes=[pltpu.VMEM((seq, KV_WIDTH), BF16),
                        pltpu.VMEM((seq // TA, KV_WIDTH, TA), BF16),
                        pltpu.VMEM((seq, LANES), BF16),
                        pltpu.VMEM((seq, LANES), BF16),
                        pltpu.VMEM((seq, QB), I32),
                        pltpu.VMEM((ATT_HEADS, HEAD_DIM, QB), F32),
                        pltpu.VMEM((1, QB), F32)]
                       + [pltpu.VMEM((2, TA, QB), F32)] * ATT_HEADS
                       + [pltpu.VMEM((2, TA, QB), BF16)] * ATT_HEADS,
        compiler_params=pltpu.CompilerParams(dimension_semantics=("parallel", "arbitrary"),
                                             vmem_limit_bytes=VMEM_LIMIT),
        name="dsa_prompt",
    )(p_main, p_main, p_main, p_small, p_small, p_small, p_small)


SCORE_ROWS = 32


def _sample_scores_kernel(pt_ref, qi_ref, wb_ref, kin_ref, cache_ref, sc_ref, snew_ref, page_buf, sems):
    b = pl.program_id(0)
    n_rows = pl.num_programs(0)
    n_pages = sc_ref.shape[1]
    slot = lax.rem(b, 2)

    def page_copy(row, p, to_slot):
        return pltpu.make_async_copy(cache_ref.at[pt_ref[row, p]], page_buf.at[to_slot, p], sems.at[to_slot])

    def start_fetch(row, to_slot):
        def body(p, carry):
            page_copy(row, p, to_slot).start()
            return carry

        lax.fori_loop(0, n_pages, body, 0)

    @pl.when(b == 0)
    def _():
        start_fetch(0, 0)

    @pl.when(b + 1 < n_rows)
    def _():
        start_fetch(b + 1, 1 - slot)

    def wait_page(p, carry):
        page_copy(b, p, slot).wait()
        return carry

    lax.fori_loop(0, n_pages, wait_page, 0)

    qi = qi_ref[0]
    wb = wb_ref[0]

    def score_rows(g, carry):
        rows = []
        for j in range(SCORE_ROWS):
            page = page_buf[slot, g * SCORE_ROWS + j]
            s = _dot(qi, page.astype(BF16))
            rows.append(jnp.sum(jnp.maximum(s, 0.0) * wb, axis=0, keepdims=True) * IDX_SCALE)
        sc_ref[0, pl.ds(pl.multiple_of(g * SCORE_ROWS, SCORE_ROWS), SCORE_ROWS), :] = jnp.concatenate(rows, axis=0)
        return carry

    lax.fori_loop(0, n_pages // SCORE_ROWS, score_rows, 0)

    ki_new = kin_ref[0].astype(BF16).astype(F32)
    s = jnp.sum(qi.astype(F32) * ki_new, axis=1, keepdims=True)
    s_new = jnp.sum(jnp.maximum(s, 0.0) * wb[:, 0:1], axis=0, keepdims=True) * IDX_SCALE
    snew_ref[0] = jnp.broadcast_to(s_new, (8, LANES))


def _sample_scores(page_table, qi16, wb, ki_new, cache_ik_t):
    bd, n_pages = page_table.shape
    assert n_pages % SCORE_ROWS == 0
    bmap = lambda b, pt: (b, 0, 0)
    return pl.pallas_call(
        _sample_scores_kernel,
        out_shape=(jax.ShapeDtypeStruct((bd, n_pages, PAGE), F32),
                   jax.ShapeDtypeStruct((bd, 8, LANES), F32)),
        grid_spec=pltpu.PrefetchScalarGridSpec(
            num_scalar_prefetch=1,
            grid=(bd,),
            in_specs=[pl.BlockSpec((1, IDX_HEADS, IDX_DIM), bmap),
                      pl.BlockSpec((1, IDX_HEADS, LANES), bmap),
                      pl.BlockSpec((1, 1, IDX_DIM), bmap),
                      pl.BlockSpec(memory_space=pl.ANY)],
            out_specs=(pl.BlockSpec((1, n_pages, PAGE), bmap),
                       pl.BlockSpec((1, 8, LANES), bmap)),
            scratch_shapes=[pltpu.VMEM((2, n_pages, IDX_DIM, PAGE), F32),
                            pltpu.SemaphoreType.DMA((2,))]),
        compiler_params=pltpu.CompilerParams(dimension_semantics=("arbitrary",),
                                             vmem_limit_bytes=VMEM_LIMIT),
        name="sample_scores",
    )(page_table, qi16, wb, ki_new, cache_ik_t)


SEL_THR, SEL_NEXT, SEL_CUT = 0, 1, 2


def _sample_select_kernel(sc_ref, snew_ref, sel_ref, *, topk):
    x = sc_ref[...]
    bd, n_pages, _ = x.shape
    past = n_pages * PAGE
    s_new = snew_ref[:, 0:1, 0:1]

    def total(v):
        return jnp.sum(jnp.sum(v, axis=1, keepdims=True), axis=2, keepdims=True)

    def count_ge(cand):
        return total(jnp.where(x >= cand, 1.0, 0.0)) + jnp.where(s_new >= cand, 1.0, 0.0)

    key = _kth_largest_key(count_ge, (bd, 1, 1), topk)
    thr = _key_to_float(key)
    thr_next = _key_to_float(key + 1)
    need = float(topk) - count_ge(thr_next)
    pos = (lax.broadcasted_iota(I32, (1, n_pages, PAGE), 1) * PAGE
           + lax.broadcasted_iota(I32, (1, n_pages, PAGE), 2)).astype(F32)

    def count_tie_le(cut):
        tie = jnp.where(x >= thr_next, 0.0, jnp.where(x >= thr, jnp.where(pos <= cut, 1.0, 0.0), 0.0))
        tie_new = jnp.where(s_new >= thr_next, 0.0, jnp.where(s_new >= thr, jnp.where(float(past) <= cut, 1.0, 0.0), 0.0))
        return total(tie) + tie_new

    cut = _tie_cutoff(count_tie_le, need, (bd, 1, 1), int(np.log2(past)) + 1)
    row = lax.broadcasted_iota(I32, (bd, 8, LANES), 1)
    sel_ref[...] = jnp.where(row == SEL_THR, thr, jnp.where(row == SEL_NEXT, thr_next, cut))


def _sample_select(scores, s_new, topk):
    bd = scores.shape[0]
    return pl.pallas_call(
        functools.partial(_sample_select_kernel, topk=topk),
        out_shape=jax.ShapeDtypeStruct((bd, 8, LANES), F32),
        compiler_params=pltpu.CompilerParams(vmem_limit_bytes=VMEM_LIMIT),
        name="sample_select",
    )(scores, s_new)


def _taken_bias(score, pos, sel):
    thr, thr_next, cut = sel[SEL_THR:SEL_THR + 1, 0:1], sel[SEL_NEXT:SEL_NEXT + 1, 0:1], sel[SEL_CUT:SEL_CUT + 1, 0:1]
    return jnp.where(score >= thr_next, 0.0,
                     jnp.where(score >= thr, jnp.where(pos <= cut, 0.0, NEG_BIG), NEG_BIG))


def _sample_compact_kernel(pt_ref, sc_ref, sel_ref, rows_ref, nsel_ref, *, slots):
    n_pages = sc_ref.shape[1]
    pos = (lax.broadcasted_iota(I32, (n_pages, PAGE), 0) * PAGE
           + lax.broadcasted_iota(I32, (n_pages, PAGE), 1)).astype(F32)
    taken = jnp.where(_taken_bias(sc_ref[0], pos, sel_ref[0]) == 0.0, 1.0, 0.0)
    before = lax.broadcasted_iota(I32, (PAGE, PAGE), 0) < lax.broadcasted_iota(I32, (PAGE, PAGE), 1)
    in_page = _dot(taken, jnp.where(before, 1.0, 0.0))
    taken_t = taken.T
    rank_t = jnp.where(taken_t > 0.0, in_page.T, -1.0)
    page_tot = jnp.sum(taken_t, axis=0, keepdims=True)
    earlier = lax.broadcasted_iota(I32, (n_pages, n_pages), 0) < lax.broadcasted_iota(I32, (n_pages, n_pages), 1)
    first_slot = _dot(jnp.broadcast_to(page_tot, (8, n_pages)),
                      jnp.where(earlier, 1.0, 0.0))[0:1]
    phys_t = (pt_ref[0] * PAGE + lax.broadcasted_iota(I32, (PAGE, n_pages), 0)).astype(F32)
    slot = lax.broadcasted_iota(I32, (slots, n_pages), 0).astype(F32)

    def rank_body(r, acc):
        r = r.astype(F32)
        row = jnp.sum(jnp.where(rank_t == r, phys_t, 0.0), axis=0, keepdims=True)
        target = jnp.where(r < page_tot, first_slot + r, -1.0)
        return acc + jnp.where(slot == target, row, 0.0)

    most = jnp.max(page_tot).astype(I32)
    acc = lax.fori_loop(0, most, rank_body, jnp.zeros((slots, n_pages), F32))
    rows_ref[0] = jnp.broadcast_to(jnp.sum(acc, axis=1, keepdims=True), (slots, LANES))
    nsel_ref[0] = jnp.broadcast_to(jnp.sum(page_tot, axis=1, keepdims=True), (8, LANES))


def _sample_compact(page_table, scores, sel, slots):
    bd, n_pages = page_table.shape
    assert n_pages % LANES == 0
    bmap = lambda b: (b, 0, 0)
    return pl.pallas_call(
        functools.partial(_sample_compact_kernel, slots=slots),
        out_shape=(jax.ShapeDtypeStruct((bd, slots, LANES), F32), jax.ShapeDtypeStruct((bd, 8, LANES), F32)),
        grid=(bd,),
        in_specs=[pl.BlockSpec((1, 1, n_pages), bmap), pl.BlockSpec((1, n_pages, PAGE), bmap),
                  pl.BlockSpec((1, 8, LANES), bmap)],
        out_specs=(pl.BlockSpec((1, slots, LANES), bmap), pl.BlockSpec((1, 8, LANES), bmap)),
        compiler_params=pltpu.CompilerParams(dimension_semantics=("parallel",), vmem_limit_bytes=VMEM_LIMIT),
        name="sample_compact",
    )(page_table.reshape(bd, 1, n_pages), scores, sel)


SC_GATHER_CHUNK = 128


def _gather_rows(table_k, table_v, idx):
    info = plsc.get_sparse_core_info()
    n_workers = info.num_cores * info.num_subcores
    n_idx = idx.shape[0]
    per_worker = n_idx // n_workers
    assert per_worker * n_workers == n_idx and per_worker % SC_GATHER_CHUNK == 0
    mesh = plsc.VectorSubcoreMesh(core_axis_name="c", subcore_axis_name="s")
    out = jax.ShapeDtypeStruct((n_idx, table_k.shape[1]), table_k.dtype)

    @functools.partial(
        pl.kernel, mesh=mesh, out_type=(out, out),
        scratch_types=[pltpu.VMEM((SC_GATHER_CHUNK,), I32),
                       pltpu.VMEM((SC_GATHER_CHUNK, table_k.shape[1]), table_k.dtype),
                       pltpu.VMEM((SC_GATHER_CHUNK, table_v.shape[1]), table_v.dtype),
                       pltpu.SemaphoreType.DMA, pltpu.SemaphoreType.DMA])
    def gather(tk_hbm, tv_hbm, idx_hbm, ok_hbm, ov_hbm, idx_v, rk_v, rv_v, sem_k, sem_v):
        worker = lax.axis_index("s") * info.num_cores + lax.axis_index("c")

        @pl.loop(0, per_worker // SC_GATHER_CHUNK)
        def _(j):
            base = worker * per_worker + j * SC_GATHER_CHUNK
            pltpu.sync_copy(idx_hbm.at[pl.ds(base, SC_GATHER_CHUNK)], idx_v)
            copy_k = pltpu.async_copy(tk_hbm.at[idx_v], rk_v, sem_k)
            copy_v = pltpu.async_copy(tv_hbm.at[idx_v], rv_v, sem_v)
            copy_k.wait()
            copy_v.wait()
            pltpu.sync_copy(rk_v, ok_hbm.at[pl.ds(base, SC_GATHER_CHUNK)])
            pltpu.sync_copy(rv_v, ov_hbm.at[pl.ds(base, SC_GATHER_CHUNK)])

    return gather(table_k, table_v, idx)


def _sample_attend_kernel(q_ref, az_ref, kn_ref, vn_ref, snew_ref, sel_ref, nsel_ref, k_ref, v_ref, out_ref, *, past):
    width = k_ref.shape[1]
    q = q_ref[0]
    col = lax.broadcasted_iota(I32, (ATT_HEADS, width), 1)
    head = lax.broadcasted_iota(I32, (ATT_HEADS, width), 0)
    own_head = col % KV_HEADS == head // GROUP
    filled = (col // KV_HEADS).astype(F32) < nsel_ref[0][0:1, 0:1]
    lg = _dot_nt(q, k_ref[0].astype(BF16))
    lg = jnp.where(own_head, jnp.where(filled, lg, NEG_BIG), NEG_BIG)
    lg_new = (jnp.sum(q.astype(F32) * kn_ref[0].astype(BF16).astype(F32), axis=1, keepdims=True)
              + _taken_bias(snew_ref[0][0:1, 0:1], float(past), sel_ref[0]))
    m = jnp.maximum(jnp.max(lg, axis=1, keepdims=True), lg_new)
    p = jnp.exp(lg - m)
    p_new = jnp.exp(lg_new - m)
    l = jnp.sum(p, axis=1, keepdims=True) + p_new
    acc = _dot(p.astype(BF16), v_ref[0].astype(BF16)) + p_new * vn_ref[0].astype(BF16).astype(F32)
    out_ref[0] = (acc / l) * _silu(az_ref[0])


def _sample_attend(q8, az8, k_new8, v_new8, s_new, sel, n_sel, k_sel, v_sel, past):
    bd, width, _ = k_sel.shape
    bmap = lambda b: (b, 0, 0)
    head_tile = pl.BlockSpec((1, ATT_HEADS, HEAD_DIM), bmap)
    par_tile = pl.BlockSpec((1, 8, LANES), bmap)
    rows_tile = pl.BlockSpec((1, width, HEAD_DIM), bmap)
    return pl.pallas_call(
        functools.partial(_sample_attend_kernel, past=past),
        out_shape=jax.ShapeDtypeStruct((bd, ATT_HEADS, HEAD_DIM), F32),
        grid=(bd,),
        in_specs=[head_tile, head_tile, head_tile, head_tile, par_tile, par_tile, par_tile, rows_tile, rows_tile],
        out_specs=head_tile,
        compiler_params=pltpu.CompilerParams(dimension_semantics=("parallel",), vmem_limit_bytes=VMEM_LIMIT),
        name="sample_attend",
    )(q8, az8, k_new8, v_new8, s_new, sel, n_sel, k_sel, v_sel)


def _mlstm_step_kernel(bi_ref, bf_ref, q_ref, k_ref, v_ref, mo_ref, mz_ref, misc_ref, nw_ref,
                       c_ref, n_ref, m_ref, out_ref, c_out, n_out, m_out):
    misc = misc_ref[0]
    eye = lax.broadcasted_iota(I32, (ML_V, ML_V), 0) == lax.broadcasted_iota(I32, (ML_V, ML_V), 1)
    for h in range(ML_HEADS):
        ig = misc[:, MISC_IG + h:MISC_IG + h + 1] + bi_ref[h]
        lf = _log_sigmoid(misc[:, MISC_FG + h:MISC_FG + h + 1] + bf_ref[h])
        m_prev = m_ref[0, h][:, 0:1]
        log_a = lf + m_prev
        m_t = jnp.maximum(log_a, ig)
        d = jnp.exp(ig - m_t)
        a = jnp.exp(log_a - m_t)
        q = q_ref[0][:, h * ML_QK:(h + 1) * ML_QK]
        k = k_ref[0][:, h * ML_QK:(h + 1) * ML_QK]
        v = v_ref[0][:, h * ML_V:(h + 1) * ML_V]
        v_col = jnp.sum(jnp.where(eye, v, 0.0), axis=1, keepdims=True)
        c = c_ref[0, h]
        n = n_ref[0, h]
        s = jnp.sum(q * k, axis=1, keepdims=True) * d
        num = a * jnp.sum(c * q, axis=1, keepdims=True) + s * v_col
        den = a * jnp.sum(n * q, axis=1, keepdims=True) + s
        h_col = num / jnp.maximum(jnp.abs(den), jnp.exp(-m_t))
        c_out[0, h] = a * c + (d * v_col) * k
        n_out[0, h] = a * n + d * k
        m_out[0, h] = jnp.broadcast_to(m_t, (1, LANES))

        h_row = jnp.sum(jnp.where(eye, h_col, 0.0), axis=0, keepdims=True)
        ms = jnp.mean(h_row * h_row, axis=1, keepdims=True)
        hn = h_row * lax.rsqrt(ms + RMS_EPS) * nw_ref[:, h * ML_V:(h + 1) * ML_V]
        gate = _sigmoid(mo_ref[0][:, h * ML_V:(h + 1) * ML_V]) * _silu(mz_ref[0][:, h * ML_V:(h + 1) * ML_V])
        out_ref[0, :, h * ML_V:(h + 1) * ML_V] = (hn * gate).astype(out_ref.dtype)


def _mlstm_step(ps_main, ps_small, b_i, b_f, ml_norm_w, state_c, state_n, state_m):
    bd = ps_main.shape[0]
    col = lambda j: (lambda b: (b, 0, j))
    st4 = lambda b: (b, 0, 0, 0)
    return pl.pallas_call(
        _mlstm_step_kernel,
        out_shape=(jax.ShapeDtypeStruct((bd, 1, ML_WIDTH), BF16),
                   jax.ShapeDtypeStruct(state_c.shape, F32),
                   jax.ShapeDtypeStruct(state_n.shape, F32),
                   jax.ShapeDtypeStruct(state_m.shape, F32)),
        grid=(bd,),
        in_specs=[pl.BlockSpec(memory_space=pltpu.SMEM),
                  pl.BlockSpec(memory_space=pltpu.SMEM),
                  pl.BlockSpec((1, 1, 512), col(12)),
                  pl.BlockSpec((1, 1, 512), col(13)),
                  pl.BlockSpec((1, 1, 1024), col(3)),
                  pl.BlockSpec((1, 1, 1024), col(4)),
                  pl.BlockSpec((1, 1, 1024), col(5)),
                  pl.BlockSpec((1, 1, LANES), col(4)),
                  pl.BlockSpec((1, ML_WIDTH), lambda b: (0, 0)),
                  pl.BlockSpec((1, ML_HEADS, ML_V, ML_QK), st4),
                  pl.BlockSpec((1, ML_HEADS, 1, ML_QK), st4),
                  pl.BlockSpec((1, ML_HEADS, 1, LANES), st4)],
        out_specs=(pl.BlockSpec((1, 1, ML_WIDTH), lambda b: (b, 0, 0)),
                   pl.BlockSpec((1, ML_HEADS, ML_V, ML_QK), st4),
                   pl.BlockSpec((1, ML_HEADS, 1, ML_QK), st4),
                   pl.BlockSpec((1, ML_HEADS, 1, LANES), st4)),
        compiler_params=pltpu.CompilerParams(dimension_semantics=("parallel",),
                                             vmem_limit_bytes=VMEM_LIMIT),
        name="mlstm_step",
    )(b_i, b_f, ps_main, ps_main, ps_main, ps_main, ps_main, ps_small, ml_norm_w.reshape(1, ML_WIDTH),
      state_c, state_n, state_m)


def _split_weights(w_in):
    w_t = jnp.swapaxes(w_in, 0, 1)
    offs = np.cumsum((0,) + IN_SIZES)
    aq, ak, av, iq, ik, iw, az, mq, mk, mv, mi, mf, mo, mz = (w_t[offs[j]:offs[j + 1]] for j in range(len(IN_SIZES)))
    w_main = jnp.concatenate([aq * ATT_SCALE, iq, az, mv, mo, mz, mq, mk * ML_QK ** -0.5], axis=0)
    pad = jnp.zeros((LANES - IDX_DIM - IDX_HEADS - 2 * ML_HEADS, w_t.shape[1]), w_t.dtype)
    w_small = jnp.concatenate([ak, av, ik, iw, mi, mf, pad], axis=0)
    assert w_main.shape[0] == MAIN_W and w_small.shape[0] == SMALL_W
    return w_main.astype(BF16), w_small.astype(BF16)


def kernel(x_prompt, x_sample, cache_k, cache_v, cache_idx_k, state_C, state_n, state_m, page_table,
           norm_w, w_in, b_igate, b_fgate, ml_norm_w, w_out, final_norm_w):
    depth = w_in.shape[0]
    batch, seq, d = x_prompt.shape
    bd, dec_seq, _ = x_sample.shape
    assert depth == 1 and dec_seq == 1 and d == D_MODEL
    n_pages = page_table.shape[1]

    w_main, w_small = _split_weights(w_in[0])
    w_o = w_out[0].astype(BF16)
    w_o_att, w_o_ml = w_o[:ATT_WIDTH], w_o[ATT_WIDTH:]

    xp = x_prompt.reshape(batch * seq, d)
    p_main = _project(xp, norm_w[0], w_main, BF16, 1024, 512, "proj_main")
    p_small = _project(xp, norm_w[0], w_small, F32, 1024, SMALL_W, "proj_small")
    att = _dsa_prompt(p_main, p_small, batch, seq)
    ml, ct, m_p = _mlstm_prompt(p_main, p_small, b_igate[0], b_fgate[0], ml_norm_w[0], batch, seq)
    y_prompt = _out_project(xp, att, ml, w_o_att, w_o_ml, final_norm_w, 256, "out_prompt").reshape(batch, seq, d)
    k_prompt = p_small[:, :KV_WIDTH].reshape(1, batch, seq, KV_HEADS, HEAD_DIM)
    v_prompt = p_small[:, KV_WIDTH:2 * KV_WIDTH].reshape(1, batch, seq, KV_HEADS, HEAD_DIM)
    ik_prompt = p_small[:, 2 * KV_WIDTH:2 * KV_WIDTH + IDX_DIM].reshape(1, batch, seq, IDX_DIM)
    c_prompt = jnp.swapaxes(ct[..., :ML_V], -1, -2)[None]
    n_prompt = ct[..., ML_V][None]
    m_prompt = m_p[:, :, 0, 0][None]

    xs = x_sample.reshape(bd, d)
    ps_main = _project(xs, norm_w[0], w_main, F32, bd, 512, "proj_main_s")
    ps_small = _project(xs, norm_w[0], w_small, F32, bd, SMALL_W, "proj_small_s")
    q8 = ps_main[:, :ATT_WIDTH].reshape(bd, ATT_HEADS, HEAD_DIM).astype(BF16)
    qi16 = ps_main[:, 1024:2048].reshape(bd, IDX_HEADS, IDX_DIM).astype(BF16)
    az8 = ps_main[:, 2048:3072].reshape(bd, ATT_HEADS, HEAD_DIM)
    misc_s = ps_small[:, 2 * KV_WIDTH:]
    ki_new = misc_s[:, :IDX_DIM].reshape(bd, 1, IDX_DIM)
    wb = jnp.broadcast_to(misc_s[:, MISC_WI:MISC_WI + IDX_HEADS, None], (bd, IDX_HEADS, LANES))
    k_new8 = jnp.repeat(ps_small[:, :KV_WIDTH].reshape(bd, KV_HEADS, HEAD_DIM), GROUP, axis=1)
    v_new8 = jnp.repeat(ps_small[:, KV_WIDTH:2 * KV_WIDTH].reshape(bd, KV_HEADS, HEAD_DIM), GROUP, axis=1)
    scores, s_new = _sample_scores(page_table, qi16, wb, ki_new, jnp.swapaxes(cache_idx_k[0], 1, 2))
    assert cache_k.shape[1] * PAGE < 2 ** 24
    topk_s = min(TOPK_MAX, (n_pages * PAGE + 1) // 4)
    sel = _sample_select(scores, s_new, topk_s)
    key_rows, n_sel = _sample_compact(page_table, scores, sel, topk_s)
    idx = (key_rows[:, :, :1].astype(I32) * KV_HEADS + jnp.arange(KV_HEADS, dtype=I32)).reshape(-1)
    k_sel, v_sel = _gather_rows(cache_k.reshape(-1, HEAD_DIM), cache_v.reshape(-1, HEAD_DIM), idx)
    att_s = _sample_attend(q8, az8, k_new8, v_new8, s_new, sel, n_sel,
                           k_sel.reshape(bd, topk_s * KV_HEADS, HEAD_DIM),
                           v_sel.reshape(bd, topk_s * KV_HEADS, HEAD_DIM), n_pages * PAGE)
    ml_s, c_s, n_s, m_s = _mlstm_step(
        ps_main.reshape(bd, 1, MAIN_W), ps_small.reshape(bd, 1, SMALL_W), b_igate[0], b_fgate[0], ml_norm_w[0],
        state_C[0], state_n[0].reshape(bd, ML_HEADS, 1, ML_QK),
        jnp.broadcast_to(state_m[0][:, :, None, None], (bd, ML_HEADS, 1, LANES)))
    y_sample = _out_project(xs, att_s.reshape(bd, ATT_WIDTH).astype(BF16), ml_s.reshape(bd, ML_WIDTH),
                            w_o_att, w_o_ml, final_norm_w, bd, "out_sample").reshape(bd, 1, d)
    k_sample = ps_small[:, :KV_WIDTH].reshape(1, bd, 1, KV_HEADS, HEAD_DIM)
    v_sample = ps_small[:, KV_WIDTH:2 * KV_WIDTH].reshape(1, bd, 1, KV_HEADS, HEAD_DIM)
    ik_sample = misc_s[:, :IDX_DIM].reshape(1, bd, 1, IDX_DIM)

    return (y_prompt, y_sample, k_prompt, v_prompt, ik_prompt, c_prompt, n_prompt, m_prompt,
            k_sample, v_sample, ik_sample, c_s[None], n_s.reshape(1, bd, ML_HEADS, ML_QK), m_s[:, :, 0, 0][None])
```

```python
import functools

import jax
import jax.numpy as jnp
import numpy as np
from jax import lax
from jax.experimental import pallas as pl
from jax.experimental.pallas import tpu as pltpu
from jax.experimental.pallas import tpu_sc as plsc

F32 = jnp.float32
BF16 = jnp.bfloat16
I32 = jnp.int32

D_MODEL = 2048
PAGE = 128
ATT_HEADS = 8
KV_HEADS = 2
HEAD_DIM = 128
GROUP = ATT_HEADS // KV_HEADS
ATT_WIDTH = ATT_HEADS * HEAD_DIM
KV_WIDTH = KV_HEADS * HEAD_DIM
ATT_SCALE = HEAD_DIM ** -0.5
IDX_HEADS = 16
IDX_DIM = 64
IDX_SCALE = (IDX_HEADS * IDX_DIM) ** -0.5
TOPK_MAX = 256
ML_HEADS = 4
ML_QK = 128
ML_V = 256
ML_WIDTH = ML_HEADS * ML_V
RMS_EPS = 1e-6
IN_SIZES = (ATT_WIDTH, KV_WIDTH, KV_WIDTH, IDX_HEADS * IDX_DIM, IDX_DIM, IDX_HEADS, ATT_WIDTH,
            ML_HEADS * ML_QK, ML_HEADS * ML_QK, ML_WIDTH, ML_HEADS, ML_HEADS, ML_WIDTH, ML_WIDTH)

LANES = 128
NEG_BIG = -1e30
VMEM_LIMIT = 56 * 1024 * 1024

MAIN_W = 7168
SMALL_W = 640
MISC_WI = IDX_DIM
MISC_IG = IDX_DIM + IDX_HEADS
MISC_FG = MISC_IG + ML_HEADS


def _dot(a, b):
    return jnp.dot(a, b, preferred_element_type=F32)


def _dot_nt(a, b):
    return lax.dot_general(a, b, (((1,), (1,)), ((), ())), preferred_element_type=F32)


def _tree_reduce(op, parts):
    parts = list(parts)
    while len(parts) > 1:
        paired = [op(parts[j], parts[j + 1]) for j in range(0, len(parts) - 1, 2)]
        parts = paired + parts[len(parts) - len(parts) % 2:]
    return parts[0]


def _fold_rows(op, x):
    return _tree_reduce(op, [x[r:r + 8] for r in range(0, x.shape[0], 8)])


def _proj_small_kernel(x_ref, nw_ref, w_ref, k_ref, v_ref, misc_ref):
    x = x_ref[...]
    ms = jnp.mean(x * x, axis=-1, keepdims=True)
    h = (x * lax.rsqrt(ms + RMS_EPS) * nw_ref[...]).astype(BF16)
    res = _dot_nt(h, w_ref[...].astype(BF16))
    tm = x.shape[0]
    for g in range(KV_HEADS):
        k_ref[pl.ds(g, tm, stride=KV_HEADS), :] = res[:, g * HEAD_DIM:(g + 1) * HEAD_DIM]
        v_ref[pl.ds(g, tm, stride=KV_HEADS), :] = res[:, KV_WIDTH + g * HEAD_DIM:KV_WIDTH + (g + 1) * HEAD_DIM]
    misc_ref[...] = res[:, 2 * KV_WIDTH:]


def _project_small(x2d, norm_w, w_small, tm, name):
    m, d = x2d.shape
    kv = jax.ShapeDtypeStruct((m * KV_HEADS, HEAD_DIM), F32)
    kv_spec = pl.BlockSpec((tm * KV_HEADS, HEAD_DIM), lambda i: (i, 0))
    return pl.pallas_call(
        _proj_small_kernel,
        out_shape=(kv, kv, jax.ShapeDtypeStruct((m, LANES), F32)),
        grid=(m // tm,),
        in_specs=[pl.BlockSpec((tm, d), lambda i: (i, 0)),
                  pl.BlockSpec((1, d), lambda i: (0, 0)),
                  pl.BlockSpec((SMALL_W, d), lambda i: (0, 0))],
        out_specs=(kv_spec, kv_spec, pl.BlockSpec((tm, LANES), lambda i: (i, 0))),
        compiler_params=pltpu.CompilerParams(dimension_semantics=("parallel",), vmem_limit_bytes=VMEM_LIMIT),
        name=name,
    )(x2d, norm_w.reshape(1, d), w_small)


MAIN_TN = 512
ROW_ALIGN = 8


def _main_tiles():
    offs = np.cumsum((0,) + IN_SIZES)
    aq, iq, az, mq, mk, mv, mo, mz = (int(offs[j]) for j in (0, 3, 6, 7, 8, 9, 12, 13))
    segments = [(aq, ATT_WIDTH, ATT_SCALE), (iq, IDX_HEADS * IDX_DIM, 1.0), (az, ATT_WIDTH, 1.0),
                (mv, ML_WIDTH, 1.0), (mo, ML_WIDTH, 1.0), (mz, ML_WIDTH, 1.0),
                (mq, ML_HEADS * ML_QK, 1.0), (mk, ML_HEADS * ML_QK, ML_QK ** -0.5)]
    rows, scales = [], []
    for start, width, scale in segments:
        assert start % ROW_ALIGN == 0 and width % MAIN_TN == 0
        for r in range(start, start + width, MAIN_TN):
            rows.append(r // ROW_ALIGN)
            scales.append(scale)
    assert len(rows) * MAIN_TN == MAIN_W
    return np.asarray(rows, np.int32), np.asarray(scales, np.float32)


def _proj_main_kernel(rows_ref, scale_ref, x_ref, nw_ref, w_ref, o_ref, h_scr):
    j = pl.program_id(1)

    @pl.when(j == 0)
    def _():
        x = x_ref[...]
        ms = jnp.mean(x * x, axis=-1, keepdims=True)
        h_scr[...] = (x * lax.rsqrt(ms + RMS_EPS) * nw_ref[...]).astype(BF16)

    w = (w_ref[...] * scale_ref[j]).astype(BF16)
    o_ref[...] = _dot_nt(h_scr[...], w).astype(o_ref.dtype)


def _project_main(x2d, norm_w, w_t, out_dtype, tm, name):
    m, d = x2d.shape
    rows, scales = _main_tiles()
    return pl.pallas_call(
        _proj_main_kernel,
        out_shape=jax.ShapeDtypeStruct((m, MAIN_W), out_dtype),
        grid_spec=pltpu.PrefetchScalarGridSpec(
            num_scalar_prefetch=1,
            grid=(m // tm, len(rows)),
            in_specs=[pl.BlockSpec(memory_space=pltpu.SMEM),
                      pl.BlockSpec((tm, d), lambda i, j, rows: (i, 0)),
                      pl.BlockSpec((1, d), lambda i, j, rows: (0, 0)),
                      pl.BlockSpec((pl.Element(MAIN_TN), pl.Element(d)), lambda i, j, rows: (rows[j] * ROW_ALIGN, 0))],
            out_specs=pl.BlockSpec((tm, MAIN_TN), lambda i, j, rows: (i, j)),
            scratch_shapes=[pltpu.VMEM((tm, d), BF16)]),
        compiler_params=pltpu.CompilerParams(dimension_semantics=("parallel", "arbitrary"),
                                             vmem_limit_bytes=VMEM_LIMIT),
        name=name,
    )(jnp.asarray(rows), jnp.asarray(scales), x2d, norm_w.reshape(1, d), w_t)


def _out_kernel(x_ref, a_ref, m_ref, wa_ref, wm_ref, fw_ref, o_ref):
    y = x_ref[...] + _dot(a_ref[...], wa_ref[...]) + _dot(m_ref[...], wm_ref[...])
    ms = jnp.mean(y * y, axis=-1, keepdims=True)
    o_ref[...] = y * lax.rsqrt(ms + RMS_EPS) * fw_ref[...]


def _out_project(x2d, a, mo, w_att, w_ml, final_w, tm, name):
    m, d = x2d.shape
    return pl.pallas_call(
        _out_kernel,
        out_shape=jax.ShapeDtypeStruct((m, d), F32),
        grid=(m // tm,),
        in_specs=[pl.BlockSpec((tm, d), lambda i: (i, 0)),
                  pl.BlockSpec((tm, ATT_WIDTH), lambda i: (i, 0)),
                  pl.BlockSpec((tm, ML_WIDTH), lambda i: (i, 0)),
                  pl.BlockSpec((ATT_WIDTH, d), lambda i: (0, 0)),
                  pl.BlockSpec((ML_WIDTH, d), lambda i: (0, 0)),
                  pl.BlockSpec((1, d), lambda i: (0, 0))],
        out_specs=pl.BlockSpec((tm, d), lambda i: (i, 0)),
        compiler_params=pltpu.CompilerParams(dimension_semantics=("parallel",),
                                             vmem_limit_bytes=VMEM_LIMIT),
        name=name,
    )(x2d, a, mo, w_att, w_ml, final_w.reshape(1, d))


def _log_sigmoid(x):
    return jnp.minimum(x, 0.0) - jnp.log(1.0 + jnp.exp(-jnp.abs(x)))


def _sigmoid(x):
    return 1.0 / (1.0 + jnp.exp(-x))


def _silu(x):
    return x * _sigmoid(x)


ML_CHUNK = 256
ML_AUG = ML_V + LANES


def _mlstm_kernel(bi_ref, bf_ref, q_ref, k_ref, v_ref, mo_ref, mz_ref, misc_ref, nw_ref,
                  out_ref, ct_ref, m_ref):
    L = ML_CHUNK

    @pl.when(pl.program_id(1) == 0)
    def _():
        ct_ref[...] = jnp.zeros_like(ct_ref)
        m_ref[...] = jnp.zeros_like(m_ref)

    misc = misc_ref[...]
    misc_t = misc.T
    t_idx = lax.broadcasted_iota(I32, (L, L), 0)
    s_idx = lax.broadcasted_iota(I32, (L, L), 1)
    causal = s_idx <= t_idx
    ones_col = jnp.where(lax.broadcasted_iota(I32, (L, LANES), 1) == 0, 1.0, 0.0).astype(BF16)

    for h in range(ML_HEADS):
        ig_row = misc_t[MISC_IG + h:MISC_IG + h + 1, :] + bi_ref[h]
        lf_row = _log_sigmoid(misc_t[MISC_FG + h:MISC_FG + h + 1, :] + bf_ref[h])
        lf_col = _log_sigmoid(misc[:, MISC_FG + h:MISC_FG + h + 1] + bf_ref[h])
        b_col = jnp.sum(jnp.where(causal, lf_row, 0.0), axis=1, keepdims=True)
        b_row = jnp.sum(jnp.where(t_idx <= s_idx, lf_col, 0.0), axis=0, keepdims=True)
        m_prev = m_ref[0, h][0:1, 0:1]
        log_d = jnp.where(causal, b_col - b_row + ig_row, -jnp.inf)
        log_a = b_col + m_prev
        m_t = jnp.maximum(log_a, jnp.max(log_d, axis=1, keepdims=True))
        d = jnp.exp(log_d - m_t)
        a = jnp.exp(log_a - m_t)

        q = q_ref[:, h * ML_QK:(h + 1) * ML_QK]
        k = k_ref[:, h * ML_QK:(h + 1) * ML_QK]
        v_aug = jnp.concatenate([v_ref[:, h * ML_V:(h + 1) * ML_V], ones_col], axis=1)
        s = (_dot_nt(q, k) * d).astype(BF16)
        ct = ct_ref[0, h]
        num_aug = a * _dot(q, ct.astype(BF16)) + _dot(s, v_aug)
        den = num_aug[:, ML_V:ML_V + 1]
        hh = num_aug[:, :ML_V] / jnp.maximum(jnp.abs(den), jnp.exp(-m_t))

        m_new = m_t[L - 1:L, :]
        a_end = a[L - 1:L, :]
        w_row = jnp.exp(b_row[:, L - 1:L] - b_row + ig_row - m_new)
        ktw = (k.astype(F32).T * w_row).astype(BF16)
        ct_new = a_end * ct + _dot(ktw, v_aug)
        ct_ref[0, h] = ct_new
        m_ref[0, h] = jnp.broadcast_to(m_new, (8, LANES))

        ms = jnp.mean(hh * hh, axis=1, keepdims=True)
        hn = hh * lax.rsqrt(ms + RMS_EPS) * nw_ref[:, h * ML_V:(h + 1) * ML_V]
        gate = _sigmoid(mo_ref[:, h * ML_V:(h + 1) * ML_V].astype(F32)) * _silu(mz_ref[:, h * ML_V:(h + 1) * ML_V].astype(F32))
        out_ref[:, h * ML_V:(h + 1) * ML_V] = (hn * gate).astype(out_ref.dtype)


def _mlstm_prompt(p_main, misc, b_i, b_f, ml_norm_w, batch, seq):
    L = ML_CHUNK
    nc = seq // L
    row = lambda b, c: b * nc + c
    return pl.pallas_call(
        _mlstm_kernel,
        out_shape=(jax.ShapeDtypeStruct((batch * seq, ML_WIDTH), BF16),
                   jax.ShapeDtypeStruct((batch, ML_HEADS, ML_QK, ML_AUG), F32),
                   jax.ShapeDtypeStruct((batch, ML_HEADS, 8, LANES), F32)),
        grid=(batch, nc),
        in_specs=[pl.BlockSpec(memory_space=pltpu.SMEM),
                  pl.BlockSpec(memory_space=pltpu.SMEM),
                  pl.BlockSpec((L, 512), lambda b, c: (row(b, c), 12)),
                  pl.BlockSpec((L, 512), lambda b, c: (row(b, c), 13)),
                  pl.BlockSpec((L, 1024), lambda b, c: (row(b, c), 3)),
                  pl.BlockSpec((L, 1024), lambda b, c: (row(b, c), 4)),
                  pl.BlockSpec((L, 1024), lambda b, c: (row(b, c), 5)),
                  pl.BlockSpec((L, LANES), lambda b, c: (row(b, c), 0)),
                  pl.BlockSpec((1, ML_WIDTH), lambda b, c: (0, 0))],
        out_specs=(pl.BlockSpec((L, ML_WIDTH), lambda b, c: (row(b, c), 0)),
                   pl.BlockSpec((1, ML_HEADS, ML_QK, ML_AUG), lambda b, c: (b, 0, 0, 0)),
                   pl.BlockSpec((1, ML_HEADS, 8, LANES), lambda b, c: (b, 0, 0, 0))),
        compiler_params=pltpu.CompilerParams(dimension_semantics=("parallel", "arbitrary"),
                                             vmem_limit_bytes=VMEM_LIMIT),
        name="mlstm_prompt",
    )(b_i, b_f, p_main, p_main, p_main, p_main, p_main, misc, ml_norm_w.reshape(1, ML_WIDTH))


INT_MIN = -2 ** 31


def _float_to_key(x):
    bits = lax.bitcast_convert_type(x, I32)
    return jnp.where(bits >= 0, bits, bits ^ jnp.int32(0x7FFFFFFF))


def _key_to_float(key):
    bits = jnp.where(key >= 0, key, key ^ jnp.int32(0x7FFFFFFF))
    return lax.bitcast_convert_type(bits, F32)


KEY_NEG_INF = INT_MIN + 0x7FFFFF


def _kth_largest_key(count_ge, shape, k):
    def bit_body(it, prefix):
        cand = prefix + lax.shift_left(jnp.int32(1), 31 - it)
        cand_f = _key_to_float(jnp.maximum(cand, KEY_NEG_INF))
        return jnp.where(count_ge(cand_f) >= float(k), cand, prefix)

    return lax.fori_loop(0, 32, bit_body, jnp.full(shape, INT_MIN, I32))


def _tie_cutoff(count_tie_le, need, shape, index_bits):
    def bit_body(it, lo):
        cand = lo + lax.shift_left(jnp.int32(1), index_bits - 1 - it).astype(F32)
        return jnp.where(count_tie_le(cand) < need, cand, lo)

    return lax.fori_loop(0, index_bits, bit_body, jnp.full(shape, -1.0, F32)) + 1.0


DSA_QB = 128
DSA_TK = 512
DSA_TS = 256
DSA_TA = 256


def _dsa_kernel(q_ref, qi_ref, az_ref, miscq_ref, k_ref, v_ref, misck_ref, out_ref,
                k_bf, vt_scr, ki_lo, ki_hi, sc_scr, acc_scr, j_scr, *head_scr, topk):
    i = pl.program_id(1)
    seq = misck_ref.shape[0]
    QB, TK, TS, TA = DSA_QB, DSA_TK, DSA_TS, DSA_TA
    nt = (i * QB + QB + TK - 1) // TK

    @pl.when(i == 0)
    def _():
        for g in range(KV_HEADS):
            k_bf[:, g * HEAD_DIM:(g + 1) * HEAD_DIM] = k_ref[pl.ds(g, seq, stride=KV_HEADS), :].astype(BF16)

        def v_tile(a, carry):
            for g in range(KV_HEADS):
                rows = pl.ds(pl.multiple_of(a * (TA * KV_HEADS), TA * KV_HEADS) + g, TA, stride=KV_HEADS)
                vt_scr[a, g * HEAD_DIM:(g + 1) * HEAD_DIM, :] = v_ref[rows, :].T.astype(BF16)
            return carry

        lax.fori_loop(0, seq // TA, v_tile, 0)
        lane = lax.broadcasted_iota(I32, (seq, LANES), 1)
        lo = jnp.where(lane < IDX_DIM, misck_ref[...], 0.0)
        ki_lo[...] = lo.astype(BF16)
        ki_hi[...] = pltpu.roll(lo, IDX_DIM, axis=1).astype(BF16)

    k_iota = lax.broadcasted_iota(I32, (TK, QB), 0)
    q_pos = i * QB + lax.broadcasted_iota(I32, (TK, QB), 1)
    w_t = miscq_ref[...].T

    def tile_rows(t):
        return pl.ds(pl.multiple_of(t * TK, TK), TK)

    def score_tile(t, carry):
        rows = pl.ds(pl.multiple_of(t * TS, TS), TS)
        klo = ki_lo[rows, :]
        khi = ki_hi[rows, :]
        acc = jnp.zeros((TS, QB), F32)
        for p in range(IDX_HEADS // 2):
            qp = qi_ref[:, p * LANES:(p + 1) * LANES]
            w0 = w_t[MISC_WI + 2 * p:MISC_WI + 2 * p + 1, :]
            w1 = w_t[MISC_WI + 2 * p + 1:MISC_WI + 2 * p + 2, :]
            acc = acc + w0 * jnp.maximum(_dot_nt(klo, qp), 0.0) + w1 * jnp.maximum(_dot_nt(khi, qp), 0.0)
        k_pos = t * TS + lax.broadcasted_iota(I32, (TS, QB), 0)
        q_pos_s = i * QB + lax.broadcasted_iota(I32, (TS, QB), 1)
        score = jnp.where(k_pos <= q_pos_s, acc * IDX_SCALE + 0.0, -jnp.inf)
        sc_scr[rows, :] = _float_to_key(score)
        return carry

    def score_tiles(t, carry):
        for sub in range(TK // TS):
            score_tile(t * (TK // TS) + sub, carry)
        return carry

    lax.fori_loop(0, nt, score_tiles, 0)

    def count_key_ge(cand):
        def body(t, cnt):
            return cnt + _fold_rows(jnp.add, jnp.where(sc_scr[tile_rows(t), :] >= cand, 1, 0))

        cnt = lax.fori_loop(0, nt, body, jnp.zeros((8, QB), I32))
        return jnp.sum(cnt, axis=0, keepdims=True)

    def bit_body(it, carry):
        prefix, cnt_at = carry
        cand = prefix + lax.shift_left(jnp.int32(1), 31 - it)
        cnt = count_key_ge(cand)
        ok = cnt >= topk
        return jnp.where(ok, cand, prefix), jnp.where(ok, cnt, cnt_at)

    thr, cnt_thr = lax.fori_loop(0, 32, bit_body, (jnp.full((1, QB), INT_MIN, I32), jnp.full((1, QB), TK, I32) * nt))

    j_scr[...] = jnp.full((1, QB), float(seq), F32)

    @pl.when(jnp.max(cnt_thr) > topk)
    def _():
        def count_keys(indicator):
            def body(t, cnt):
                k_pos = (t * TK + k_iota).astype(F32)
                return cnt + _fold_rows(jnp.add, indicator(sc_scr[tile_rows(t), :], k_pos))

            return jnp.sum(lax.fori_loop(0, nt, body, jnp.zeros((8, QB), F32)), axis=0, keepdims=True)

        need = float(topk) - count_keys(lambda key, k_pos: jnp.where(key > thr, 1.0, 0.0))
        count_tie_le = lambda cut: count_keys(
            lambda key, k_pos: jnp.where(key == thr, jnp.where(k_pos <= cut, 1.0, 0.0), 0.0))
        j_scr[...] = _tie_cutoff(count_tie_le, need, (1, QB), int(np.log2(seq)))

    cut = j_scr[...]

    def bias_tile(t, carry):
        key = sc_scr[tile_rows(t), :]
        k_pos = t * TK + k_iota
        taken = jnp.where(key > thr, 0.0,
                          jnp.where(key == thr, jnp.where(k_pos.astype(F32) <= cut, 0.0, NEG_BIG), NEG_BIG))
        sc_scr[tile_rows(t), :] = lax.bitcast_convert_type(jnp.where(k_pos <= q_pos, taken, NEG_BIG), I32)
        return carry

    lax.fori_loop(0, nt, bias_tile, 0)

    assert TK == 2 * TA
    az = az_ref[...].astype(F32)
    lg_scr, p_scr = head_scr[:ATT_HEADS], head_scr[ATT_HEADS:]
    acc_scr[...] = jnp.zeros_like(acc_scr)
    last = 2 * nt - 1

    def sub_rows(a):
        return pl.ds(pl.multiple_of(a * TA, TA), TA)

    def logits_stage(h, a, slot):
        g = h // GROUP
        kt = k_bf[sub_rows(a), g * HEAD_DIM:(g + 1) * HEAD_DIM]
        lg_scr[h][slot] = _dot_nt(kt, q_ref[:, h * HEAD_DIM:(h + 1) * HEAD_DIM])

    def softmax_stage(h, bias, slot, state):
        m, l8, _ = state
        lg = lg_scr[h][slot] + bias
        m_new = jnp.maximum(m, jnp.max(_fold_rows(jnp.maximum, lg), axis=0, keepdims=True))
        alpha = jnp.exp(m - m_new)
        p = jnp.exp(lg - m_new)
        p_scr[h][slot] = p.astype(BF16)
        return m_new, alpha * l8 + _fold_rows(jnp.add, p), alpha

    def value_stage(h, a, slot, state):
        g = h // GROUP
        vt = vt_scr[a][g * HEAD_DIM:(g + 1) * HEAD_DIM, :]
        acc_scr[h] = state[2] * acc_scr[h] + _dot(vt, p_scr[h][slot])

    def att_tile(t, state):
        for slot in range(2):
            a = 2 * t + slot
            bias = lax.bitcast_convert_type(sc_scr[sub_rows(a), :], F32)
            out = []
            for h in range(ATT_HEADS):
                value_stage(h, jnp.maximum(a - 1, 0), 1 - slot, state[h])
                out.append(softmax_stage(h, bias, slot, state[h]))
                logits_stage(h, jnp.minimum(a + 1, last), 1 - slot)
            state = tuple(out)
        return state

    init = tuple((jnp.full((1, QB), NEG_BIG, F32), jnp.zeros((8, QB), F32), jnp.ones((1, QB), F32))
                 for _ in range(ATT_HEADS))
    for h in range(ATT_HEADS):
        p_scr[h][1] = jnp.zeros((TA, QB), BF16)
        logits_stage(h, 0, 0)
    fin = lax.fori_loop(0, nt, att_tile, init)
    for h in range(ATT_HEADS):
        value_stage(h, last, 1, fin[h])
    for h in range(ATT_HEADS):
        o = (acc_scr[h] / jnp.sum(fin[h][1], axis=0, keepdims=True)).T
        c0 = h * HEAD_DIM
        out_ref[:, c0:c0 + HEAD_DIM] = (o * _silu(az[:, c0:c0 + HEAD_DIM])).astype(out_ref.dtype)


def _dsa_prompt(p_main, k_rows, v_rows, misc, batch, seq):
    QB, TK = DSA_QB, DSA_TK
    nb = seq // QB
    topk = min(TOPK_MAX, seq // 4)
    TA = DSA_TA
    assert TK >= topk and seq % TK == 0 and seq % QB == 0 and TK % DSA_TS == 0
    qrow = lambda b, i: b * nb + i
    return pl.pallas_call(
        functools.partial(_dsa_kernel, topk=topk),
        out_shape=jax.ShapeDtypeStruct((batch * seq, ATT_WIDTH), BF16),
        grid=(batch, nb),
        in_specs=[pl.BlockSpec((QB, ATT_WIDTH), lambda b, i: (qrow(b, i), 0)),
                  pl.BlockSpec((QB, 1024), lambda b, i: (qrow(b, i), 1)),
                  pl.BlockSpec((QB, ATT_WIDTH), lambda b, i: (qrow(b, i), 2)),
                  pl.BlockSpec((QB, LANES), lambda b, i: (qrow(b, i), 0)),
                  pl.BlockSpec((seq * KV_HEADS, HEAD_DIM), lambda b, i: (b, 0)),
                  pl.BlockSpec((seq * KV_HEADS, HEAD_DIM), lambda b, i: (b, 0)),
                  pl.BlockSpec((seq, LANES), lambda b, i: (b, 0))],
        out_specs=pl.BlockSpec((QB, ATT_WIDTH), lambda b, i: (qrow(b, i), 0)),
        scratch_shapes=[pltpu.VMEM((seq, KV_WIDTH), BF16),
                        pltpu.VMEM((seq // TA, KV_WIDTH, TA), BF16),
                        pltpu.VMEM((seq, LANES), BF16),
                        pltpu.VMEM((seq, LANES), BF16),
                        pltpu.VMEM((seq, QB), I32),
                        pltpu.VMEM((ATT_HEADS, HEAD_DIM, QB), F32),
                        pltpu.VMEM((1, QB), F32)]
                       + [pltpu.VMEM((2, TA, QB), F32)] * ATT_HEADS
                       + [pltpu.VMEM((2, TA, QB), BF16)] * ATT_HEADS,
        compiler_params=pltpu.CompilerParams(dimension_semantics=("parallel", "arbitrary"),
                                             vmem_limit_bytes=VMEM_LIMIT),
        name="dsa_prompt",
    )(p_main, p_main, p_main, misc, k_rows, v_rows, misc)


SCORE_ROWS = 32


def _sample_scores_kernel(pt_ref, qi_ref, wb_ref, kin_ref, cache_ref, sc_ref, snew_ref, page_buf, sems):
    b = pl.program_id(0)
    n_rows = pl.num_programs(0)
    n_pages = sc_ref.shape[1]
    slot = lax.rem(b, 2)

    def page_copy(row, p, to_slot):
        return pltpu.make_async_copy(cache_ref.at[pt_ref[row, p]], page_buf.at[to_slot, p], sems.at[to_slot])

    def start_fetch(row, to_slot):
        def body(p, carry):
            page_copy(row, p, to_slot).start()
            return carry

        lax.fori_loop(0, n_pages, body, 0)

    @pl.when(b == 0)
    def _():
        start_fetch(0, 0)

    @pl.when(b + 1 < n_rows)
    def _():
        start_fetch(b + 1, 1 - slot)

    def wait_page(p, carry):
        page_copy(b, p, slot).wait()
        return carry

    lax.fori_loop(0, n_pages, wait_page, 0)

    qi = qi_ref[0]
    wb = wb_ref[0]

    def score_rows(g, carry):
        rows = []
        for j in range(SCORE_ROWS):
            page = page_buf[slot, g * SCORE_ROWS + j]
            s = _dot(qi, page.astype(BF16))
            rows.append(jnp.sum(jnp.maximum(s, 0.0) * wb, axis=0, keepdims=True) * IDX_SCALE)
        sc_ref[0, pl.ds(pl.multiple_of(g * SCORE_ROWS, SCORE_ROWS), SCORE_ROWS), :] = jnp.concatenate(rows, axis=0)
        return carry

    lax.fori_loop(0, n_pages // SCORE_ROWS, score_rows, 0)

    ki_new = kin_ref[0].astype(BF16).astype(F32)
    s = jnp.sum(qi.astype(F32) * ki_new, axis=1, keepdims=True)
    s_new = jnp.sum(jnp.maximum(s, 0.0) * wb[:, 0:1], axis=0, keepdims=True) * IDX_SCALE
    snew_ref[0] = jnp.broadcast_to(s_new, (8, LANES))


def _sample_scores(page_table, qi16, wb, ki_new, cache_ik_t):
    bd, n_pages = page_table.shape
    assert n_pages % SCORE_ROWS == 0
    bmap = lambda b, pt: (b, 0, 0)
    return pl.pallas_call(
        _sample_scores_kernel,
        out_shape=(jax.ShapeDtypeStruct((bd, n_pages, PAGE), F32),
                   jax.ShapeDtypeStruct((bd, 8, LANES), F32)),
        grid_spec=pltpu.PrefetchScalarGridSpec(
            num_scalar_prefetch=1,
            grid=(bd,),
            in_specs=[pl.BlockSpec((1, IDX_HEADS, IDX_DIM), bmap),
                      pl.BlockSpec((1, IDX_HEADS, LANES), bmap),
                      pl.BlockSpec((1, 1, IDX_DIM), bmap),
                      pl.BlockSpec(memory_space=pl.ANY)],
            out_specs=(pl.BlockSpec((1, n_pages, PAGE), bmap),
                       pl.BlockSpec((1, 8, LANES), bmap)),
            scratch_shapes=[pltpu.VMEM((2, n_pages, IDX_DIM, PAGE), F32),
                            pltpu.SemaphoreType.DMA((2,))]),
        compiler_params=pltpu.CompilerParams(dimension_semantics=("arbitrary",),
                                             vmem_limit_bytes=VMEM_LIMIT),
        name="sample_scores",
    )(page_table, qi16, wb, ki_new, cache_ik_t)


SEL_THR, SEL_NEXT, SEL_CUT = 0, 1, 2


def _sample_select_kernel(sc_ref, snew_ref, sel_ref, *, topk):
    x = sc_ref[...]
    bd, n_pages, _ = x.shape
    past = n_pages * PAGE
    s_new = snew_ref[:, 0:1, 0:1]

    def total(v):
        return jnp.sum(jnp.sum(v, axis=1, keepdims=True), axis=2, keepdims=True)

    def count_ge(cand):
        return total(jnp.where(x >= cand, 1.0, 0.0)) + jnp.where(s_new >= cand, 1.0, 0.0)

    key = _kth_largest_key(count_ge, (bd, 1, 1), topk)
    thr = _key_to_float(key)
    thr_next = _key_to_float(key + 1)
    need = float(topk) - count_ge(thr_next)
    pos = (lax.broadcasted_iota(I32, (1, n_pages, PAGE), 1) * PAGE
           + lax.broadcasted_iota(I32, (1, n_pages, PAGE), 2)).astype(F32)

    def count_tie_le(cut):
        tie = jnp.where(x >= thr_next, 0.0, jnp.where(x >= thr, jnp.where(pos <= cut, 1.0, 0.0), 0.0))
        tie_new = jnp.where(s_new >= thr_next, 0.0, jnp.where(s_new >= thr, jnp.where(float(past) <= cut, 1.0, 0.0), 0.0))
        return total(tie) + tie_new

    cut = _tie_cutoff(count_tie_le, need, (bd, 1, 1), int(np.log2(past)) + 1)
    row = lax.broadcasted_iota(I32, (bd, 8, LANES), 1)
    sel_ref[...] = jnp.where(row == SEL_THR, thr, jnp.where(row == SEL_NEXT, thr_next, cut))


def _sample_select(scores, s_new, topk):
    bd = scores.shape[0]
    return pl.pallas_call(
        functools.partial(_sample_select_kernel, topk=topk),
        out_shape=jax.ShapeDtypeStruct((bd, 8, LANES), F32),
        compiler_params=pltpu.CompilerParams(vmem_limit_bytes=VMEM_LIMIT),
        name="sample_select",
    )(scores, s_new)


def _taken_bias(score, pos, sel):
    thr, thr_next, cut = sel[SEL_THR:SEL_THR + 1, 0:1], sel[SEL_NEXT:SEL_NEXT + 1, 0:1], sel[SEL_CUT:SEL_CUT + 1, 0:1]
    return jnp.where(score >= thr_next, 0.0,
                     jnp.where(score >= thr, jnp.where(pos <= cut, 0.0, NEG_BIG), NEG_BIG))


def _sample_compact_kernel(pt_ref, sc_ref, sel_ref, rows_ref, nsel_ref, *, slots):
    n_pages = sc_ref.shape[1]
    pos = (lax.broadcasted_iota(I32, (n_pages, PAGE), 0) * PAGE
           + lax.broadcasted_iota(I32, (n_pages, PAGE), 1)).astype(F32)
    taken = jnp.where(_taken_bias(sc_ref[0], pos, sel_ref[0]) == 0.0, 1.0, 0.0)
    before = lax.broadcasted_iota(I32, (PAGE, PAGE), 0) < lax.broadcasted_iota(I32, (PAGE, PAGE), 1)
    in_page = _dot(taken, jnp.where(before, 1.0, 0.0))
    taken_t = taken.T
    rank_t = jnp.where(taken_t > 0.0, in_page.T, -1.0)
    page_tot = jnp.sum(taken_t, axis=0, keepdims=True)
    earlier = lax.broadcasted_iota(I32, (n_pages, n_pages), 0) < lax.broadcasted_iota(I32, (n_pages, n_pages), 1)
    first_slot = _dot(jnp.broadcast_to(page_tot, (8, n_pages)),
                      jnp.where(earlier, 1.0, 0.0))[0:1]
    phys_t = (pt_ref[0] * PAGE + lax.broadcasted_iota(I32, (PAGE, n_pages), 0)).astype(F32)
    slot = lax.broadcasted_iota(I32, (slots, n_pages), 0).astype(F32)

    def rank_body(r, acc):
        r = r.astype(F32)
        row = jnp.sum(jnp.where(rank_t == r, phys_t, 0.0), axis=0, keepdims=True)
        target = jnp.where(r < page_tot, first_slot + r, -1.0)
        return acc + jnp.where(slot == target, row, 0.0)

    most = jnp.max(page_tot).astype(I32)
    acc = lax.fori_loop(0, most, rank_body, jnp.zeros((slots, n_pages), F32))
    rows_ref[0] = jnp.broadcast_to(jnp.sum(acc, axis=1, keepdims=True), (slots, LANES))
    nsel_ref[0] = jnp.broadcast_to(jnp.sum(page_tot, axis=1, keepdims=True), (8, LANES))


def _sample_compact(page_table, scores, sel, slots):
    bd, n_pages = page_table.shape
    assert n_pages % LANES == 0
    bmap = lambda b: (b, 0, 0)
    return pl.pallas_call(
        functools.partial(_sample_compact_kernel, slots=slots),
        out_shape=(jax.ShapeDtypeStruct((bd, slots, LANES), F32), jax.ShapeDtypeStruct((bd, 8, LANES), F32)),
        grid=(bd,),
        in_specs=[pl.BlockSpec((1, 1, n_pages), bmap), pl.BlockSpec((1, n_pages, PAGE), bmap),
                  pl.BlockSpec((1, 8, LANES), bmap)],
        out_specs=(pl.BlockSpec((1, slots, LANES), bmap), pl.BlockSpec((1, 8, LANES), bmap)),
        compiler_params=pltpu.CompilerParams(dimension_semantics=("parallel",), vmem_limit_bytes=VMEM_LIMIT),
        name="sample_compact",
    )(page_table.reshape(bd, 1, n_pages), scores, sel)


SC_GATHER_CHUNK = 128


def _gather_rows(table_k, table_v, idx):
    info = plsc.get_sparse_core_info()
    n_workers = info.num_cores * info.num_subcores
    n_idx = idx.shape[0]
    per_worker = n_idx // n_workers
    assert per_worker * n_workers == n_idx and per_worker % SC_GATHER_CHUNK == 0
    mesh = plsc.VectorSubcoreMesh(core_axis_name="c", subcore_axis_name="s")
    out = jax.ShapeDtypeStruct((n_idx, table_k.shape[1]), table_k.dtype)

    @functools.partial(
        pl.kernel, mesh=mesh, out_type=(out, out),
        scratch_types=[pltpu.VMEM((SC_GATHER_CHUNK,), I32),
                       pltpu.VMEM((SC_GATHER_CHUNK, table_k.shape[1]), table_k.dtype),
                       pltpu.VMEM((SC_GATHER_CHUNK, table_v.shape[1]), table_v.dtype),
                       pltpu.SemaphoreType.DMA, pltpu.SemaphoreType.DMA])
    def gather(tk_hbm, tv_hbm, idx_hbm, ok_hbm, ov_hbm, idx_v, rk_v, rv_v, sem_k, sem_v):
        worker = lax.axis_index("s") * info.num_cores + lax.axis_index("c")

        @pl.loop(0, per_worker // SC_GATHER_CHUNK)
        def _(j):
            base = worker * per_worker + j * SC_GATHER_CHUNK
            pltpu.sync_copy(idx_hbm.at[pl.ds(base, SC_GATHER_CHUNK)], idx_v)
            copy_k = pltpu.async_copy(tk_hbm.at[idx_v], rk_v, sem_k)
            copy_v = pltpu.async_copy(tv_hbm.at[idx_v], rv_v, sem_v)
            copy_k.wait()
            copy_v.wait()
            pltpu.sync_copy(rk_v, ok_hbm.at[pl.ds(base, SC_GATHER_CHUNK)])
            pltpu.sync_copy(rv_v, ov_hbm.at[pl.ds(base, SC_GATHER_CHUNK)])

    return gather(table_k, table_v, idx)


def _sample_attend_kernel(q_ref, az_ref, kn_ref, vn_ref, snew_ref, sel_ref, nsel_ref, k_ref, v_ref, out_ref, *, past):
    width = k_ref.shape[1]
    q = q_ref[0]
    col = lax.broadcasted_iota(I32, (ATT_HEADS, width), 1)
    head = lax.broadcasted_iota(I32, (ATT_HEADS, width), 0)
    own_head = col % KV_HEADS == head // GROUP
    filled = (col // KV_HEADS).astype(F32) < nsel_ref[0][0:1, 0:1]
    lg = _dot_nt(q, k_ref[0].astype(BF16))
    lg = jnp.where(own_head, jnp.where(filled, lg, NEG_BIG), NEG_BIG)
    lg_new = (jnp.sum(q.astype(F32) * kn_ref[0].astype(BF16).astype(F32), axis=1, keepdims=True)
              + _taken_bias(snew_ref[0][0:1, 0:1], float(past), sel_ref[0]))
    m = jnp.maximum(jnp.max(lg, axis=1, keepdims=True), lg_new)
    p = jnp.exp(lg - m)
    p_new = jnp.exp(lg_new - m)
    l = jnp.sum(p, axis=1, keepdims=True) + p_new
    acc = _dot(p.astype(BF16), v_ref[0].astype(BF16)) + p_new * vn_ref[0].astype(BF16).astype(F32)
    out_ref[0] = (acc / l) * _silu(az_ref[0])


def _sample_attend(q8, az8, k_new8, v_new8, s_new, sel, n_sel, k_sel, v_sel, past):
    bd, width, _ = k_sel.shape
    bmap = lambda b: (b, 0, 0)
    head_tile = pl.BlockSpec((1, ATT_HEADS, HEAD_DIM), bmap)
    par_tile = pl.BlockSpec((1, 8, LANES), bmap)
    rows_tile = pl.BlockSpec((1, width, HEAD_DIM), bmap)
    return pl.pallas_call(
        functools.partial(_sample_attend_kernel, past=past),
        out_shape=jax.ShapeDtypeStruct((bd, ATT_HEADS, HEAD_DIM), F32),
        grid=(bd,),
        in_specs=[head_tile, head_tile, head_tile, head_tile, par_tile, par_tile, par_tile, rows_tile, rows_tile],
        out_specs=head_tile,
        compiler_params=pltpu.CompilerParams(dimension_semantics=("parallel",), vmem_limit_bytes=VMEM_LIMIT),
        name="sample_attend",
    )(q8, az8, k_new8, v_new8, s_new, sel, n_sel, k_sel, v_sel)


def _mlstm_step_kernel(bi_ref, bf_ref, q_ref, k_ref, v_ref, mo_ref, mz_ref, misc_ref, nw_ref,
                       c_ref, n_ref, m_ref, out_ref, c_out, n_out, m_out):
    misc = misc_ref[0]
    eye = lax.broadcasted_iota(I32, (ML_V, ML_V), 0) == lax.broadcasted_iota(I32, (ML_V, ML_V), 1)
    for h in range(ML_HEADS):
        ig = misc[:, MISC_IG + h:MISC_IG + h + 1] + bi_ref[h]
        lf = _log_sigmoid(misc[:, MISC_FG + h:MISC_FG + h + 1] + bf_ref[h])
        m_prev = m_ref[0, h][:, 0:1]
        log_a = lf + m_prev
        m_t = jnp.maximum(log_a, ig)
        d = jnp.exp(ig - m_t)
        a = jnp.exp(log_a - m_t)
        q = q_ref[0][:, h * ML_QK:(h + 1) * ML_QK]
        k = k_ref[0][:, h * ML_QK:(h + 1) * ML_QK]
        v = v_ref[0][:, h * ML_V:(h + 1) * ML_V]
        v_col = jnp.sum(jnp.where(eye, v, 0.0), axis=1, keepdims=True)
        c = c_ref[0, h]
        n = n_ref[0, h]
        s = jnp.sum(q * k, axis=1, keepdims=True) * d
        num = a * jnp.sum(c * q, axis=1, keepdims=True) + s * v_col
        den = a * jnp.sum(n * q, axis=1, keepdims=True) + s
        h_col = num / jnp.maximum(jnp.abs(den), jnp.exp(-m_t))
        c_out[0, h] = a * c + (d * v_col) * k
        n_out[0, h] = a * n + d * k
        m_out[0, h] = jnp.broadcast_to(m_t, (1, LANES))

        h_row = jnp.sum(jnp.where(eye, h_col, 0.0), axis=0, keepdims=True)
        ms = jnp.mean(h_row * h_row, axis=1, keepdims=True)
        hn = h_row * lax.rsqrt(ms + RMS_EPS) * nw_ref[:, h * ML_V:(h + 1) * ML_V]
        gate = _sigmoid(mo_ref[0][:, h * ML_V:(h + 1) * ML_V]) * _silu(mz_ref[0][:, h * ML_V:(h + 1) * ML_V])
        out_ref[0, :, h * ML_V:(h + 1) * ML_V] = (hn * gate).astype(out_ref.dtype)


def _mlstm_step(ps_main, misc, b_i, b_f, ml_norm_w, state_c, state_n, state_m):
    bd = ps_main.shape[0]
    col = lambda j: (lambda b: (b, 0, j))
    st4 = lambda b: (b, 0, 0, 0)
    return pl.pallas_call(
        _mlstm_step_kernel,
        out_shape=(jax.ShapeDtypeStruct((bd, 1, ML_WIDTH), BF16),
                   jax.ShapeDtypeStruct(state_c.shape, F32),
                   jax.ShapeDtypeStruct(state_n.shape, F32),
                   jax.ShapeDtypeStruct(state_m.shape, F32)),
        grid=(bd,),
        in_specs=[pl.BlockSpec(memory_space=pltpu.SMEM),
                  pl.BlockSpec(memory_space=pltpu.SMEM),
                  pl.BlockSpec((1, 1, 512), col(12)),
                  pl.BlockSpec((1, 1, 512), col(13)),
                  pl.BlockSpec((1, 1, 1024), col(3)),
                  pl.BlockSpec((1, 1, 1024), col(4)),
                  pl.BlockSpec((1, 1, 1024), col(5)),
                  pl.BlockSpec((1, 1, LANES), col(0)),
                  pl.BlockSpec((1, ML_WIDTH), lambda b: (0, 0)),
                  pl.BlockSpec((1, ML_HEADS, ML_V, ML_QK), st4),
                  pl.BlockSpec((1, ML_HEADS, 1, ML_QK), st4),
                  pl.BlockSpec((1, ML_HEADS, 1, LANES), st4)],
        out_specs=(pl.BlockSpec((1, 1, ML_WIDTH), lambda b: (b, 0, 0)),
                   pl.BlockSpec((1, ML_HEADS, ML_V, ML_QK), st4),
                   pl.BlockSpec((1, ML_HEADS, 1, ML_QK), st4),
                   pl.BlockSpec((1, ML_HEADS, 1, LANES), st4)),
        compiler_params=pltpu.CompilerParams(dimension_semantics=("parallel",),
                                             vmem_limit_bytes=VMEM_LIMIT),
        name="mlstm_step",
    )(b_i, b_f, ps_main, ps_main, ps_main, ps_main, ps_main, misc, ml_norm_w.reshape(1, ML_WIDTH),
      state_c, state_n, state_m)


def _small_weight(w_t):
    offs = np.cumsum((0,) + IN_SIZES)
    ak, av, ik, iw, mi, mf = (w_t[offs[j]:offs[j + 1]] for j in (1, 2, 4, 5, 10, 11))
    pad = jnp.zeros((LANES - IDX_DIM - IDX_HEADS - 2 * ML_HEADS, w_t.shape[1]), w_t.dtype)
    w_small = jnp.concatenate([ak, av, ik, iw, mi, mf, pad], axis=0)
    assert w_small.shape[0] == SMALL_W
    return w_small


def kernel(x_prompt, x_sample, cache_k, cache_v, cache_idx_k, state_C, state_n, state_m, page_table,
           norm_w, w_in, b_igate, b_fgate, ml_norm_w, w_out, final_norm_w):
    depth = w_in.shape[0]
    batch, seq, d = x_prompt.shape
    bd, dec_seq, _ = x_sample.shape
    assert depth == 1 and dec_seq == 1 and d == D_MODEL
    n_pages = page_table.shape[1]

    w_t = jnp.swapaxes(w_in[0], 0, 1)
    w_small = _small_weight(w_t)
    w_o = w_out[0].astype(BF16)
    w_o_att, w_o_ml = w_o[:ATT_WIDTH], w_o[ATT_WIDTH:]

    xp = x_prompt.reshape(batch * seq, d)
    p_main = _project_main(xp, norm_w[0], w_t, BF16, 1024, "proj_main")
    k_rows, v_rows, misc = _project_small(xp, norm_w[0], w_small, 1024, "proj_small")
    att = _dsa_prompt(p_main, k_rows, v_rows, misc, batch, seq)
    ml, ct, m_p = _mlstm_prompt(p_main, misc, b_igate[0], b_fgate[0], ml_norm_w[0], batch, seq)
    y_prompt = _out_project(xp, att, ml, w_o_att, w_o_ml, final_norm_w, 256, "out_prompt").reshape(batch, seq, d)
    k_prompt = k_rows.reshape(1, batch, seq, KV_HEADS, HEAD_DIM)
    v_prompt = v_rows.reshape(1, batch, seq, KV_HEADS, HEAD_DIM)
    ik_prompt = misc[:, :IDX_DIM].reshape(1, batch, seq, IDX_DIM)
    c_prompt = jnp.swapaxes(ct[..., :ML_V], -1, -2)[None]
    n_prompt = ct[..., ML_V][None]
    m_prompt = m_p[:, :, 0, 0][None]

    xs = x_sample.reshape(bd, d)
    ps_main = _project_main(xs, norm_w[0], w_t, F32, bd, "proj_main_s")
    ks_rows, vs_rows, misc_s = _project_small(xs, norm_w[0], w_small, bd, "proj_small_s")
    q8 = ps_main[:, :ATT_WIDTH].reshape(bd, ATT_HEADS, HEAD_DIM).astype(BF16)
    qi16 = ps_main[:, 1024:2048].reshape(bd, IDX_HEADS, IDX_DIM).astype(BF16)
    az8 = ps_main[:, 2048:3072].reshape(bd, ATT_HEADS, HEAD_DIM)
    ki_new = misc_s[:, :IDX_DIM].reshape(bd, 1, IDX_DIM)
    wb = jnp.broadcast_to(misc_s[:, MISC_WI:MISC_WI + IDX_HEADS, None], (bd, IDX_HEADS, LANES))
    k_new8 = jnp.repeat(ks_rows.reshape(bd, KV_HEADS, HEAD_DIM), GROUP, axis=1)
    v_new8 = jnp.repeat(vs_rows.reshape(bd, KV_HEADS, HEAD_DIM), GROUP, axis=1)
    scores, s_new = _sample_scores(page_table, qi16, wb, ki_new, jnp.swapaxes(cache_idx_k[0], 1, 2))
    assert cache_k.shape[1] * PAGE < 2 ** 24
    topk_s = min(TOPK_MAX, (n_pages * PAGE + 1) // 4)
    sel = _sample_select(scores, s_new, topk_s)
    key_rows, n_sel = _sample_compact(page_table, scores, sel, topk_s)
    idx = (key_rows[:, :, :1].astype(I32) * KV_HEADS + jnp.arange(KV_HEADS, dtype=I32)).reshape(-1)
    k_sel, v_sel = _gather_rows(cache_k.reshape(-1, HEAD_DIM), cache_v.reshape(-1, HEAD_DIM), idx)
    att_s = _sample_attend(q8, az8, k_new8, v_new8, s_new, sel, n_sel,
                           k_sel.reshape(bd, topk_s * KV_HEADS, HEAD_DIM),
                           v_sel.reshape(bd, topk_s * KV_HEADS, HEAD_DIM), n_pages * PAGE)
    ml_s, c_s, n_s, m_s = _mlstm_step(
        ps_main.reshape(bd, 1, MAIN_W), misc_s.reshape(bd, 1, LANES), b_igate[0], b_fgate[0], ml_norm_w[0],
        state_C[0], state_n[0].reshape(bd, ML_HEADS, 1, ML_QK),
        jnp.broadcast_to(state_m[0][:, :, None, None], (bd, ML_HEADS, 1, LANES)))
    y_sample = _out_project(xs, att_s.reshape(bd, ATT_WIDTH).astype(BF16), ml_s.reshape(bd, ML_WIDTH),
                            w_o_att, w_o_ml, final_norm_w, bd, "out_sample").reshape(bd, 1, d)
    k_sample = ks_rows.reshape(1, bd, 1, KV_HEADS, HEAD_DIM)
    v_sample = vs_rows.reshape(1, bd, 1, KV_HEADS, HEAD_DIM)
    ik_sample = misc_s[:, :IDX_DIM].reshape(1, bd, 1, IDX_DIM)

    return (y_prompt, y_sample, k_prompt, v_prompt, ik_prompt, c_prompt, n_prompt, m_prompt,
            k_sample, v_sample, ik_sample, c_s[None], n_s.reshape(1, bd, ML_HEADS, ML_QK), m_s[:, :, 0, 0][None])
```

```python
import functools

import jax
import jax.numpy as jnp
import numpy as np
from jax import lax
from jax.experimental import pallas as pl
from jax.experimental.pallas import tpu as pltpu
from jax.experimental.pallas import tpu_sc as plsc

F32 = jnp.float32
BF16 = jnp.bfloat16
I32 = jnp.int32

D_MODEL = 2048
PAGE = 128
ATT_HEADS = 8
KV_HEADS = 2
HEAD_DIM = 128
GROUP = ATT_HEADS // KV_HEADS
ATT_WIDTH = ATT_HEADS * HEAD_DIM
KV_WIDTH = KV_HEADS * HEAD_DIM
ATT_SCALE = HEAD_DIM ** -0.5
Q_SCALE = ATT_SCALE * float(np.log2(np.e))
IDX_HEADS = 16
IDX_DIM = 64
IDX_SCALE = (IDX_HEADS * IDX_DIM) ** -0.5
TOPK_MAX = 256
ML_HEADS = 4
ML_QK = 128
ML_V = 256
ML_WIDTH = ML_HEADS * ML_V
RMS_EPS = 1e-6
IN_SIZES = (ATT_WIDTH, KV_WIDTH, KV_WIDTH, IDX_HEADS * IDX_DIM, IDX_DIM, IDX_HEADS, ATT_WIDTH,
            ML_HEADS * ML_QK, ML_HEADS * ML_QK, ML_WIDTH, ML_HEADS, ML_HEADS, ML_WIDTH, ML_WIDTH)

LANES = 128
NEG_BIG = -1e30
VMEM_LIMIT = 56 * 1024 * 1024

MAIN_W = 7168
SMALL_W = 640
MISC_WI = IDX_DIM
MISC_IG = IDX_DIM + IDX_HEADS
MISC_FG = MISC_IG + ML_HEADS


def _dot(a, b):
    return jnp.dot(a, b, preferred_element_type=F32)


def _dot_nt(a, b):
    return lax.dot_general(a, b, (((1,), (1,)), ((), ())), preferred_element_type=F32)


def _tree_reduce(op, parts):
    parts = list(parts)
    while len(parts) > 1:
        paired = [op(parts[j], parts[j + 1]) for j in range(0, len(parts) - 1, 2)]
        parts = paired + parts[len(parts) - len(parts) % 2:]
    return parts[0]


def _fold_rows(op, x):
    return _tree_reduce(op, [x[r:r + 8] for r in range(0, x.shape[0], 8)])


def _proj_small_kernel(x_ref, nw_ref, w_ref, k_ref, v_ref, misc_ref):
    x = x_ref[...]
    ms = jnp.mean(x * x, axis=-1, keepdims=True)
    h = (x * lax.rsqrt(ms + RMS_EPS) * nw_ref[...]).astype(BF16)
    res = _dot_nt(h, w_ref[...].astype(BF16))
    tm = x.shape[0]
    for g in range(KV_HEADS):
        k_ref[pl.ds(g, tm, stride=KV_HEADS), :] = res[:, g * HEAD_DIM:(g + 1) * HEAD_DIM]
        v_ref[pl.ds(g, tm, stride=KV_HEADS), :] = res[:, KV_WIDTH + g * HEAD_DIM:KV_WIDTH + (g + 1) * HEAD_DIM]
    misc_ref[...] = res[:, 2 * KV_WIDTH:]


def _project_small(x2d, norm_w, w_small, tm, name):
    m, d = x2d.shape
    kv = jax.ShapeDtypeStruct((m * KV_HEADS, HEAD_DIM), F32)
    kv_spec = pl.BlockSpec((tm * KV_HEADS, HEAD_DIM), lambda i: (i, 0))
    return pl.pallas_call(
        _proj_small_kernel,
        out_shape=(kv, kv, jax.ShapeDtypeStruct((m, LANES), F32)),
        grid=(m // tm,),
        in_specs=[pl.BlockSpec((tm, d), lambda i: (i, 0)),
                  pl.BlockSpec((1, d), lambda i: (0, 0)),
                  pl.BlockSpec((SMALL_W, d), lambda i: (0, 0))],
        out_specs=(kv_spec, kv_spec, pl.BlockSpec((tm, LANES), lambda i: (i, 0))),
        compiler_params=pltpu.CompilerParams(dimension_semantics=("parallel",), vmem_limit_bytes=VMEM_LIMIT),
        name=name,
    )(x2d, norm_w.reshape(1, d), w_small)


MAIN_TN = 512
ROW_ALIGN = 8


def _main_tiles():
    offs = np.cumsum((0,) + IN_SIZES)
    aq, iq, az, mq, mk, mv, mo, mz = (int(offs[j]) for j in (0, 3, 6, 7, 8, 9, 12, 13))
    segments = [(aq, ATT_WIDTH, Q_SCALE), (iq, IDX_HEADS * IDX_DIM, 1.0), (az, ATT_WIDTH, 1.0),
                (mv, ML_WIDTH, 1.0), (mo, ML_WIDTH, 1.0), (mz, ML_WIDTH, 1.0),
                (mq, ML_HEADS * ML_QK, 1.0), (mk, ML_HEADS * ML_QK, ML_QK ** -0.5)]
    rows, scales = [], []
    for start, width, scale in segments:
        assert start % ROW_ALIGN == 0 and width % MAIN_TN == 0
        for r in range(start, start + width, MAIN_TN):
            rows.append(r // ROW_ALIGN)
            scales.append(scale)
    assert len(rows) * MAIN_TN == MAIN_W
    return np.asarray(rows, np.int32), np.asarray(scales, np.float32)


def _proj_main_kernel(rows_ref, scale_ref, x_ref, nw_ref, w_ref, o_ref, h_scr):
    j = pl.program_id(1)

    @pl.when(j == 0)
    def _():
        x = x_ref[...]
        ms = jnp.mean(x * x, axis=-1, keepdims=True)
        h_scr[...] = (x * lax.rsqrt(ms + RMS_EPS) * nw_ref[...]).astype(BF16)

    w = (w_ref[...] * scale_ref[j]).astype(BF16)
    o_ref[...] = _dot_nt(h_scr[...], w).astype(o_ref.dtype)


def _project_main(x2d, norm_w, w_t, out_dtype, tm, name):
    m, d = x2d.shape
    rows, scales = _main_tiles()
    return pl.pallas_call(
        _proj_main_kernel,
        out_shape=jax.ShapeDtypeStruct((m, MAIN_W), out_dtype),
        grid_spec=pltpu.PrefetchScalarGridSpec(
            num_scalar_prefetch=1,
            grid=(m // tm, len(rows)),
            in_specs=[pl.BlockSpec(memory_space=pltpu.SMEM),
                      pl.BlockSpec((tm, d), lambda i, j, rows: (i, 0)),
                      pl.BlockSpec((1, d), lambda i, j, rows: (0, 0)),
                      pl.BlockSpec((pl.Element(MAIN_TN), pl.Element(d)), lambda i, j, rows: (rows[j] * ROW_ALIGN, 0))],
            out_specs=pl.BlockSpec((tm, MAIN_TN), lambda i, j, rows: (i, j)),
            scratch_shapes=[pltpu.VMEM((tm, d), BF16)]),
        compiler_params=pltpu.CompilerParams(dimension_semantics=("parallel", "arbitrary"),
                                             vmem_limit_bytes=VMEM_LIMIT),
        name=name,
    )(jnp.asarray(rows), jnp.asarray(scales), x2d, norm_w.reshape(1, d), w_t)


def _out_kernel(x_ref, a_ref, m_ref, wa_ref, wm_ref, fw_ref, o_ref):
    y = x_ref[...] + _dot(a_ref[...], wa_ref[...]) + _dot(m_ref[...], wm_ref[...])
    ms = jnp.mean(y * y, axis=-1, keepdims=True)
    o_ref[...] = y * lax.rsqrt(ms + RMS_EPS) * fw_ref[...]


def _out_project(x2d, a, mo, w_att, w_ml, final_w, tm, name):
    m, d = x2d.shape
    return pl.pallas_call(
        _out_kernel,
        out_shape=jax.ShapeDtypeStruct((m, d), F32),
        grid=(m // tm,),
        in_specs=[pl.BlockSpec((tm, d), lambda i: (i, 0)),
                  pl.BlockSpec((tm, ATT_WIDTH), lambda i: (i, 0)),
                  pl.BlockSpec((tm, ML_WIDTH), lambda i: (i, 0)),
                  pl.BlockSpec((ATT_WIDTH, d), lambda i: (0, 0)),
                  pl.BlockSpec((ML_WIDTH, d), lambda i: (0, 0)),
                  pl.BlockSpec((1, d), lambda i: (0, 0))],
        out_specs=pl.BlockSpec((tm, d), lambda i: (i, 0)),
        compiler_params=pltpu.CompilerParams(dimension_semantics=("parallel",),
                                             vmem_limit_bytes=VMEM_LIMIT),
        name=name,
    )(x2d, a, mo, w_att, w_ml, final_w.reshape(1, d))


def _log_sigmoid(x):
    return jnp.minimum(x, 0.0) - jnp.log(1.0 + jnp.exp(-jnp.abs(x)))


def _sigmoid(x):
    return 1.0 / (1.0 + jnp.exp(-x))


def _silu(x):
    return x * _sigmoid(x)


ML_CHUNK = 256
ML_AUG = ML_V + LANES


def _mlstm_kernel(bi_ref, bf_ref, q_ref, k_ref, v_ref, mo_ref, mz_ref, misc_ref, nw_ref,
                  out_ref, ct_ref, m_ref):
    L = ML_CHUNK

    @pl.when(pl.program_id(1) == 0)
    def _():
        ct_ref[...] = jnp.zeros_like(ct_ref)
        m_ref[...] = jnp.zeros_like(m_ref)

    misc = misc_ref[...]
    misc_t = misc.T
    t_idx = lax.broadcasted_iota(I32, (L, L), 0)
    s_idx = lax.broadcasted_iota(I32, (L, L), 1)
    causal = s_idx <= t_idx
    ones_col = jnp.where(lax.broadcasted_iota(I32, (L, LANES), 1) == 0, 1.0, 0.0).astype(BF16)

    for h in range(ML_HEADS):
        ig_row = misc_t[MISC_IG + h:MISC_IG + h + 1, :] + bi_ref[h]
        lf_row = _log_sigmoid(misc_t[MISC_FG + h:MISC_FG + h + 1, :] + bf_ref[h])
        lf_col = _log_sigmoid(misc[:, MISC_FG + h:MISC_FG + h + 1] + bf_ref[h])
        b_col = jnp.sum(jnp.where(causal, lf_row, 0.0), axis=1, keepdims=True)
        b_row = jnp.sum(jnp.where(t_idx <= s_idx, lf_col, 0.0), axis=0, keepdims=True)
        m_prev = m_ref[0, h][0:1, 0:1]
        log_d = jnp.where(causal, b_col - b_row + ig_row, -jnp.inf)
        log_a = b_col + m_prev
        m_t = jnp.maximum(log_a, jnp.max(log_d, axis=1, keepdims=True))
        d = jnp.exp(log_d - m_t)
        a = jnp.exp(log_a - m_t)

        q = q_ref[:, h * ML_QK:(h + 1) * ML_QK]
        k = k_ref[:, h * ML_QK:(h + 1) * ML_QK]
        v_aug = jnp.concatenate([v_ref[:, h * ML_V:(h + 1) * ML_V], ones_col], axis=1)
        s = (_dot_nt(q, k) * d).astype(BF16)
        ct = ct_ref[0, h]
        num_aug = a * _dot(q, ct.astype(BF16)) + _dot(s, v_aug)
        den = num_aug[:, ML_V:ML_V + 1]
        hh = num_aug[:, :ML_V] / jnp.maximum(jnp.abs(den), jnp.exp(-m_t))

        m_new = m_t[L - 1:L, :]
        a_end = a[L - 1:L, :]
        w_row = jnp.exp(b_row[:, L - 1:L] - b_row + ig_row - m_new)
        ktw = (k.astype(F32).T * w_row).astype(BF16)
        ct_new = a_end * ct + _dot(ktw, v_aug)
        ct_ref[0, h] = ct_new
        m_ref[0, h] = jnp.broadcast_to(m_new, (8, LANES))

        ms = jnp.mean(hh * hh, axis=1, keepdims=True)
        hn = hh * lax.rsqrt(ms + RMS_EPS) * nw_ref[:, h * ML_V:(h + 1) * ML_V]
        gate = _sigmoid(mo_ref[:, h * ML_V:(h + 1) * ML_V].astype(F32)) * _silu(mz_ref[:, h * ML_V:(h + 1) * ML_V].astype(F32))
        out_ref[:, h * ML_V:(h + 1) * ML_V] = (hn * gate).astype(out_ref.dtype)


def _mlstm_prompt(p_main, misc, b_i, b_f, ml_norm_w, batch, seq):
    L = ML_CHUNK
    nc = seq // L
    row = lambda b, c: b * nc + c
    return pl.pallas_call(
        _mlstm_kernel,
        out_shape=(jax.ShapeDtypeStruct((batch * seq, ML_WIDTH), BF16),
                   jax.ShapeDtypeStruct((batch, ML_HEADS, ML_QK, ML_AUG), F32),
                   jax.ShapeDtypeStruct((batch, ML_HEADS, 8, LANES), F32)),
        grid=(batch, nc),
        in_specs=[pl.BlockSpec(memory_space=pltpu.SMEM),
                  pl.BlockSpec(memory_space=pltpu.SMEM),
                  pl.BlockSpec((L, 512), lambda b, c: (row(b, c), 12)),
                  pl.BlockSpec((L, 512), lambda b, c: (row(b, c), 13)),
                  pl.BlockSpec((L, 1024), lambda b, c: (row(b, c), 3)),
                  pl.BlockSpec((L, 1024), lambda b, c: (row(b, c), 4)),
                  pl.BlockSpec((L, 1024), lambda b, c: (row(b, c), 5)),
                  pl.BlockSpec((L, LANES), lambda b, c: (row(b, c), 0)),
                  pl.BlockSpec((1, ML_WIDTH), lambda b, c: (0, 0))],
        out_specs=(pl.BlockSpec((L, ML_WIDTH), lambda b, c: (row(b, c), 0)),
                   pl.BlockSpec((1, ML_HEADS, ML_QK, ML_AUG), lambda b, c: (b, 0, 0, 0)),
                   pl.BlockSpec((1, ML_HEADS, 8, LANES), lambda b, c: (b, 0, 0, 0))),
        compiler_params=pltpu.CompilerParams(dimension_semantics=("parallel", "arbitrary"),
                                             vmem_limit_bytes=VMEM_LIMIT),
        name="mlstm_prompt",
    )(b_i, b_f, p_main, p_main, p_main, p_main, p_main, misc, ml_norm_w.reshape(1, ML_WIDTH))


INT_MIN = -2 ** 31


def _float_to_key(x):
    bits = lax.bitcast_convert_type(x, I32)
    return jnp.where(bits >= 0, bits, bits ^ jnp.int32(0x7FFFFFFF))


def _key_to_float(key):
    bits = jnp.where(key >= 0, key, key ^ jnp.int32(0x7FFFFFFF))
    return lax.bitcast_convert_type(bits, F32)


KEY_NEG_INF = INT_MIN + 0x7FFFFF


def _kth_largest_key(count_ge, shape, k):
    def bit_body(it, prefix):
        cand = prefix + lax.shift_left(jnp.int32(1), 31 - it)
        cand_f = _key_to_float(jnp.maximum(cand, KEY_NEG_INF))
        return jnp.where(count_ge(cand_f) >= float(k), cand, prefix)

    return lax.fori_loop(0, 32, bit_body, jnp.full(shape, INT_MIN, I32))


def _tie_cutoff(count_tie_le, need, shape, index_bits):
    def bit_body(it, lo):
        cand = lo + lax.shift_left(jnp.int32(1), index_bits - 1 - it).astype(F32)
        return jnp.where(count_tie_le(cand) < need, cand, lo)

    return lax.fori_loop(0, index_bits, bit_body, jnp.full(shape, -1.0, F32)) + 1.0


DSA_QB = 128
DSA_TK = 512
DSA_TS = 256
DSA_TA = 256
HEAD_PAIR = 2


def _dsa_kernel(q_ref, qi_ref, az_ref, miscq_ref, k_ref, v_ref, misck_ref, out_ref,
                k_bf, vt_scr, ki_lo, ki_hi, sc_scr, acc_scr, j_scr, *head_scr, topk):
    i = pl.program_id(1)
    seq = misck_ref.shape[0]
    QB, TK, TS, TA = DSA_QB, DSA_TK, DSA_TS, DSA_TA
    nt = (i * QB + QB + TK - 1) // TK

    @pl.when(i == 0)
    def _():
        for g in range(KV_HEADS):
            k_bf[:, g * HEAD_DIM:(g + 1) * HEAD_DIM] = k_ref[pl.ds(g, seq, stride=KV_HEADS), :].astype(BF16)

        def v_tile(a, carry):
            for g in range(KV_HEADS):
                rows = pl.ds(pl.multiple_of(a * (TA * KV_HEADS), TA * KV_HEADS) + g, TA, stride=KV_HEADS)
                vt_scr[a, g * HEAD_DIM:(g + 1) * HEAD_DIM, :] = v_ref[rows, :].T.astype(BF16)
            return carry

        lax.fori_loop(0, seq // TA, v_tile, 0)
        lane = lax.broadcasted_iota(I32, (seq, LANES), 1)
        lo = jnp.where(lane < IDX_DIM, misck_ref[...], 0.0)
        ki_lo[...] = lo.astype(BF16)
        ki_hi[...] = pltpu.roll(lo, IDX_DIM, axis=1).astype(BF16)

    k_iota = lax.broadcasted_iota(I32, (TK, QB), 0)
    q_pos = i * QB + lax.broadcasted_iota(I32, (TK, QB), 1)
    w_t = miscq_ref[...].T

    def tile_rows(t):
        return pl.ds(pl.multiple_of(t * TK, TK), TK)

    def score_tile(t, carry):
        rows = pl.ds(pl.multiple_of(t * TS, TS), TS)
        klo = ki_lo[rows, :]
        khi = ki_hi[rows, :]
        acc = jnp.zeros((TS, QB), F32)
        for p in range(IDX_HEADS // 2):
            qp = qi_ref[:, p * LANES:(p + 1) * LANES]
            w0 = w_t[MISC_WI + 2 * p:MISC_WI + 2 * p + 1, :]
            w1 = w_t[MISC_WI + 2 * p + 1:MISC_WI + 2 * p + 2, :]
            acc = acc + w0 * jnp.maximum(_dot_nt(klo, qp), 0.0) + w1 * jnp.maximum(_dot_nt(khi, qp), 0.0)
        k_pos = t * TS + lax.broadcasted_iota(I32, (TS, QB), 0)
        q_pos_s = i * QB + lax.broadcasted_iota(I32, (TS, QB), 1)
        score = jnp.where(k_pos <= q_pos_s, acc * IDX_SCALE + 0.0, -jnp.inf)
        sc_scr[rows, :] = _float_to_key(score)
        return carry

    def score_tiles(t, carry):
        for sub in range(TK // TS):
            score_tile(t * (TK // TS) + sub, carry)
        return carry

    lax.fori_loop(0, nt, score_tiles, 0)

    def count_key_ge(cand):
        def body(t, cnt):
            return cnt + _fold_rows(jnp.add, jnp.where(sc_scr[tile_rows(t), :] >= cand, 1, 0))

        cnt = lax.fori_loop(0, nt, body, jnp.zeros((8, QB), I32))
        return jnp.sum(cnt, axis=0, keepdims=True)

    def bit_body(it, carry):
        prefix, cnt_at = carry
        cand = prefix + lax.shift_left(jnp.int32(1), 31 - it)
        cnt = count_key_ge(cand)
        ok = cnt >= topk
        return jnp.where(ok, cand, prefix), jnp.where(ok, cnt, cnt_at)

    thr, cnt_thr = lax.fori_loop(0, 32, bit_body, (jnp.full((1, QB), INT_MIN, I32), jnp.full((1, QB), TK, I32) * nt))

    j_scr[...] = jnp.full((1, QB), float(seq), F32)

    @pl.when(jnp.max(cnt_thr) > topk)
    def _():
        def count_keys(indicator):
            def body(t, cnt):
                k_pos = (t * TK + k_iota).astype(F32)
                return cnt + _fold_rows(jnp.add, indicator(sc_scr[tile_rows(t), :], k_pos))

            return jnp.sum(lax.fori_loop(0, nt, body, jnp.zeros((8, QB), F32)), axis=0, keepdims=True)

        need = float(topk) - count_keys(lambda key, k_pos: jnp.where(key > thr, 1.0, 0.0))
        count_tie_le = lambda cut: count_keys(
            lambda key, k_pos: jnp.where(key == thr, jnp.where(k_pos <= cut, 1.0, 0.0), 0.0))
        j_scr[...] = _tie_cutoff(count_tie_le, need, (1, QB), int(np.log2(seq)))

    cut = j_scr[...]

    def bias_tile(t, carry):
        key = sc_scr[tile_rows(t), :]
        k_pos = t * TK + k_iota
        taken = jnp.where(key > thr, 0.0,
                          jnp.where(key == thr, jnp.where(k_pos.astype(F32) <= cut, 0.0, NEG_BIG), NEG_BIG))
        sc_scr[tile_rows(t), :] = lax.bitcast_convert_type(jnp.where(k_pos <= q_pos, taken, NEG_BIG), I32)
        return carry

    lax.fori_loop(0, nt, bias_tile, 0)

    assert TK == 2 * TA and GROUP % HEAD_PAIR == 0
    az = az_ref[...].astype(F32)
    n_pairs = ATT_HEADS // HEAD_PAIR
    PQ = HEAD_PAIR * QB
    lg_scr, p_scr = head_scr[:n_pairs], head_scr[n_pairs:]
    acc_scr[...] = jnp.zeros_like(acc_scr)
    last = 2 * nt - 1

    def sub_rows(a):
        return pl.ds(pl.multiple_of(a * TA, TA), TA)

    def kv_cols(j):
        g = (j * HEAD_PAIR) // GROUP
        return slice(g * HEAD_DIM, (g + 1) * HEAD_DIM)

    def logits_stage(j, a, slot):
        q_pair = jnp.concatenate([q_ref[:, h * HEAD_DIM:(h + 1) * HEAD_DIM]
                                  for h in range(j * HEAD_PAIR, (j + 1) * HEAD_PAIR)], axis=0)
        lg_scr[j][slot] = _dot_nt(k_bf[sub_rows(a), kv_cols(j)], q_pair)

    CH = 64

    def softmax_stage(j, a, slot, state):
        m, l8, _ = state
        maxes = []
        for r in range(0, TA, CH):
            bias = lax.bitcast_convert_type(sc_scr[pl.ds(pl.multiple_of(a * TA, TA) + r, CH), :], F32)
            lg = lg_scr[j][slot, r:r + CH, :] + jnp.concatenate([bias] * HEAD_PAIR, axis=1)
            lg_scr[j][slot, r:r + CH, :] = lg
            maxes.append(_fold_rows(jnp.maximum, lg))
        m_new = jnp.maximum(m, jnp.max(_tree_reduce(jnp.maximum, maxes), axis=0, keepdims=True))
        alpha = jnp.exp2(m - m_new)
        sums = []
        for r in range(0, TA, CH):
            p = jnp.exp2(lg_scr[j][slot, r:r + CH, :] - m_new)
            p_scr[j][slot, r:r + CH, :] = p.astype(BF16)
            sums.append(_fold_rows(jnp.add, p))
        return m_new, alpha * l8 + _tree_reduce(jnp.add, sums), alpha

    def value_stage(j, a, slot, state):
        vt = vt_scr[a][kv_cols(j), :]
        acc_scr[j] = state[2] * acc_scr[j] + _dot(vt, p_scr[j][slot])

    def att_tile(t, state):
        for slot in range(2):
            a = 2 * t + slot
            out = []
            for j in range(n_pairs):
                value_stage(j, jnp.maximum(a - 1, 0), 1 - slot, state[j])
                out.append(softmax_stage(j, a, slot, state[j]))
                logits_stage(j, jnp.minimum(a + 1, last), 1 - slot)
            state = tuple(out)
        return state

    init = tuple((jnp.full((1, PQ), NEG_BIG, F32), jnp.zeros((8, PQ), F32), jnp.ones((1, PQ), F32))
                 for _ in range(n_pairs))
    for j in range(n_pairs):
        p_scr[j][1] = jnp.zeros((TA, PQ), BF16)
        logits_stage(j, 0, 0)
    fin = lax.fori_loop(0, nt, att_tile, init)
    for j in range(n_pairs):
        value_stage(j, last, 1, fin[j])
    for j in range(n_pairs):
        o_t = acc_scr[j] / jnp.sum(fin[j][1], axis=0, keepdims=True)
        for hh in range(HEAD_PAIR):
            c0 = (j * HEAD_PAIR + hh) * HEAD_DIM
            o = o_t[:, hh * QB:(hh + 1) * QB].T
            out_ref[:, c0:c0 + HEAD_DIM] = (o * _silu(az[:, c0:c0 + HEAD_DIM])).astype(out_ref.dtype)


def _dsa_prompt(p_main, k_rows, v_rows, misc, batch, seq):
    QB, TK = DSA_QB, DSA_TK
    nb = seq // QB
    topk = min(TOPK_MAX, seq // 4)
    TA = DSA_TA
    assert TK >= topk and seq % TK == 0 and seq % QB == 0 and TK % DSA_TS == 0
    qrow = lambda b, i: b * nb + i
    return pl.pallas_call(
        functools.partial(_dsa_kernel, topk=topk),
        out_shape=jax.ShapeDtypeStruct((batch * seq, ATT_WIDTH), BF16),
        grid=(batch, nb),
        in_specs=[pl.BlockSpec((QB, ATT_WIDTH), lambda b, i: (qrow(b, i), 0)),
                  pl.BlockSpec((QB, 1024), lambda b, i: (qrow(b, i), 1)),
                  pl.BlockSpec((QB, ATT_WIDTH), lambda b, i: (qrow(b, i), 2)),
                  pl.BlockSpec((QB, LANES), lambda b, i: (qrow(b, i), 0)),
                  pl.BlockSpec((seq * KV_HEADS, HEAD_DIM), lambda b, i: (b, 0)),
                  pl.BlockSpec((seq * KV_HEADS, HEAD_DIM), lambda b, i: (b, 0)),
                  pl.BlockSpec((seq, LANES), lambda b, i: (b, 0))],
        out_specs=pl.BlockSpec((QB, ATT_WIDTH), lambda b, i: (qrow(b, i), 0)),
        scratch_shapes=[pltpu.VMEM((seq, KV_WIDTH), BF16),
                        pltpu.VMEM((seq // TA, KV_WIDTH, TA), BF16),
                        pltpu.VMEM((seq, LANES), BF16),
                        pltpu.VMEM((seq, LANES), BF16),
                        pltpu.VMEM((seq, QB), I32),
                        pltpu.VMEM((ATT_HEADS // HEAD_PAIR, HEAD_DIM, HEAD_PAIR * QB), F32),
                        pltpu.VMEM((1, QB), F32)]
                       + [pltpu.VMEM((2, TA, HEAD_PAIR * QB), F32)] * (ATT_HEADS // HEAD_PAIR)
                       + [pltpu.VMEM((2, TA, HEAD_PAIR * QB), BF16)] * (ATT_HEADS // HEAD_PAIR),
        compiler_params=pltpu.CompilerParams(dimension_semantics=("parallel", "arbitrary"),
                                             vmem_limit_bytes=VMEM_LIMIT),
        name="dsa_prompt",
    )(p_main, p_main, p_main, misc, k_rows, v_rows, misc)


SCORE_ROWS = 32


def _sample_scores_kernel(pt_ref, qi_ref, wb_ref, kin_ref, cache_ref, sc_ref, snew_ref, page_buf, sems):
    b = pl.program_id(0)
    n_rows = pl.num_programs(0)
    n_pages = sc_ref.shape[1]
    slot = lax.rem(b, 2)

    def page_copy(row, p, to_slot):
        return pltpu.make_async_copy(cache_ref.at[pt_ref[row, p]], page_buf.at[to_slot, p], sems.at[to_slot])

    def start_fetch(row, to_slot):
        def body(p, carry):
            page_copy(row, p, to_slot).start()
            return carry

        lax.fori_loop(0, n_pages, body, 0)

    @pl.when(b == 0)
    def _():
        start_fetch(0, 0)

    @pl.when(b + 1 < n_rows)
    def _():
        start_fetch(b + 1, 1 - slot)

    def wait_page(p, carry):
        page_copy(b, p, slot).wait()
        return carry

    lax.fori_loop(0, n_pages, wait_page, 0)

    qi = qi_ref[0]
    wb = wb_ref[0]

    def score_rows(g, carry):
        rows = []
        for j in range(SCORE_ROWS):
            page = page_buf[slot, g * SCORE_ROWS + j]
            s = _dot(qi, page.astype(BF16))
            rows.append(jnp.sum(jnp.maximum(s, 0.0) * wb, axis=0, keepdims=True) * IDX_SCALE)
        sc_ref[0, pl.ds(pl.multiple_of(g * SCORE_ROWS, SCORE_ROWS), SCORE_ROWS), :] = jnp.concatenate(rows, axis=0)
        return carry

    lax.fori_loop(0, n_pages // SCORE_ROWS, score_rows, 0)

    ki_new = kin_ref[0].astype(BF16).astype(F32)
    s = jnp.sum(qi.astype(F32) * ki_new, axis=1, keepdims=True)
    s_new = jnp.sum(jnp.maximum(s, 0.0) * wb[:, 0:1], axis=0, keepdims=True) * IDX_SCALE
    snew_ref[0] = jnp.broadcast_to(s_new, (8, LANES))


def _sample_scores(page_table, qi16, wb, ki_new, cache_ik_t):
    bd, n_pages = page_table.shape
    assert n_pages % SCORE_ROWS == 0
    bmap = lambda b, pt: (b, 0, 0)
    return pl.pallas_call(
        _sample_scores_kernel,
        out_shape=(jax.ShapeDtypeStruct((bd, n_pages, PAGE), F32),
                   jax.ShapeDtypeStruct((bd, 8, LANES), F32)),
        grid_spec=pltpu.PrefetchScalarGridSpec(
            num_scalar_prefetch=1,
            grid=(bd,),
            in_specs=[pl.BlockSpec((1, IDX_HEADS, IDX_DIM), bmap),
                      pl.BlockSpec((1, IDX_HEADS, LANES), bmap),
                      pl.BlockSpec((1, 1, IDX_DIM), bmap),
                      pl.BlockSpec(memory_space=pl.ANY)],
            out_specs=(pl.BlockSpec((1, n_pages, PAGE), bmap),
                       pl.BlockSpec((1, 8, LANES), bmap)),
            scratch_shapes=[pltpu.VMEM((2, n_pages, IDX_DIM, PAGE), F32),
                            pltpu.SemaphoreType.DMA((2,))]),
        compiler_params=pltpu.CompilerParams(dimension_semantics=("arbitrary",),
                                             vmem_limit_bytes=VMEM_LIMIT),
        name="sample_scores",
    )(page_table, qi16, wb, ki_new, cache_ik_t)


SEL_THR, SEL_NEXT, SEL_CUT = 0, 1, 2


def _sample_select_kernel(sc_ref, snew_ref, sel_ref, *, topk):
    x = sc_ref[...]
    bd, n_pages, _ = x.shape
    past = n_pages * PAGE
    s_new = snew_ref[:, 0:1, 0:1]

    def total(v):
        return jnp.sum(jnp.sum(v, axis=1, keepdims=True), axis=2, keepdims=True)

    def count_ge(cand):
        return total(jnp.where(x >= cand, 1.0, 0.0)) + jnp.where(s_new >= cand, 1.0, 0.0)

    key = _kth_largest_key(count_ge, (bd, 1, 1), topk)
    thr = _key_to_float(key)
    thr_next = _key_to_float(key + 1)
    need = float(topk) - count_ge(thr_next)
    pos = (lax.broadcasted_iota(I32, (1, n_pages, PAGE), 1) * PAGE
           + lax.broadcasted_iota(I32, (1, n_pages, PAGE), 2)).astype(F32)

    def count_tie_le(cut):
        tie = jnp.where(x >= thr_next, 0.0, jnp.where(x >= thr, jnp.where(pos <= cut, 1.0, 0.0), 0.0))
        tie_new = jnp.where(s_new >= thr_next, 0.0, jnp.where(s_new >= thr, jnp.where(float(past) <= cut, 1.0, 0.0), 0.0))
        return total(tie) + tie_new

    cut = _tie_cutoff(count_tie_le, need, (bd, 1, 1), int(np.log2(past)) + 1)
    row = lax.broadcasted_iota(I32, (bd, 8, LANES), 1)
    sel_ref[...] = jnp.where(row == SEL_THR, thr, jnp.where(row == SEL_NEXT, thr_next, cut))


def _sample_select(scores, s_new, topk):
    bd = scores.shape[0]
    return pl.pallas_call(
        functools.partial(_sample_select_kernel, topk=topk),
        out_shape=jax.ShapeDtypeStruct((bd, 8, LANES), F32),
        compiler_params=pltpu.CompilerParams(vmem_limit_bytes=VMEM_LIMIT),
        name="sample_select",
    )(scores, s_new)


def _taken_bias(score, pos, sel):
    thr, thr_next, cut = sel[SEL_THR:SEL_THR + 1, 0:1], sel[SEL_NEXT:SEL_NEXT + 1, 0:1], sel[SEL_CUT:SEL_CUT + 1, 0:1]
    return jnp.where(score >= thr_next, 0.0,
                     jnp.where(score >= thr, jnp.where(pos <= cut, 0.0, NEG_BIG), NEG_BIG))


def _sample_compact_kernel(pt_ref, sc_ref, sel_ref, rows_ref, nsel_ref, *, slots):
    n_pages = sc_ref.shape[1]
    pos = (lax.broadcasted_iota(I32, (n_pages, PAGE), 0) * PAGE
           + lax.broadcasted_iota(I32, (n_pages, PAGE), 1)).astype(F32)
    taken = jnp.where(_taken_bias(sc_ref[0], pos, sel_ref[0]) == 0.0, 1.0, 0.0)
    before = lax.broadcasted_iota(I32, (PAGE, PAGE), 0) < lax.broadcasted_iota(I32, (PAGE, PAGE), 1)
    in_page = _dot(taken, jnp.where(before, 1.0, 0.0))
    taken_t = taken.T
    rank_t = jnp.where(taken_t > 0.0, in_page.T, -1.0)
    page_tot = jnp.sum(taken_t, axis=0, keepdims=True)
    earlier = lax.broadcasted_iota(I32, (n_pages, n_pages), 0) < lax.broadcasted_iota(I32, (n_pages, n_pages), 1)
    first_slot = _dot(jnp.broadcast_to(page_tot, (8, n_pages)),
                      jnp.where(earlier, 1.0, 0.0))[0:1]
    phys_t = (pt_ref[0] * PAGE + lax.broadcasted_iota(I32, (PAGE, n_pages), 0)).astype(F32)
    slot = lax.broadcasted_iota(I32, (slots, n_pages), 0).astype(F32)

    def rank_body(r, acc):
        r = r.astype(F32)
        row = jnp.sum(jnp.where(rank_t == r, phys_t, 0.0), axis=0, keepdims=True)
        target = jnp.where(r < page_tot, first_slot + r, -1.0)
        return acc + jnp.where(slot == target, row, 0.0)

    most = jnp.max(page_tot).astype(I32)
    acc = lax.fori_loop(0, most, rank_body, jnp.zeros((slots, n_pages), F32))
    rows_ref[0] = jnp.broadcast_to(jnp.sum(acc, axis=1, keepdims=True), (slots, LANES))
    nsel_ref[0] = jnp.broadcast_to(jnp.sum(page_tot, axis=1, keepdims=True), (8, LANES))


def _sample_compact(page_table, scores, sel, slots):
    bd, n_pages = page_table.shape
    assert n_pages % LANES == 0
    bmap = lambda b: (b, 0, 0)
    return pl.pallas_call(
        functools.partial(_sample_compact_kernel, slots=slots),
        out_shape=(jax.ShapeDtypeStruct((bd, slots, LANES), F32), jax.ShapeDtypeStruct((bd, 8, LANES), F32)),
        grid=(bd,),
        in_specs=[pl.BlockSpec((1, 1, n_pages), bmap), pl.BlockSpec((1, n_pages, PAGE), bmap),
                  pl.BlockSpec((1, 8, LANES), bmap)],
        out_specs=(pl.BlockSpec((1, slots, LANES), bmap), pl.BlockSpec((1, 8, LANES), bmap)),
        compiler_params=pltpu.CompilerParams(dimension_semantics=("parallel",), vmem_limit_bytes=VMEM_LIMIT),
        name="sample_compact",
    )(page_table.reshape(bd, 1, n_pages), scores, sel)


SC_GATHER_CHUNK = 128


def _gather_rows(table_k, table_v, idx):
    info = plsc.get_sparse_core_info()
    n_workers = info.num_cores * info.num_subcores
    n_idx = idx.shape[0]
    per_worker = n_idx // n_workers
    assert per_worker * n_workers == n_idx and per_worker % SC_GATHER_CHUNK == 0
    mesh = plsc.VectorSubcoreMesh(core_axis_name="c", subcore_axis_name="s")
    out = jax.ShapeDtypeStruct((n_idx, table_k.shape[1]), table_k.dtype)

    @functools.partial(
        pl.kernel, mesh=mesh, out_type=(out, out),
        scratch_types=[pltpu.VMEM((SC_GATHER_CHUNK,), I32),
                       pltpu.VMEM((SC_GATHER_CHUNK, table_k.shape[1]), table_k.dtype),
                       pltpu.VMEM((SC_GATHER_CHUNK, table_v.shape[1]), table_v.dtype),
                       pltpu.SemaphoreType.DMA, pltpu.SemaphoreType.DMA])
    def gather(tk_hbm, tv_hbm, idx_hbm, ok_hbm, ov_hbm, idx_v, rk_v, rv_v, sem_k, sem_v):
        worker = lax.axis_index("s") * info.num_cores + lax.axis_index("c")

        @pl.loop(0, per_worker // SC_GATHER_CHUNK)
        def _(j):
            base = worker * per_worker + j * SC_GATHER_CHUNK
            pltpu.sync_copy(idx_hbm.at[pl.ds(base, SC_GATHER_CHUNK)], idx_v)
            copy_k = pltpu.async_copy(tk_hbm.at[idx_v], rk_v, sem_k)
            copy_v = pltpu.async_copy(tv_hbm.at[idx_v], rv_v, sem_v)
            copy_k.wait()
            copy_v.wait()
            pltpu.sync_copy(rk_v, ok_hbm.at[pl.ds(base, SC_GATHER_CHUNK)])
            pltpu.sync_copy(rv_v, ov_hbm.at[pl.ds(base, SC_GATHER_CHUNK)])

    return gather(table_k, table_v, idx)


def _sample_attend_kernel(q_ref, az_ref, kn_ref, vn_ref, snew_ref, sel_ref, nsel_ref, k_ref, v_ref, out_ref, *, past):
    width = k_ref.shape[1]
    q = q_ref[0]
    col = lax.broadcasted_iota(I32, (ATT_HEADS, width), 1)
    head = lax.broadcasted_iota(I32, (ATT_HEADS, width), 0)
    own_head = col % KV_HEADS == head // GROUP
    filled = (col // KV_HEADS).astype(F32) < nsel_ref[0][0:1, 0:1]
    lg = _dot_nt(q, k_ref[0].astype(BF16))
    lg = jnp.where(own_head, jnp.where(filled, lg, NEG_BIG), NEG_BIG)
    lg_new = (jnp.sum(q.astype(F32) * kn_ref[0].astype(BF16).astype(F32), axis=1, keepdims=True)
              + _taken_bias(snew_ref[0][0:1, 0:1], float(past), sel_ref[0]))
    m = jnp.maximum(jnp.max(lg, axis=1, keepdims=True), lg_new)
    p = jnp.exp2(lg - m)
    p_new = jnp.exp2(lg_new - m)
    l = jnp.sum(p, axis=1, keepdims=True) + p_new
    acc = _dot(p.astype(BF16), v_ref[0].astype(BF16)) + p_new * vn_ref[0].astype(BF16).astype(F32)
    out_ref[0] = (acc / l) * _silu(az_ref[0])


def _sample_attend(q8, az8, k_new8, v_new8, s_new, sel, n_sel, k_sel, v_sel, past):
    bd, width, _ = k_sel.shape
    bmap = lambda b: (b, 0, 0)
    head_tile = pl.BlockSpec((1, ATT_HEADS, HEAD_DIM), bmap)
    par_tile = pl.BlockSpec((1, 8, LANES), bmap)
    rows_tile = pl.BlockSpec((1, width, HEAD_DIM), bmap)
    return pl.pallas_call(
        functools.partial(_sample_attend_kernel, past=past),
        out_shape=jax.ShapeDtypeStruct((bd, ATT_HEADS, HEAD_DIM), F32),
        grid=(bd,),
        in_specs=[head_tile, head_tile, head_tile, head_tile, par_tile, par_tile, par_tile, rows_tile, rows_tile],
        out_specs=head_tile,
        compiler_params=pltpu.CompilerParams(dimension_semantics=("parallel",), vmem_limit_bytes=VMEM_LIMIT),
        name="sample_attend",
    )(q8, az8, k_new8, v_new8, s_new, sel, n_sel, k_sel, v_sel)


def _mlstm_step_kernel(bi_ref, bf_ref, q_ref, k_ref, v_ref, mo_ref, mz_ref, misc_ref, nw_ref,
                       c_ref, n_ref, m_ref, out_ref, c_out, n_out, m_out):
    misc = misc_ref[0]
    eye = lax.broadcasted_iota(I32, (ML_V, ML_V), 0) == lax.broadcasted_iota(I32, (ML_V, ML_V), 1)
    for h in range(ML_HEADS):
        ig = misc[:, MISC_IG + h:MISC_IG + h + 1] + bi_ref[h]
        lf = _log_sigmoid(misc[:, MISC_FG + h:MISC_FG + h + 1] + bf_ref[h])
        m_prev = m_ref[0, h][:, 0:1]
        log_a = lf + m_prev
        m_t = jnp.maximum(log_a, ig)
        d = jnp.exp(ig - m_t)
        a = jnp.exp(log_a - m_t)
        q = q_ref[0][:, h * ML_QK:(h + 1) * ML_QK]
        k = k_ref[0][:, h * ML_QK:(h + 1) * ML_QK]
        v = v_ref[0][:, h * ML_V:(h + 1) * ML_V]
        v_col = jnp.sum(jnp.where(eye, v, 0.0), axis=1, keepdims=True)
        c = c_ref[0, h]
        n = n_ref[0, h]
        s = jnp.sum(q * k, axis=1, keepdims=True) * d
        num = a * jnp.sum(c * q, axis=1, keepdims=True) + s * v_col
        den = a * jnp.sum(n * q, axis=1, keepdims=True) + s
        h_col = num / jnp.maximum(jnp.abs(den), jnp.exp(-m_t))
        c_out[0, h] = a * c + (d * v_col) * k
        n_out[0, h] = a * n + d * k
        m_out[0, h] = jnp.broadcast_to(m_t, (1, LANES))

        h_row = jnp.sum(jnp.where(eye, h_col, 0.0), axis=0, keepdims=True)
        ms = jnp.mean(h_row * h_row, axis=1, keepdims=True)
        hn = h_row * lax.rsqrt(ms + RMS_EPS) * nw_ref[:, h * ML_V:(h + 1) * ML_V]
        gate = _sigmoid(mo_ref[0][:, h * ML_V:(h + 1) * ML_V]) * _silu(mz_ref[0][:, h * ML_V:(h + 1) * ML_V])
        out_ref[0, :, h * ML_V:(h + 1) * ML_V] = (hn * gate).astype(out_ref.dtype)


def _mlstm_step(ps_main, misc, b_i, b_f, ml_norm_w, state_c, state_n, state_m):
    bd = ps_main.shape[0]
    col = lambda j: (lambda b: (b, 0, j))
    st4 = lambda b: (b, 0, 0, 0)
    return pl.pallas_call(
        _mlstm_step_kernel,
        out_shape=(jax.ShapeDtypeStruct((bd, 1, ML_WIDTH), BF16),
                   jax.ShapeDtypeStruct(state_c.shape, F32),
                   jax.ShapeDtypeStruct(state_n.shape, F32),
                   jax.ShapeDtypeStruct(state_m.shape, F32)),
        grid=(bd,),
        in_specs=[pl.BlockSpec(memory_space=pltpu.SMEM),
                  pl.BlockSpec(memory_space=pltpu.SMEM),
                  pl.BlockSpec((1, 1, 512), col(12)),
                  pl.BlockSpec((1, 1, 512), col(13)),
                  pl.BlockSpec((1, 1, 1024), col(3)),
                  pl.BlockSpec((1, 1, 1024), col(4)),
                  pl.BlockSpec((1, 1, 1024), col(5)),
                  pl.BlockSpec((1, 1, LANES), col(0)),
                  pl.BlockSpec((1, ML_WIDTH), lambda b: (0, 0)),
                  pl.BlockSpec((1, ML_HEADS, ML_V, ML_QK), st4),
                  pl.BlockSpec((1, ML_HEADS, 1, ML_QK), st4),
                  pl.BlockSpec((1, ML_HEADS, 1, LANES), st4)],
        out_specs=(pl.BlockSpec((1, 1, ML_WIDTH), lambda b: (b, 0, 0)),
                   pl.BlockSpec((1, ML_HEADS, ML_V, ML_QK), st4),
                   pl.BlockSpec((1, ML_HEADS, 1, ML_QK), st4),
                   pl.BlockSpec((1, ML_HEADS, 1, LANES), st4)),
        compiler_params=pltpu.CompilerParams(dimension_semantics=("parallel",),
                                             vmem_limit_bytes=VMEM_LIMIT),
        name="mlstm_step",
    )(b_i, b_f, ps_main, ps_main, ps_main, ps_main, ps_main, misc, ml_norm_w.reshape(1, ML_WIDTH),
      state_c, state_n, state_m)


def _small_weight(w_t):
    offs = np.cumsum((0,) + IN_SIZES)
    ak, av, ik, iw, mi, mf = (w_t[offs[j]:offs[j + 1]] for j in (1, 2, 4, 5, 10, 11))
    pad = jnp.zeros((LANES - IDX_DIM - IDX_HEADS - 2 * ML_HEADS, w_t.shape[1]), w_t.dtype)
    w_small = jnp.concatenate([ak, av, ik, iw, mi, mf, pad], axis=0)
    assert w_small.shape[0] == SMALL_W
    return w_small


def kernel(x_prompt, x_sample, cache_k, cache_v, cache_idx_k, state_C, state_n, state_m, page_table,
           norm_w, w_in, b_igate, b_fgate, ml_norm_w, w_out, final_norm_w):
    depth = w_in.shape[0]
    batch, seq, d = x_prompt.shape
    bd, dec_seq, _ = x_sample.shape
    assert depth == 1 and dec_seq == 1 and d == D_MODEL
    n_pages = page_table.shape[1]

    w_t = jnp.swapaxes(w_in[0], 0, 1)
    w_small = _small_weight(w_t)
    w_o = w_out[0].astype(BF16)
    w_o_att, w_o_ml = w_o[:ATT_WIDTH], w_o[ATT_WIDTH:]

    xp = x_prompt.reshape(batch * seq, d)
    p_main = _project_main(xp, norm_w[0], w_t, BF16, 1024, "proj_main")
    k_rows, v_rows, misc = _project_small(xp, norm_w[0], w_small, 1024, "proj_small")
    att = _dsa_prompt(p_main, k_rows, v_rows, misc, batch, seq)
    ml, ct, m_p = _mlstm_prompt(p_main, misc, b_igate[0], b_fgate[0], ml_norm_w[0], batch, seq)
    y_prompt = _out_project(xp, att, ml, w_o_att, w_o_ml, final_norm_w, 256, "out_prompt").reshape(batch, seq, d)
    k_prompt = k_rows.reshape(1, batch, seq, KV_HEADS, HEAD_DIM)
    v_prompt = v_rows.reshape(1, batch, seq, KV_HEADS, HEAD_DIM)
    ik_prompt = misc[:, :IDX_DIM].reshape(1, batch, seq, IDX_DIM)
    c_prompt = jnp.swapaxes(ct[..., :ML_V], -1, -2)[None]
    n_prompt = ct[..., ML_V][None]
    m_prompt = m_p[:, :, 0, 0][None]

    xs = x_sample.reshape(bd, d)
    ps_main = _project_main(xs, norm_w[0], w_t, F32, bd, "proj_main_s")
    ks_rows, vs_rows, misc_s = _project_small(xs, norm_w[0], w_small, bd, "proj_small_s")
    q8 = ps_main[:, :ATT_WIDTH].reshape(bd, ATT_HEADS, HEAD_DIM).astype(BF16)
    qi16 = ps_main[:, 1024:2048].reshape(bd, IDX_HEADS, IDX_DIM).astype(BF16)
    az8 = ps_main[:, 2048:3072].reshape(bd, ATT_HEADS, HEAD_DIM)
    ki_new = misc_s[:, :IDX_DIM].reshape(bd, 1, IDX_DIM)
    wb = jnp.broadcast_to(misc_s[:, MISC_WI:MISC_WI + IDX_HEADS, None], (bd, IDX_HEADS, LANES))
    k_new8 = jnp.repeat(ks_rows.reshape(bd, KV_HEADS, HEAD_DIM), GROUP, axis=1)
    v_new8 = jnp.repeat(vs_rows.reshape(bd, KV_HEADS, HEAD_DIM), GROUP, axis=1)
    scores, s_new = _sample_scores(page_table, qi16, wb, ki_new, jnp.swapaxes(cache_idx_k[0], 1, 2))
    assert cache_k.shape[1] * PAGE < 2 ** 24
    topk_s = min(TOPK_MAX, (n_pages * PAGE + 1) // 4)
    sel = _sample_select(scores, s_new, topk_s)
    key_rows, n_sel = _sample_compact(page_table, scores, sel, topk_s)
    idx = (key_rows[:, :, :1].astype(I32) * KV_HEADS + jnp.arange(KV_HEADS, dtype=I32)).reshape(-1)
    k_sel, v_sel = _gather_rows(cache_k.reshape(-1, HEAD_DIM), cache_v.reshape(-1, HEAD_DIM), idx)
    att_s = _sample_attend(q8, az8, k_new8, v_new8, s_new, sel, n_sel,
                           k_sel.reshape(bd, topk_s * KV_HEADS, HEAD_DIM),
                           v_sel.reshape(bd, topk_s * KV_HEADS, HEAD_DIM), n_pages * PAGE)
    ml_s, c_s, n_s, m_s = _mlstm_step(
        ps_main.reshape(bd, 1, MAIN_W), misc_s.reshape(bd, 1, LANES), b_igate[0], b_fgate[0], ml_norm_w[0],
        state_C[0], state_n[0].reshape(bd, ML_HEADS, 1, ML_QK),
        jnp.broadcast_to(state_m[0][:, :, None, None], (bd, ML_HEADS, 1, LANES)))
    y_sample = _out_project(xs, att_s.reshape(bd, ATT_WIDTH).astype(BF16), ml_s.reshape(bd, ML_WIDTH),
                            w_o_att, w_o_ml, final_norm_w, bd, "out_sample").reshape(bd, 1, d)
    k_sample = ks_rows.reshape(1, bd, 1, KV_HEADS, HEAD_DIM)
    v_sample = vs_rows.reshape(1, bd, 1, KV_HEADS, HEAD_DIM)
    ik_sample = misc_s[:, :IDX_DIM].reshape(1, bd, 1, IDX_DIM)

    return (y_prompt, y_sample, k_prompt, v_prompt, ik_prompt, c_prompt, n_prompt, m_prompt,
            k_sample, v_sample, ik_sample, c_s[None], n_s.reshape(1, bd, ML_HEADS, ML_QK), m_s[:, :, 0, 0][None])
```

```python
import functools

import jax
import jax.numpy as jnp
import numpy as np
from jax import lax
from jax.experimental import pallas as pl
from jax.experimental.pallas import tpu as pltpu
from jax.experimental.pallas import tpu_sc as plsc

F32 = jnp.float32
BF16 = jnp.bfloat16
I32 = jnp.int32

D_MODEL = 2048
PAGE = 128
ATT_HEADS = 8
KV_HEADS = 2
HEAD_DIM = 128
GROUP = ATT_HEADS // KV_HEADS
ATT_WIDTH = ATT_HEADS * HEAD_DIM
KV_WIDTH = KV_HEADS * HEAD_DIM
ATT_SCALE = HEAD_DIM ** -0.5
Q_SCALE = ATT_SCALE * float(np.log2(np.e))
IDX_HEADS = 16
IDX_DIM = 64
IDX_SCALE = (IDX_HEADS * IDX_DIM) ** -0.5
TOPK_MAX = 256
ML_HEADS = 4
ML_QK = 128
ML_V = 256
ML_WIDTH = ML_HEADS * ML_V
RMS_EPS = 1e-6
IN_SIZES = (ATT_WIDTH, KV_WIDTH, KV_WIDTH, IDX_HEADS * IDX_DIM, IDX_DIM, IDX_HEADS, ATT_WIDTH,
            ML_HEADS * ML_QK, ML_HEADS * ML_QK, ML_WIDTH, ML_HEADS, ML_HEADS, ML_WIDTH, ML_WIDTH)

LANES = 128
NEG_BIG = -1e30
VMEM_LIMIT = 56 * 1024 * 1024

MAIN_W = 7168
SMALL_W = 640
MISC_WI = IDX_DIM
MISC_IG = IDX_DIM + IDX_HEADS
MISC_FG = MISC_IG + ML_HEADS


def _dot(a, b):
    return jnp.dot(a, b, preferred_element_type=F32)


def _dot_nt(a, b):
    return lax.dot_general(a, b, (((1,), (1,)), ((), ())), preferred_element_type=F32)


def _tree_reduce(op, parts):
    parts = list(parts)
    while len(parts) > 1:
        paired = [op(parts[j], parts[j + 1]) for j in range(0, len(parts) - 1, 2)]
        parts = paired + parts[len(parts) - len(parts) % 2:]
    return parts[0]


def _fold_rows(op, x):
    return _tree_reduce(op, [x[r:r + 8] for r in range(0, x.shape[0], 8)])


def _proj_small_kernel(x_ref, nw_ref, w_ref, k_ref, v_ref, misc_ref):
    x = x_ref[...]
    ms = jnp.mean(x * x, axis=-1, keepdims=True)
    h = (x * lax.rsqrt(ms + RMS_EPS) * nw_ref[...]).astype(BF16)
    res = _dot_nt(h, w_ref[...].astype(BF16))
    tm = x.shape[0]
    for g in range(KV_HEADS):
        k_ref[pl.ds(g, tm, stride=KV_HEADS), :] = res[:, g * HEAD_DIM:(g + 1) * HEAD_DIM]
        v_ref[pl.ds(g, tm, stride=KV_HEADS), :] = res[:, KV_WIDTH + g * HEAD_DIM:KV_WIDTH + (g + 1) * HEAD_DIM]
    misc_ref[...] = res[:, 2 * KV_WIDTH:]


def _project_small(x2d, norm_w, w_small, tm, name):
    m, d = x2d.shape
    kv = jax.ShapeDtypeStruct((m * KV_HEADS, HEAD_DIM), F32)
    kv_spec = pl.BlockSpec((tm * KV_HEADS, HEAD_DIM), lambda i: (i, 0))
    return pl.pallas_call(
        _proj_small_kernel,
        out_shape=(kv, kv, jax.ShapeDtypeStruct((m, LANES), F32)),
        grid=(m // tm,),
        in_specs=[pl.BlockSpec((tm, d), lambda i: (i, 0)),
                  pl.BlockSpec((1, d), lambda i: (0, 0)),
                  pl.BlockSpec((SMALL_W, d), lambda i: (0, 0))],
        out_specs=(kv_spec, kv_spec, pl.BlockSpec((tm, LANES), lambda i: (i, 0))),
        compiler_params=pltpu.CompilerParams(dimension_semantics=("parallel",), vmem_limit_bytes=VMEM_LIMIT),
        name=name,
    )(x2d, norm_w.reshape(1, d), w_small)


MAIN_TN = 512
ROW_ALIGN = 8


def _main_tiles():
    offs = np.cumsum((0,) + IN_SIZES)
    aq, iq, az, mq, mk, mv, mo, mz = (int(offs[j]) for j in (0, 3, 6, 7, 8, 9, 12, 13))
    segments = [(aq, ATT_WIDTH, Q_SCALE), (iq, IDX_HEADS * IDX_DIM, 1.0), (az, ATT_WIDTH, 1.0),
                (mv, ML_WIDTH, 1.0), (mo, ML_WIDTH, 1.0), (mz, ML_WIDTH, 1.0),
                (mq, ML_HEADS * ML_QK, 1.0), (mk, ML_HEADS * ML_QK, ML_QK ** -0.5)]
    rows, scales = [], []
    for start, width, scale in segments:
        assert start % ROW_ALIGN == 0 and width % MAIN_TN == 0
        for r in range(start, start + width, MAIN_TN):
            rows.append(r // ROW_ALIGN)
            scales.append(scale)
    assert len(rows) * MAIN_TN == MAIN_W
    return np.asarray(rows, np.int32), np.asarray(scales, np.float32)


def _proj_main_kernel(rows_ref, scale_ref, x_ref, nw_ref, w_ref, o_ref, h_scr):
    j = pl.program_id(1)

    @pl.when(j == 0)
    def _():
        x = x_ref[...]
        ms = jnp.mean(x * x, axis=-1, keepdims=True)
        h_scr[...] = (x * lax.rsqrt(ms + RMS_EPS) * nw_ref[...]).astype(BF16)

    w = (w_ref[...] * scale_ref[j]).astype(BF16)
    o_ref[...] = _dot_nt(h_scr[...], w).astype(o_ref.dtype)


def _project_main(x2d, norm_w, w_t, out_dtype, tm, name):
    m, d = x2d.shape
    rows, scales = _main_tiles()
    return pl.pallas_call(
        _proj_main_kernel,
        out_shape=jax.ShapeDtypeStruct((m, MAIN_W), out_dtype),
        grid_spec=pltpu.PrefetchScalarGridSpec(
            num_scalar_prefetch=1,
            grid=(m // tm, len(rows)),
            in_specs=[pl.BlockSpec(memory_space=pltpu.SMEM),
                      pl.BlockSpec((tm, d), lambda i, j, rows: (i, 0)),
                      pl.BlockSpec((1, d), lambda i, j, rows: (0, 0)),
                      pl.BlockSpec((pl.Element(MAIN_TN), pl.Element(d)), lambda i, j, rows: (rows[j] * ROW_ALIGN, 0))],
            out_specs=pl.BlockSpec((tm, MAIN_TN), lambda i, j, rows: (i, j)),
            scratch_shapes=[pltpu.VMEM((tm, d), BF16)]),
        compiler_params=pltpu.CompilerParams(dimension_semantics=("parallel", "arbitrary"),
                                             vmem_limit_bytes=VMEM_LIMIT),
        name=name,
    )(jnp.asarray(rows), jnp.asarray(scales), x2d, norm_w.reshape(1, d), w_t)


def _out_kernel(x_ref, a_ref, m_ref, wa_ref, wm_ref, fw_ref, o_ref):
    y = x_ref[...] + _dot(a_ref[...], wa_ref[...]) + _dot(m_ref[...], wm_ref[...])
    ms = jnp.mean(y * y, axis=-1, keepdims=True)
    o_ref[...] = y * lax.rsqrt(ms + RMS_EPS) * fw_ref[...]


def _out_project(x2d, a, mo, w_att, w_ml, final_w, tm, name):
    m, d = x2d.shape
    return pl.pallas_call(
        _out_kernel,
        out_shape=jax.ShapeDtypeStruct((m, d), F32),
        grid=(m // tm,),
        in_specs=[pl.BlockSpec((tm, d), lambda i: (i, 0)),
                  pl.BlockSpec((tm, ATT_WIDTH), lambda i: (i, 0)),
                  pl.BlockSpec((tm, ML_WIDTH), lambda i: (i, 0)),
                  pl.BlockSpec((ATT_WIDTH, d), lambda i: (0, 0)),
                  pl.BlockSpec((ML_WIDTH, d), lambda i: (0, 0)),
                  pl.BlockSpec((1, d), lambda i: (0, 0))],
        out_specs=pl.BlockSpec((tm, d), lambda i: (i, 0)),
        compiler_params=pltpu.CompilerParams(dimension_semantics=("parallel",),
                                             vmem_limit_bytes=VMEM_LIMIT),
        name=name,
    )(x2d, a, mo, w_att, w_ml, final_w.reshape(1, d))


def _log_sigmoid(x):
    return jnp.minimum(x, 0.0) - jnp.log(1.0 + jnp.exp(-jnp.abs(x)))


def _sigmoid(x):
    return 0.5 * jnp.tanh(0.5 * x) + 0.5


def _silu(x):
    return x * _sigmoid(x)


ML_CHUNK = 256
ML_AUG = ML_V + LANES


def _mlstm_kernel(bi_ref, bf_ref, q_ref, k_ref, v_ref, mo_ref, mz_ref, misc_ref, nw_ref,
                  out_ref, ct_ref, m_ref):
    L = ML_CHUNK

    @pl.when(pl.program_id(1) == 0)
    def _():
        ct_ref[...] = jnp.zeros_like(ct_ref)
        m_ref[...] = jnp.zeros_like(m_ref)

    lane = lax.broadcasted_iota(I32, (1, LANES), 1)
    bias = jnp.zeros((1, LANES), F32)
    for h in range(ML_HEADS):
        bias = jnp.where(lane == MISC_IG + h, bi_ref[h], jnp.where(lane == MISC_FG + h, bf_ref[h], bias))
    pre = misc_ref[...] + bias
    gates = jnp.where(jnp.logical_and(lane >= MISC_FG, lane < MISC_FG + ML_HEADS), _log_sigmoid(pre), pre)
    gates_t = gates.T
    t_idx = lax.broadcasted_iota(I32, (L, L), 0)
    s_idx = lax.broadcasted_iota(I32, (L, L), 1)
    causal = s_idx <= t_idx
    ones_col = jnp.where(lax.broadcasted_iota(I32, (L, LANES), 1) == 0, 1.0, 0.0).astype(BF16)

    for h in range(ML_HEADS):
        ig_row = gates_t[MISC_IG + h:MISC_IG + h + 1, :]
        lf_row = gates_t[MISC_FG + h:MISC_FG + h + 1, :]
        lf_col = gates[:, MISC_FG + h:MISC_FG + h + 1]
        b_col = jnp.sum(jnp.where(causal, lf_row, 0.0), axis=1, keepdims=True)
        b_row = jnp.sum(jnp.where(t_idx <= s_idx, lf_col, 0.0), axis=0, keepdims=True)
        m_prev = m_ref[0, h][0:1, 0:1]
        log_d = jnp.where(causal, b_col - b_row + ig_row, -jnp.inf)
        log_a = b_col + m_prev
        m_t = jnp.maximum(log_a, jnp.max(log_d, axis=1, keepdims=True))
        d = jnp.exp(log_d - m_t)
        a = jnp.exp(log_a - m_t)

        q = q_ref[:, h * ML_QK:(h + 1) * ML_QK]
        k = k_ref[:, h * ML_QK:(h + 1) * ML_QK]
        v_aug = jnp.concatenate([v_ref[:, h * ML_V:(h + 1) * ML_V], ones_col], axis=1)
        s = (_dot_nt(q, k) * d).astype(BF16)
        ct = ct_ref[0, h]
        num_aug = a * _dot(q, ct.astype(BF16)) + _dot(s, v_aug)
        den = num_aug[:, ML_V:ML_V + 1]
        hh = num_aug[:, :ML_V] / jnp.maximum(jnp.abs(den), jnp.exp(-m_t))

        m_new = m_t[L - 1:L, :]
        a_end = a[L - 1:L, :]
        w_row = jnp.exp(b_row[:, L - 1:L] - b_row + ig_row - m_new)
        ktw = (k.astype(F32).T * w_row).astype(BF16)
        ct_new = a_end * ct + _dot(ktw, v_aug)
        ct_ref[0, h] = ct_new
        m_ref[0, h] = jnp.broadcast_to(m_new, (8, LANES))

        ms = jnp.mean(hh * hh, axis=1, keepdims=True)
        hn = hh * lax.rsqrt(ms + RMS_EPS) * nw_ref[:, h * ML_V:(h + 1) * ML_V]
        gate = _sigmoid(mo_ref[:, h * ML_V:(h + 1) * ML_V].astype(F32)) * _silu(mz_ref[:, h * ML_V:(h + 1) * ML_V].astype(F32))
        out_ref[:, h * ML_V:(h + 1) * ML_V] = (hn * gate).astype(out_ref.dtype)


def _mlstm_prompt(p_main, misc, b_i, b_f, ml_norm_w, batch, seq):
    L = ML_CHUNK
    nc = seq // L
    row = lambda b, c: b * nc + c
    return pl.pallas_call(
        _mlstm_kernel,
        out_shape=(jax.ShapeDtypeStruct((batch * seq, ML_WIDTH), BF16),
                   jax.ShapeDtypeStruct((batch, ML_HEADS, ML_QK, ML_AUG), F32),
                   jax.ShapeDtypeStruct((batch, ML_HEADS, 8, LANES), F32)),
        grid=(batch, nc),
        in_specs=[pl.BlockSpec(memory_space=pltpu.SMEM),
                  pl.BlockSpec(memory_space=pltpu.SMEM),
                  pl.BlockSpec((L, 512), lambda b, c: (row(b, c), 12)),
                  pl.BlockSpec((L, 512), lambda b, c: (row(b, c), 13)),
                  pl.BlockSpec((L, 1024), lambda b, c: (row(b, c), 3)),
                  pl.BlockSpec((L, 1024), lambda b, c: (row(b, c), 4)),
                  pl.BlockSpec((L, 1024), lambda b, c: (row(b, c), 5)),
                  pl.BlockSpec((L, LANES), lambda b, c: (row(b, c), 0)),
                  pl.BlockSpec((1, ML_WIDTH), lambda b, c: (0, 0))],
        out_specs=(pl.BlockSpec((L, ML_WIDTH), lambda b, c: (row(b, c), 0)),
                   pl.BlockSpec((1, ML_HEADS, ML_QK, ML_AUG), lambda b, c: (b, 0, 0, 0)),
                   pl.BlockSpec((1, ML_HEADS, 8, LANES), lambda b, c: (b, 0, 0, 0))),
        compiler_params=pltpu.CompilerParams(dimension_semantics=("parallel", "arbitrary"),
                                             vmem_limit_bytes=VMEM_LIMIT),
        name="mlstm_prompt",
    )(b_i, b_f, p_main, p_main, p_main, p_main, p_main, misc, ml_norm_w.reshape(1, ML_WIDTH))


INT_MIN = -2 ** 31


def _float_to_key(x):
    bits = lax.bitcast_convert_type(x, I32)
    return jnp.where(bits >= 0, bits, bits ^ jnp.int32(0x7FFFFFFF))


def _key_to_float(key):
    bits = jnp.where(key >= 0, key, key ^ jnp.int32(0x7FFFFFFF))
    return lax.bitcast_convert_type(bits, F32)


KEY_NEG_INF = INT_MIN + 0x7FFFFF


def _kth_largest_key(count_ge, shape, k):
    def bit_body(it, prefix):
        cand = prefix + lax.shift_left(jnp.int32(1), 31 - it)
        cand_f = _key_to_float(jnp.maximum(cand, KEY_NEG_INF))
        return jnp.where(count_ge(cand_f) >= float(k), cand, prefix)

    return lax.fori_loop(0, 32, bit_body, jnp.full(shape, INT_MIN, I32))


def _tie_cutoff(count_tie_le, need, shape, index_bits):
    def bit_body(it, lo):
        cand = lo + lax.shift_left(jnp.int32(1), index_bits - 1 - it).astype(F32)
        return jnp.where(count_tie_le(cand) < need, cand, lo)

    return lax.fori_loop(0, index_bits, bit_body, jnp.full(shape, -1.0, F32)) + 1.0


DSA_QB = 128
DSA_TK = 512
DSA_TS = 256
DSA_TA = 256
HEAD_PAIR = 2


def _dsa_kernel(q_ref, qi_ref, az_ref, miscq_ref, k_ref, v_ref, misck_ref, out_ref,
                k_bf, vt_scr, ki_lo, ki_hi, sc_scr, acc_scr, j_scr, *head_scr, topk):
    i = pl.program_id(1)
    seq = misck_ref.shape[0]
    QB, TK, TS, TA = DSA_QB, DSA_TK, DSA_TS, DSA_TA
    nt = (i * QB + QB + TK - 1) // TK

    @pl.when(i == 0)
    def _():
        for g in range(KV_HEADS):
            k_bf[:, g * HEAD_DIM:(g + 1) * HEAD_DIM] = k_ref[pl.ds(g, seq, stride=KV_HEADS), :].astype(BF16)

        def v_tile(a, carry):
            for g in range(KV_HEADS):
                rows = pl.ds(pl.multiple_of(a * (TA * KV_HEADS), TA * KV_HEADS) + g, TA, stride=KV_HEADS)
                vt_scr[a, g * HEAD_DIM:(g + 1) * HEAD_DIM, :] = v_ref[rows, :].T.astype(BF16)
            return carry

        lax.fori_loop(0, seq // TA, v_tile, 0)
        lane = lax.broadcasted_iota(I32, (seq, LANES), 1)
        lo = jnp.where(lane < IDX_DIM, misck_ref[...], 0.0)
        ki_lo[...] = lo.astype(BF16)
        ki_hi[...] = pltpu.roll(lo, IDX_DIM, axis=1).astype(BF16)

    k_iota = lax.broadcasted_iota(I32, (TK, QB), 0)
    q_pos = i * QB + lax.broadcasted_iota(I32, (TK, QB), 1)
    w_t = miscq_ref[...].T

    def tile_rows(t):
        return pl.ds(pl.multiple_of(t * TK, TK), TK)

    def score_tile(t, carry):
        rows = pl.ds(pl.multiple_of(t * TS, TS), TS)
        klo = ki_lo[rows, :]
        khi = ki_hi[rows, :]
        acc = jnp.zeros((TS, QB), F32)
        for p in range(IDX_HEADS // 2):
            qp = qi_ref[:, p * LANES:(p + 1) * LANES]
            w0 = w_t[MISC_WI + 2 * p:MISC_WI + 2 * p + 1, :]
            w1 = w_t[MISC_WI + 2 * p + 1:MISC_WI + 2 * p + 2, :]
            acc = acc + w0 * jnp.maximum(_dot_nt(klo, qp), 0.0) + w1 * jnp.maximum(_dot_nt(khi, qp), 0.0)
        k_pos = t * TS + lax.broadcasted_iota(I32, (TS, QB), 0)
        q_pos_s = i * QB + lax.broadcasted_iota(I32, (TS, QB), 1)
        score = jnp.where(k_pos <= q_pos_s, acc * IDX_SCALE + 0.0, -jnp.inf)
        sc_scr[rows, :] = _float_to_key(score)
        return carry

    def score_tiles(t, carry):
        for sub in range(TK // TS):
            score_tile(t * (TK // TS) + sub, carry)
        return carry

    lax.fori_loop(0, nt, score_tiles, 0)

    def count_key_ge(cand):
        def body(t, cnt):
            return cnt + _fold_rows(jnp.add, jnp.where(sc_scr[tile_rows(t), :] >= cand, 1, 0))

        cnt = lax.fori_loop(0, nt, body, jnp.zeros((8, QB), I32))
        return jnp.sum(cnt, axis=0, keepdims=True)

    def bit_body(it, carry):
        prefix, cnt_at = carry
        cand = prefix + lax.shift_left(jnp.int32(1), 31 - it)
        cnt = count_key_ge(cand)
        ok = cnt >= topk
        return jnp.where(ok, cand, prefix), jnp.where(ok, cnt, cnt_at)

    thr, cnt_thr = lax.fori_loop(0, 32, bit_body, (jnp.full((1, QB), INT_MIN, I32), jnp.full((1, QB), TK, I32) * nt))

    j_scr[...] = jnp.full((1, QB), float(seq), F32)

    @pl.when(jnp.max(cnt_thr) > topk)
    def _():
        def count_keys(indicator):
            def body(t, cnt):
                k_pos = (t * TK + k_iota).astype(F32)
                return cnt + _fold_rows(jnp.add, indicator(sc_scr[tile_rows(t), :], k_pos))

            return jnp.sum(lax.fori_loop(0, nt, body, jnp.zeros((8, QB), F32)), axis=0, keepdims=True)

        need = float(topk) - count_keys(lambda key, k_pos: jnp.where(key > thr, 1.0, 0.0))
        count_tie_le = lambda cut: count_keys(
            lambda key, k_pos: jnp.where(key == thr, jnp.where(k_pos <= cut, 1.0, 0.0), 0.0))
        j_scr[...] = _tie_cutoff(count_tie_le, need, (1, QB), int(np.log2(seq)))

    cut = j_scr[...]

    def bias_tile(t, carry):
        key = sc_scr[tile_rows(t), :]
        k_pos = t * TK + k_iota
        taken = jnp.where(key > thr, 0.0,
                          jnp.where(key == thr, jnp.where(k_pos.astype(F32) <= cut, 0.0, NEG_BIG), NEG_BIG))
        sc_scr[tile_rows(t), :] = lax.bitcast_convert_type(jnp.where(k_pos <= q_pos, taken, NEG_BIG), I32)
        return carry

    lax.fori_loop(0, nt, bias_tile, 0)

    assert TK == 2 * TA and GROUP % HEAD_PAIR == 0
    az = az_ref[...].astype(F32)
    n_pairs = ATT_HEADS // HEAD_PAIR
    PQ = HEAD_PAIR * QB
    lg_scr, p_scr = head_scr[:n_pairs], head_scr[n_pairs:]
    acc_scr[...] = jnp.zeros_like(acc_scr)
    last = 2 * nt - 1

    def sub_rows(a):
        return pl.ds(pl.multiple_of(a * TA, TA), TA)

    def kv_cols(j):
        g = (j * HEAD_PAIR) // GROUP
        return slice(g * HEAD_DIM, (g + 1) * HEAD_DIM)

    def logits_stage(j, a, slot):
        q_pair = jnp.concatenate([q_ref[:, h * HEAD_DIM:(h + 1) * HEAD_DIM]
                                  for h in range(j * HEAD_PAIR, (j + 1) * HEAD_PAIR)], axis=0)
        lg_scr[j][slot] = _dot_nt(k_bf[sub_rows(a), kv_cols(j)], q_pair)

    CH = 64

    def softmax_stage(j, a, slot, state):
        m, l8, _ = state
        maxes = []
        for r in range(0, TA, CH):
            bias = lax.bitcast_convert_type(sc_scr[pl.ds(pl.multiple_of(a * TA, TA) + r, CH), :], F32)
            lg = lg_scr[j][slot, r:r + CH, :] + jnp.concatenate([bias] * HEAD_PAIR, axis=1)
            lg_scr[j][slot, r:r + CH, :] = lg
            maxes.append(_fold_rows(jnp.maximum, lg))
        m_new = jnp.maximum(m, jnp.max(_tree_reduce(jnp.maximum, maxes), axis=0, keepdims=True))
        alpha = jnp.exp2(m - m_new)
        sums = []
        for r in range(0, TA, CH):
            p = jnp.exp2(lg_scr[j][slot, r:r + CH, :] - m_new)
            p_scr[j][slot, r:r + CH, :] = p.astype(BF16)
            sums.append(_fold_rows(jnp.add, p))
        return m_new, alpha * l8 + _tree_reduce(jnp.add, sums), alpha

    def value_stage(j, a, slot, state):
        vt = vt_scr[a][kv_cols(j), :]
        acc_scr[j] = state[2] * acc_scr[j] + _dot(vt, p_scr[j][slot])

    def att_tile(t, state):
        for slot in range(2):
            a = 2 * t + slot
            out = []
            for j in range(n_pairs):
                value_stage(j, jnp.maximum(a - 1, 0), 1 - slot, state[j])
                out.append(softmax_stage(j, a, slot, state[j]))
                logits_stage(j, jnp.minimum(a + 1, last), 1 - slot)
            state = tuple(out)
        return state

    init = tuple((jnp.full((1, PQ), NEG_BIG, F32), jnp.zeros((8, PQ), F32), jnp.ones((1, PQ), F32))
                 for _ in range(n_pairs))
    for j in range(n_pairs):
        p_scr[j][1] = jnp.zeros((TA, PQ), BF16)
        logits_stage(j, 0, 0)
    fin = lax.fori_loop(0, nt, att_tile, init)
    for j in range(n_pairs):
        value_stage(j, last, 1, fin[j])
    for j in range(n_pairs):
        o_t = acc_scr[j] / jnp.sum(fin[j][1], axis=0, keepdims=True)
        for hh in range(HEAD_PAIR):
            c0 = (j * HEAD_PAIR + hh) * HEAD_DIM
            o = o_t[:, hh * QB:(hh + 1) * QB].T
            out_ref[:, c0:c0 + HEAD_DIM] = (o * _silu(az[:, c0:c0 + HEAD_DIM])).astype(out_ref.dtype)


def _dsa_prompt(p_main, k_rows, v_rows, misc, batch, seq):
    QB, TK = DSA_QB, DSA_TK
    nb = seq // QB
    topk = min(TOPK_MAX, seq // 4)
    TA = DSA_TA
    assert TK >= topk and seq % TK == 0 and seq % QB == 0 and TK % DSA_TS == 0
    qrow = lambda b, i: b * nb + i
    return pl.pallas_call(
        functools.partial(_dsa_kernel, topk=topk),
        out_shape=jax.ShapeDtypeStruct((batch * seq, ATT_WIDTH), BF16),
        grid=(batch, nb),
        in_specs=[pl.BlockSpec((QB, ATT_WIDTH), lambda b, i: (qrow(b, i), 0)),
                  pl.BlockSpec((QB, 1024), lambda b, i: (qrow(b, i), 1)),
                  pl.BlockSpec((QB, ATT_WIDTH), lambda b, i: (qrow(b, i), 2)),
                  pl.BlockSpec((QB, LANES), lambda b, i: (qrow(b, i), 0)),
                  pl.BlockSpec((seq * KV_HEADS, HEAD_DIM), lambda b, i: (b, 0)),
                  pl.BlockSpec((seq * KV_HEADS, HEAD_DIM), lambda b, i: (b, 0)),
                  pl.BlockSpec((seq, LANES), lambda b, i: (b, 0))],
        out_specs=pl.BlockSpec((QB, ATT_WIDTH), lambda b, i: (qrow(b, i), 0)),
        scratch_shapes=[pltpu.VMEM((seq, KV_WIDTH), BF16),
                        pltpu.VMEM((seq // TA, KV_WIDTH, TA), BF16),
                        pltpu.VMEM((seq, LANES), BF16),
                        pltpu.VMEM((seq, LANES), BF16),
                        pltpu.VMEM((seq, QB), I32),
                        pltpu.VMEM((ATT_HEADS // HEAD_PAIR, HEAD_DIM, HEAD_PAIR * QB), F32),
                        pltpu.VMEM((1, QB), F32)]
                       + [pltpu.VMEM((2, TA, HEAD_PAIR * QB), F32)] * (ATT_HEADS // HEAD_PAIR)
                       + [pltpu.VMEM((2, TA, HEAD_PAIR * QB), BF16)] * (ATT_HEADS // HEAD_PAIR),
        compiler_params=pltpu.CompilerParams(dimension_semantics=("parallel", "arbitrary"),
                                             vmem_limit_bytes=VMEM_LIMIT),
        name="dsa_prompt",
    )(p_main, p_main, p_main, misc, k_rows, v_rows, misc)


SCORE_ROWS = 32


def _sample_scores_kernel(pt_ref, qi_ref, wb_ref, kin_ref, cache_ref, sc_ref, snew_ref, page_buf, sems):
    b = pl.program_id(0)
    n_rows = pl.num_programs(0)
    n_pages = sc_ref.shape[1]
    slot = lax.rem(b, 2)

    def page_copy(row, p, to_slot):
        return pltpu.make_async_copy(cache_ref.at[pt_ref[row, p]], page_buf.at[to_slot, p], sems.at[to_slot])

    def start_fetch(row, to_slot):
        def body(p, carry):
            page_copy(row, p, to_slot).start()
            return carry

        lax.fori_loop(0, n_pages, body, 0)

    @pl.when(b == 0)
    def _():
        start_fetch(0, 0)

    @pl.when(b + 1 < n_rows)
    def _():
        start_fetch(b + 1, 1 - slot)

    def wait_page(p, carry):
        page_copy(b, p, slot).wait()
        return carry

    lax.fori_loop(0, n_pages, wait_page, 0)

    qi = qi_ref[0]
    wb = wb_ref[0]

    def score_rows(g, carry):
        rows = []
        for j in range(SCORE_ROWS):
            page = page_buf[slot, g * SCORE_ROWS + j]
            s = _dot(qi, page.astype(BF16))
            rows.append(jnp.sum(jnp.maximum(s, 0.0) * wb, axis=0, keepdims=True) * IDX_SCALE)
        sc_ref[0, pl.ds(pl.multiple_of(g * SCORE_ROWS, SCORE_ROWS), SCORE_ROWS), :] = jnp.concatenate(rows, axis=0)
        return carry

    lax.fori_loop(0, n_pages // SCORE_ROWS, score_rows, 0)

    ki_new = kin_ref[0].astype(BF16).astype(F32)
    s = jnp.sum(qi.astype(F32) * ki_new, axis=1, keepdims=True)
    s_new = jnp.sum(jnp.maximum(s, 0.0) * wb[:, 0:1], axis=0, keepdims=True) * IDX_SCALE
    snew_ref[0] = jnp.broadcast_to(s_new, (8, LANES))


def _sample_scores(page_table, qi16, wb, ki_new, cache_ik_t):
    bd, n_pages = page_table.shape
    assert n_pages % SCORE_ROWS == 0
    bmap = lambda b, pt: (b, 0, 0)
    return pl.pallas_call(
        _sample_scores_kernel,
        out_shape=(jax.ShapeDtypeStruct((bd, n_pages, PAGE), F32),
                   jax.ShapeDtypeStruct((bd, 8, LANES), F32)),
        grid_spec=pltpu.PrefetchScalarGridSpec(
            num_scalar_prefetch=1,
            grid=(bd,),
            in_specs=[pl.BlockSpec((1, IDX_HEADS, IDX_DIM), bmap),
                      pl.BlockSpec((1, IDX_HEADS, LANES), bmap),
                      pl.BlockSpec((1, 1, IDX_DIM), bmap),
                      pl.BlockSpec(memory_space=pl.ANY)],
            out_specs=(pl.BlockSpec((1, n_pages, PAGE), bmap),
                       pl.BlockSpec((1, 8, LANES), bmap)),
            scratch_shapes=[pltpu.VMEM((2, n_pages, IDX_DIM, PAGE), F32),
                            pltpu.SemaphoreType.DMA((2,))]),
        compiler_params=pltpu.CompilerParams(dimension_semantics=("arbitrary",),
                                             vmem_limit_bytes=VMEM_LIMIT),
        name="sample_scores",
    )(page_table, qi16, wb, ki_new, cache_ik_t)


SEL_THR, SEL_NEXT, SEL_CUT = 0, 1, 2


def _sample_select_kernel(sc_ref, snew_ref, sel_ref, *, topk):
    x = sc_ref[...]
    bd, n_pages, _ = x.shape
    past = n_pages * PAGE
    s_new = snew_ref[:, 0:1, 0:1]

    def total(v):
        return jnp.sum(jnp.sum(v, axis=1, keepdims=True), axis=2, keepdims=True)

    def count_ge(cand):
        return total(jnp.where(x >= cand, 1.0, 0.0)) + jnp.where(s_new >= cand, 1.0, 0.0)

    key = _kth_largest_key(count_ge, (bd, 1, 1), topk)
    thr = _key_to_float(key)
    thr_next = _key_to_float(key + 1)
    need = float(topk) - count_ge(thr_next)
    pos = (lax.broadcasted_iota(I32, (1, n_pages, PAGE), 1) * PAGE
           + lax.broadcasted_iota(I32, (1, n_pages, PAGE), 2)).astype(F32)

    def count_tie_le(cut):
        tie = jnp.where(x >= thr_next, 0.0, jnp.where(x >= thr, jnp.where(pos <= cut, 1.0, 0.0), 0.0))
        tie_new = jnp.where(s_new >= thr_next, 0.0, jnp.where(s_new >= thr, jnp.where(float(past) <= cut, 1.0, 0.0), 0.0))
        return total(tie) + tie_new

    cut = _tie_cutoff(count_tie_le, need, (bd, 1, 1), int(np.log2(past)) + 1)
    row = lax.broadcasted_iota(I32, (bd, 8, LANES), 1)
    sel_ref[...] = jnp.where(row == SEL_THR, thr, jnp.where(row == SEL_NEXT, thr_next, cut))


def _sample_select(scores, s_new, topk):
    bd = scores.shape[0]
    return pl.pallas_call(
        functools.partial(_sample_select_kernel, topk=topk),
        out_shape=jax.ShapeDtypeStruct((bd, 8, LANES), F32),
        compiler_params=pltpu.CompilerParams(vmem_limit_bytes=VMEM_LIMIT),
        name="sample_select",
    )(scores, s_new)


def _taken_bias(score, pos, sel):
    thr, thr_next, cut = sel[SEL_THR:SEL_THR + 1, 0:1], sel[SEL_NEXT:SEL_NEXT + 1, 0:1], sel[SEL_CUT:SEL_CUT + 1, 0:1]
    return jnp.where(score >= thr_next, 0.0,
                     jnp.where(score >= thr, jnp.where(pos <= cut, 0.0, NEG_BIG), NEG_BIG))


def _sample_compact_kernel(pt_ref, sc_ref, sel_ref, rows_ref, nsel_ref, *, slots):
    n_pages = sc_ref.shape[1]
    pos = (lax.broadcasted_iota(I32, (n_pages, PAGE), 0) * PAGE
           + lax.broadcasted_iota(I32, (n_pages, PAGE), 1)).astype(F32)
    taken = jnp.where(_taken_bias(sc_ref[0], pos, sel_ref[0]) == 0.0, 1.0, 0.0)
    before = lax.broadcasted_iota(I32, (PAGE, PAGE), 0) < lax.broadcasted_iota(I32, (PAGE, PAGE), 1)
    in_page = _dot(taken, jnp.where(before, 1.0, 0.0))
    taken_t = taken.T
    rank_t = jnp.where(taken_t > 0.0, in_page.T, -1.0)
    page_tot = jnp.sum(taken_t, axis=0, keepdims=True)
    earlier = lax.broadcasted_iota(I32, (n_pages, n_pages), 0) < lax.broadcasted_iota(I32, (n_pages, n_pages), 1)
    first_slot = _dot(jnp.broadcast_to(page_tot, (8, n_pages)),
                      jnp.where(earlier, 1.0, 0.0))[0:1]
    phys_t = (pt_ref[0] * PAGE + lax.broadcasted_iota(I32, (PAGE, n_pages), 0)).astype(F32)
    slot = lax.broadcasted_iota(I32, (slots, n_pages), 0).astype(F32)

    def rank_body(r, acc):
        r = r.astype(F32)
        row = jnp.sum(jnp.where(rank_t == r, phys_t, 0.0), axis=0, keepdims=True)
        target = jnp.where(r < page_tot, first_slot + r, -1.0)
        return acc + jnp.where(slot == target, row, 0.0)

    most = jnp.max(page_tot).astype(I32)
    acc = lax.fori_loop(0, most, rank_body, jnp.zeros((slots, n_pages), F32))
    rows_ref[0] = jnp.broadcast_to(jnp.sum(acc, axis=1, keepdims=True), (slots, LANES))
    nsel_ref[0] = jnp.broadcast_to(jnp.sum(page_tot, axis=1, keepdims=True), (8, LANES))


def _sample_compact(page_table, scores, sel, slots):
    bd, n_pages = page_table.shape
    assert n_pages % LANES == 0
    bmap = lambda b: (b, 0, 0)
    return pl.pallas_call(
        functools.partial(_sample_compact_kernel, slots=slots),
        out_shape=(jax.ShapeDtypeStruct((bd, slots, LANES), F32), jax.ShapeDtypeStruct((bd, 8, LANES), F32)),
        grid=(bd,),
        in_specs=[pl.BlockSpec((1, 1, n_pages), bmap), pl.BlockSpec((1, n_pages, PAGE), bmap),
                  pl.BlockSpec((1, 8, LANES), bmap)],
        out_specs=(pl.BlockSpec((1, slots, LANES), bmap), pl.BlockSpec((1, 8, LANES), bmap)),
        compiler_params=pltpu.CompilerParams(dimension_semantics=("parallel",), vmem_limit_bytes=VMEM_LIMIT),
        name="sample_compact",
    )(page_table.reshape(bd, 1, n_pages), scores, sel)


SC_GATHER_CHUNK = 128


def _gather_rows(table_k, table_v, idx):
    info = plsc.get_sparse_core_info()
    n_workers = info.num_cores * info.num_subcores
    n_idx = idx.shape[0]
    per_worker = n_idx // n_workers
    assert per_worker * n_workers == n_idx and per_worker % SC_GATHER_CHUNK == 0
    mesh = plsc.VectorSubcoreMesh(core_axis_name="c", subcore_axis_name="s")
    out = jax.ShapeDtypeStruct((n_idx, table_k.shape[1]), table_k.dtype)

    @functools.partial(
        pl.kernel, mesh=mesh, out_type=(out, out),
        scratch_types=[pltpu.VMEM((SC_GATHER_CHUNK,), I32),
                       pltpu.VMEM((SC_GATHER_CHUNK, table_k.shape[1]), table_k.dtype),
                       pltpu.VMEM((SC_GATHER_CHUNK, table_v.shape[1]), table_v.dtype),
                       pltpu.SemaphoreType.DMA, pltpu.SemaphoreType.DMA])
    def gather(tk_hbm, tv_hbm, idx_hbm, ok_hbm, ov_hbm, idx_v, rk_v, rv_v, sem_k, sem_v):
        worker = lax.axis_index("s") * info.num_cores + lax.axis_index("c")

        @pl.loop(0, per_worker // SC_GATHER_CHUNK)
        def _(j):
            base = worker * per_worker + j * SC_GATHER_CHUNK
            pltpu.sync_copy(idx_hbm.at[pl.ds(base, SC_GATHER_CHUNK)], idx_v)
            copy_k = pltpu.async_copy(tk_hbm.at[idx_v], rk_v, sem_k)
            copy_v = pltpu.async_copy(tv_hbm.at[idx_v], rv_v, sem_v)
            copy_k.wait()
            copy_v.wait()
            pltpu.sync_copy(rk_v, ok_hbm.at[pl.ds(base, SC_GATHER_CHUNK)])
            pltpu.sync_copy(rv_v, ov_hbm.at[pl.ds(base, SC_GATHER_CHUNK)])

    return gather(table_k, table_v, idx)


def _sample_attend_kernel(q_ref, az_ref, kn_ref, vn_ref, snew_ref, sel_ref, nsel_ref, k_ref, v_ref, out_ref, *, past):
    width = k_ref.shape[1]
    q = q_ref[0]
    col = lax.broadcasted_iota(I32, (ATT_HEADS, width), 1)
    head = lax.broadcasted_iota(I32, (ATT_HEADS, width), 0)
    own_head = col % KV_HEADS == head // GROUP
    filled = (col // KV_HEADS).astype(F32) < nsel_ref[0][0:1, 0:1]
    lg = _dot_nt(q, k_ref[0].astype(BF16))
    lg = jnp.where(own_head, jnp.where(filled, lg, NEG_BIG), NEG_BIG)
    lg_new = (jnp.sum(q.astype(F32) * kn_ref[0].astype(BF16).astype(F32), axis=1, keepdims=True)
              + _taken_bias(snew_ref[0][0:1, 0:1], float(past), sel_ref[0]))
    m = jnp.maximum(jnp.max(lg, axis=1, keepdims=True), lg_new)
    p = jnp.exp2(lg - m)
    p_new = jnp.exp2(lg_new - m)
    l = jnp.sum(p, axis=1, keepdims=True) + p_new
    acc = _dot(p.astype(BF16), v_ref[0].astype(BF16)) + p_new * vn_ref[0].astype(BF16).astype(F32)
    out_ref[0] = (acc / l) * _silu(az_ref[0])


def _sample_attend(q8, az8, k_new8, v_new8, s_new, sel, n_sel, k_sel, v_sel, past):
    bd, width, _ = k_sel.shape
    bmap = lambda b: (b, 0, 0)
    head_tile = pl.BlockSpec((1, ATT_HEADS, HEAD_DIM), bmap)
    par_tile = pl.BlockSpec((1, 8, LANES), bmap)
    rows_tile = pl.BlockSpec((1, width, HEAD_DIM), bmap)
    return pl.pallas_call(
        functools.partial(_sample_attend_kernel, past=past),
        out_shape=jax.ShapeDtypeStruct((bd, ATT_HEADS, HEAD_DIM), F32),
        grid=(bd,),
        in_specs=[head_tile, head_tile, head_tile, head_tile, par_tile, par_tile, par_tile, rows_tile, rows_tile],
        out_specs=head_tile,
        compiler_params=pltpu.CompilerParams(dimension_semantics=("parallel",), vmem_limit_bytes=VMEM_LIMIT),
        name="sample_attend",
    )(q8, az8, k_new8, v_new8, s_new, sel, n_sel, k_sel, v_sel)


def _mlstm_step_kernel(bi_ref, bf_ref, q_ref, k_ref, v_ref, mo_ref, mz_ref, misc_ref, nw_ref,
                       c_ref, n_ref, m_ref, out_ref, c_out, n_out, m_out):
    misc = misc_ref[0]
    eye = lax.broadcasted_iota(I32, (ML_V, ML_V), 0) == lax.broadcasted_iota(I32, (ML_V, ML_V), 1)
    for h in range(ML_HEADS):
        ig = misc[:, MISC_IG + h:MISC_IG + h + 1] + bi_ref[h]
        lf = _log_sigmoid(misc[:, MISC_FG + h:MISC_FG + h + 1] + bf_ref[h])
        m_prev = m_ref[0, h][:, 0:1]
        log_a = lf + m_prev
        m_t = jnp.maximum(log_a, ig)
        d = jnp.exp(ig - m_t)
        a = jnp.exp(log_a - m_t)
        q = q_ref[0][:, h * ML_QK:(h + 1) * ML_QK]
        k = k_ref[0][:, h * ML_QK:(h + 1) * ML_QK]
        v = v_ref[0][:, h * ML_V:(h + 1) * ML_V]
        v_col = jnp.sum(jnp.where(eye, v, 0.0), axis=1, keepdims=True)
        c = c_ref[0, h]
        n = n_ref[0, h]
        s = jnp.sum(q * k, axis=1, keepdims=True) * d
        num = a * jnp.sum(c * q, axis=1, keepdims=True) + s * v_col
        den = a * jnp.sum(n * q, axis=1, keepdims=True) + s
        h_col = num / jnp.maximum(jnp.abs(den), jnp.exp(-m_t))
        c_out[0, h] = a * c + (d * v_col) * k
        n_out[0, h] = a * n + d * k
        m_out[0, h] = jnp.broadcast_to(m_t, (1, LANES))

        h_row = jnp.sum(jnp.where(eye, h_col, 0.0), axis=0, keepdims=True)
        ms = jnp.mean(h_row * h_row, axis=1, keepdims=True)
        hn = h_row * lax.rsqrt(ms + RMS_EPS) * nw_ref[:, h * ML_V:(h + 1) * ML_V]
        gate = _sigmoid(mo_ref[0][:, h * ML_V:(h + 1) * ML_V]) * _silu(mz_ref[0][:, h * ML_V:(h + 1) * ML_V])
        out_ref[0, :, h * ML_V:(h + 1) * ML_V] = (hn * gate).astype(out_ref.dtype)


def _mlstm_step(ps_main, misc, b_i, b_f, ml_norm_w, state_c, state_n, state_m):
    bd = ps_main.shape[0]
    col = lambda j: (lambda b: (b, 0, j))
    st4 = lambda b: (b, 0, 0, 0)
    return pl.pallas_call(
        _mlstm_step_kernel,
        out_shape=(jax.ShapeDtypeStruct((bd, 1, ML_WIDTH), BF16),
                   jax.ShapeDtypeStruct(state_c.shape, F32),
                   jax.ShapeDtypeStruct(state_n.shape, F32),
                   jax.ShapeDtypeStruct(state_m.shape, F32)),
        grid=(bd,),
        in_specs=[pl.BlockSpec(memory_space=pltpu.SMEM),
                  pl.BlockSpec(memory_space=pltpu.SMEM),
                  pl.BlockSpec((1, 1, 512), col(12)),
                  pl.BlockSpec((1, 1, 512), col(13)),
                  pl.BlockSpec((1, 1, 1024), col(3)),
                  pl.BlockSpec((1, 1, 1024), col(4)),
                  pl.BlockSpec((1, 1, 1024), col(5)),
                  pl.BlockSpec((1, 1, LANES), col(0)),
                  pl.BlockSpec((1, ML_WIDTH), lambda b: (0, 0)),
                  pl.BlockSpec((1, ML_HEADS, ML_V, ML_QK), st4),
                  pl.BlockSpec((1, ML_HEADS, 1, ML_QK), st4),
                  pl.BlockSpec((1, ML_HEADS, 1, LANES), st4)],
        out_specs=(pl.BlockSpec((1, 1, ML_WIDTH), lambda b: (b, 0, 0)),
                   pl.BlockSpec((1, ML_HEADS, ML_V, ML_QK), st4),
                   pl.BlockSpec((1, ML_HEADS, 1, ML_QK), st4),
                   pl.BlockSpec((1, ML_HEADS, 1, LANES), st4)),
        compiler_params=pltpu.CompilerParams(dimension_semantics=("parallel",),
                                             vmem_limit_bytes=VMEM_LIMIT),
        name="mlstm_step",
    )(b_i, b_f, ps_main, ps_main, ps_main, ps_main, ps_main, misc, ml_norm_w.reshape(1, ML_WIDTH),
      state_c, state_n, state_m)


def _small_weight(w_t):
    offs = np.cumsum((0,) + IN_SIZES)
    ak, av, ik, iw, mi, mf = (w_t[offs[j]:offs[j + 1]] for j in (1, 2, 4, 5, 10, 11))
    pad = jnp.zeros((LANES - IDX_DIM - IDX_HEADS - 2 * ML_HEADS, w_t.shape[1]), w_t.dtype)
    w_small = jnp.concatenate([ak, av, ik, iw, mi, mf, pad], axis=0)
    assert w_small.shape[0] == SMALL_W
    return w_small


def kernel(x_prompt, x_sample, cache_k, cache_v, cache_idx_k, state_C, state_n, state_m, page_table,
           norm_w, w_in, b_igate, b_fgate, ml_norm_w, w_out, final_norm_w):
    depth = w_in.shape[0]
    batch, seq, d = x_prompt.shape
    bd, dec_seq, _ = x_sample.shape
    assert depth == 1 and dec_seq == 1 and d == D_MODEL
    n_pages = page_table.shape[1]

    w_t = jnp.swapaxes(w_in[0], 0, 1)
    w_small = _small_weight(w_t)
    w_o = w_out[0].astype(BF16)
    w_o_att, w_o_ml = w_o[:ATT_WIDTH], w_o[ATT_WIDTH:]

    xp = x_prompt.reshape(batch * seq, d)
    p_main = _project_main(xp, norm_w[0], w_t, BF16, 1024, "proj_main")
    k_rows, v_rows, misc = _project_small(xp, norm_w[0], w_small, 1024, "proj_small")
    att = _dsa_prompt(p_main, k_rows, v_rows, misc, batch, seq)
    ml, ct, m_p = _mlstm_prompt(p_main, misc, b_igate[0], b_fgate[0], ml_norm_w[0], batch, seq)
    y_prompt = _out_project(xp, att, ml, w_o_att, w_o_ml, final_norm_w, 256, "out_prompt").reshape(batch, seq, d)
    k_prompt = k_rows.reshape(1, batch, seq, KV_HEADS, HEAD_DIM)
    v_prompt = v_rows.reshape(1, batch, seq, KV_HEADS, HEAD_DIM)
    ik_prompt = misc[:, :IDX_DIM].reshape(1, batch, seq, IDX_DIM)
    c_prompt = jnp.swapaxes(ct[..., :ML_V], -1, -2)[None]
    n_prompt = ct[..., ML_V][None]
    m_prompt = m_p[:, :, 0, 0][None]

    xs = x_sample.reshape(bd, d)
    ps_main = _project_main(xs, norm_w[0], w_t, F32, bd, "proj_main_s")
    ks_rows, vs_rows, misc_s = _project_small(xs, norm_w[0], w_small, bd, "proj_small_s")
    q8 = ps_main[:, :ATT_WIDTH].reshape(bd, ATT_HEADS, HEAD_DIM).astype(BF16)
    qi16 = ps_main[:, 1024:2048].reshape(bd, IDX_HEADS, IDX_DIM).astype(BF16)
    az8 = ps_main[:, 2048:3072].reshape(bd, ATT_HEADS, HEAD_DIM)
    ki_new = misc_s[:, :IDX_DIM].reshape(bd, 1, IDX_DIM)
    wb = jnp.broadcast_to(misc_s[:, MISC_WI:MISC_WI + IDX_HEADS, None], (bd, IDX_HEADS, LANES))
    k_new8 = jnp.repeat(ks_rows.reshape(bd, KV_HEADS, HEAD_DIM), GROUP, axis=1)
    v_new8 = jnp.repeat(vs_rows.reshape(bd, KV_HEADS, HEAD_DIM), GROUP, axis=1)
    scores, s_new = _sample_scores(page_table, qi16, wb, ki_new, jnp.swapaxes(cache_idx_k[0], 1, 2))
    assert cache_k.shape[1] * PAGE < 2 ** 24
    topk_s = min(TOPK_MAX, (n_pages * PAGE + 1) // 4)
    sel = _sample_select(scores, s_new, topk_s)
    key_rows, n_sel = _sample_compact(page_table, scores, sel, topk_s)
    idx = (key_rows[:, :, :1].astype(I32) * KV_HEADS + jnp.arange(KV_HEADS, dtype=I32)).reshape(-1)
    k_sel, v_sel = _gather_rows(cache_k.reshape(-1, HEAD_DIM), cache_v.reshape(-1, HEAD_DIM), idx)
    att_s = _sample_attend(q8, az8, k_new8, v_new8, s_new, sel, n_sel,
                           k_sel.reshape(bd, topk_s * KV_HEADS, HEAD_DIM),
                           v_sel.reshape(bd, topk_s * KV_HEADS, HEAD_DIM), n_pages * PAGE)
    ml_s, c_s, n_s, m_s = _mlstm_step(
        ps_main.reshape(bd, 1, MAIN_W), misc_s.reshape(bd, 1, LANES), b_igate[0], b_fgate[0], ml_norm_w[0],
        state_C[0], state_n[0].reshape(bd, ML_HEADS, 1, ML_QK),
        jnp.broadcast_to(state_m[0][:, :, None, None], (bd, ML_HEADS, 1, LANES)))
    y_sample = _out_project(xs, att_s.reshape(bd, ATT_WIDTH).astype(BF16), ml_s.reshape(bd, ML_WIDTH),
                            w_o_att, w_o_ml, final_norm_w, bd, "out_sample").reshape(bd, 1, d)
    k_sample = ks_rows.reshape(1, bd, 1, KV_HEADS, HEAD_DIM)
    v_sample = vs_rows.reshape(1, bd, 1, KV_HEADS, HEAD_DIM)
    ik_sample = misc_s[:, :IDX_DIM].reshape(1, bd, 1, IDX_DIM)

    return (y_prompt, y_sample, k_prompt, v_prompt, ik_prompt, c_prompt, n_prompt, m_prompt,
            k_sample, v_sample, ik_sample, c_s[None], n_s.reshape(1, bd, ML_HEADS, ML_QK), m_s[:, :, 0, 0][None])
```

```python
import functools

import jax
import jax.numpy as jnp
import numpy as np
from jax import lax
from jax.experimental import pallas as pl
from jax.experimental.pallas import tpu as pltpu
from jax.experimental.pallas import tpu_sc as plsc

F32 = jnp.float32
BF16 = jnp.bfloat16
I32 = jnp.int32

D_MODEL = 2048
PAGE = 128
ATT_HEADS = 8
KV_HEADS = 2
HEAD_DIM = 128
GROUP = ATT_HEADS // KV_HEADS
ATT_WIDTH = ATT_HEADS * HEAD_DIM
KV_WIDTH = KV_HEADS * HEAD_DIM
ATT_SCALE = HEAD_DIM ** -0.5
Q_SCALE = ATT_SCALE * float(np.log2(np.e))
IDX_HEADS = 16
IDX_DIM = 64
IDX_SCALE = (IDX_HEADS * IDX_DIM) ** -0.5
TOPK_MAX = 256
ML_HEADS = 4
ML_QK = 128
ML_V = 256
ML_WIDTH = ML_HEADS * ML_V
RMS_EPS = 1e-6
IN_SIZES = (ATT_WIDTH, KV_WIDTH, KV_WIDTH, IDX_HEADS * IDX_DIM, IDX_DIM, IDX_HEADS, ATT_WIDTH,
            ML_HEADS * ML_QK, ML_HEADS * ML_QK, ML_WIDTH, ML_HEADS, ML_HEADS, ML_WIDTH, ML_WIDTH)

LANES = 128
NEG_BIG = -1e30
VMEM_LIMIT = 56 * 1024 * 1024

MAIN_W = 7168
SMALL_W = 640
MISC_WI = IDX_DIM
MISC_IG = IDX_DIM + IDX_HEADS
MISC_FG = MISC_IG + ML_HEADS


def _dot(a, b):
    return jnp.dot(a, b, preferred_element_type=F32)


def _dot_nt(a, b):
    return lax.dot_general(a, b, (((1,), (1,)), ((), ())), preferred_element_type=F32)


def _tree_reduce(op, parts):
    parts = list(parts)
    while len(parts) > 1:
        paired = [op(parts[j], parts[j + 1]) for j in range(0, len(parts) - 1, 2)]
        parts = paired + parts[len(parts) - len(parts) % 2:]
    return parts[0]


def _fold_rows(op, x):
    return _tree_reduce(op, [x[r:r + 8] for r in range(0, x.shape[0], 8)])


def _proj_small_kernel(x_ref, nw_ref, w_ref, k_ref, v_ref, misc_ref):
    x = x_ref[...]
    ms = jnp.mean(x * x, axis=-1, keepdims=True)
    h = (x * lax.rsqrt(ms + RMS_EPS) * nw_ref[...]).astype(BF16)
    res = _dot_nt(h, w_ref[...].astype(BF16))
    tm = x.shape[0]
    for g in range(KV_HEADS):
        k_ref[pl.ds(g, tm, stride=KV_HEADS), :] = res[:, g * HEAD_DIM:(g + 1) * HEAD_DIM]
        v_ref[pl.ds(g, tm, stride=KV_HEADS), :] = res[:, KV_WIDTH + g * HEAD_DIM:KV_WIDTH + (g + 1) * HEAD_DIM]
    misc_ref[...] = res[:, 2 * KV_WIDTH:]


def _project_small(x2d, norm_w, w_small, tm, name):
    m, d = x2d.shape
    kv = jax.ShapeDtypeStruct((m * KV_HEADS, HEAD_DIM), F32)
    kv_spec = pl.BlockSpec((tm * KV_HEADS, HEAD_DIM), lambda i: (i, 0))
    return pl.pallas_call(
        _proj_small_kernel,
        out_shape=(kv, kv, jax.ShapeDtypeStruct((m, LANES), F32)),
        grid=(m // tm,),
        in_specs=[pl.BlockSpec((tm, d), lambda i: (i, 0)),
                  pl.BlockSpec((1, d), lambda i: (0, 0)),
                  pl.BlockSpec((SMALL_W, d), lambda i: (0, 0))],
        out_specs=(kv_spec, kv_spec, pl.BlockSpec((tm, LANES), lambda i: (i, 0))),
        compiler_params=pltpu.CompilerParams(dimension_semantics=("parallel",), vmem_limit_bytes=VMEM_LIMIT),
        name=name,
    )(x2d, norm_w.reshape(1, d), w_small)


MAIN_TN = 512
ROW_ALIGN = 8


def _main_tiles():
    offs = np.cumsum((0,) + IN_SIZES)
    aq, iq, az, mq, mk, mv, mo, mz = (int(offs[j]) for j in (0, 3, 6, 7, 8, 9, 12, 13))
    segments = [(aq, ATT_WIDTH, Q_SCALE), (iq, IDX_HEADS * IDX_DIM, 1.0), (az, ATT_WIDTH, 1.0),
                (mv, ML_WIDTH, 1.0), (mo, ML_WIDTH, 1.0), (mz, ML_WIDTH, 1.0),
                (mq, ML_HEADS * ML_QK, 1.0), (mk, ML_HEADS * ML_QK, ML_QK ** -0.5)]
    rows, scales = [], []
    for start, width, scale in segments:
        assert start % ROW_ALIGN == 0 and width % MAIN_TN == 0
        for r in range(start, start + width, MAIN_TN):
            rows.append(r // ROW_ALIGN)
            scales.append(scale)
    assert len(rows) * MAIN_TN == MAIN_W
    return np.asarray(rows, np.int32), np.asarray(scales, np.float32)


def _proj_main_kernel(rows_ref, scale_ref, x_ref, nw_ref, w_ref, o_ref, h_scr):
    j = pl.program_id(1)

    @pl.when(j == 0)
    def _():
        x = x_ref[...]
        ms = jnp.mean(x * x, axis=-1, keepdims=True)
        h_scr[...] = (x * lax.rsqrt(ms + RMS_EPS) * nw_ref[...]).astype(BF16)

    w = (w_ref[...] * scale_ref[j]).astype(BF16)
    o_ref[...] = _dot_nt(h_scr[...], w).astype(o_ref.dtype)


def _project_main(x2d, norm_w, w_t, out_dtype, tm, name):
    m, d = x2d.shape
    rows, scales = _main_tiles()
    return pl.pallas_call(
        _proj_main_kernel,
        out_shape=jax.ShapeDtypeStruct((m, MAIN_W), out_dtype),
        grid_spec=pltpu.PrefetchScalarGridSpec(
            num_scalar_prefetch=1,
            grid=(m // tm, len(rows)),
            in_specs=[pl.BlockSpec(memory_space=pltpu.SMEM),
                      pl.BlockSpec((tm, d), lambda i, j, rows: (i, 0)),
                      pl.BlockSpec((1, d), lambda i, j, rows: (0, 0)),
                      pl.BlockSpec((pl.Element(MAIN_TN), pl.Element(d)), lambda i, j, rows: (rows[j] * ROW_ALIGN, 0))],
            out_specs=pl.BlockSpec((tm, MAIN_TN), lambda i, j, rows: (i, j)),
            scratch_shapes=[pltpu.VMEM((tm, d), BF16)]),
        compiler_params=pltpu.CompilerParams(dimension_semantics=("parallel", "arbitrary"),
                                             vmem_limit_bytes=VMEM_LIMIT),
        name=name,
    )(jnp.asarray(rows), jnp.asarray(scales), x2d, norm_w.reshape(1, d), w_t)


def _out_kernel(x_ref, a_ref, m_ref, wa_ref, wm_ref, fw_ref, o_ref):
    y = x_ref[...] + _dot(a_ref[...], wa_ref[...]) + _dot(m_ref[...], wm_ref[...])
    ms = jnp.mean(y * y, axis=-1, keepdims=True)
    o_ref[...] = y * lax.rsqrt(ms + RMS_EPS) * fw_ref[...]


def _out_project(x2d, a, mo, w_att, w_ml, final_w, tm, name):
    m, d = x2d.shape
    return pl.pallas_call(
        _out_kernel,
        out_shape=jax.ShapeDtypeStruct((m, d), F32),
        grid=(m // tm,),
        in_specs=[pl.BlockSpec((tm, d), lambda i: (i, 0)),
                  pl.BlockSpec((tm, ATT_WIDTH), lambda i: (i, 0)),
                  pl.BlockSpec((tm, ML_WIDTH), lambda i: (i, 0)),
                  pl.BlockSpec((ATT_WIDTH, d), lambda i: (0, 0)),
                  pl.BlockSpec((ML_WIDTH, d), lambda i: (0, 0)),
                  pl.BlockSpec((1, d), lambda i: (0, 0))],
        out_specs=pl.BlockSpec((tm, d), lambda i: (i, 0)),
        compiler_params=pltpu.CompilerParams(dimension_semantics=("parallel",),
                                             vmem_limit_bytes=VMEM_LIMIT),
        name=name,
    )(x2d, a, mo, w_att, w_ml, final_w.reshape(1, d))


def _log_sigmoid(x):
    return jnp.minimum(x, 0.0) - jnp.log(1.0 + jnp.exp(-jnp.abs(x)))


def _sigmoid(x):
    return 0.5 * jnp.tanh(0.5 * x) + 0.5


def _silu(x):
    return x * _sigmoid(x)


ML_CHUNK = 256
ML_AUG = ML_V + LANES


def _mlstm_kernel(bi_ref, bf_ref, q_ref, k_ref, v_ref, mo_ref, mz_ref, misc_ref, nw_ref,
                  out_ref, ct_ref, m_ref):
    L = ML_CHUNK

    @pl.when(pl.program_id(1) == 0)
    def _():
        ct_ref[...] = jnp.zeros_like(ct_ref)
        m_ref[...] = jnp.zeros_like(m_ref)

    lane = lax.broadcasted_iota(I32, (1, LANES), 1)
    bias = jnp.zeros((1, LANES), F32)
    for h in range(ML_HEADS):
        bias = jnp.where(lane == MISC_IG + h, bi_ref[h], jnp.where(lane == MISC_FG + h, bf_ref[h], bias))
    pre = misc_ref[...] + bias
    gates = jnp.where(jnp.logical_and(lane >= MISC_FG, lane < MISC_FG + ML_HEADS), _log_sigmoid(pre), pre)
    gates_t = gates.T
    t_idx = lax.broadcasted_iota(I32, (L, L), 0)
    s_idx = lax.broadcasted_iota(I32, (L, L), 1)
    causal = s_idx <= t_idx
    ones_col = jnp.where(lax.broadcasted_iota(I32, (L, LANES), 1) == 0, 1.0, 0.0).astype(BF16)

    for h in range(ML_HEADS):
        ig_row = gates_t[MISC_IG + h:MISC_IG + h + 1, :]
        lf_row = gates_t[MISC_FG + h:MISC_FG + h + 1, :]
        lf_col = gates[:, MISC_FG + h:MISC_FG + h + 1]
        b_col = jnp.sum(jnp.where(causal, lf_row, 0.0), axis=1, keepdims=True)
        b_row = jnp.sum(jnp.where(t_idx <= s_idx, lf_col, 0.0), axis=0, keepdims=True)
        m_prev = m_ref[0, h][0:1, 0:1]
        log_d = jnp.where(causal, b_col - b_row + ig_row, -jnp.inf)
        log_a = b_col + m_prev
        m_t = jnp.maximum(log_a, jnp.max(log_d, axis=1, keepdims=True))
        d = jnp.exp(log_d - m_t)
        a = jnp.exp(log_a - m_t)

        q = q_ref[:, h * ML_QK:(h + 1) * ML_QK]
        k = k_ref[:, h * ML_QK:(h + 1) * ML_QK]
        v_aug = jnp.concatenate([v_ref[:, h * ML_V:(h + 1) * ML_V], ones_col], axis=1)
        s = (_dot_nt(q, k) * d).astype(BF16)
        ct = ct_ref[0, h]
        num_aug = a * _dot(q, ct.astype(BF16)) + _dot(s, v_aug)
        den = num_aug[:, ML_V:ML_V + 1]
        hh = num_aug[:, :ML_V] / jnp.maximum(jnp.abs(den), jnp.exp(-m_t))

        m_new = m_t[L - 1:L, :]
        a_end = a[L - 1:L, :]
        w_row = jnp.exp(b_row[:, L - 1:L] - b_row + ig_row - m_new)
        ktw = (k.astype(F32).T * w_row).astype(BF16)
        ct_new = a_end * ct + _dot(ktw, v_aug)
        ct_ref[0, h] = ct_new
        m_ref[0, h] = jnp.broadcast_to(m_new, (8, LANES))

        ms = jnp.mean(hh * hh, axis=1, keepdims=True)
        hn = hh * lax.rsqrt(ms + RMS_EPS) * nw_ref[:, h * ML_V:(h + 1) * ML_V]
        gate = _sigmoid(mo_ref[:, h * ML_V:(h + 1) * ML_V].astype(F32)) * _silu(mz_ref[:, h * ML_V:(h + 1) * ML_V].astype(F32))
        out_ref[:, h * ML_V:(h + 1) * ML_V] = (hn * gate).astype(out_ref.dtype)


def _mlstm_prompt(p_main, misc, b_i, b_f, ml_norm_w, batch, seq):
    L = ML_CHUNK
    nc = seq // L
    row = lambda b, c: b * nc + c
    return pl.pallas_call(
        _mlstm_kernel,
        out_shape=(jax.ShapeDtypeStruct((batch * seq, ML_WIDTH), BF16),
                   jax.ShapeDtypeStruct((batch, ML_HEADS, ML_QK, ML_AUG), F32),
                   jax.ShapeDtypeStruct((batch, ML_HEADS, 8, LANES), F32)),
        grid=(batch, nc),
        in_specs=[pl.BlockSpec(memory_space=pltpu.SMEM),
                  pl.BlockSpec(memory_space=pltpu.SMEM),
                  pl.BlockSpec((L, 512), lambda b, c: (row(b, c), 12)),
                  pl.BlockSpec((L, 512), lambda b, c: (row(b, c), 13)),
                  pl.BlockSpec((L, 1024), lambda b, c: (row(b, c), 3)),
                  pl.BlockSpec((L, 1024), lambda b, c: (row(b, c), 4)),
                  pl.BlockSpec((L, 1024), lambda b, c: (row(b, c), 5)),
                  pl.BlockSpec((L, LANES), lambda b, c: (row(b, c), 0)),
                  pl.BlockSpec((1, ML_WIDTH), lambda b, c: (0, 0))],
        out_specs=(pl.BlockSpec((L, ML_WIDTH), lambda b, c: (row(b, c), 0)),
                   pl.BlockSpec((1, ML_HEADS, ML_QK, ML_AUG), lambda b, c: (b, 0, 0, 0)),
                   pl.BlockSpec((1, ML_HEADS, 8, LANES), lambda b, c: (b, 0, 0, 0))),
        compiler_params=pltpu.CompilerParams(dimension_semantics=("parallel", "arbitrary"),
                                             vmem_limit_bytes=VMEM_LIMIT),
        name="mlstm_prompt",
    )(b_i, b_f, p_main, p_main, p_main, p_main, p_main, misc, ml_norm_w.reshape(1, ML_WIDTH))


INT_MIN = -2 ** 31


def _float_to_key(x):
    bits = lax.bitcast_convert_type(x, I32)
    return jnp.where(bits >= 0, bits, bits ^ jnp.int32(0x7FFFFFFF))


def _key_to_float(key):
    bits = jnp.where(key >= 0, key, key ^ jnp.int32(0x7FFFFFFF))
    return lax.bitcast_convert_type(bits, F32)


KEY_NEG_INF = INT_MIN + 0x7FFFFF


def _kth_largest_key(count_ge, shape, k):
    def bit_body(it, prefix):
        cand = prefix + lax.shift_left(jnp.int32(1), 31 - it)
        cand_f = _key_to_float(jnp.maximum(cand, KEY_NEG_INF))
        return jnp.where(count_ge(cand_f) >= float(k), cand, prefix)

    return lax.fori_loop(0, 32, bit_body, jnp.full(shape, INT_MIN, I32))


def _tie_cutoff(count_tie_le, need, shape, index_bits):
    def bit_body(it, lo):
        cand = lo + lax.shift_left(jnp.int32(1), index_bits - 1 - it).astype(F32)
        return jnp.where(count_tie_le(cand) < need, cand, lo)

    return lax.fori_loop(0, index_bits, bit_body, jnp.full(shape, -1.0, F32)) + 1.0


DSA_QB = 128
DSA_TK = 512
DSA_TS = 256
DSA_TA = 256
HEAD_PAIR = 2


def _dsa_kernel(q_ref, qi_ref, az_ref, miscq_ref, k_ref, v_ref, misck_ref, out_ref,
                k_bf, vt_scr, ki_lo, ki_hi, sc_scr, acc_scr, j_scr, *head_scr, topk):
    i = pl.program_id(1)
    seq = misck_ref.shape[0]
    QB, TK, TS, TA = DSA_QB, DSA_TK, DSA_TS, DSA_TA
    nt = (i * QB + QB + TK - 1) // TK

    @pl.when(i == 0)
    def _():
        for g in range(KV_HEADS):
            k_bf[:, g * HEAD_DIM:(g + 1) * HEAD_DIM] = k_ref[pl.ds(g, seq, stride=KV_HEADS), :].astype(BF16)

        def v_tile(a, carry):
            for g in range(KV_HEADS):
                rows = pl.ds(pl.multiple_of(a * (TA * KV_HEADS), TA * KV_HEADS) + g, TA, stride=KV_HEADS)
                vt_scr[a, g * HEAD_DIM:(g + 1) * HEAD_DIM, :] = v_ref[rows, :].T.astype(BF16)
            return carry

        lax.fori_loop(0, seq // TA, v_tile, 0)
        lane = lax.broadcasted_iota(I32, (seq, LANES), 1)
        lo = jnp.where(lane < IDX_DIM, misck_ref[...], 0.0)
        ki_lo[...] = lo.astype(BF16)
        ki_hi[...] = pltpu.roll(lo, IDX_DIM, axis=1).astype(BF16)

    k_iota = lax.broadcasted_iota(I32, (TK, QB), 0)
    q_pos = i * QB + lax.broadcasted_iota(I32, (TK, QB), 1)
    w_t = miscq_ref[...].T

    def tile_rows(t):
        return pl.ds(pl.multiple_of(t * TK, TK), TK)

    def score_tile(t, carry):
        rows = pl.ds(pl.multiple_of(t * TS, TS), TS)
        klo = ki_lo[rows, :]
        khi = ki_hi[rows, :]
        acc = jnp.zeros((TS, QB), F32)
        for p in range(IDX_HEADS // 2):
            qp = qi_ref[:, p * LANES:(p + 1) * LANES]
            w0 = w_t[MISC_WI + 2 * p:MISC_WI + 2 * p + 1, :]
            w1 = w_t[MISC_WI + 2 * p + 1:MISC_WI + 2 * p + 2, :]
            acc = acc + w0 * jnp.maximum(_dot_nt(klo, qp), 0.0) + w1 * jnp.maximum(_dot_nt(khi, qp), 0.0)
        k_pos = t * TS + lax.broadcasted_iota(I32, (TS, QB), 0)
        q_pos_s = i * QB + lax.broadcasted_iota(I32, (TS, QB), 1)
        score = jnp.where(k_pos <= q_pos_s, acc * IDX_SCALE + 0.0, -jnp.inf)
        sc_scr[rows, :] = _float_to_key(score)
        return carry

    def score_tiles(t, carry):
        for sub in range(TK // TS):
            score_tile(t * (TK // TS) + sub, carry)
        return carry

    lax.fori_loop(0, nt, score_tiles, 0)

    def count_key_ge(cand):
        def body(t, cnt):
            return cnt + _fold_rows(jnp.add, jnp.where(sc_scr[tile_rows(t), :] >= cand, 1, 0))

        cnt = lax.fori_loop(0, nt, body, jnp.zeros((8, QB), I32))
        return jnp.sum(cnt, axis=0, keepdims=True)

    def bit_body(it, carry):
        prefix, cnt_at = carry
        cand = prefix + lax.shift_left(jnp.int32(1), 31 - it)
        cnt = count_key_ge(cand)
        ok = cnt >= topk
        return jnp.where(ok, cand, prefix), jnp.where(ok, cnt, cnt_at)

    thr, cnt_thr = lax.fori_loop(0, 32, bit_body, (jnp.full((1, QB), INT_MIN, I32), jnp.full((1, QB), TK, I32) * nt))

    j_scr[...] = jnp.full((1, QB), float(seq), F32)

    @pl.when(jnp.max(cnt_thr) > topk)
    def _():
        def count_keys(indicator):
            def body(t, cnt):
                k_pos = (t * TK + k_iota).astype(F32)
                return cnt + _fold_rows(jnp.add, indicator(sc_scr[tile_rows(t), :], k_pos))

            return jnp.sum(lax.fori_loop(0, nt, body, jnp.zeros((8, QB), F32)), axis=0, keepdims=True)

        need = float(topk) - count_keys(lambda key, k_pos: jnp.where(key > thr, 1.0, 0.0))
        count_tie_le = lambda cut: count_keys(
            lambda key, k_pos: jnp.where(key == thr, jnp.where(k_pos <= cut, 1.0, 0.0), 0.0))
        j_scr[...] = _tie_cutoff(count_tie_le, need, (1, QB), int(np.log2(seq)))

    cut = j_scr[...]

    def bias_tile(t, carry):
        key = sc_scr[tile_rows(t), :]
        k_pos = t * TK + k_iota
        taken = jnp.where(key > thr, 0.0,
                          jnp.where(key == thr, jnp.where(k_pos.astype(F32) <= cut, 0.0, NEG_BIG), NEG_BIG))
        sc_scr[tile_rows(t), :] = lax.bitcast_convert_type(jnp.where(k_pos <= q_pos, taken, NEG_BIG), I32)
        return carry

    lax.fori_loop(0, nt, bias_tile, 0)

    assert TK == 2 * TA and GROUP % HEAD_PAIR == 0
    az = az_ref[...].astype(F32)
    n_pairs = ATT_HEADS // HEAD_PAIR
    PQ = HEAD_PAIR * QB
    lg_scr, p_scr = head_scr[:n_pairs], head_scr[n_pairs:]
    acc_scr[...] = jnp.zeros_like(acc_scr)
    last = 2 * nt - 1

    def sub_rows(a):
        return pl.ds(pl.multiple_of(a * TA, TA), TA)

    def kv_cols(j):
        g = (j * HEAD_PAIR) // GROUP
        return slice(g * HEAD_DIM, (g + 1) * HEAD_DIM)

    def logits_stage(j, a, slot):
        q_pair = jnp.concatenate([q_ref[:, h * HEAD_DIM:(h + 1) * HEAD_DIM]
                                  for h in range(j * HEAD_PAIR, (j + 1) * HEAD_PAIR)], axis=0)
        lg_scr[j][slot] = _dot_nt(k_bf[sub_rows(a), kv_cols(j)], q_pair)

    CH = 64

    def softmax_stage(j, a, slot, state):
        m, l8, _ = state
        maxes = []
        for r in range(0, TA, CH):
            bias = lax.bitcast_convert_type(sc_scr[pl.ds(pl.multiple_of(a * TA, TA) + r, CH), :], F32)
            lg = lg_scr[j][slot, r:r + CH, :] + jnp.concatenate([bias] * HEAD_PAIR, axis=1)
            lg_scr[j][slot, r:r + CH, :] = lg
            maxes.append(_fold_rows(jnp.maximum, lg))
        m_new = jnp.maximum(m, jnp.max(_tree_reduce(jnp.maximum, maxes), axis=0, keepdims=True))
        alpha = jnp.exp2(m - m_new)
        sums = []
        for r in range(0, TA, CH):
            p = jnp.exp2(lg_scr[j][slot, r:r + CH, :] - m_new)
            p_scr[j][slot, r:r + CH, :] = p.astype(BF16)
            sums.append(_fold_rows(jnp.add, p))
        return m_new, alpha * l8 + _tree_reduce(jnp.add, sums), alpha

    def value_stage(j, a, slot, state):
        vt = vt_scr[a][kv_cols(j), :]
        acc_scr[j] = state[2] * acc_scr[j] + _dot(vt, p_scr[j][slot])

    def att_tile(t, state):
        for slot in range(2):
            a = 2 * t + slot
            out = []
            for j in range(n_pairs):
                value_stage(j, jnp.maximum(a - 1, 0), 1 - slot, state[j])
                out.append(softmax_stage(j, a, slot, state[j]))
                logits_stage(j, jnp.minimum(a + 1, last), 1 - slot)
            state = tuple(out)
        return state

    init = tuple((jnp.full((1, PQ), NEG_BIG, F32), jnp.zeros((8, PQ), F32), jnp.ones((1, PQ), F32))
                 for _ in range(n_pairs))
    for j in range(n_pairs):
        p_scr[j][1] = jnp.zeros((TA, PQ), BF16)
        logits_stage(j, 0, 0)
    fin = lax.fori_loop(0, nt, att_tile, init)
    for j in range(n_pairs):
        value_stage(j, last, 1, fin[j])
    for j in range(n_pairs):
        o_t = acc_scr[j] / jnp.sum(fin[j][1], axis=0, keepdims=True)
        for hh in range(HEAD_PAIR):
            c0 = (j * HEAD_PAIR + hh) * HEAD_DIM
            o = o_t[:, hh * QB:(hh + 1) * QB].T
            out_ref[:, c0:c0 + HEAD_DIM] = (o * _silu(az[:, c0:c0 + HEAD_DIM])).astype(out_ref.dtype)


def _dsa_prompt(p_main, k_rows, v_rows, misc, batch, seq):
    QB, TK = DSA_QB, DSA_TK
    nb = seq // QB
    topk = min(TOPK_MAX, seq // 4)
    TA = DSA_TA
    assert TK >= topk and seq % TK == 0 and seq % QB == 0 and TK % DSA_TS == 0
    qrow = lambda b, i: b * nb + i
    return pl.pallas_call(
        functools.partial(_dsa_kernel, topk=topk),
        out_shape=jax.ShapeDtypeStruct((batch * seq, ATT_WIDTH), BF16),
        grid=(batch, nb),
        in_specs=[pl.BlockSpec((QB, ATT_WIDTH), lambda b, i: (qrow(b, i), 0)),
                  pl.BlockSpec((QB, 1024), lambda b, i: (qrow(b, i), 1)),
                  pl.BlockSpec((QB, ATT_WIDTH), lambda b, i: (qrow(b, i), 2)),
                  pl.BlockSpec((QB, LANES), lambda b, i: (qrow(b, i), 0)),
                  pl.BlockSpec((seq * KV_HEADS, HEAD_DIM), lambda b, i: (b, 0)),
                  pl.BlockSpec((seq * KV_HEADS, HEAD_DIM), lambda b, i: (b, 0)),
                  pl.BlockSpec((seq, LANES), lambda b, i: (b, 0))],
        out_specs=pl.BlockSpec((QB, ATT_WIDTH), lambda b, i: (qrow(b, i), 0)),
        scratch_shapes=[pltpu.VMEM((seq, KV_WIDTH), BF16),
                        pltpu.VMEM((seq // TA, KV_WIDTH, TA), BF16),
                        pltpu.VMEM((seq, LANES), BF16),
                        pltpu.VMEM((seq, LANES), BF16),
                        pltpu.VMEM((seq, QB), I32),
                        pltpu.VMEM((ATT_HEADS // HEAD_PAIR, HEAD_DIM, HEAD_PAIR * QB), F32),
                        pltpu.VMEM((1, QB), F32)]
                       + [pltpu.VMEM((2, TA, HEAD_PAIR * QB), F32)] * (ATT_HEADS // HEAD_PAIR)
                       + [pltpu.VMEM((2, TA, HEAD_PAIR * QB), BF16)] * (ATT_HEADS // HEAD_PAIR),
        compiler_params=pltpu.CompilerParams(dimension_semantics=("parallel", "arbitrary"),
                                             vmem_limit_bytes=VMEM_LIMIT),
        name="dsa_prompt",
    )(p_main, p_main, p_main, misc, k_rows, v_rows, misc)


SCORE_ROWS = 32


def _sample_scores_kernel(pt_ref, qi_ref, wb_ref, kin_ref, cache_ref, sc_ref, snew_ref, page_buf, sems):
    b = pl.program_id(0)
    n_rows = pl.num_programs(0)
    n_pages = sc_ref.shape[1]
    slot = lax.rem(b, 2)

    def page_copy(row, p, to_slot):
        return pltpu.make_async_copy(cache_ref.at[pt_ref[row, p]], page_buf.at[to_slot, p], sems.at[to_slot])

    def start_fetch(row, to_slot):
        def body(p, carry):
            page_copy(row, p, to_slot).start()
            return carry

        lax.fori_loop(0, n_pages, body, 0, unroll=8)

    @pl.when(b == 0)
    def _():
        start_fetch(0, 0)

    @pl.when(b + 1 < n_rows)
    def _():
        start_fetch(b + 1, 1 - slot)

    def wait_page(p, carry):
        page_copy(b, p, slot).wait()
        return carry

    lax.fori_loop(0, n_pages, wait_page, 0, unroll=8)

    qi = qi_ref[0]
    wb = wb_ref[0]

    def score_rows(g, carry):
        rows = []
        for j in range(SCORE_ROWS):
            page = page_buf[slot, g * SCORE_ROWS + j]
            s = _dot(qi, page.astype(BF16))
            rows.append(jnp.sum(jnp.maximum(s, 0.0) * wb, axis=0, keepdims=True) * IDX_SCALE)
        sc_ref[0, pl.ds(pl.multiple_of(g * SCORE_ROWS, SCORE_ROWS), SCORE_ROWS), :] = jnp.concatenate(rows, axis=0)
        return carry

    lax.fori_loop(0, n_pages // SCORE_ROWS, score_rows, 0)

    ki_new = kin_ref[0].astype(BF16).astype(F32)
    s = jnp.sum(qi.astype(F32) * ki_new, axis=1, keepdims=True)
    s_new = jnp.sum(jnp.maximum(s, 0.0) * wb[:, 0:1], axis=0, keepdims=True) * IDX_SCALE
    snew_ref[0] = jnp.broadcast_to(s_new, (8, LANES))


def _sample_scores(page_table, qi16, wb, ki_new, cache_ik_t):
    bd, n_pages = page_table.shape
    assert n_pages % SCORE_ROWS == 0
    bmap = lambda b, pt: (b, 0, 0)
    return pl.pallas_call(
        _sample_scores_kernel,
        out_shape=(jax.ShapeDtypeStruct((bd, n_pages, PAGE), F32),
                   jax.ShapeDtypeStruct((bd, 8, LANES), F32)),
        grid_spec=pltpu.PrefetchScalarGridSpec(
            num_scalar_prefetch=1,
            grid=(bd,),
            in_specs=[pl.BlockSpec((1, IDX_HEADS, IDX_DIM), bmap),
                      pl.BlockSpec((1, IDX_HEADS, LANES), bmap),
                      pl.BlockSpec((1, 1, IDX_DIM), bmap),
                      pl.BlockSpec(memory_space=pl.ANY)],
            out_specs=(pl.BlockSpec((1, n_pages, PAGE), bmap),
                       pl.BlockSpec((1, 8, LANES), bmap)),
            scratch_shapes=[pltpu.VMEM((2, n_pages, IDX_DIM, PAGE), F32),
                            pltpu.SemaphoreType.DMA((2,))]),
        compiler_params=pltpu.CompilerParams(dimension_semantics=("arbitrary",),
                                             vmem_limit_bytes=VMEM_LIMIT),
        name="sample_scores",
    )(page_table, qi16, wb, ki_new, cache_ik_t)


SEL_THR, SEL_NEXT, SEL_CUT = 0, 1, 2


def _sample_select_kernel(sc_ref, snew_ref, sel_ref, *, topk):
    x = sc_ref[...]
    bd, n_pages, _ = x.shape
    past = n_pages * PAGE
    s_new = snew_ref[:, 0:1, 0:1]

    def total(v):
        return jnp.sum(jnp.sum(v, axis=1, keepdims=True), axis=2, keepdims=True)

    def count_ge(cand):
        return total(jnp.where(x >= cand, 1.0, 0.0)) + jnp.where(s_new >= cand, 1.0, 0.0)

    key = _kth_largest_key(count_ge, (bd, 1, 1), topk)
    thr = _key_to_float(key)
    thr_next = _key_to_float(key + 1)
    need = float(topk) - count_ge(thr_next)
    pos = (lax.broadcasted_iota(I32, (1, n_pages, PAGE), 1) * PAGE
           + lax.broadcasted_iota(I32, (1, n_pages, PAGE), 2)).astype(F32)

    def count_tie_le(cut):
        tie = jnp.where(x >= thr_next, 0.0, jnp.where(x >= thr, jnp.where(pos <= cut, 1.0, 0.0), 0.0))
        tie_new = jnp.where(s_new >= thr_next, 0.0, jnp.where(s_new >= thr, jnp.where(float(past) <= cut, 1.0, 0.0), 0.0))
        return total(tie) + tie_new

    row = lax.broadcasted_iota(I32, (bd, 8, LANES), 1)
    sel_ref[...] = jnp.where(row == SEL_THR, thr, jnp.where(row == SEL_NEXT, thr_next, float(past + 1)))

    @pl.when(jnp.max(count_ge(thr)) > float(topk))
    def _():
        cut = _tie_cutoff(count_tie_le, need, (bd, 1, 1), int(np.log2(past)) + 1)
        sel_ref[...] = jnp.where(row == SEL_THR, thr, jnp.where(row == SEL_NEXT, thr_next, cut))


def _sample_select(scores, s_new, topk):
    bd = scores.shape[0]
    return pl.pallas_call(
        functools.partial(_sample_select_kernel, topk=topk),
        out_shape=jax.ShapeDtypeStruct((bd, 8, LANES), F32),
        compiler_params=pltpu.CompilerParams(vmem_limit_bytes=VMEM_LIMIT),
        name="sample_select",
    )(scores, s_new)


def _taken_bias(score, pos, sel):
    thr, thr_next, cut = sel[SEL_THR:SEL_THR + 1, 0:1], sel[SEL_NEXT:SEL_NEXT + 1, 0:1], sel[SEL_CUT:SEL_CUT + 1, 0:1]
    return jnp.where(score >= thr_next, 0.0,
                     jnp.where(score >= thr, jnp.where(pos <= cut, 0.0, NEG_BIG), NEG_BIG))


def _sample_compact_kernel(pt_ref, sc_ref, sel_ref, rows_ref, nsel_ref, *, slots):
    n_pages = sc_ref.shape[1]
    pos = (lax.broadcasted_iota(I32, (n_pages, PAGE), 0) * PAGE
           + lax.broadcasted_iota(I32, (n_pages, PAGE), 1)).astype(F32)
    taken = jnp.where(_taken_bias(sc_ref[0], pos, sel_ref[0]) == 0.0, 1.0, 0.0)
    before = lax.broadcasted_iota(I32, (PAGE, PAGE), 0) < lax.broadcasted_iota(I32, (PAGE, PAGE), 1)
    in_page = _dot(taken, jnp.where(before, 1.0, 0.0))
    taken_t = taken.T
    rank_t = jnp.where(taken_t > 0.0, in_page.T, -1.0)
    page_tot = jnp.sum(taken_t, axis=0, keepdims=True)
    earlier = lax.broadcasted_iota(I32, (n_pages, n_pages), 0) < lax.broadcasted_iota(I32, (n_pages, n_pages), 1)
    first_slot = _dot(jnp.broadcast_to(page_tot, (8, n_pages)),
                      jnp.where(earlier, 1.0, 0.0))[0:1]
    phys_t = (pt_ref[0] * PAGE + lax.broadcasted_iota(I32, (PAGE, n_pages), 0)).astype(F32)
    slot = lax.broadcasted_iota(I32, (slots, n_pages), 0).astype(F32)

    def rank_body(r, acc):
        r = r.astype(F32)
        row = jnp.sum(jnp.where(rank_t == r, phys_t, 0.0), axis=0, keepdims=True)
        target = jnp.where(r < page_tot, first_slot + r, -1.0)
        return acc + jnp.where(slot == target, row, 0.0)

    most = jnp.max(page_tot).astype(I32)
    acc = lax.fori_loop(0, most, rank_body, jnp.zeros((slots, n_pages), F32))
    rows_ref[0] = jnp.broadcast_to(jnp.sum(acc, axis=1, keepdims=True), (slots, LANES))
    nsel_ref[0] = jnp.broadcast_to(jnp.sum(page_tot, axis=1, keepdims=True), (8, LANES))


def _sample_compact(page_table, scores, sel, slots):
    bd, n_pages = page_table.shape
    assert n_pages % LANES == 0
    bmap = lambda b: (b, 0, 0)
    return pl.pallas_call(
        functools.partial(_sample_compact_kernel, slots=slots),
        out_shape=(jax.ShapeDtypeStruct((bd, slots, LANES), F32), jax.ShapeDtypeStruct((bd, 8, LANES), F32)),
        grid=(bd,),
        in_specs=[pl.BlockSpec((1, 1, n_pages), bmap), pl.BlockSpec((1, n_pages, PAGE), bmap),
                  pl.BlockSpec((1, 8, LANES), bmap)],
        out_specs=(pl.BlockSpec((1, slots, LANES), bmap), pl.BlockSpec((1, 8, LANES), bmap)),
        compiler_params=pltpu.CompilerParams(dimension_semantics=("parallel",), vmem_limit_bytes=VMEM_LIMIT),
        name="sample_compact",
    )(page_table.reshape(bd, 1, n_pages), scores, sel)


SC_GATHER_CHUNK = 128


def _gather_rows(table_k, table_v, idx):
    info = plsc.get_sparse_core_info()
    n_workers = info.num_cores * info.num_subcores
    n_idx = idx.shape[0]
    per_worker = n_idx // n_workers
    assert per_worker * n_workers == n_idx and per_worker % SC_GATHER_CHUNK == 0
    mesh = plsc.VectorSubcoreMesh(core_axis_name="c", subcore_axis_name="s")
    out = jax.ShapeDtypeStruct((n_idx, table_k.shape[1]), table_k.dtype)

    @functools.partial(
        pl.kernel, mesh=mesh, out_type=(out, out),
        scratch_types=[pltpu.VMEM((SC_GATHER_CHUNK,), I32),
                       pltpu.VMEM((SC_GATHER_CHUNK, table_k.shape[1]), table_k.dtype),
                       pltpu.VMEM((SC_GATHER_CHUNK, table_v.shape[1]), table_v.dtype),
                       pltpu.SemaphoreType.DMA, pltpu.SemaphoreType.DMA])
    def gather(tk_hbm, tv_hbm, idx_hbm, ok_hbm, ov_hbm, idx_v, rk_v, rv_v, sem_k, sem_v):
        worker = lax.axis_index("s") * info.num_cores + lax.axis_index("c")

        @pl.loop(0, per_worker // SC_GATHER_CHUNK)
        def _(j):
            base = worker * per_worker + j * SC_GATHER_CHUNK
            pltpu.sync_copy(idx_hbm.at[pl.ds(base, SC_GATHER_CHUNK)], idx_v)
            copy_k = pltpu.async_copy(tk_hbm.at[idx_v], rk_v, sem_k)
            copy_v = pltpu.async_copy(tv_hbm.at[idx_v], rv_v, sem_v)
            copy_k.wait()
            copy_v.wait()
            pltpu.sync_copy(rk_v, ok_hbm.at[pl.ds(base, SC_GATHER_CHUNK)])
            pltpu.sync_copy(rv_v, ov_hbm.at[pl.ds(base, SC_GATHER_CHUNK)])

    return gather(table_k, table_v, idx)


def _sample_attend_kernel(q_ref, az_ref, kn_ref, vn_ref, snew_ref, sel_ref, nsel_ref, k_ref, v_ref, out_ref, *, past):
    width = k_ref.shape[1]
    q = q_ref[0]
    col = lax.broadcasted_iota(I32, (ATT_HEADS, width), 1)
    head = lax.broadcasted_iota(I32, (ATT_HEADS, width), 0)
    own_head = col % KV_HEADS == head // GROUP
    filled = (col // KV_HEADS).astype(F32) < nsel_ref[0][0:1, 0:1]
    lg = _dot_nt(q, k_ref[0].astype(BF16))
    lg = jnp.where(own_head, jnp.where(filled, lg, NEG_BIG), NEG_BIG)
    lg_new = (jnp.sum(q.astype(F32) * kn_ref[0].astype(BF16).astype(F32), axis=1, keepdims=True)
              + _taken_bias(snew_ref[0][0:1, 0:1], float(past), sel_ref[0]))
    m = jnp.maximum(jnp.max(lg, axis=1, keepdims=True), lg_new)
    p = jnp.exp2(lg - m)
    p_new = jnp.exp2(lg_new - m)
    l = jnp.sum(p, axis=1, keepdims=True) + p_new
    acc = _dot(p.astype(BF16), v_ref[0].astype(BF16)) + p_new * vn_ref[0].astype(BF16).astype(F32)
    out_ref[0] = (acc / l) * _silu(az_ref[0])


def _sample_attend(q8, az8, k_new8, v_new8, s_new, sel, n_sel, k_sel, v_sel, past):
    bd, width, _ = k_sel.shape
    bmap = lambda b: (b, 0, 0)
    head_tile = pl.BlockSpec((1, ATT_HEADS, HEAD_DIM), bmap)
    par_tile = pl.BlockSpec((1, 8, LANES), bmap)
    rows_tile = pl.BlockSpec((1, width, HEAD_DIM), bmap)
    return pl.pallas_call(
        functools.partial(_sample_attend_kernel, past=past),
        out_shape=jax.ShapeDtypeStruct((bd, ATT_HEADS, HEAD_DIM), F32),
        grid=(bd,),
        in_specs=[head_tile, head_tile, head_tile, head_tile, par_tile, par_tile, par_tile, rows_tile, rows_tile],
        out_specs=head_tile,
        compiler_params=pltpu.CompilerParams(dimension_semantics=("parallel",), vmem_limit_bytes=VMEM_LIMIT),
        name="sample_attend",
    )(q8, az8, k_new8, v_new8, s_new, sel, n_sel, k_sel, v_sel)


def _mlstm_step_kernel(bi_ref, bf_ref, q_ref, k_ref, v_ref, mo_ref, mz_ref, misc_ref, nw_ref,
                       c_ref, n_ref, m_ref, out_ref, c_out, n_out, m_out):
    misc = misc_ref[0]
    eye = lax.broadcasted_iota(I32, (ML_V, ML_V), 0) == lax.broadcasted_iota(I32, (ML_V, ML_V), 1)
    for h in range(ML_HEADS):
        ig = misc[:, MISC_IG + h:MISC_IG + h + 1] + bi_ref[h]
        lf = _log_sigmoid(misc[:, MISC_FG + h:MISC_FG + h + 1] + bf_ref[h])
        m_prev = m_ref[0, h][:, 0:1]
        log_a = lf + m_prev
        m_t = jnp.maximum(log_a, ig)
        d = jnp.exp(ig - m_t)
        a = jnp.exp(log_a - m_t)
        q = q_ref[0][:, h * ML_QK:(h + 1) * ML_QK]
        k = k_ref[0][:, h * ML_QK:(h + 1) * ML_QK]
        v = v_ref[0][:, h * ML_V:(h + 1) * ML_V]
        v_col = jnp.sum(jnp.where(eye, v, 0.0), axis=1, keepdims=True)
        c = c_ref[0, h]
        n = n_ref[0, h]
        s = jnp.sum(q * k, axis=1, keepdims=True) * d
        num = a * jnp.sum(c * q, axis=1, keepdims=True) + s * v_col
        den = a * jnp.sum(n * q, axis=1, keepdims=True) + s
        h_col = num / jnp.maximum(jnp.abs(den), jnp.exp(-m_t))
        c_out[0, h] = a * c + (d * v_col) * k
        n_out[0, h] = a * n + d * k
        m_out[0, h] = jnp.broadcast_to(m_t, (1, LANES))

        h_row = jnp.sum(jnp.where(eye, h_col, 0.0), axis=0, keepdims=True)
        ms = jnp.mean(h_row * h_row, axis=1, keepdims=True)
        hn = h_row * lax.rsqrt(ms + RMS_EPS) * nw_ref[:, h * ML_V:(h + 1) * ML_V]
        gate = _sigmoid(mo_ref[0][:, h * ML_V:(h + 1) * ML_V]) * _silu(mz_ref[0][:, h * ML_V:(h + 1) * ML_V])
        out_ref[0, :, h * ML_V:(h + 1) * ML_V] = (hn * gate).astype(out_ref.dtype)


def _mlstm_step(ps_main, misc, b_i, b_f, ml_norm_w, state_c, state_n, state_m):
    bd = ps_main.shape[0]
    col = lambda j: (lambda b: (b, 0, j))
    st4 = lambda b: (b, 0, 0, 0)
    return pl.pallas_call(
        _mlstm_step_kernel,
        out_shape=(jax.ShapeDtypeStruct((bd, 1, ML_WIDTH), BF16),
                   jax.ShapeDtypeStruct(state_c.shape, F32),
                   jax.ShapeDtypeStruct(state_n.shape, F32),
                   jax.ShapeDtypeStruct(state_m.shape, F32)),
        grid=(bd,),
        in_specs=[pl.BlockSpec(memory_space=pltpu.SMEM),
                  pl.BlockSpec(memory_space=pltpu.SMEM),
                  pl.BlockSpec((1, 1, 512), col(12)),
                  pl.BlockSpec((1, 1, 512), col(13)),
                  pl.BlockSpec((1, 1, 1024), col(3)),
                  pl.BlockSpec((1, 1, 1024), col(4)),
                  pl.BlockSpec((1, 1, 1024), col(5)),
                  pl.BlockSpec((1, 1, LANES), col(0)),
                  pl.BlockSpec((1, ML_WIDTH), lambda b: (0, 0)),
                  pl.BlockSpec((1, ML_HEADS, ML_V, ML_QK), st4),
                  pl.BlockSpec((1, ML_HEADS, 1, ML_QK), st4),
                  pl.BlockSpec((1, ML_HEADS, 1, LANES), st4)],
        out_specs=(pl.BlockSpec((1, 1, ML_WIDTH), lambda b: (b, 0, 0)),
                   pl.BlockSpec((1, ML_HEADS, ML_V, ML_QK), st4),
                   pl.BlockSpec((1, ML_HEADS, 1, ML_QK), st4),
                   pl.BlockSpec((1, ML_HEADS, 1, LANES), st4)),
        compiler_params=pltpu.CompilerParams(dimension_semantics=("parallel",),
                                             vmem_limit_bytes=VMEM_LIMIT),
        name="mlstm_step",
    )(b_i, b_f, ps_main, ps_main, ps_main, ps_main, ps_main, misc, ml_norm_w.reshape(1, ML_WIDTH),
      state_c, state_n, state_m)


def _small_weight(w_t):
    offs = np.cumsum((0,) + IN_SIZES)
    ak, av, ik, iw, mi, mf = (w_t[offs[j]:offs[j + 1]] for j in (1, 2, 4, 5, 10, 11))
    pad = jnp.zeros((LANES - IDX_DIM - IDX_HEADS - 2 * ML_HEADS, w_t.shape[1]), w_t.dtype)
    w_small = jnp.concatenate([ak, av, ik, iw, mi, mf, pad], axis=0)
    assert w_small.shape[0] == SMALL_W
    return w_small


def kernel(x_prompt, x_sample, cache_k, cache_v, cache_idx_k, state_C, state_n, state_m, page_table,
           norm_w, w_in, b_igate, b_fgate, ml_norm_w, w_out, final_norm_w):
    depth = w_in.shape[0]
    batch, seq, d = x_prompt.shape
    bd, dec_seq, _ = x_sample.shape
    assert depth == 1 and dec_seq == 1 and d == D_MODEL
    n_pages = page_table.shape[1]

    w_t = jnp.swapaxes(w_in[0], 0, 1)
    w_small = _small_weight(w_t)
    w_o = w_out[0].astype(BF16)
    w_o_att, w_o_ml = w_o[:ATT_WIDTH], w_o[ATT_WIDTH:]

    xp = x_prompt.reshape(batch * seq, d)
    p_main = _project_main(xp, norm_w[0], w_t, BF16, 1024, "proj_main")
    k_rows, v_rows, misc = _project_small(xp, norm_w[0], w_small, 1024, "proj_small")
    att = _dsa_prompt(p_main, k_rows, v_rows, misc, batch, seq)
    ml, ct, m_p = _mlstm_prompt(p_main, misc, b_igate[0], b_fgate[0], ml_norm_w[0], batch, seq)
    y_prompt = _out_project(xp, att, ml, w_o_att, w_o_ml, final_norm_w, 256, "out_prompt").reshape(batch, seq, d)
    k_prompt = k_rows.reshape(1, batch, seq, KV_HEADS, HEAD_DIM)
    v_prompt = v_rows.reshape(1, batch, seq, KV_HEADS, HEAD_DIM)
    ik_prompt = misc[:, :IDX_DIM].reshape(1, batch, seq, IDX_DIM)
    c_prompt = jnp.swapaxes(ct[..., :ML_V], -1, -2)[None]
    n_prompt = ct[..., ML_V][None]
    m_prompt = m_p[:, :, 0, 0][None]

    xs = x_sample.reshape(bd, d)
    ps_main = _project_main(xs, norm_w[0], w_t, F32, bd, "proj_main_s")
    ks_rows, vs_rows, misc_s = _project_small(xs, norm_w[0], w_small, bd, "proj_small_s")
    q8 = ps_main[:, :ATT_WIDTH].reshape(bd, ATT_HEADS, HEAD_DIM).astype(BF16)
    qi16 = ps_main[:, 1024:2048].reshape(bd, IDX_HEADS, IDX_DIM).astype(BF16)
    az8 = ps_main[:, 2048:3072].reshape(bd, ATT_HEADS, HEAD_DIM)
    ki_new = misc_s[:, :IDX_DIM].reshape(bd, 1, IDX_DIM)
    wb = jnp.broadcast_to(misc_s[:, MISC_WI:MISC_WI + IDX_HEADS, None], (bd, IDX_HEADS, LANES))
    k_new8 = jnp.repeat(ks_rows.reshape(bd, KV_HEADS, HEAD_DIM), GROUP, axis=1)
    v_new8 = jnp.repeat(vs_rows.reshape(bd, KV_HEADS, HEAD_DIM), GROUP, axis=1)
    scores, s_new = _sample_scores(page_table, qi16, wb, ki_new, jnp.swapaxes(cache_idx_k[0], 1, 2))
    assert cache_k.shape[1] * PAGE < 2 ** 24
    topk_s = min(TOPK_MAX, (n_pages * PAGE + 1) // 4)
    sel = _sample_select(scores, s_new, topk_s)
    key_rows, n_sel = _sample_compact(page_table, scores, sel, topk_s)
    idx = (key_rows[:, :, :1].astype(I32) * KV_HEADS + jnp.arange(KV_HEADS, dtype=I32)).reshape(-1)
    k_sel, v_sel = _gather_rows(cache_k.reshape(-1, HEAD_DIM), cache_v.reshape(-1, HEAD_DIM), idx)
    att_s = _sample_attend(q8, az8, k_new8, v_new8, s_new, sel, n_sel,
                           k_sel.reshape(bd, topk_s * KV_HEADS, HEAD_DIM),
                           v_sel.reshape(bd, topk_s * KV_HEADS, HEAD_DIM), n_pages * PAGE)
    ml_s, c_s, n_s, m_s = _mlstm_step(
        ps_main.reshape(bd, 1, MAIN_W), misc_s.reshape(bd, 1, LANES), b_igate[0], b_fgate[0], ml_norm_w[0],
        state_C[0], state_n[0].reshape(bd, ML_HEADS, 1, ML_QK),
        jnp.broadcast_to(state_m[0][:, :, None, None], (bd, ML_HEADS, 1, LANES)))
    y_sample = _out_project(xs, att_s.reshape(bd, ATT_WIDTH).astype(BF16), ml_s.reshape(bd, ML_WIDTH),
                            w_o_att, w_o_ml, final_norm_w, bd, "out_sample").reshape(bd, 1, d)
    k_sample = ks_rows.reshape(1, bd, 1, KV_HEADS, HEAD_DIM)
    v_sample = vs_rows.reshape(1, bd, 1, KV_HEADS, HEAD_DIM)
    ik_sample = misc_s[:, :IDX_DIM].reshape(1, bd, 1, IDX_DIM)

    return (y_prompt, y_sample, k_prompt, v_prompt, ik_prompt, c_prompt, n_prompt, m_prompt,
            k_sample, v_sample, ik_sample, c_s[None], n_s.reshape(1, bd, ML_HEADS, ML_QK), m_s[:, :, 0, 0][None])
```

```python
import functools

import jax
import jax.numpy as jnp
import numpy as np
from jax import lax
from jax.experimental import pallas as pl
from jax.experimental.pallas import tpu as pltpu
from jax.experimental.pallas import tpu_sc as plsc

F32 = jnp.float32
BF16 = jnp.bfloat16
I32 = jnp.int32

D_MODEL = 2048
PAGE = 128
ATT_HEADS = 8
KV_HEADS = 2
HEAD_DIM = 128
GROUP = ATT_HEADS // KV_HEADS
ATT_WIDTH = ATT_HEADS * HEAD_DIM
KV_WIDTH = KV_HEADS * HEAD_DIM
ATT_SCALE = HEAD_DIM ** -0.5
Q_SCALE = ATT_SCALE * float(np.log2(np.e))
IDX_HEADS = 16
IDX_DIM = 64
IDX_SCALE = (IDX_HEADS * IDX_DIM) ** -0.5
TOPK_MAX = 256
ML_HEADS = 4
ML_QK = 128
ML_V = 256
ML_WIDTH = ML_HEADS * ML_V
RMS_EPS = 1e-6
IN_SIZES = (ATT_WIDTH, KV_WIDTH, KV_WIDTH, IDX_HEADS * IDX_DIM, IDX_DIM, IDX_HEADS, ATT_WIDTH,
            ML_HEADS * ML_QK, ML_HEADS * ML_QK, ML_WIDTH, ML_HEADS, ML_HEADS, ML_WIDTH, ML_WIDTH)

LANES = 128
NEG_BIG = -1e30
VMEM_LIMIT = 56 * 1024 * 1024

MAIN_W = 7168
SMALL_W = 640
MISC_WI = IDX_DIM
MISC_IG = IDX_DIM + IDX_HEADS
MISC_FG = MISC_IG + ML_HEADS


def _dot(a, b):
    return jnp.dot(a, b, preferred_element_type=F32)


def _dot_nt(a, b):
    return lax.dot_general(a, b, (((1,), (1,)), ((), ())), preferred_element_type=F32)


def _tree_reduce(op, parts):
    parts = list(parts)
    while len(parts) > 1:
        paired = [op(parts[j], parts[j + 1]) for j in range(0, len(parts) - 1, 2)]
        parts = paired + parts[len(parts) - len(parts) % 2:]
    return parts[0]


def _fold_rows(op, x):
    return _tree_reduce(op, [x[r:r + 8] for r in range(0, x.shape[0], 8)])


def _proj_small_kernel(x_ref, nw_ref, w_ref, k_ref, v_ref, misc_ref):
    x = x_ref[...]
    ms = jnp.mean(x * x, axis=-1, keepdims=True)
    h = (x * lax.rsqrt(ms + RMS_EPS) * nw_ref[...]).astype(BF16)
    res = _dot_nt(h, w_ref[...].astype(BF16))
    tm = x.shape[0]
    for g in range(KV_HEADS):
        k_ref[pl.ds(g, tm, stride=KV_HEADS), :] = res[:, g * HEAD_DIM:(g + 1) * HEAD_DIM]
        v_ref[pl.ds(g, tm, stride=KV_HEADS), :] = res[:, KV_WIDTH + g * HEAD_DIM:KV_WIDTH + (g + 1) * HEAD_DIM]
    misc_ref[...] = res[:, 2 * KV_WIDTH:]


def _project_small(x2d, norm_w, w_small, tm, name):
    m, d = x2d.shape
    kv = jax.ShapeDtypeStruct((m * KV_HEADS, HEAD_DIM), F32)
    kv_spec = pl.BlockSpec((tm * KV_HEADS, HEAD_DIM), lambda i: (i, 0))
    return pl.pallas_call(
        _proj_small_kernel,
        out_shape=(kv, kv, jax.ShapeDtypeStruct((m, LANES), F32)),
        grid=(m // tm,),
        in_specs=[pl.BlockSpec((tm, d), lambda i: (i, 0)),
                  pl.BlockSpec((1, d), lambda i: (0, 0)),
                  pl.BlockSpec((SMALL_W, d), lambda i: (0, 0))],
        out_specs=(kv_spec, kv_spec, pl.BlockSpec((tm, LANES), lambda i: (i, 0))),
        compiler_params=pltpu.CompilerParams(dimension_semantics=("parallel",), vmem_limit_bytes=VMEM_LIMIT),
        name=name,
    )(x2d, norm_w.reshape(1, d), w_small)


MAIN_TN = 512
ROW_ALIGN = 8


def _main_tiles():
    offs = np.cumsum((0,) + IN_SIZES)
    aq, iq, az, mq, mk, mv, mo, mz = (int(offs[j]) for j in (0, 3, 6, 7, 8, 9, 12, 13))
    segments = [(aq, ATT_WIDTH, Q_SCALE), (iq, IDX_HEADS * IDX_DIM, 1.0), (az, ATT_WIDTH, 1.0),
                (mv, ML_WIDTH, 1.0), (mo, ML_WIDTH, 1.0), (mz, ML_WIDTH, 1.0),
                (mq, ML_HEADS * ML_QK, 1.0), (mk, ML_HEADS * ML_QK, ML_QK ** -0.5)]
    rows, scales = [], []
    for start, width, scale in segments:
        assert start % ROW_ALIGN == 0 and width % MAIN_TN == 0
        for r in range(start, start + width, MAIN_TN):
            rows.append(r // ROW_ALIGN)
            scales.append(scale)
    assert len(rows) * MAIN_TN == MAIN_W
    return np.asarray(rows, np.int32), np.asarray(scales, np.float32)


def _proj_main_kernel(rows_ref, scale_ref, x_ref, nw_ref, w_ref, o_ref, h_scr):
    j = pl.program_id(1)

    @pl.when(j == 0)
    def _():
        x = x_ref[...]
        ms = jnp.mean(x * x, axis=-1, keepdims=True)
        h_scr[...] = (x * lax.rsqrt(ms + RMS_EPS) * nw_ref[...]).astype(BF16)

    w = (w_ref[...] * scale_ref[j]).astype(BF16)
    o_ref[...] = _dot_nt(h_scr[...], w).astype(o_ref.dtype)


def _project_main(x2d, norm_w, w_t, out_dtype, tm, name):
    m, d = x2d.shape
    rows, scales = _main_tiles()
    return pl.pallas_call(
        _proj_main_kernel,
        out_shape=jax.ShapeDtypeStruct((m, MAIN_W), out_dtype),
        grid_spec=pltpu.PrefetchScalarGridSpec(
            num_scalar_prefetch=1,
            grid=(m // tm, len(rows)),
            in_specs=[pl.BlockSpec(memory_space=pltpu.SMEM),
                      pl.BlockSpec((tm, d), lambda i, j, rows: (i, 0)),
                      pl.BlockSpec((1, d), lambda i, j, rows: (0, 0)),
                      pl.BlockSpec((pl.Element(MAIN_TN), pl.Element(d)), lambda i, j, rows: (rows[j] * ROW_ALIGN, 0))],
            out_specs=pl.BlockSpec((tm, MAIN_TN), lambda i, j, rows: (i, j)),
            scratch_shapes=[pltpu.VMEM((tm, d), BF16)]),
        compiler_params=pltpu.CompilerParams(dimension_semantics=("parallel", "arbitrary"),
                                             vmem_limit_bytes=VMEM_LIMIT),
        name=name,
    )(jnp.asarray(rows), jnp.asarray(scales), x2d, norm_w.reshape(1, d), w_t)


def _out_kernel(x_ref, a_ref, m_ref, w_ref, fw_ref, o_ref, w_scr):
    @pl.when(pl.program_id(0) == 0)
    def _():
        w_scr[...] = w_ref[...].astype(BF16)

    y = (x_ref[...] + _dot(a_ref[...], w_scr[:ATT_WIDTH, :]) + _dot(m_ref[...], w_scr[ATT_WIDTH:, :]))
    ms = jnp.mean(y * y, axis=-1, keepdims=True)
    o_ref[...] = y * lax.rsqrt(ms + RMS_EPS) * fw_ref[...]


def _out_project(x2d, a, mo, w_out, final_w, tm, name):
    m, d = x2d.shape
    return pl.pallas_call(
        _out_kernel,
        out_shape=jax.ShapeDtypeStruct((m, d), F32),
        grid=(m // tm,),
        in_specs=[pl.BlockSpec((tm, d), lambda i: (i, 0)),
                  pl.BlockSpec((tm, ATT_WIDTH), lambda i: (i, 0)),
                  pl.BlockSpec((tm, ML_WIDTH), lambda i: (i, 0)),
                  pl.BlockSpec((ATT_WIDTH + ML_WIDTH, d), lambda i: (0, 0)),
                  pl.BlockSpec((1, d), lambda i: (0, 0))],
        out_specs=pl.BlockSpec((tm, d), lambda i: (i, 0)),
        scratch_shapes=[pltpu.VMEM((ATT_WIDTH + ML_WIDTH, d), BF16)],
        compiler_params=pltpu.CompilerParams(dimension_semantics=("arbitrary",),
                                             vmem_limit_bytes=VMEM_LIMIT),
        name=name,
    )(x2d, a, mo, w_out, final_w.reshape(1, d))


def _log_sigmoid(x):
    return jnp.minimum(x, 0.0) - jnp.log(1.0 + jnp.exp(-jnp.abs(x)))


def _sigmoid(x):
    return 0.5 * jnp.tanh(0.5 * x) + 0.5


def _silu(x):
    return x * _sigmoid(x)


ML_CHUNK = 256
ML_AUG = ML_V + LANES


def _mlstm_kernel(bi_ref, bf_ref, q_ref, k_ref, v_ref, mo_ref, mz_ref, misc_ref, nw_ref,
                  out_ref, ct_ref, m_ref):
    L = ML_CHUNK

    @pl.when(pl.program_id(1) == 0)
    def _():
        ct_ref[...] = jnp.zeros_like(ct_ref)
        m_ref[...] = jnp.zeros_like(m_ref)

    lane = lax.broadcasted_iota(I32, (1, LANES), 1)
    bias = jnp.zeros((1, LANES), F32)
    for h in range(ML_HEADS):
        bias = jnp.where(lane == MISC_IG + h, bi_ref[h], jnp.where(lane == MISC_FG + h, bf_ref[h], bias))
    pre = misc_ref[...] + bias
    gates = jnp.where(jnp.logical_and(lane >= MISC_FG, lane < MISC_FG + ML_HEADS), _log_sigmoid(pre), pre)
    gates_t = gates.T
    t_idx = lax.broadcasted_iota(I32, (L, L), 0)
    s_idx = lax.broadcasted_iota(I32, (L, L), 1)
    causal = s_idx <= t_idx
    ones_col = jnp.where(lax.broadcasted_iota(I32, (L, LANES), 1) == 0, 1.0, 0.0).astype(BF16)

    for h in range(ML_HEADS):
        ig_row = gates_t[MISC_IG + h:MISC_IG + h + 1, :]
        lf_row = gates_t[MISC_FG + h:MISC_FG + h + 1, :]
        lf_col = gates[:, MISC_FG + h:MISC_FG + h + 1]
        b_col = jnp.sum(jnp.where(causal, lf_row, 0.0), axis=1, keepdims=True)
        b_row = jnp.sum(jnp.where(t_idx <= s_idx, lf_col, 0.0), axis=0, keepdims=True)
        m_prev = m_ref[0, h][0:1, 0:1]
        log_d = jnp.where(causal, b_col - b_row + ig_row, -jnp.inf)
        log_a = b_col + m_prev
        m_t = jnp.maximum(log_a, jnp.max(log_d, axis=1, keepdims=True))
        d = jnp.exp(log_d - m_t)
        a = jnp.exp(log_a - m_t)

        q = q_ref[:, h * ML_QK:(h + 1) * ML_QK]
        k = k_ref[:, h * ML_QK:(h + 1) * ML_QK]
        v_aug = jnp.concatenate([v_ref[:, h * ML_V:(h + 1) * ML_V], ones_col], axis=1)
        s = (_dot_nt(q, k) * d).astype(BF16)
        ct = ct_ref[0, h]
        num_aug = a * _dot(q, ct.astype(BF16)) + _dot(s, v_aug)
        den = num_aug[:, ML_V:ML_V + 1]
        hh = num_aug[:, :ML_V] / jnp.maximum(jnp.abs(den), jnp.exp(-m_t))

        m_new = m_t[L - 1:L, :]
        a_end = a[L - 1:L, :]
        w_row = jnp.exp(b_row[:, L - 1:L] - b_row + ig_row - m_new)
        ktw = (k.astype(F32).T * w_row).astype(BF16)
        ct_new = a_end * ct + _dot(ktw, v_aug)
        ct_ref[0, h] = ct_new
        m_ref[0, h] = jnp.broadcast_to(m_new, (8, LANES))

        ms = jnp.mean(hh * hh, axis=1, keepdims=True)
        hn = hh * lax.rsqrt(ms + RMS_EPS) * nw_ref[:, h * ML_V:(h + 1) * ML_V]
        gate = _sigmoid(mo_ref[:, h * ML_V:(h + 1) * ML_V].astype(F32)) * _silu(mz_ref[:, h * ML_V:(h + 1) * ML_V].astype(F32))
        out_ref[:, h * ML_V:(h + 1) * ML_V] = (hn * gate).astype(out_ref.dtype)


def _mlstm_prompt(p_main, misc, b_i, b_f, ml_norm_w, batch, seq):
    L = ML_CHUNK
    nc = seq // L
    row = lambda b, c: b * nc + c
    return pl.pallas_call(
        _mlstm_kernel,
        out_shape=(jax.ShapeDtypeStruct((batch * seq, ML_WIDTH), BF16),
                   jax.ShapeDtypeStruct((batch, ML_HEADS, ML_QK, ML_AUG), F32),
                   jax.ShapeDtypeStruct((batch, ML_HEADS, 8, LANES), F32)),
        grid=(batch, nc),
        in_specs=[pl.BlockSpec(memory_space=pltpu.SMEM),
                  pl.BlockSpec(memory_space=pltpu.SMEM),
                  pl.BlockSpec((L, 512), lambda b, c: (row(b, c), 12)),
                  pl.BlockSpec((L, 512), lambda b, c: (row(b, c), 13)),
                  pl.BlockSpec((L, 1024), lambda b, c: (row(b, c), 3)),
                  pl.BlockSpec((L, 1024), lambda b, c: (row(b, c), 4)),
                  pl.BlockSpec((L, 1024), lambda b, c: (row(b, c), 5)),
                  pl.BlockSpec((L, LANES), lambda b, c: (row(b, c), 0)),
                  pl.BlockSpec((1, ML_WIDTH), lambda b, c: (0, 0))],
        out_specs=(pl.BlockSpec((L, ML_WIDTH), lambda b, c: (row(b, c), 0)),
                   pl.BlockSpec((1, ML_HEADS, ML_QK, ML_AUG), lambda b, c: (b, 0, 0, 0)),
                   pl.BlockSpec((1, ML_HEADS, 8, LANES), lambda b, c: (b, 0, 0, 0))),
        compiler_params=pltpu.CompilerParams(dimension_semantics=("parallel", "arbitrary"),
                                             vmem_limit_bytes=VMEM_LIMIT),
        name="mlstm_prompt",
    )(b_i, b_f, p_main, p_main, p_main, p_main, p_main, misc, ml_norm_w.reshape(1, ML_WIDTH))


INT_MIN = -2 ** 31


def _float_to_key(x):
    bits = lax.bitcast_convert_type(x, I32)
    return jnp.where(bits >= 0, bits, bits ^ jnp.int32(0x7FFFFFFF))


def _key_to_float(key):
    bits = jnp.where(key >= 0, key, key ^ jnp.int32(0x7FFFFFFF))
    return lax.bitcast_convert_type(bits, F32)


KEY_NEG_INF = INT_MIN + 0x7FFFFF


def _kth_largest_key(count_ge, shape, k):
    def bit_body(it, prefix):
        cand = prefix + lax.shift_left(jnp.int32(1), 31 - it)
        cand_f = _key_to_float(jnp.maximum(cand, KEY_NEG_INF))
        return jnp.where(count_ge(cand_f) >= float(k), cand, prefix)

    return lax.fori_loop(0, 32, bit_body, jnp.full(shape, INT_MIN, I32))


def _tie_cutoff(count_tie_le, need, shape, index_bits):
    def bit_body(it, lo):
        cand = lo + lax.shift_left(jnp.int32(1), index_bits - 1 - it).astype(F32)
        return jnp.where(count_tie_le(cand) < need, cand, lo)

    return lax.fori_loop(0, index_bits, bit_body, jnp.full(shape, -1.0, F32)) + 1.0


DSA_QB = 128
DSA_TK = 512
DSA_TS = 256
DSA_TA = 256
HEAD_PAIR = 2


def _dsa_kernel(q_ref, qi_ref, az_ref, miscq_ref, k_ref, v_ref, misck_ref, out_ref,
                k_bf, vt_scr, ki_lo, ki_hi, sc_scr, acc_scr, j_scr, *head_scr, topk):
    i = pl.program_id(1)
    seq = misck_ref.shape[0]
    QB, TK, TS, TA = DSA_QB, DSA_TK, DSA_TS, DSA_TA
    nt = (i * QB + QB + TK - 1) // TK

    @pl.when(i == 0)
    def _():
        for g in range(KV_HEADS):
            k_bf[:, g * HEAD_DIM:(g + 1) * HEAD_DIM] = k_ref[pl.ds(g, seq, stride=KV_HEADS), :].astype(BF16)

        def v_tile(a, carry):
            for g in range(KV_HEADS):
                rows = pl.ds(pl.multiple_of(a * (TA * KV_HEADS), TA * KV_HEADS) + g, TA, stride=KV_HEADS)
                vt_scr[a, g * HEAD_DIM:(g + 1) * HEAD_DIM, :] = v_ref[rows, :].T.astype(BF16)
            return carry

        lax.fori_loop(0, seq // TA, v_tile, 0)
        lane = lax.broadcasted_iota(I32, (seq, LANES), 1)
        lo = jnp.where(lane < IDX_DIM, misck_ref[...], 0.0)
        ki_lo[...] = lo.astype(BF16)
        ki_hi[...] = pltpu.roll(lo, IDX_DIM, axis=1).astype(BF16)

    k_iota = lax.broadcasted_iota(I32, (TK, QB), 0)
    q_pos = i * QB + lax.broadcasted_iota(I32, (TK, QB), 1)
    w_t = miscq_ref[...].T

    def tile_rows(t):
        return pl.ds(pl.multiple_of(t * TK, TK), TK)

    def score_tile(t, carry):
        rows = pl.ds(pl.multiple_of(t * TS, TS), TS)
        klo = ki_lo[rows, :]
        khi = ki_hi[rows, :]
        acc = jnp.zeros((TS, QB), F32)
        for p in range(IDX_HEADS // 2):
            qp = qi_ref[:, p * LANES:(p + 1) * LANES]
            w0 = w_t[MISC_WI + 2 * p:MISC_WI + 2 * p + 1, :]
            w1 = w_t[MISC_WI + 2 * p + 1:MISC_WI + 2 * p + 2, :]
            acc = acc + w0 * jnp.maximum(_dot_nt(klo, qp), 0.0) + w1 * jnp.maximum(_dot_nt(khi, qp), 0.0)
        k_pos = t * TS + lax.broadcasted_iota(I32, (TS, QB), 0)
        q_pos_s = i * QB + lax.broadcasted_iota(I32, (TS, QB), 1)
        score = jnp.where(k_pos <= q_pos_s, acc * IDX_SCALE + 0.0, -jnp.inf)
        sc_scr[rows, :] = _float_to_key(score)
        return carry

    def score_tiles(t, carry):
        for sub in range(TK // TS):
            score_tile(t * (TK // TS) + sub, carry)
        return carry

    lax.fori_loop(0, nt, score_tiles, 0)

    def count_key_ge(cand):
        def body(t, cnt):
            return cnt + _fold_rows(jnp.add, jnp.where(sc_scr[tile_rows(t), :] >= cand, 1, 0))

        cnt = lax.fori_loop(0, nt, body, jnp.zeros((8, QB), I32))
        return jnp.sum(cnt, axis=0, keepdims=True)

    def bit_body(it, carry):
        prefix, cnt_at = carry
        cand = prefix + lax.shift_left(jnp.int32(1), 31 - it)
        cnt = count_key_ge(cand)
        ok = cnt >= topk
        return jnp.where(ok, cand, prefix), jnp.where(ok, cnt, cnt_at)

    thr, cnt_thr = lax.fori_loop(0, 32, bit_body, (jnp.full((1, QB), INT_MIN, I32), jnp.full((1, QB), TK, I32) * nt))

    j_scr[...] = jnp.full((1, QB), float(seq), F32)

    @pl.when(jnp.max(cnt_thr) > topk)
    def _():
        def count_keys(indicator):
            def body(t, cnt):
                k_pos = (t * TK + k_iota).astype(F32)
                return cnt + _fold_rows(jnp.add, indicator(sc_scr[tile_rows(t), :], k_pos))

            return jnp.sum(lax.fori_loop(0, nt, body, jnp.zeros((8, QB), F32)), axis=0, keepdims=True)

        need = float(topk) - count_keys(lambda key, k_pos: jnp.where(key > thr, 1.0, 0.0))
        count_tie_le = lambda cut: count_keys(
            lambda key, k_pos: jnp.where(key == thr, jnp.where(k_pos <= cut, 1.0, 0.0), 0.0))
        j_scr[...] = _tie_cutoff(count_tie_le, need, (1, QB), int(np.log2(seq)))

    cut = j_scr[...]

    def bias_tile(t, carry):
        key = sc_scr[tile_rows(t), :]
        k_pos = t * TK + k_iota
        taken = jnp.where(key > thr, 0.0,
                          jnp.where(key == thr, jnp.where(k_pos.astype(F32) <= cut, 0.0, NEG_BIG), NEG_BIG))
        sc_scr[tile_rows(t), :] = lax.bitcast_convert_type(jnp.where(k_pos <= q_pos, taken, NEG_BIG), I32)
        return carry

    lax.fori_loop(0, nt, bias_tile, 0)

    assert TK == 2 * TA and GROUP % HEAD_PAIR == 0
    az = az_ref[...].astype(F32)
    n_pairs = ATT_HEADS // HEAD_PAIR
    PQ = HEAD_PAIR * QB
    lg_scr, p_scr = head_scr[:n_pairs], head_scr[n_pairs:]
    acc_scr[...] = jnp.zeros_like(acc_scr)
    last = 2 * nt - 1

    def sub_rows(a):
        return pl.ds(pl.multiple_of(a * TA, TA), TA)

    def kv_cols(j):
        g = (j * HEAD_PAIR) // GROUP
        return slice(g * HEAD_DIM, (g + 1) * HEAD_DIM)

    def logits_stage(j, a, slot):
        q_pair = jnp.concatenate([q_ref[:, h * HEAD_DIM:(h + 1) * HEAD_DIM]
                                  for h in range(j * HEAD_PAIR, (j + 1) * HEAD_PAIR)], axis=0)
        lg_scr[j][slot] = _dot_nt(k_bf[sub_rows(a), kv_cols(j)], q_pair)

    CH = 64

    def softmax_stage(j, a, slot, state):
        m, l8, _ = state
        maxes = []
        for r in range(0, TA, CH):
            bias = lax.bitcast_convert_type(sc_scr[pl.ds(pl.multiple_of(a * TA, TA) + r, CH), :], F32)
            lg = lg_scr[j][slot, r:r + CH, :] + jnp.concatenate([bias] * HEAD_PAIR, axis=1)
            lg_scr[j][slot, r:r + CH, :] = lg
            maxes.append(_fold_rows(jnp.maximum, lg))
        m_new = jnp.maximum(m, jnp.max(_tree_reduce(jnp.maximum, maxes), axis=0, keepdims=True))
        alpha = jnp.exp2(m - m_new)
        sums = []
        for r in range(0, TA, CH):
            p = jnp.exp2(lg_scr[j][slot, r:r + CH, :] - m_new)
            p_scr[j][slot, r:r + CH, :] = p.astype(BF16)
            sums.append(_fold_rows(jnp.add, p))
        return m_new, alpha * l8 + _tree_reduce(jnp.add, sums), alpha

    def value_stage(j, a, slot, state):
        vt = vt_scr[a][kv_cols(j), :]
        acc_scr[j] = state[2] * acc_scr[j] + _dot(vt, p_scr[j][slot])

    def att_tile(t, state):
        for slot in range(2):
            a = 2 * t + slot
            out = []
            for j in range(n_pairs):
                value_stage(j, jnp.maximum(a - 1, 0), 1 - slot, state[j])
                out.append(softmax_stage(j, a, slot, state[j]))
                logits_stage(j, jnp.minimum(a + 1, last), 1 - slot)
            state = tuple(out)
        return state

    init = tuple((jnp.full((1, PQ), NEG_BIG, F32), jnp.zeros((8, PQ), F32), jnp.ones((1, PQ), F32))
                 for _ in range(n_pairs))
    for j in range(n_pairs):
        p_scr[j][1] = jnp.zeros((TA, PQ), BF16)
        logits_stage(j, 0, 0)
    fin = lax.fori_loop(0, nt, att_tile, init)
    for j in range(n_pairs):
        value_stage(j, last, 1, fin[j])
    for j in range(n_pairs):
        o_t = acc_scr[j] / jnp.sum(fin[j][1], axis=0, keepdims=True)
        for hh in range(HEAD_PAIR):
            c0 = (j * HEAD_PAIR + hh) * HEAD_DIM
            o = o_t[:, hh * QB:(hh + 1) * QB].T
            out_ref[:, c0:c0 + HEAD_DIM] = (o * _silu(az[:, c0:c0 + HEAD_DIM])).astype(out_ref.dtype)


def _dsa_prompt(p_main, k_rows, v_rows, misc, batch, seq):
    QB, TK = DSA_QB, DSA_TK
    nb = seq // QB
    topk = min(TOPK_MAX, seq // 4)
    TA = DSA_TA
    assert TK >= topk and seq % TK == 0 and seq % QB == 0 and TK % DSA_TS == 0
    qrow = lambda b, i: b * nb + i
    return pl.pallas_call(
        functools.partial(_dsa_kernel, topk=topk),
        out_shape=jax.ShapeDtypeStruct((batch * seq, ATT_WIDTH), BF16),
        grid=(batch, nb),
        in_specs=[pl.BlockSpec((QB, ATT_WIDTH), lambda b, i: (qrow(b, i), 0)),
                  pl.BlockSpec((QB, 1024), lambda b, i: (qrow(b, i), 1)),
                  pl.BlockSpec((QB, ATT_WIDTH), lambda b, i: (qrow(b, i), 2)),
                  pl.BlockSpec((QB, LANES), lambda b, i: (qrow(b, i), 0)),
                  pl.BlockSpec((seq * KV_HEADS, HEAD_DIM), lambda b, i: (b, 0)),
                  pl.BlockSpec((seq * KV_HEADS, HEAD_DIM), lambda b, i: (b, 0)),
                  pl.BlockSpec((seq, LANES), lambda b, i: (b, 0))],
        out_specs=pl.BlockSpec((QB, ATT_WIDTH), lambda b, i: (qrow(b, i), 0)),
        scratch_shapes=[pltpu.VMEM((seq, KV_WIDTH), BF16),
                        pltpu.VMEM((seq // TA, KV_WIDTH, TA), BF16),
                        pltpu.VMEM((seq, LANES), BF16),
                        pltpu.VMEM((seq, LANES), BF16),
                        pltpu.VMEM((seq, QB), I32),
                        pltpu.VMEM((ATT_HEADS // HEAD_PAIR, HEAD_DIM, HEAD_PAIR * QB), F32),
                        pltpu.VMEM((1, QB), F32)]
                       + [pltpu.VMEM((2, TA, HEAD_PAIR * QB), F32)] * (ATT_HEADS // HEAD_PAIR)
                       + [pltpu.VMEM((2, TA, HEAD_PAIR * QB), BF16)] * (ATT_HEADS // HEAD_PAIR),
        compiler_params=pltpu.CompilerParams(dimension_semantics=("parallel", "arbitrary"),
                                             vmem_limit_bytes=VMEM_LIMIT),
        name="dsa_prompt",
    )(p_main, p_main, p_main, misc, k_rows, v_rows, misc)


SCORE_ROWS = 128


def _sample_scores_kernel(pt_ref, qi_ref, wb_ref, kin_ref, cache_ref, sc_ref, snew_ref, page_buf, sems):
    b = pl.program_id(0)
    n_rows = pl.num_programs(0)
    n_pages = sc_ref.shape[1]
    slot = lax.rem(b, 2)

    def page_copy(row, p, to_slot):
        return pltpu.make_async_copy(cache_ref.at[pt_ref[row, p]], page_buf.at[to_slot, p], sems.at[to_slot])

    def start_fetch(row, to_slot):
        def body(p, carry):
            page_copy(row, p, to_slot).start()
            return carry

        lax.fori_loop(0, n_pages, body, 0, unroll=8)

    @pl.when(b == 0)
    def _():
        start_fetch(0, 0)

    @pl.when(b + 1 < n_rows)
    def _():
        start_fetch(b + 1, 1 - slot)

    def wait_page(p, carry):
        page_copy(b, p, slot).wait()
        return carry

    lax.fori_loop(0, n_pages, wait_page, 0, unroll=8)

    qi = qi_ref[0]
    wb = wb_ref[0]

    def score_rows(g, carry):
        rows = []
        for j in range(SCORE_ROWS):
            page = page_buf[slot, g * SCORE_ROWS + j]
            s = _dot(qi, page.astype(BF16))
            rows.append(jnp.sum(jnp.maximum(s, 0.0) * wb, axis=0, keepdims=True) * IDX_SCALE)
        sc_ref[0, pl.ds(pl.multiple_of(g * SCORE_ROWS, SCORE_ROWS), SCORE_ROWS), :] = jnp.concatenate(rows, axis=0)
        return carry

    lax.fori_loop(0, n_pages // SCORE_ROWS, score_rows, 0)

    ki_new = kin_ref[0].astype(BF16).astype(F32)
    s = jnp.sum(qi.astype(F32) * ki_new, axis=1, keepdims=True)
    s_new = jnp.sum(jnp.maximum(s, 0.0) * wb[:, 0:1], axis=0, keepdims=True) * IDX_SCALE
    snew_ref[0] = jnp.broadcast_to(s_new, (8, LANES))


def _sample_scores(page_table, qi16, wb, ki_new, cache_ik_t):
    bd, n_pages = page_table.shape
    assert n_pages % SCORE_ROWS == 0
    bmap = lambda b, pt: (b, 0, 0)
    return pl.pallas_call(
        _sample_scores_kernel,
        out_shape=(jax.ShapeDtypeStruct((bd, n_pages, PAGE), F32),
                   jax.ShapeDtypeStruct((bd, 8, LANES), F32)),
        grid_spec=pltpu.PrefetchScalarGridSpec(
            num_scalar_prefetch=1,
            grid=(bd,),
            in_specs=[pl.BlockSpec((1, IDX_HEADS, IDX_DIM), bmap),
                      pl.BlockSpec((1, IDX_HEADS, LANES), bmap),
                      pl.BlockSpec((1, 1, IDX_DIM), bmap),
                      pl.BlockSpec(memory_space=pl.ANY)],
            out_specs=(pl.BlockSpec((1, n_pages, PAGE), bmap),
                       pl.BlockSpec((1, 8, LANES), bmap)),
            scratch_shapes=[pltpu.VMEM((2, n_pages, IDX_DIM, PAGE), F32),
                            pltpu.SemaphoreType.DMA((2,))]),
        compiler_params=pltpu.CompilerParams(dimension_semantics=("arbitrary",),
                                             vmem_limit_bytes=VMEM_LIMIT),
        name="sample_scores",
    )(page_table, qi16, wb, ki_new, cache_ik_t)


SEL_THR, SEL_NEXT, SEL_CUT = 0, 1, 2


def _sample_select_kernel(sc_ref, snew_ref, sel_ref, *, topk):
    x = sc_ref[...]
    bd, n_pages, _ = x.shape
    past = n_pages * PAGE
    s_new = snew_ref[:, 0:1, 0:1]

    def total(v):
        return jnp.sum(jnp.sum(v, axis=1, keepdims=True), axis=2, keepdims=True)

    def count_ge(cand):
        return total(jnp.where(x >= cand, 1.0, 0.0)) + jnp.where(s_new >= cand, 1.0, 0.0)

    key = _kth_largest_key(count_ge, (bd, 1, 1), topk)
    thr = _key_to_float(key)
    thr_next = _key_to_float(key + 1)
    need = float(topk) - count_ge(thr_next)
    pos = (lax.broadcasted_iota(I32, (1, n_pages, PAGE), 1) * PAGE
           + lax.broadcasted_iota(I32, (1, n_pages, PAGE), 2)).astype(F32)

    def count_tie_le(cut):
        tie = jnp.where(x >= thr_next, 0.0, jnp.where(x >= thr, jnp.where(pos <= cut, 1.0, 0.0), 0.0))
        tie_new = jnp.where(s_new >= thr_next, 0.0, jnp.where(s_new >= thr, jnp.where(float(past) <= cut, 1.0, 0.0), 0.0))
        return total(tie) + tie_new

    row = lax.broadcasted_iota(I32, (bd, 8, LANES), 1)
    sel_ref[...] = jnp.where(row == SEL_THR, thr, jnp.where(row == SEL_NEXT, thr_next, float(past + 1)))

    @pl.when(jnp.max(count_ge(thr)) > float(topk))
    def _():
        cut = _tie_cutoff(count_tie_le, need, (bd, 1, 1), int(np.log2(past)) + 1)
        sel_ref[...] = jnp.where(row == SEL_THR, thr, jnp.where(row == SEL_NEXT, thr_next, cut))


def _sample_select(scores, s_new, topk):
    bd = scores.shape[0]
    return pl.pallas_call(
        functools.partial(_sample_select_kernel, topk=topk),
        out_shape=jax.ShapeDtypeStruct((bd, 8, LANES), F32),
        compiler_params=pltpu.CompilerParams(vmem_limit_bytes=VMEM_LIMIT),
        name="sample_select",
    )(scores, s_new)


def _taken_bias(score, pos, sel):
    thr, thr_next, cut = sel[SEL_THR:SEL_THR + 1, 0:1], sel[SEL_NEXT:SEL_NEXT + 1, 0:1], sel[SEL_CUT:SEL_CUT + 1, 0:1]
    return jnp.where(score >= thr_next, 0.0,
                     jnp.where(score >= thr, jnp.where(pos <= cut, 0.0, NEG_BIG), NEG_BIG))


def _sample_compact_kernel(pt_ref, sc_ref, sel_ref, rows_ref, nsel_ref, *, slots):
    n_pages = sc_ref.shape[1]
    pos = (lax.broadcasted_iota(I32, (n_pages, PAGE), 0) * PAGE
           + lax.broadcasted_iota(I32, (n_pages, PAGE), 1)).astype(F32)
    taken = jnp.where(_taken_bias(sc_ref[0], pos, sel_ref[0]) == 0.0, 1.0, 0.0)
    before = lax.broadcasted_iota(I32, (PAGE, PAGE), 0) < lax.broadcasted_iota(I32, (PAGE, PAGE), 1)
    in_page = _dot(taken, jnp.where(before, 1.0, 0.0))
    taken_t = taken.T
    rank_t = jnp.where(taken_t > 0.0, in_page.T, -1.0)
    page_tot = jnp.sum(taken_t, axis=0, keepdims=True)
    earlier = lax.broadcasted_iota(I32, (n_pages, n_pages), 0) < lax.broadcasted_iota(I32, (n_pages, n_pages), 1)
    first_slot = _dot(jnp.broadcast_to(page_tot, (8, n_pages)),
                      jnp.where(earlier, 1.0, 0.0))[0:1]
    phys_t = (pt_ref[0] * PAGE + lax.broadcasted_iota(I32, (PAGE, n_pages), 0)).astype(F32)
    slot = lax.broadcasted_iota(I32, (slots, n_pages), 0).astype(F32)

    def rank_body(r, acc):
        r = r.astype(F32)
        row = jnp.sum(jnp.where(rank_t == r, phys_t, 0.0), axis=0, keepdims=True)
        target = jnp.where(r < page_tot, first_slot + r, -1.0)
        return acc + jnp.where(slot == target, row, 0.0)

    most = jnp.max(page_tot).astype(I32)
    acc = lax.fori_loop(0, most, rank_body, jnp.zeros((slots, n_pages), F32))
    rows_ref[0] = jnp.broadcast_to(jnp.sum(acc, axis=1, keepdims=True), (slots, LANES))
    nsel_ref[0] = jnp.broadcast_to(jnp.sum(page_tot, axis=1, keepdims=True), (8, LANES))


def _sample_compact(page_table, scores, sel, slots):
    bd, n_pages = page_table.shape
    assert n_pages % LANES == 0
    bmap = lambda b: (b, 0, 0)
    return pl.pallas_call(
        functools.partial(_sample_compact_kernel, slots=slots),
        out_shape=(jax.ShapeDtypeStruct((bd, slots, LANES), F32), jax.ShapeDtypeStruct((bd, 8, LANES), F32)),
        grid=(bd,),
        in_specs=[pl.BlockSpec((1, 1, n_pages), bmap), pl.BlockSpec((1, n_pages, PAGE), bmap),
                  pl.BlockSpec((1, 8, LANES), bmap)],
        out_specs=(pl.BlockSpec((1, slots, LANES), bmap), pl.BlockSpec((1, 8, LANES), bmap)),
        compiler_params=pltpu.CompilerParams(dimension_semantics=("parallel",), vmem_limit_bytes=VMEM_LIMIT),
        name="sample_compact",
    )(page_table.reshape(bd, 1, n_pages), scores, sel)


SC_GATHER_CHUNK = 128


def _gather_rows(table_k, table_v, idx):
    info = plsc.get_sparse_core_info()
    n_workers = info.num_cores * info.num_subcores
    n_idx = idx.shape[0]
    per_worker = n_idx // n_workers
    assert per_worker * n_workers == n_idx and per_worker % SC_GATHER_CHUNK == 0
    mesh = plsc.VectorSubcoreMesh(core_axis_name="c", subcore_axis_name="s")
    out = jax.ShapeDtypeStruct((n_idx, table_k.shape[1]), table_k.dtype)

    @functools.partial(
        pl.kernel, mesh=mesh, out_type=(out, out),
        scratch_types=[pltpu.VMEM((SC_GATHER_CHUNK,), I32),
                       pltpu.VMEM((SC_GATHER_CHUNK, table_k.shape[1]), table_k.dtype),
                       pltpu.VMEM((SC_GATHER_CHUNK, table_v.shape[1]), table_v.dtype),
                       pltpu.SemaphoreType.DMA, pltpu.SemaphoreType.DMA])
    def gather(tk_hbm, tv_hbm, idx_hbm, ok_hbm, ov_hbm, idx_v, rk_v, rv_v, sem_k, sem_v):
        worker = lax.axis_index("s") * info.num_cores + lax.axis_index("c")

        @pl.loop(0, per_worker // SC_GATHER_CHUNK)
        def _(j):
            base = worker * per_worker + j * SC_GATHER_CHUNK
            pltpu.sync_copy(idx_hbm.at[pl.ds(base, SC_GATHER_CHUNK)], idx_v)
            copy_k = pltpu.async_copy(tk_hbm.at[idx_v], rk_v, sem_k)
            copy_v = pltpu.async_copy(tv_hbm.at[idx_v], rv_v, sem_v)
            copy_k.wait()
            copy_v.wait()
            pltpu.sync_copy(rk_v, ok_hbm.at[pl.ds(base, SC_GATHER_CHUNK)])
            pltpu.sync_copy(rv_v, ov_hbm.at[pl.ds(base, SC_GATHER_CHUNK)])

    return gather(table_k, table_v, idx)


def _sample_attend_kernel(q_ref, az_ref, kn_ref, vn_ref, snew_ref, sel_ref, nsel_ref, k_ref, v_ref, out_ref, *, past):
    width = k_ref.shape[1]
    q = q_ref[0]
    col = lax.broadcasted_iota(I32, (ATT_HEADS, width), 1)
    head = lax.broadcasted_iota(I32, (ATT_HEADS, width), 0)
    own_head = col % KV_HEADS == head // GROUP
    filled = (col // KV_HEADS).astype(F32) < nsel_ref[0][0:1, 0:1]
    lg = _dot_nt(q, k_ref[0].astype(BF16))
    lg = jnp.where(own_head, jnp.where(filled, lg, NEG_BIG), NEG_BIG)
    lg_new = (jnp.sum(q.astype(F32) * kn_ref[0].astype(BF16).astype(F32), axis=1, keepdims=True)
              + _taken_bias(snew_ref[0][0:1, 0:1], float(past), sel_ref[0]))
    m = jnp.maximum(jnp.max(lg, axis=1, keepdims=True), lg_new)
    p = jnp.exp2(lg - m)
    p_new = jnp.exp2(lg_new - m)
    l = jnp.sum(p, axis=1, keepdims=True) + p_new
    acc = _dot(p.astype(BF16), v_ref[0].astype(BF16)) + p_new * vn_ref[0].astype(BF16).astype(F32)
    out_ref[0] = (acc / l) * _silu(az_ref[0])


def _sample_attend(q8, az8, k_new8, v_new8, s_new, sel, n_sel, k_sel, v_sel, past):
    bd, width, _ = k_sel.shape
    bmap = lambda b: (b, 0, 0)
    head_tile = pl.BlockSpec((1, ATT_HEADS, HEAD_DIM), bmap)
    par_tile = pl.BlockSpec((1, 8, LANES), bmap)
    rows_tile = pl.BlockSpec((1, width, HEAD_DIM), bmap)
    return pl.pallas_call(
        functools.partial(_sample_attend_kernel, past=past),
        out_shape=jax.ShapeDtypeStruct((bd, ATT_HEADS, HEAD_DIM), F32),
        grid=(bd,),
        in_specs=[head_tile, head_tile, head_tile, head_tile, par_tile, par_tile, par_tile, rows_tile, rows_tile],
        out_specs=head_tile,
        compiler_params=pltpu.CompilerParams(dimension_semantics=("parallel",), vmem_limit_bytes=VMEM_LIMIT),
        name="sample_attend",
    )(q8, az8, k_new8, v_new8, s_new, sel, n_sel, k_sel, v_sel)


def _mlstm_step_kernel(bi_ref, bf_ref, q_ref, k_ref, v_ref, mo_ref, mz_ref, misc_ref, nw_ref,
                       c_ref, n_ref, m_ref, out_ref, c_out, n_out, m_out):
    misc = misc_ref[0]
    eye = lax.broadcasted_iota(I32, (ML_V, ML_V), 0) == lax.broadcasted_iota(I32, (ML_V, ML_V), 1)
    for h in range(ML_HEADS):
        ig = misc[:, MISC_IG + h:MISC_IG + h + 1] + bi_ref[h]
        lf = _log_sigmoid(misc[:, MISC_FG + h:MISC_FG + h + 1] + bf_ref[h])
        m_prev = m_ref[0, h][:, 0:1]
        log_a = lf + m_prev
        m_t = jnp.maximum(log_a, ig)
        d = jnp.exp(ig - m_t)
        a = jnp.exp(log_a - m_t)
        q = q_ref[0][:, h * ML_QK:(h + 1) * ML_QK]
        k = k_ref[0][:, h * ML_QK:(h + 1) * ML_QK]
        v = v_ref[0][:, h * ML_V:(h + 1) * ML_V]
        v_col = jnp.sum(jnp.where(eye, v, 0.0), axis=1, keepdims=True)
        c = c_ref[0, h]
        n = n_ref[0, h]
        s = jnp.sum(q * k, axis=1, keepdims=True) * d
        num = a * jnp.sum(c * q, axis=1, keepdims=True) + s * v_col
        den = a * jnp.sum(n * q, axis=1, keepdims=True) + s
        h_col = num / jnp.maximum(jnp.abs(den), jnp.exp(-m_t))
        c_out[0, h] = a * c + (d * v_col) * k
        n_out[0, h] = a * n + d * k
        m_out[0, h] = jnp.broadcast_to(m_t, (1, LANES))

        h_row = jnp.sum(jnp.where(eye, h_col, 0.0), axis=0, keepdims=True)
        ms = jnp.mean(h_row * h_row, axis=1, keepdims=True)
        hn = h_row * lax.rsqrt(ms + RMS_EPS) * nw_ref[:, h * ML_V:(h + 1) * ML_V]
        gate = _sigmoid(mo_ref[0][:, h * ML_V:(h + 1) * ML_V]) * _silu(mz_ref[0][:, h * ML_V:(h + 1) * ML_V])
        out_ref[0, :, h * ML_V:(h + 1) * ML_V] = (hn * gate).astype(out_ref.dtype)


def _mlstm_step(ps_main, misc, b_i, b_f, ml_norm_w, state_c, state_n, state_m):
    bd = ps_main.shape[0]
    col = lambda j: (lambda b: (b, 0, j))
    st4 = lambda b: (b, 0, 0, 0)
    return pl.pallas_call(
        _mlstm_step_kernel,
        out_shape=(jax.ShapeDtypeStruct((bd, 1, ML_WIDTH), BF16),
                   jax.ShapeDtypeStruct(state_c.shape, F32),
                   jax.ShapeDtypeStruct(state_n.shape, F32),
                   jax.ShapeDtypeStruct(state_m.shape, F32)),
        grid=(bd,),
        in_specs=[pl.BlockSpec(memory_space=pltpu.SMEM),
                  pl.BlockSpec(memory_space=pltpu.SMEM),
                  pl.BlockSpec((1, 1, 512), col(12)),
                  pl.BlockSpec((1, 1, 512), col(13)),
                  pl.BlockSpec((1, 1, 1024), col(3)),
                  pl.BlockSpec((1, 1, 1024), col(4)),
                  pl.BlockSpec((1, 1, 1024), col(5)),
                  pl.BlockSpec((1, 1, LANES), col(0)),
                  pl.BlockSpec((1, ML_WIDTH), lambda b: (0, 0)),
                  pl.BlockSpec((1, ML_HEADS, ML_V, ML_QK), st4),
                  pl.BlockSpec((1, ML_HEADS, 1, ML_QK), st4),
                  pl.BlockSpec((1, ML_HEADS, 1, LANES), st4)],
        out_specs=(pl.BlockSpec((1, 1, ML_WIDTH), lambda b: (b, 0, 0)),
                   pl.BlockSpec((1, ML_HEADS, ML_V, ML_QK), st4),
                   pl.BlockSpec((1, ML_HEADS, 1, ML_QK), st4),
                   pl.BlockSpec((1, ML_HEADS, 1, LANES), st4)),
        compiler_params=pltpu.CompilerParams(dimension_semantics=("parallel",),
                                             vmem_limit_bytes=VMEM_LIMIT),
        name="mlstm_step",
    )(b_i, b_f, ps_main, ps_main, ps_main, ps_main, ps_main, misc, ml_norm_w.reshape(1, ML_WIDTH),
      state_c, state_n, state_m)


def _small_weight(w_t):
    offs = np.cumsum((0,) + IN_SIZES)
    ak, av, ik, iw, mi, mf = (w_t[offs[j]:offs[j + 1]] for j in (1, 2, 4, 5, 10, 11))
    pad = jnp.zeros((LANES - IDX_DIM - IDX_HEADS - 2 * ML_HEADS, w_t.shape[1]), w_t.dtype)
    w_small = jnp.concatenate([ak, av, ik, iw, mi, mf, pad], axis=0)
    assert w_small.shape[0] == SMALL_W
    return w_small


def kernel(x_prompt, x_sample, cache_k, cache_v, cache_idx_k, state_C, state_n, state_m, page_table,
           norm_w, w_in, b_igate, b_fgate, ml_norm_w, w_out, final_norm_w):
    depth = w_in.shape[0]
    batch, seq, d = x_prompt.shape
    bd, dec_seq, _ = x_sample.shape
    assert depth == 1 and dec_seq == 1 and d == D_MODEL
    n_pages = page_table.shape[1]

    w_t = jnp.swapaxes(w_in[0], 0, 1)
    w_small = _small_weight(w_t)

    xp = x_prompt.reshape(batch * seq, d)
    p_main = _project_main(xp, norm_w[0], w_t, BF16, 1024, "proj_main")
    k_rows, v_rows, misc = _project_small(xp, norm_w[0], w_small, 1024, "proj_small")
    att = _dsa_prompt(p_main, k_rows, v_rows, misc, batch, seq)
    ml, ct, m_p = _mlstm_prompt(p_main, misc, b_igate[0], b_fgate[0], ml_norm_w[0], batch, seq)
    y_prompt = _out_project(xp, att, ml, w_out[0], final_norm_w, 256, "out_prompt").reshape(batch, seq, d)
    k_prompt = k_rows.reshape(1, batch, seq, KV_HEADS, HEAD_DIM)
    v_prompt = v_rows.reshape(1, batch, seq, KV_HEADS, HEAD_DIM)
    ik_prompt = misc[:, :IDX_DIM].reshape(1, batch, seq, IDX_DIM)
    c_prompt = jnp.swapaxes(ct[..., :ML_V], -1, -2)[None]
    n_prompt = ct[..., ML_V][None]
    m_prompt = m_p[:, :, 0, 0][None]

    xs = x_sample.reshape(bd, d)
    ps_main = _project_main(xs, norm_w[0], w_t, F32, bd, "proj_main_s")
    ks_rows, vs_rows, misc_s = _project_small(xs, norm_w[0], w_small, bd, "proj_small_s")
    q8 = ps_main[:, :ATT_WIDTH].reshape(bd, ATT_HEADS, HEAD_DIM).astype(BF16)
    qi16 = ps_main[:, 1024:2048].reshape(bd, IDX_HEADS, IDX_DIM).astype(BF16)
    az8 = ps_main[:, 2048:3072].reshape(bd, ATT_HEADS, HEAD_DIM)
    ki_new = misc_s[:, :IDX_DIM].reshape(bd, 1, IDX_DIM)
    wb = jnp.broadcast_to(misc_s[:, MISC_WI:MISC_WI + IDX_HEADS, None], (bd, IDX_HEADS, LANES))
    k_new8 = jnp.repeat(ks_rows.reshape(bd, KV_HEADS, HEAD_DIM), GROUP, axis=1)
    v_new8 = jnp.repeat(vs_rows.reshape(bd, KV_HEADS, HEAD_DIM), GROUP, axis=1)
    scores, s_new = _sample_scores(page_table, qi16, wb, ki_new, jnp.swapaxes(cache_idx_k[0], 1, 2))
    assert cache_k.shape[1] * PAGE < 2 ** 24
    topk_s = min(TOPK_MAX, (n_pages * PAGE + 1) // 4)
    sel = _sample_select(scores, s_new, topk_s)
    key_rows, n_sel = _sample_compact(page_table, scores, sel, topk_s)
    idx = (key_rows[:, :, :1].astype(I32) * KV_HEADS + jnp.arange(KV_HEADS, dtype=I32)).reshape(-1)
    k_sel, v_sel = _gather_rows(cache_k.reshape(-1, HEAD_DIM), cache_v.reshape(-1, HEAD_DIM), idx)
    att_s = _sample_attend(q8, az8, k_new8, v_new8, s_new, sel, n_sel,
                           k_sel.reshape(bd, topk_s * KV_HEADS, HEAD_DIM),
                           v_sel.reshape(bd, topk_s * KV_HEADS, HEAD_DIM), n_pages * PAGE)
    ml_s, c_s, n_s, m_s = _mlstm_step(
        ps_main.reshape(bd, 1, MAIN_W), misc_s.reshape(bd, 1, LANES), b_igate[0], b_fgate[0], ml_norm_w[0],
        state_C[0], state_n[0].reshape(bd, ML_HEADS, 1, ML_QK),
        jnp.broadcast_to(state_m[0][:, :, None, None], (bd, ML_HEADS, 1, LANES)))
    y_sample = _out_project(xs, att_s.reshape(bd, ATT_WIDTH).astype(BF16), ml_s.reshape(bd, ML_WIDTH),
                            w_out[0], final_norm_w, bd, "out_sample").reshape(bd, 1, d)
    k_sample = ks_rows.reshape(1, bd, 1, KV_HEADS, HEAD_DIM)
    v_sample = vs_rows.reshape(1, bd, 1, KV_HEADS, HEAD_DIM)
    ik_sample = misc_s[:, :IDX_DIM].reshape(1, bd, 1, IDX_DIM)

    return (y_prompt, y_sample, k_prompt, v_prompt, ik_prompt, c_prompt, n_prompt, m_prompt,
            k_sample, v_sample, ik_sample, c_s[None], n_s.reshape(1, bd, ML_HEADS, ML_QK), m_s[:, :, 0, 0][None])
```

```python
import functools

import jax
import jax.numpy as jnp
import numpy as np
from jax import lax
from jax.experimental import pallas as pl
from jax.experimental.pallas import tpu as pltpu
from jax.experimental.pallas import tpu_sc as plsc

F32 = jnp.float32
BF16 = jnp.bfloat16
I32 = jnp.int32

D_MODEL = 2048
PAGE = 128
ATT_HEADS = 8
KV_HEADS = 2
HEAD_DIM = 128
GROUP = ATT_HEADS // KV_HEADS
ATT_WIDTH = ATT_HEADS * HEAD_DIM
KV_WIDTH = KV_HEADS * HEAD_DIM
ATT_SCALE = HEAD_DIM ** -0.5
Q_SCALE = ATT_SCALE * float(np.log2(np.e))
IDX_HEADS = 16
IDX_DIM = 64
IDX_SCALE = (IDX_HEADS * IDX_DIM) ** -0.5
TOPK_MAX = 256
ML_HEADS = 4
ML_QK = 128
ML_V = 256
ML_WIDTH = ML_HEADS * ML_V
RMS_EPS = 1e-6
IN_SIZES = (ATT_WIDTH, KV_WIDTH, KV_WIDTH, IDX_HEADS * IDX_DIM, IDX_DIM, IDX_HEADS, ATT_WIDTH,
            ML_HEADS * ML_QK, ML_HEADS * ML_QK, ML_WIDTH, ML_HEADS, ML_HEADS, ML_WIDTH, ML_WIDTH)

LANES = 128
NEG_BIG = -1e30
VMEM_LIMIT = 56 * 1024 * 1024

MAIN_W = 7168
SMALL_W = 640
MISC_WI = IDX_DIM
MISC_IG = IDX_DIM + IDX_HEADS
MISC_FG = MISC_IG + ML_HEADS


def _dot(a, b):
    return jnp.dot(a, b, preferred_element_type=F32)


def _dot_nt(a, b):
    return lax.dot_general(a, b, (((1,), (1,)), ((), ())), preferred_element_type=F32)


def _tree_reduce(op, parts):
    parts = list(parts)
    while len(parts) > 1:
        paired = [op(parts[j], parts[j + 1]) for j in range(0, len(parts) - 1, 2)]
        parts = paired + parts[len(parts) - len(parts) % 2:]
    return parts[0]


def _fold_rows(op, x):
    return _tree_reduce(op, [x[r:r + 8] for r in range(0, x.shape[0], 8)])


def _proj_small_kernel(h_ref, w_ref, k_ref, v_ref, misc_ref):
    res = _dot_nt(h_ref[...], w_ref[...].astype(BF16))
    tm = h_ref.shape[0]
    for g in range(KV_HEADS):
        k_ref[pl.ds(g, tm, stride=KV_HEADS), :] = res[:, g * HEAD_DIM:(g + 1) * HEAD_DIM]
        v_ref[pl.ds(g, tm, stride=KV_HEADS), :] = res[:, KV_WIDTH + g * HEAD_DIM:KV_WIDTH + (g + 1) * HEAD_DIM]
    misc_ref[...] = res[:, 2 * KV_WIDTH:]


def _project_small(h, w_small, tm, name):
    m, d = h.shape
    kv = jax.ShapeDtypeStruct((m * KV_HEADS, HEAD_DIM), F32)
    kv_spec = pl.BlockSpec((tm * KV_HEADS, HEAD_DIM), lambda i: (i, 0))
    return pl.pallas_call(
        _proj_small_kernel,
        out_shape=(kv, kv, jax.ShapeDtypeStruct((m, LANES), F32)),
        grid=(m // tm,),
        in_specs=[pl.BlockSpec((tm, d), lambda i: (i, 0)),
                  pl.BlockSpec((SMALL_W, d), lambda i: (0, 0))],
        out_specs=(kv_spec, kv_spec, pl.BlockSpec((tm, LANES), lambda i: (i, 0))),
        compiler_params=pltpu.CompilerParams(dimension_semantics=("parallel",), vmem_limit_bytes=VMEM_LIMIT),
        name=name,
    )(h, w_small)


MAIN_TN = 512
ROW_ALIGN = 8


def _main_tiles():
    offs = np.cumsum((0,) + IN_SIZES)
    aq, iq, az, mq, mk, mv, mo, mz = (int(offs[j]) for j in (0, 3, 6, 7, 8, 9, 12, 13))
    segments = [(aq, ATT_WIDTH, Q_SCALE), (iq, IDX_HEADS * IDX_DIM, 1.0), (az, ATT_WIDTH, 1.0),
                (mv, ML_WIDTH, 1.0), (mo, ML_WIDTH, 1.0), (mz, ML_WIDTH, 1.0),
                (mq, ML_HEADS * ML_QK, 1.0), (mk, ML_HEADS * ML_QK, ML_QK ** -0.5)]
    rows, scales = [], []
    for start, width, scale in segments:
        assert start % ROW_ALIGN == 0 and width % MAIN_TN == 0
        for r in range(start, start + width, MAIN_TN):
            rows.append(r // ROW_ALIGN)
            scales.append(scale)
    assert len(rows) * MAIN_TN == MAIN_W
    return np.asarray(rows, np.int32), np.asarray(scales, np.float32)


def _proj_main_kernel(rows_ref, scale_ref, x_ref, nw_ref, w_ref, o_ref, h_ref):
    j = pl.program_id(1)

    @pl.when(j == 0)
    def _():
        x = x_ref[...]
        ms = jnp.mean(x * x, axis=-1, keepdims=True)
        h_ref[...] = (x * lax.rsqrt(ms + RMS_EPS) * nw_ref[...]).astype(BF16)

    w = (w_ref[...] * scale_ref[j]).astype(BF16)
    o_ref[...] = _dot_nt(h_ref[...], w).astype(o_ref.dtype)


def _project_main(x2d, norm_w, w_t, out_dtype, tm, name):
    m, d = x2d.shape
    rows, scales = _main_tiles()
    return pl.pallas_call(
        _proj_main_kernel,
        out_shape=(jax.ShapeDtypeStruct((m, MAIN_W), out_dtype), jax.ShapeDtypeStruct((m, d), BF16)),
        grid_spec=pltpu.PrefetchScalarGridSpec(
            num_scalar_prefetch=1,
            grid=(m // tm, len(rows)),
            in_specs=[pl.BlockSpec(memory_space=pltpu.SMEM),
                      pl.BlockSpec((tm, d), lambda i, j, rows: (i, 0)),
                      pl.BlockSpec((1, d), lambda i, j, rows: (0, 0)),
                      pl.BlockSpec((pl.Element(MAIN_TN), pl.Element(d)), lambda i, j, rows: (rows[j] * ROW_ALIGN, 0))],
            out_specs=(pl.BlockSpec((tm, MAIN_TN), lambda i, j, rows: (i, j)),
                       pl.BlockSpec((tm, d), lambda i, j, rows: (i, 0)))),
        compiler_params=pltpu.CompilerParams(dimension_semantics=("parallel", "arbitrary"),
                                             vmem_limit_bytes=VMEM_LIMIT),
        name=name,
    )(jnp.asarray(rows), jnp.asarray(scales), x2d, norm_w.reshape(1, d), w_t)


def _out_kernel(x_ref, a_ref, m_ref, w_ref, fw_ref, o_ref, w_scr):
    @pl.when(pl.program_id(0) == 0)
    def _():
        w_scr[...] = w_ref[...].astype(BF16)

    y = (x_ref[...] + _dot(a_ref[...], w_scr[:ATT_WIDTH, :]) + _dot(m_ref[...], w_scr[ATT_WIDTH:, :]))
    ms = jnp.mean(y * y, axis=-1, keepdims=True)
    o_ref[...] = y * lax.rsqrt(ms + RMS_EPS) * fw_ref[...]


def _out_project(x2d, a, mo, w_out, final_w, tm, name):
    m, d = x2d.shape
    return pl.pallas_call(
        _out_kernel,
        out_shape=jax.ShapeDtypeStruct((m, d), F32),
        grid=(m // tm,),
        in_specs=[pl.BlockSpec((tm, d), lambda i: (i, 0)),
                  pl.BlockSpec((tm, ATT_WIDTH), lambda i: (i, 0)),
                  pl.BlockSpec((tm, ML_WIDTH), lambda i: (i, 0)),
                  pl.BlockSpec((ATT_WIDTH + ML_WIDTH, d), lambda i: (0, 0)),
                  pl.BlockSpec((1, d), lambda i: (0, 0))],
        out_specs=pl.BlockSpec((tm, d), lambda i: (i, 0)),
        scratch_shapes=[pltpu.VMEM((ATT_WIDTH + ML_WIDTH, d), BF16)],
        compiler_params=pltpu.CompilerParams(dimension_semantics=("arbitrary",),
                                             vmem_limit_bytes=VMEM_LIMIT),
        name=name,
    )(x2d, a, mo, w_out, final_w.reshape(1, d))


def _log_sigmoid(x):
    return jnp.minimum(x, 0.0) - jnp.log(1.0 + jnp.exp(-jnp.abs(x)))


def _sigmoid(x):
    return 0.5 * jnp.tanh(0.5 * x) + 0.5


def _silu(x):
    return x * _sigmoid(x)


ML_CHUNK = 256
ML_AUG = ML_V + LANES


def _mlstm_kernel(bi_ref, bf_ref, q_ref, k_ref, v_ref, mo_ref, mz_ref, misc_ref, nw_ref,
                  out_ref, ct_ref, m_ref):
    L = ML_CHUNK

    @pl.when(pl.program_id(1) == 0)
    def _():
        ct_ref[...] = jnp.zeros_like(ct_ref)
        m_ref[...] = jnp.zeros_like(m_ref)

    lane = lax.broadcasted_iota(I32, (1, LANES), 1)
    bias = jnp.zeros((1, LANES), F32)
    for h in range(ML_HEADS):
        bias = jnp.where(lane == MISC_IG + h, bi_ref[h], jnp.where(lane == MISC_FG + h, bf_ref[h], bias))
    pre = misc_ref[...] + bias
    gates = jnp.where(jnp.logical_and(lane >= MISC_FG, lane < MISC_FG + ML_HEADS), _log_sigmoid(pre), pre)
    gates_t = gates.T
    t_idx = lax.broadcasted_iota(I32, (L, L), 0)
    s_idx = lax.broadcasted_iota(I32, (L, L), 1)
    causal = s_idx <= t_idx
    ones_col = jnp.where(lax.broadcasted_iota(I32, (L, LANES), 1) == 0, 1.0, 0.0).astype(BF16)

    for h in range(ML_HEADS):
        ig_row = gates_t[MISC_IG + h:MISC_IG + h + 1, :]
        lf_row = gates_t[MISC_FG + h:MISC_FG + h + 1, :]
        lf_col = gates[:, MISC_FG + h:MISC_FG + h + 1]
        b_col = jnp.sum(jnp.where(causal, lf_row, 0.0), axis=1, keepdims=True)
        b_row = jnp.sum(jnp.where(t_idx <= s_idx, lf_col, 0.0), axis=0, keepdims=True)
        m_prev = m_ref[0, h][0:1, 0:1]
        log_d = jnp.where(causal, b_col - b_row + ig_row, -jnp.inf)
        log_a = b_col + m_prev
        m_t = jnp.maximum(log_a, jnp.max(log_d, axis=1, keepdims=True))
        d = jnp.exp(log_d - m_t)
        a = jnp.exp(log_a - m_t)

        q = q_ref[:, h * ML_QK:(h + 1) * ML_QK]
        k = k_ref[:, h * ML_QK:(h + 1) * ML_QK]
        v_aug = jnp.concatenate([v_ref[:, h * ML_V:(h + 1) * ML_V], ones_col], axis=1)
        s = (_dot_nt(q, k) * d).astype(BF16)
        ct = ct_ref[0, h]
        num_aug = a * _dot(q, ct.astype(BF16)) + _dot(s, v_aug)
        den = num_aug[:, ML_V:ML_V + 1]
        hh = num_aug[:, :ML_V] / jnp.maximum(jnp.abs(den), jnp.exp(-m_t))

        m_new = m_t[L - 1:L, :]
        a_end = a[L - 1:L, :]
        w_row = jnp.exp(b_row[:, L - 1:L] - b_row + ig_row - m_new)
        ktw = (k.astype(F32).T * w_row).astype(BF16)
        ct_new = a_end * ct + _dot(ktw, v_aug)
        ct_ref[0, h] = ct_new
        m_ref[0, h] = jnp.broadcast_to(m_new, (8, LANES))

        ms = jnp.mean(hh * hh, axis=1, keepdims=True)
        hn = hh * lax.rsqrt(ms + RMS_EPS) * nw_ref[:, h * ML_V:(h + 1) * ML_V]
        gate = _sigmoid(mo_ref[:, h * ML_V:(h + 1) * ML_V].astype(F32)) * _silu(mz_ref[:, h * ML_V:(h + 1) * ML_V].astype(F32))
        out_ref[:, h * ML_V:(h + 1) * ML_V] = (hn * gate).astype(out_ref.dtype)


def _mlstm_prompt(p_main, misc, b_i, b_f, ml_norm_w, batch, seq):
    L = ML_CHUNK
    nc = seq // L
    row = lambda b, c: b * nc + c
    return pl.pallas_call(
        _mlstm_kernel,
        out_shape=(jax.ShapeDtypeStruct((batch * seq, ML_WIDTH), BF16),
                   jax.ShapeDtypeStruct((batch, ML_HEADS, ML_QK, ML_AUG), F32),
                   jax.ShapeDtypeStruct((batch, ML_HEADS, 8, LANES), F32)),
        grid=(batch, nc),
        in_specs=[pl.BlockSpec(memory_space=pltpu.SMEM),
                  pl.BlockSpec(memory_space=pltpu.SMEM),
                  pl.BlockSpec((L, 512), lambda b, c: (row(b, c), 12)),
                  pl.BlockSpec((L, 512), lambda b, c: (row(b, c), 13)),
                  pl.BlockSpec((L, 1024), lambda b, c: (row(b, c), 3)),
                  pl.BlockSpec((L, 1024), lambda b, c: (row(b, c), 4)),
                  pl.BlockSpec((L, 1024), lambda b, c: (row(b, c), 5)),
                  pl.BlockSpec((L, LANES), lambda b, c: (row(b, c), 0)),
                  pl.BlockSpec((1, ML_WIDTH), lambda b, c: (0, 0))],
        out_specs=(pl.BlockSpec((L, ML_WIDTH), lambda b, c: (row(b, c), 0)),
                   pl.BlockSpec((1, ML_HEADS, ML_QK, ML_AUG), lambda b, c: (b, 0, 0, 0)),
                   pl.BlockSpec((1, ML_HEADS, 8, LANES), lambda b, c: (b, 0, 0, 0))),
        compiler_params=pltpu.CompilerParams(dimension_semantics=("parallel", "arbitrary"),
                                             vmem_limit_bytes=VMEM_LIMIT),
        name="mlstm_prompt",
    )(b_i, b_f, p_main, p_main, p_main, p_main, p_main, misc, ml_norm_w.reshape(1, ML_WIDTH))


INT_MIN = -2 ** 31


def _float_to_key(x):
    bits = lax.bitcast_convert_type(x, I32)
    return jnp.where(bits >= 0, bits, bits ^ jnp.int32(0x7FFFFFFF))


def _key_to_float(key):
    bits = jnp.where(key >= 0, key, key ^ jnp.int32(0x7FFFFFFF))
    return lax.bitcast_convert_type(bits, F32)


KEY_NEG_INF = INT_MIN + 0x7FFFFF


def _kth_largest_key(count_ge, shape, k):
    def bit_body(it, prefix):
        cand = prefix + lax.shift_left(jnp.int32(1), 31 - it)
        cand_f = _key_to_float(jnp.maximum(cand, KEY_NEG_INF))
        return jnp.where(count_ge(cand_f) >= float(k), cand, prefix)

    return lax.fori_loop(0, 32, bit_body, jnp.full(shape, INT_MIN, I32))


def _tie_cutoff(count_tie_le, need, shape, index_bits):
    def bit_body(it, lo):
        cand = lo + lax.shift_left(jnp.int32(1), index_bits - 1 - it).astype(F32)
        return jnp.where(count_tie_le(cand) < need, cand, lo)

    return lax.fori_loop(0, index_bits, bit_body, jnp.full(shape, -1.0, F32)) + 1.0


DSA_QB = 128
DSA_TK = 512
DSA_TS = 256
DSA_TA = 256
HEAD_PAIR = 2


def _dsa_kernel(q_ref, qi_ref, az_ref, miscq_ref, k_ref, v_ref, misck_ref, out_ref,
                k_bf, vt_scr, ki_lo, ki_hi, sc_scr, acc_scr, j_scr, *head_scr, topk):
    i = pl.program_id(1)
    seq = misck_ref.shape[0]
    QB, TK, TS, TA = DSA_QB, DSA_TK, DSA_TS, DSA_TA
    nt = (i * QB + QB + TK - 1) // TK

    @pl.when(i == 0)
    def _():
        for g in range(KV_HEADS):
            k_bf[:, g * HEAD_DIM:(g + 1) * HEAD_DIM] = k_ref[pl.ds(g, seq, stride=KV_HEADS), :].astype(BF16)

        def v_tile(a, carry):
            for g in range(KV_HEADS):
                rows = pl.ds(pl.multiple_of(a * (TA * KV_HEADS), TA * KV_HEADS) + g, TA, stride=KV_HEADS)
                vt_scr[a, g * HEAD_DIM:(g + 1) * HEAD_DIM, :] = v_ref[rows, :].T.astype(BF16)
            return carry

        lax.fori_loop(0, seq // TA, v_tile, 0)
        lane = lax.broadcasted_iota(I32, (seq, LANES), 1)
        lo = jnp.where(lane < IDX_DIM, misck_ref[...], 0.0)
        ki_lo[...] = lo.astype(BF16)
        ki_hi[...] = pltpu.roll(lo, IDX_DIM, axis=1).astype(BF16)

    k_iota = lax.broadcasted_iota(I32, (TK, QB), 0)
    q_pos = i * QB + lax.broadcasted_iota(I32, (TK, QB), 1)
    w_t = miscq_ref[...].T

    def tile_rows(t):
        return pl.ds(pl.multiple_of(t * TK, TK), TK)

    def score_tile(t, carry):
        rows = pl.ds(pl.multiple_of(t * TS, TS), TS)
        klo = ki_lo[rows, :]
        khi = ki_hi[rows, :]
        acc = jnp.zeros((TS, QB), F32)
        for p in range(IDX_HEADS // 2):
            qp = qi_ref[:, p * LANES:(p + 1) * LANES]
            w0 = w_t[MISC_WI + 2 * p:MISC_WI + 2 * p + 1, :]
            w1 = w_t[MISC_WI + 2 * p + 1:MISC_WI + 2 * p + 2, :]
            acc = acc + w0 * jnp.maximum(_dot_nt(klo, qp), 0.0) + w1 * jnp.maximum(_dot_nt(khi, qp), 0.0)
        k_pos = t * TS + lax.broadcasted_iota(I32, (TS, QB), 0)
        q_pos_s = i * QB + lax.broadcasted_iota(I32, (TS, QB), 1)
        score = jnp.where(k_pos <= q_pos_s, acc * IDX_SCALE + 0.0, -jnp.inf)
        sc_scr[rows, :] = _float_to_key(score)
        return carry

    def score_tiles(t, carry):
        for sub in range(TK // TS):
            score_tile(t * (TK // TS) + sub, carry)
        return carry

    lax.fori_loop(0, nt, score_tiles, 0)

    def count_key_ge(cand):
        def body(t, cnt):
            return cnt + _fold_rows(jnp.add, jnp.where(sc_scr[tile_rows(t), :] >= cand, 1, 0))

        cnt = lax.fori_loop(0, nt, body, jnp.zeros((8, QB), I32))
        return jnp.sum(cnt, axis=0, keepdims=True)

    def bit_body(it, carry):
        prefix, cnt_at = carry
        cand = prefix + lax.shift_left(jnp.int32(1), 31 - it)
        cnt = count_key_ge(cand)
        ok = cnt >= topk
        return jnp.where(ok, cand, prefix), jnp.where(ok, cnt, cnt_at)

    thr, cnt_thr = lax.fori_loop(0, 32, bit_body, (jnp.full((1, QB), INT_MIN, I32), jnp.full((1, QB), TK, I32) * nt))

    j_scr[...] = jnp.full((1, QB), float(seq), F32)

    @pl.when(jnp.max(cnt_thr) > topk)
    def _():
        def count_keys(indicator):
            def body(t, cnt):
                k_pos = (t * TK + k_iota).astype(F32)
                return cnt + _fold_rows(jnp.add, indicator(sc_scr[tile_rows(t), :], k_pos))

            return jnp.sum(lax.fori_loop(0, nt, body, jnp.zeros((8, QB), F32)), axis=0, keepdims=True)

        need = float(topk) - count_keys(lambda key, k_pos: jnp.where(key > thr, 1.0, 0.0))
        count_tie_le = lambda cut: count_keys(
            lambda key, k_pos: jnp.where(key == thr, jnp.where(k_pos <= cut, 1.0, 0.0), 0.0))
        j_scr[...] = _tie_cutoff(count_tie_le, need, (1, QB), int(np.log2(seq)))

    cut = j_scr[...]

    def bias_tile(t, carry):
        key = sc_scr[tile_rows(t), :]
        k_pos = t * TK + k_iota
        taken = jnp.where(key > thr, 0.0,
                          jnp.where(key == thr, jnp.where(k_pos.astype(F32) <= cut, 0.0, NEG_BIG), NEG_BIG))
        sc_scr[tile_rows(t), :] = lax.bitcast_convert_type(jnp.where(k_pos <= q_pos, taken, NEG_BIG), I32)
        return carry

    lax.fori_loop(0, nt, bias_tile, 0)

    assert TK == 2 * TA and GROUP % HEAD_PAIR == 0
    az = az_ref[...].astype(F32)
    n_pairs = ATT_HEADS // HEAD_PAIR
    PQ = HEAD_PAIR * QB
    lg_scr, p_scr = head_scr[:n_pairs], head_scr[n_pairs:]
    acc_scr[...] = jnp.zeros_like(acc_scr)
    last = 2 * nt - 1

    def sub_rows(a):
        return pl.ds(pl.multiple_of(a * TA, TA), TA)

    def kv_cols(j):
        g = (j * HEAD_PAIR) // GROUP
        return slice(g * HEAD_DIM, (g + 1) * HEAD_DIM)

    def logits_stage(j, a, slot):
        q_pair = jnp.concatenate([q_ref[:, h * HEAD_DIM:(h + 1) * HEAD_DIM]
                                  for h in range(j * HEAD_PAIR, (j + 1) * HEAD_PAIR)], axis=0)
        lg_scr[j][slot] = _dot_nt(k_bf[sub_rows(a), kv_cols(j)], q_pair)

    CH = 64

    def softmax_stage(j, a, slot, state):
        m, l8, _ = state
        maxes = []
        for r in range(0, TA, CH):
            bias = lax.bitcast_convert_type(sc_scr[pl.ds(pl.multiple_of(a * TA, TA) + r, CH), :], F32)
            lg = lg_scr[j][slot, r:r + CH, :] + jnp.concatenate([bias] * HEAD_PAIR, axis=1)
            lg_scr[j][slot, r:r + CH, :] = lg
            maxes.append(_fold_rows(jnp.maximum, lg))
        m_new = jnp.maximum(m, jnp.max(_tree_reduce(jnp.maximum, maxes), axis=0, keepdims=True))
        alpha = jnp.exp2(m - m_new)
        sums = []
        for r in range(0, TA, CH):
            p = jnp.exp2(lg_scr[j][slot, r:r + CH, :] - m_new)
            p_scr[j][slot, r:r + CH, :] = p.astype(BF16)
            sums.append(_fold_rows(jnp.add, p))
        return m_new, alpha * l8 + _tree_reduce(jnp.add, sums), alpha

    def value_stage(j, a, slot, state):
        vt = vt_scr[a][kv_cols(j), :]
        acc_scr[j] = state[2] * acc_scr[j] + _dot(vt, p_scr[j][slot])

    def att_tile(t, state):
        for slot in range(2):
            a = 2 * t + slot
            out = []
            for j in range(n_pairs):
                value_stage(j, jnp.maximum(a - 1, 0), 1 - slot, state[j])
                out.append(softmax_stage(j, a, slot, state[j]))
                logits_stage(j, jnp.minimum(a + 1, last), 1 - slot)
            state = tuple(out)
        return state

    init = tuple((jnp.full((1, PQ), NEG_BIG, F32), jnp.zeros((8, PQ), F32), jnp.ones((1, PQ), F32))
                 for _ in range(n_pairs))
    for j in range(n_pairs):
        p_scr[j][1] = jnp.zeros((TA, PQ), BF16)
        logits_stage(j, 0, 0)
    fin = lax.fori_loop(0, nt, att_tile, init)
    for j in range(n_pairs):
        value_stage(j, last, 1, fin[j])
    for j in range(n_pairs):
        o_t = acc_scr[j] / jnp.sum(fin[j][1], axis=0, keepdims=True)
        for hh in range(HEAD_PAIR):
            c0 = (j * HEAD_PAIR + hh) * HEAD_DIM
            o = o_t[:, hh * QB:(hh + 1) * QB].T
            out_ref[:, c0:c0 + HEAD_DIM] = (o * _silu(az[:, c0:c0 + HEAD_DIM])).astype(out_ref.dtype)


def _dsa_prompt(p_main, k_rows, v_rows, misc, batch, seq):
    QB, TK = DSA_QB, DSA_TK
    nb = seq // QB
    topk = min(TOPK_MAX, seq // 4)
    TA = DSA_TA
    assert TK >= topk and seq % TK == 0 and seq % QB == 0 and TK % DSA_TS == 0
    qrow = lambda b, i: b * nb + i
    return pl.pallas_call(
        functools.partial(_dsa_kernel, topk=topk),
        out_shape=jax.ShapeDtypeStruct((batch * seq, ATT_WIDTH), BF16),
        grid=(batch, nb),
        in_specs=[pl.BlockSpec((QB, ATT_WIDTH), lambda b, i: (qrow(b, i), 0)),
                  pl.BlockSpec((QB, 1024), lambda b, i: (qrow(b, i), 1)),
                  pl.BlockSpec((QB, ATT_WIDTH), lambda b, i: (qrow(b, i), 2)),
                  pl.BlockSpec((QB, LANES), lambda b, i: (qrow(b, i), 0)),
                  pl.BlockSpec((seq * KV_HEADS, HEAD_DIM), lambda b, i: (b, 0)),
                  pl.BlockSpec((seq * KV_HEADS, HEAD_DIM), lambda b, i: (b, 0)),
                  pl.BlockSpec((seq, LANES), lambda b, i: (b, 0))],
        out_specs=pl.BlockSpec((QB, ATT_WIDTH), lambda b, i: (qrow(b, i), 0)),
        scratch_shapes=[pltpu.VMEM((seq, KV_WIDTH), BF16),
                        pltpu.VMEM((seq // TA, KV_WIDTH, TA), BF16),
                        pltpu.VMEM((seq, LANES), BF16),
                        pltpu.VMEM((seq, LANES), BF16),
                        pltpu.VMEM((seq, QB), I32),
                        pltpu.VMEM((ATT_HEADS // HEAD_PAIR, HEAD_DIM, HEAD_PAIR * QB), F32),
                        pltpu.VMEM((1, QB), F32)]
                       + [pltpu.VMEM((2, TA, HEAD_PAIR * QB), F32)] * (ATT_HEADS // HEAD_PAIR)
                       + [pltpu.VMEM((2, TA, HEAD_PAIR * QB), BF16)] * (ATT_HEADS // HEAD_PAIR),
        compiler_params=pltpu.CompilerParams(dimension_semantics=("parallel", "arbitrary"),
                                             vmem_limit_bytes=VMEM_LIMIT),
        name="dsa_prompt",
    )(p_main, p_main, p_main, misc, k_rows, v_rows, misc)


SCORE_ROWS = 128


def _sample_scores_kernel(pt_ref, qi_ref, wb_ref, kin_ref, cache_ref, sc_ref, snew_ref, page_buf, sems):
    b = pl.program_id(0)
    n_rows = pl.num_programs(0)
    n_pages = sc_ref.shape[1]
    slot = lax.rem(b, 2)

    def page_copy(row, p, to_slot):
        return pltpu.make_async_copy(cache_ref.at[pt_ref[row, p]], page_buf.at[to_slot, p], sems.at[to_slot])

    def start_fetch(row, to_slot):
        def body(p, carry):
            page_copy(row, p, to_slot).start()
            return carry

        lax.fori_loop(0, n_pages, body, 0, unroll=8)

    @pl.when(b == 0)
    def _():
        start_fetch(0, 0)

    @pl.when(b + 1 < n_rows)
    def _():
        start_fetch(b + 1, 1 - slot)

    def wait_page(p, carry):
        page_copy(b, p, slot).wait()
        return carry

    lax.fori_loop(0, n_pages, wait_page, 0, unroll=8)

    qi = qi_ref[0]
    wb = wb_ref[0]

    def score_rows(g, carry):
        rows = []
        for j in range(SCORE_ROWS):
            page = page_buf[slot, g * SCORE_ROWS + j]
            s = _dot(qi, page.astype(BF16))
            rows.append(jnp.sum(jnp.maximum(s, 0.0) * wb, axis=0, keepdims=True) * IDX_SCALE)
        sc_ref[0, pl.ds(pl.multiple_of(g * SCORE_ROWS, SCORE_ROWS), SCORE_ROWS), :] = jnp.concatenate(rows, axis=0)
        return carry

    lax.fori_loop(0, n_pages // SCORE_ROWS, score_rows, 0)

    ki_new = kin_ref[0].astype(BF16).astype(F32)
    s = jnp.sum(qi.astype(F32) * ki_new, axis=1, keepdims=True)
    s_new = jnp.sum(jnp.maximum(s, 0.0) * wb[:, 0:1], axis=0, keepdims=True) * IDX_SCALE
    snew_ref[0] = jnp.broadcast_to(s_new, (8, LANES))


def _sample_scores(page_table, qi16, wb, ki_new, cache_ik_t):
    bd, n_pages = page_table.shape
    assert n_pages % SCORE_ROWS == 0
    bmap = lambda b, pt: (b, 0, 0)
    return pl.pallas_call(
        _sample_scores_kernel,
        out_shape=(jax.ShapeDtypeStruct((bd, n_pages, PAGE), F32),
                   jax.ShapeDtypeStruct((bd, 8, LANES), F32)),
        grid_spec=pltpu.PrefetchScalarGridSpec(
            num_scalar_prefetch=1,
            grid=(bd,),
            in_specs=[pl.BlockSpec((1, IDX_HEADS, IDX_DIM), bmap),
                      pl.BlockSpec((1, IDX_HEADS, LANES), bmap),
                      pl.BlockSpec((1, 1, IDX_DIM), bmap),
                      pl.BlockSpec(memory_space=pl.ANY)],
            out_specs=(pl.BlockSpec((1, n_pages, PAGE), bmap),
                       pl.BlockSpec((1, 8, LANES), bmap)),
            scratch_shapes=[pltpu.VMEM((2, n_pages, IDX_DIM, PAGE), F32),
                            pltpu.SemaphoreType.DMA((2,))]),
        compiler_params=pltpu.CompilerParams(dimension_semantics=("arbitrary",),
                                             vmem_limit_bytes=VMEM_LIMIT),
        name="sample_scores",
    )(page_table, qi16, wb, ki_new, cache_ik_t)


SEL_THR, SEL_NEXT, SEL_CUT = 0, 1, 2


def _sample_select_kernel(sc_ref, snew_ref, sel_ref, *, topk):
    x = sc_ref[...]
    bd, n_pages, _ = x.shape
    past = n_pages * PAGE
    s_new = snew_ref[:, 0:1, 0:1]

    def total(v):
        return jnp.sum(jnp.sum(v, axis=1, keepdims=True), axis=2, keepdims=True)

    def count_ge(cand):
        return total(jnp.where(x >= cand, 1.0, 0.0)) + jnp.where(s_new >= cand, 1.0, 0.0)

    key = _kth_largest_key(count_ge, (bd, 1, 1), topk)
    thr = _key_to_float(key)
    thr_next = _key_to_float(key + 1)
    need = float(topk) - count_ge(thr_next)
    pos = (lax.broadcasted_iota(I32, (1, n_pages, PAGE), 1) * PAGE
           + lax.broadcasted_iota(I32, (1, n_pages, PAGE), 2)).astype(F32)

    def count_tie_le(cut):
        tie = jnp.where(x >= thr_next, 0.0, jnp.where(x >= thr, jnp.where(pos <= cut, 1.0, 0.0), 0.0))
        tie_new = jnp.where(s_new >= thr_next, 0.0, jnp.where(s_new >= thr, jnp.where(float(past) <= cut, 1.0, 0.0), 0.0))
        return total(tie) + tie_new

    row = lax.broadcasted_iota(I32, (bd, 8, LANES), 1)
    sel_ref[...] = jnp.where(row == SEL_THR, thr, jnp.where(row == SEL_NEXT, thr_next, float(past + 1)))

    @pl.when(jnp.max(count_ge(thr)) > float(topk))
    def _():
        cut = _tie_cutoff(count_tie_le, need, (bd, 1, 1), int(np.log2(past)) + 1)
        sel_ref[...] = jnp.where(row == SEL_THR, thr, jnp.where(row == SEL_NEXT, thr_next, cut))


def _sample_select(scores, s_new, topk):
    bd = scores.shape[0]
    return pl.pallas_call(
        functools.partial(_sample_select_kernel, topk=topk),
        out_shape=jax.ShapeDtypeStruct((bd, 8, LANES), F32),
        compiler_params=pltpu.CompilerParams(vmem_limit_bytes=VMEM_LIMIT),
        name="sample_select",
    )(scores, s_new)


def _taken_bias(score, pos, sel):
    thr, thr_next, cut = sel[SEL_THR:SEL_THR + 1, 0:1], sel[SEL_NEXT:SEL_NEXT + 1, 0:1], sel[SEL_CUT:SEL_CUT + 1, 0:1]
    return jnp.where(score >= thr_next, 0.0,
                     jnp.where(score >= thr, jnp.where(pos <= cut, 0.0, NEG_BIG), NEG_BIG))


def _sample_compact_kernel(pt_ref, sc_ref, sel_ref, rows_ref, nsel_ref, *, slots):
    n_pages = sc_ref.shape[1]
    pos = (lax.broadcasted_iota(I32, (n_pages, PAGE), 0) * PAGE
           + lax.broadcasted_iota(I32, (n_pages, PAGE), 1)).astype(F32)
    taken = jnp.where(_taken_bias(sc_ref[0], pos, sel_ref[0]) == 0.0, 1.0, 0.0)
    before = lax.broadcasted_iota(I32, (PAGE, PAGE), 0) < lax.broadcasted_iota(I32, (PAGE, PAGE), 1)
    in_page = _dot(taken, jnp.where(before, 1.0, 0.0))
    taken_t = taken.T
    rank_t = jnp.where(taken_t > 0.0, in_page.T, -1.0)
    page_tot = jnp.sum(taken_t, axis=0, keepdims=True)
    earlier = lax.broadcasted_iota(I32, (n_pages, n_pages), 0) < lax.broadcasted_iota(I32, (n_pages, n_pages), 1)
    first_slot = _dot(jnp.broadcast_to(page_tot, (8, n_pages)),
                      jnp.where(earlier, 1.0, 0.0))[0:1]
    phys_t = (pt_ref[0] * PAGE + lax.broadcasted_iota(I32, (PAGE, n_pages), 0)).astype(F32)
    slot = lax.broadcasted_iota(I32, (slots, n_pages), 0).astype(F32)

    def rank_body(r, acc):
        r = r.astype(F32)
        row = jnp.sum(jnp.where(rank_t == r, phys_t, 0.0), axis=0, keepdims=True)
        target = jnp.where(r < page_tot, first_slot + r, -1.0)
        return acc + jnp.where(slot == target, row, 0.0)

    most = jnp.max(page_tot).astype(I32)
    acc = lax.fori_loop(0, most, rank_body, jnp.zeros((slots, n_pages), F32))
    rows_ref[0] = jnp.broadcast_to(jnp.sum(acc, axis=1, keepdims=True), (slots, LANES))
    nsel_ref[0] = jnp.broadcast_to(jnp.sum(page_tot, axis=1, keepdims=True), (8, LANES))


def _sample_compact(page_table, scores, sel, slots):
    bd, n_pages = page_table.shape
    assert n_pages % LANES == 0
    bmap = lambda b: (b, 0, 0)
    return pl.pallas_call(
        functools.partial(_sample_compact_kernel, slots=slots),
        out_shape=(jax.ShapeDtypeStruct((bd, slots, LANES), F32), jax.ShapeDtypeStruct((bd, 8, LANES), F32)),
        grid=(bd,),
        in_specs=[pl.BlockSpec((1, 1, n_pages), bmap), pl.BlockSpec((1, n_pages, PAGE), bmap),
                  pl.BlockSpec((1, 8, LANES), bmap)],
        out_specs=(pl.BlockSpec((1, slots, LANES), bmap), pl.BlockSpec((1, 8, LANES), bmap)),
        compiler_params=pltpu.CompilerParams(dimension_semantics=("parallel",), vmem_limit_bytes=VMEM_LIMIT),
        name="sample_compact",
    )(page_table.reshape(bd, 1, n_pages), scores, sel)


SC_GATHER_CHUNK = 128


def _gather_rows(table_k, table_v, idx):
    info = plsc.get_sparse_core_info()
    n_workers = info.num_cores * info.num_subcores
    n_idx = idx.shape[0]
    per_worker = n_idx // n_workers
    assert per_worker * n_workers == n_idx and per_worker % SC_GATHER_CHUNK == 0
    mesh = plsc.VectorSubcoreMesh(core_axis_name="c", subcore_axis_name="s")
    out = jax.ShapeDtypeStruct((n_idx, table_k.shape[1]), table_k.dtype)

    @functools.partial(
        pl.kernel, mesh=mesh, out_type=(out, out),
        scratch_types=[pltpu.VMEM((SC_GATHER_CHUNK,), I32),
                       pltpu.VMEM((SC_GATHER_CHUNK, table_k.shape[1]), table_k.dtype),
                       pltpu.VMEM((SC_GATHER_CHUNK, table_v.shape[1]), table_v.dtype),
                       pltpu.SemaphoreType.DMA, pltpu.SemaphoreType.DMA])
    def gather(tk_hbm, tv_hbm, idx_hbm, ok_hbm, ov_hbm, idx_v, rk_v, rv_v, sem_k, sem_v):
        worker = lax.axis_index("s") * info.num_cores + lax.axis_index("c")

        @pl.loop(0, per_worker // SC_GATHER_CHUNK)
        def _(j):
            base = worker * per_worker + j * SC_GATHER_CHUNK
            pltpu.sync_copy(idx_hbm.at[pl.ds(base, SC_GATHER_CHUNK)], idx_v)
            copy_k = pltpu.async_copy(tk_hbm.at[idx_v], rk_v, sem_k)
            copy_v = pltpu.async_copy(tv_hbm.at[idx_v], rv_v, sem_v)
            copy_k.wait()
            copy_v.wait()
            pltpu.sync_copy(rk_v, ok_hbm.at[pl.ds(base, SC_GATHER_CHUNK)])
            pltpu.sync_copy(rv_v, ov_hbm.at[pl.ds(base, SC_GATHER_CHUNK)])

    return gather(table_k, table_v, idx)


def _sample_attend_kernel(q_ref, az_ref, kn_ref, vn_ref, snew_ref, sel_ref, nsel_ref, k_ref, v_ref, out_ref, *, past):
    width = k_ref.shape[1]
    q = q_ref[0]
    col = lax.broadcasted_iota(I32, (ATT_HEADS, width), 1)
    head = lax.broadcasted_iota(I32, (ATT_HEADS, width), 0)
    own_head = col % KV_HEADS == head // GROUP
    filled = (col // KV_HEADS).astype(F32) < nsel_ref[0][0:1, 0:1]
    lg = _dot_nt(q, k_ref[0].astype(BF16))
    lg = jnp.where(own_head, jnp.where(filled, lg, NEG_BIG), NEG_BIG)
    lg_new = (jnp.sum(q.astype(F32) * kn_ref[0].astype(BF16).astype(F32), axis=1, keepdims=True)
              + _taken_bias(snew_ref[0][0:1, 0:1], float(past), sel_ref[0]))
    m = jnp.maximum(jnp.max(lg, axis=1, keepdims=True), lg_new)
    p = jnp.exp2(lg - m)
    p_new = jnp.exp2(lg_new - m)
    l = jnp.sum(p, axis=1, keepdims=True) + p_new
    acc = _dot(p.astype(BF16), v_ref[0].astype(BF16)) + p_new * vn_ref[0].astype(BF16).astype(F32)
    out_ref[0] = (acc / l) * _silu(az_ref[0])


def _sample_attend(q8, az8, k_new8, v_new8, s_new, sel, n_sel, k_sel, v_sel, past):
    bd, width, _ = k_sel.shape
    bmap = lambda b: (b, 0, 0)
    head_tile = pl.BlockSpec((1, ATT_HEADS, HEAD_DIM), bmap)
    par_tile = pl.BlockSpec((1, 8, LANES), bmap)
    rows_tile = pl.BlockSpec((1, width, HEAD_DIM), bmap)
    return pl.pallas_call(
        functools.partial(_sample_attend_kernel, past=past),
        out_shape=jax.ShapeDtypeStruct((bd, ATT_HEADS, HEAD_DIM), F32),
        grid=(bd,),
        in_specs=[head_tile, head_tile, head_tile, head_tile, par_tile, par_tile, par_tile, rows_tile, rows_tile],
        out_specs=head_tile,
        compiler_params=pltpu.CompilerParams(dimension_semantics=("parallel",), vmem_limit_bytes=VMEM_LIMIT),
        name="sample_attend",
    )(q8, az8, k_new8, v_new8, s_new, sel, n_sel, k_sel, v_sel)


def _mlstm_step_kernel(bi_ref, bf_ref, q_ref, k_ref, v_ref, mo_ref, mz_ref, misc_ref, nw_ref,
                       c_ref, n_ref, m_ref, out_ref, c_out, n_out, m_out):
    misc = misc_ref[0]
    eye = lax.broadcasted_iota(I32, (ML_V, ML_V), 0) == lax.broadcasted_iota(I32, (ML_V, ML_V), 1)
    for h in range(ML_HEADS):
        ig = misc[:, MISC_IG + h:MISC_IG + h + 1] + bi_ref[h]
        lf = _log_sigmoid(misc[:, MISC_FG + h:MISC_FG + h + 1] + bf_ref[h])
        m_prev = m_ref[0, h][:, 0:1]
        log_a = lf + m_prev
        m_t = jnp.maximum(log_a, ig)
        d = jnp.exp(ig - m_t)
        a = jnp.exp(log_a - m_t)
        q = q_ref[0][:, h * ML_QK:(h + 1) * ML_QK]
        k = k_ref[0][:, h * ML_QK:(h + 1) * ML_QK]
        v = v_ref[0][:, h * ML_V:(h + 1) * ML_V]
        v_col = jnp.sum(jnp.where(eye, v, 0.0), axis=1, keepdims=True)
        c = c_ref[0, h]
        n = n_ref[0, h]
        s = jnp.sum(q * k, axis=1, keepdims=True) * d
        num = a * jnp.sum(c * q, axis=1, keepdims=True) + s * v_col
        den = a * jnp.sum(n * q, axis=1, keepdims=True) + s
        h_col = num / jnp.maximum(jnp.abs(den), jnp.exp(-m_t))
        c_out[0, h] = a * c + (d * v_col) * k
        n_out[0, h] = a * n + d * k
        m_out[0, h] = jnp.broadcast_to(m_t, (1, LANES))

        h_row = jnp.sum(jnp.where(eye, h_col, 0.0), axis=0, keepdims=True)
        ms = jnp.mean(h_row * h_row, axis=1, keepdims=True)
        hn = h_row * lax.rsqrt(ms + RMS_EPS) * nw_ref[:, h * ML_V:(h + 1) * ML_V]
        gate = _sigmoid(mo_ref[0][:, h * ML_V:(h + 1) * ML_V]) * _silu(mz_ref[0][:, h * ML_V:(h + 1) * ML_V])
        out_ref[0, :, h * ML_V:(h + 1) * ML_V] = (hn * gate).astype(out_ref.dtype)


def _mlstm_step(ps_main, misc, b_i, b_f, ml_norm_w, state_c, state_n, state_m):
    bd = ps_main.shape[0]
    col = lambda j: (lambda b: (b, 0, j))
    st4 = lambda b: (b, 0, 0, 0)
    return pl.pallas_call(
        _mlstm_step_kernel,
        out_shape=(jax.ShapeDtypeStruct((bd, 1, ML_WIDTH), BF16),
                   jax.ShapeDtypeStruct(state_c.shape, F32),
                   jax.ShapeDtypeStruct(state_n.shape, F32),
                   jax.ShapeDtypeStruct(state_m.shape, F32)),
        grid=(bd,),
        in_specs=[pl.BlockSpec(memory_space=pltpu.SMEM),
                  pl.BlockSpec(memory_space=pltpu.SMEM),
                  pl.BlockSpec((1, 1, 512), col(12)),
                  pl.BlockSpec((1, 1, 512), col(13)),
                  pl.BlockSpec((1, 1, 1024), col(3)),
                  pl.BlockSpec((1, 1, 1024), col(4)),
                  pl.BlockSpec((1, 1, 1024), col(5)),
                  pl.BlockSpec((1, 1, LANES), col(0)),
                  pl.BlockSpec((1, ML_WIDTH), lambda b: (0, 0)),
                  pl.BlockSpec((1, ML_HEADS, ML_V, ML_QK), st4),
                  pl.BlockSpec((1, ML_HEADS, 1, ML_QK), st4),
                  pl.BlockSpec((1, ML_HEADS, 1, LANES), st4)],
        out_specs=(pl.BlockSpec((1, 1, ML_WIDTH), lambda b: (b, 0, 0)),
                   pl.BlockSpec((1, ML_HEADS, ML_V, ML_QK), st4),
                   pl.BlockSpec((1, ML_HEADS, 1, ML_QK), st4),
                   pl.BlockSpec((1, ML_HEADS, 1, LANES), st4)),
        compiler_params=pltpu.CompilerParams(dimension_semantics=("parallel",),
                                             vmem_limit_bytes=VMEM_LIMIT),
        name="mlstm_step",
    )(b_i, b_f, ps_main, ps_main, ps_main, ps_main, ps_main, misc, ml_norm_w.reshape(1, ML_WIDTH),
      state_c, state_n, state_m)


def _small_weight(w_t):
    offs = np.cumsum((0,) + IN_SIZES)
    ak, av, ik, iw, mi, mf = (w_t[offs[j]:offs[j + 1]] for j in (1, 2, 4, 5, 10, 11))
    pad = jnp.zeros((LANES - IDX_DIM - IDX_HEADS - 2 * ML_HEADS, w_t.shape[1]), w_t.dtype)
    w_small = jnp.concatenate([ak, av, ik, iw, mi, mf, pad], axis=0)
    assert w_small.shape[0] == SMALL_W
    return w_small


def kernel(x_prompt, x_sample, cache_k, cache_v, cache_idx_k, state_C, state_n, state_m, page_table,
           norm_w, w_in, b_igate, b_fgate, ml_norm_w, w_out, final_norm_w):
    depth = w_in.shape[0]
    batch, seq, d = x_prompt.shape
    bd, dec_seq, _ = x_sample.shape
    assert depth == 1 and dec_seq == 1 and d == D_MODEL
    n_pages = page_table.shape[1]

    w_t = jnp.swapaxes(w_in[0], 0, 1)
    w_small = _small_weight(w_t)

    xp = x_prompt.reshape(batch * seq, d)
    p_main, h_p = _project_main(xp, norm_w[0], w_t, BF16, 1024, "proj_main")
    k_rows, v_rows, misc = _project_small(h_p, w_small, 1024, "proj_small")
    att = _dsa_prompt(p_main, k_rows, v_rows, misc, batch, seq)
    ml, ct, m_p = _mlstm_prompt(p_main, misc, b_igate[0], b_fgate[0], ml_norm_w[0], batch, seq)
    y_prompt = _out_project(xp, att, ml, w_out[0], final_norm_w, 256, "out_prompt").reshape(batch, seq, d)
    k_prompt = k_rows.reshape(1, batch, seq, KV_HEADS, HEAD_DIM)
    v_prompt = v_rows.reshape(1, batch, seq, KV_HEADS, HEAD_DIM)
    ik_prompt = misc[:, :IDX_DIM].reshape(1, batch, seq, IDX_DIM)
    c_prompt = jnp.swapaxes(ct[..., :ML_V], -1, -2)[None]
    n_prompt = ct[..., ML_V][None]
    m_prompt = m_p[:, :, 0, 0][None]

    xs = x_sample.reshape(bd, d)
    ps_main, h_s = _project_main(xs, norm_w[0], w_t, F32, bd, "proj_main_s")
    ks_rows, vs_rows, misc_s = _project_small(h_s, w_small, bd, "proj_small_s")
    q8 = ps_main[:, :ATT_WIDTH].reshape(bd, ATT_HEADS, HEAD_DIM).astype(BF16)
    qi16 = ps_main[:, 1024:2048].reshape(bd, IDX_HEADS, IDX_DIM).astype(BF16)
    az8 = ps_main[:, 2048:3072].reshape(bd, ATT_HEADS, HEAD_DIM)
    ki_new = misc_s[:, :IDX_DIM].reshape(bd, 1, IDX_DIM)
    wb = jnp.broadcast_to(misc_s[:, MISC_WI:MISC_WI + IDX_HEADS, None], (bd, IDX_HEADS, LANES))
    k_new8 = jnp.repeat(ks_rows.reshape(bd, KV_HEADS, HEAD_DIM), GROUP, axis=1)
    v_new8 = jnp.repeat(vs_rows.reshape(bd, KV_HEADS, HEAD_DIM), GROUP, axis=1)
    scores, s_new = _sample_scores(page_table, qi16, wb, ki_new, jnp.swapaxes(cache_idx_k[0], 1, 2))
    assert cache_k.shape[1] * PAGE < 2 ** 24
    topk_s = min(TOPK_MAX, (n_pages * PAGE + 1) // 4)
    sel = _sample_select(scores, s_new, topk_s)
    key_rows, n_sel = _sample_compact(page_table, scores, sel, topk_s)
    idx = (key_rows[:, :, :1].astype(I32) * KV_HEADS + jnp.arange(KV_HEADS, dtype=I32)).reshape(-1)
    k_sel, v_sel = _gather_rows(cache_k.reshape(-1, HEAD_DIM), cache_v.reshape(-1, HEAD_DIM), idx)
    att_s = _sample_attend(q8, az8, k_new8, v_new8, s_new, sel, n_sel,
                           k_sel.reshape(bd, topk_s * KV_HEADS, HEAD_DIM),
                           v_sel.reshape(bd, topk_s * KV_HEADS, HEAD_DIM), n_pages * PAGE)
    ml_s, c_s, n_s, m_s = _mlstm_step(
        ps_main.reshape(bd, 1, MAIN_W), misc_s.reshape(bd, 1, LANES), b_igate[0], b_fgate[0], ml_norm_w[0],
        state_C[0], state_n[0].reshape(bd, ML_HEADS, 1, ML_QK),
        jnp.broadcast_to(state_m[0][:, :, None, None], (bd, ML_HEADS, 1, LANES)))
    y_sample = _out_project(xs, att_s.reshape(bd, ATT_WIDTH).astype(BF16), ml_s.reshape(bd, ML_WIDTH),
                            w_out[0], final_norm_w, bd, "out_sample").reshape(bd, 1, d)
    k_sample = ks_rows.reshape(1, bd, 1, KV_HEADS, HEAD_DIM)
    v_sample = vs_rows.reshape(1, bd, 1, KV_HEADS, HEAD_DIM)
    ik_sample = misc_s[:, :IDX_DIM].reshape(1, bd, 1, IDX_DIM)

    return (y_prompt, y_sample, k_prompt, v_prompt, ik_prompt, c_prompt, n_prompt, m_prompt,
            k_sample, v_sample, ik_sample, c_s[None], n_s.reshape(1, bd, ML_HEADS, ML_QK), m_s[:, :, 0, 0][None])
```

```python
import functools

import jax
import jax.numpy as jnp
import numpy as np
from jax import lax
from jax.experimental import pallas as pl
from jax.experimental.pallas import tpu as pltpu
from jax.experimental.pallas import tpu_sc as plsc

F32 = jnp.float32
BF16 = jnp.bfloat16
I32 = jnp.int32

D_MODEL = 2048
PAGE = 128
ATT_HEADS = 8
KV_HEADS = 2
HEAD_DIM = 128
GROUP = ATT_HEADS // KV_HEADS
ATT_WIDTH = ATT_HEADS * HEAD_DIM
KV_WIDTH = KV_HEADS * HEAD_DIM
ATT_SCALE = HEAD_DIM ** -0.5
Q_SCALE = ATT_SCALE * float(np.log2(np.e))
IDX_HEADS = 16
IDX_DIM = 64
IDX_SCALE = (IDX_HEADS * IDX_DIM) ** -0.5
TOPK_MAX = 256
ML_HEADS = 4
ML_QK = 128
ML_V = 256
ML_WIDTH = ML_HEADS * ML_V
RMS_EPS = 1e-6
IN_SIZES = (ATT_WIDTH, KV_WIDTH, KV_WIDTH, IDX_HEADS * IDX_DIM, IDX_DIM, IDX_HEADS, ATT_WIDTH,
            ML_HEADS * ML_QK, ML_HEADS * ML_QK, ML_WIDTH, ML_HEADS, ML_HEADS, ML_WIDTH, ML_WIDTH)

LANES = 128
NEG_BIG = -1e30
VMEM_LIMIT = 56 * 1024 * 1024

MAIN_W = 7168
SMALL_W = 640
MISC_WI = IDX_DIM
MISC_IG = IDX_DIM + IDX_HEADS
MISC_FG = MISC_IG + ML_HEADS


def _dot(a, b):
    return jnp.dot(a, b, preferred_element_type=F32)


def _dot_nt(a, b):
    return lax.dot_general(a, b, (((1,), (1,)), ((), ())), preferred_element_type=F32)


def _tree_reduce(op, parts):
    parts = list(parts)
    while len(parts) > 1:
        paired = [op(parts[j], parts[j + 1]) for j in range(0, len(parts) - 1, 2)]
        parts = paired + parts[len(parts) - len(parts) % 2:]
    return parts[0]


def _fold_rows(op, x):
    return _tree_reduce(op, [x[r:r + 8] for r in range(0, x.shape[0], 8)])


def _proj_small_kernel(x_ref, nw_ref, w_ref, k_ref, v_ref, misc_ref):
    x = x_ref[...]
    ms = jnp.mean(x * x, axis=-1, keepdims=True)
    h = (x * lax.rsqrt(ms + RMS_EPS) * nw_ref[...]).astype(BF16)
    res = _dot_nt(h, w_ref[...].astype(BF16))
    tm = x.shape[0]
    for g in range(KV_HEADS):
        k_ref[pl.ds(g, tm, stride=KV_HEADS), :] = res[:, g * HEAD_DIM:(g + 1) * HEAD_DIM]
        v_ref[pl.ds(g, tm, stride=KV_HEADS), :] = res[:, KV_WIDTH + g * HEAD_DIM:KV_WIDTH + (g + 1) * HEAD_DIM]
    misc_ref[...] = res[:, 2 * KV_WIDTH:]


def _project_small(x2d, norm_w, w_small, tm, name):
    m, d = x2d.shape
    kv = jax.ShapeDtypeStruct((m * KV_HEADS, HEAD_DIM), F32)
    kv_spec = pl.BlockSpec((tm * KV_HEADS, HEAD_DIM), lambda i: (i, 0))
    return pl.pallas_call(
        _proj_small_kernel,
        out_shape=(kv, kv, jax.ShapeDtypeStruct((m, LANES), F32)),
        grid=(m // tm,),
        in_specs=[pl.BlockSpec((tm, d), lambda i: (i, 0)),
                  pl.BlockSpec((1, d), lambda i: (0, 0)),
                  pl.BlockSpec((SMALL_W, d), lambda i: (0, 0))],
        out_specs=(kv_spec, kv_spec, pl.BlockSpec((tm, LANES), lambda i: (i, 0))),
        compiler_params=pltpu.CompilerParams(dimension_semantics=("parallel",), vmem_limit_bytes=VMEM_LIMIT),
        name=name,
    )(x2d, norm_w.reshape(1, d), w_small)


MAIN_TN = 1024
MAIN_TS = 512
ROW_ALIGN = 8


def _main_tiles():
    offs = np.cumsum((0,) + IN_SIZES)
    aq, iq, az, mq, mk, mv, mo, mz = (int(offs[j]) for j in (0, 3, 6, 7, 8, 9, 12, 13))
    segments = [(aq, ATT_WIDTH, Q_SCALE), (iq, IDX_HEADS * IDX_DIM, 1.0), (az, ATT_WIDTH, 1.0),
                (mv, ML_WIDTH, 1.0), (mo, ML_WIDTH, 1.0), (mz, ML_WIDTH, 1.0),
                (mq, ML_HEADS * ML_QK, 1.0), (mk, ML_HEADS * ML_QK, ML_QK ** -0.5)]
    starts, scales = [], []
    for start, width, scale in segments:
        assert start % ROW_ALIGN == 0 and width % MAIN_TS == 0
        for r in range(start, start + width, MAIN_TS):
            starts.append(r)
            scales.append(scale)
    per_tile = MAIN_TN // MAIN_TS
    rows = []
    for t in range(0, len(starts), per_tile):
        assert all(starts[t + u] == starts[t] + u * MAIN_TS for u in range(per_tile))
        rows.append(starts[t] // ROW_ALIGN)
    assert len(rows) * MAIN_TN == MAIN_W
    return np.asarray(rows, np.int32), np.asarray(scales, np.float32)


def _proj_main_kernel(rows_ref, scale_ref, x_ref, nw_ref, w_ref, o_ref, h_scr):
    j = pl.program_id(1)

    @pl.when(j == 0)
    def _():
        x = x_ref[...]
        ms = jnp.mean(x * x, axis=-1, keepdims=True)
        h_scr[...] = (x * lax.rsqrt(ms + RMS_EPS) * nw_ref[...]).astype(BF16)

    per_tile = MAIN_TN // MAIN_TS
    w = jnp.concatenate([(w_ref[u * MAIN_TS:(u + 1) * MAIN_TS, :] * scale_ref[j * per_tile + u]).astype(BF16)
                         for u in range(per_tile)], axis=0)
    o_ref[...] = _dot_nt(h_scr[...], w).astype(o_ref.dtype)


def _project_main(x2d, norm_w, w_t, out_dtype, tm, name):
    m, d = x2d.shape
    rows, scales = _main_tiles()
    return pl.pallas_call(
        _proj_main_kernel,
        out_shape=jax.ShapeDtypeStruct((m, MAIN_W), out_dtype),
        grid_spec=pltpu.PrefetchScalarGridSpec(
            num_scalar_prefetch=1,
            grid=(m // tm, len(rows)),
            in_specs=[pl.BlockSpec(memory_space=pltpu.SMEM),
                      pl.BlockSpec((tm, d), lambda i, j, rows: (i, 0)),
                      pl.BlockSpec((1, d), lambda i, j, rows: (0, 0)),
                      pl.BlockSpec((pl.Element(MAIN_TN), pl.Element(d)), lambda i, j, rows: (rows[j] * ROW_ALIGN, 0))],
            out_specs=pl.BlockSpec((tm, MAIN_TN), lambda i, j, rows: (i, j)),
            scratch_shapes=[pltpu.VMEM((tm, d), BF16)]),
        compiler_params=pltpu.CompilerParams(dimension_semantics=("parallel", "arbitrary"),
                                             vmem_limit_bytes=VMEM_LIMIT),
        name=name,
    )(jnp.asarray(rows), jnp.asarray(scales), x2d, norm_w.reshape(1, d), w_t)


def _out_kernel(x_ref, a_ref, m_ref, w_ref, fw_ref, o_ref, w_scr):
    @pl.when(pl.program_id(0) == 0)
    def _():
        w_scr[...] = w_ref[...].astype(BF16)

    y = (x_ref[...] + _dot(a_ref[...], w_scr[:ATT_WIDTH, :]) + _dot(m_ref[...], w_scr[ATT_WIDTH:, :]))
    ms = jnp.mean(y * y, axis=-1, keepdims=True)
    o_ref[...] = y * lax.rsqrt(ms + RMS_EPS) * fw_ref[...]


def _out_project(x2d, a, mo, w_out, final_w, tm, name):
    m, d = x2d.shape
    return pl.pallas_call(
        _out_kernel,
        out_shape=jax.ShapeDtypeStruct((m, d), F32),
        grid=(m // tm,),
        in_specs=[pl.BlockSpec((tm, d), lambda i: (i, 0)),
                  pl.BlockSpec((tm, ATT_WIDTH), lambda i: (i, 0)),
                  pl.BlockSpec((tm, ML_WIDTH), lambda i: (i, 0)),
                  pl.BlockSpec((ATT_WIDTH + ML_WIDTH, d), lambda i: (0, 0)),
                  pl.BlockSpec((1, d), lambda i: (0, 0))],
        out_specs=pl.BlockSpec((tm, d), lambda i: (i, 0)),
        scratch_shapes=[pltpu.VMEM((ATT_WIDTH + ML_WIDTH, d), BF16)],
        compiler_params=pltpu.CompilerParams(dimension_semantics=("arbitrary",),
                                             vmem_limit_bytes=VMEM_LIMIT),
        name=name,
    )(x2d, a, mo, w_out, final_w.reshape(1, d))


def _log_sigmoid(x):
    return jnp.minimum(x, 0.0) - jnp.log(1.0 + jnp.exp(-jnp.abs(x)))


def _sigmoid(x):
    return 0.5 * jnp.tanh(0.5 * x) + 0.5


def _silu(x):
    return x * _sigmoid(x)


ML_CHUNK = 256
ML_AUG = ML_V + LANES


def _mlstm_kernel(bi_ref, bf_ref, q_ref, k_ref, v_ref, mo_ref, mz_ref, misc_ref, nw_ref,
                  out_ref, ct_ref, m_ref):
    L = ML_CHUNK

    @pl.when(pl.program_id(1) == 0)
    def _():
        ct_ref[...] = jnp.zeros_like(ct_ref)
        m_ref[...] = jnp.zeros_like(m_ref)

    lane = lax.broadcasted_iota(I32, (1, LANES), 1)
    bias = jnp.zeros((1, LANES), F32)
    for h in range(ML_HEADS):
        bias = jnp.where(lane == MISC_IG + h, bi_ref[h], jnp.where(lane == MISC_FG + h, bf_ref[h], bias))
    pre = misc_ref[...] + bias
    gates = jnp.where(jnp.logical_and(lane >= MISC_FG, lane < MISC_FG + ML_HEADS), _log_sigmoid(pre), pre)
    gates_t = gates.T
    t_idx = lax.broadcasted_iota(I32, (L, L), 0)
    s_idx = lax.broadcasted_iota(I32, (L, L), 1)
    causal = s_idx <= t_idx
    ones_col = jnp.where(lax.broadcasted_iota(I32, (L, LANES), 1) == 0, 1.0, 0.0).astype(BF16)

    for h in range(ML_HEADS):
        ig_row = gates_t[MISC_IG + h:MISC_IG + h + 1, :]
        lf_row = gates_t[MISC_FG + h:MISC_FG + h + 1, :]
        lf_col = gates[:, MISC_FG + h:MISC_FG + h + 1]
        b_col = jnp.sum(jnp.where(causal, lf_row, 0.0), axis=1, keepdims=True)
        b_row = jnp.sum(jnp.where(t_idx <= s_idx, lf_col, 0.0), axis=0, keepdims=True)
        m_prev = m_ref[0, h][0:1, 0:1]
        log_d = jnp.where(causal, b_col - b_row + ig_row, -jnp.inf)
        log_a = b_col + m_prev
        m_t = jnp.maximum(log_a, jnp.max(log_d, axis=1, keepdims=True))
        d = jnp.exp(log_d - m_t)
        a = jnp.exp(log_a - m_t)

        q = q_ref[:, h * ML_QK:(h + 1) * ML_QK]
        k = k_ref[:, h * ML_QK:(h + 1) * ML_QK]
        v_aug = jnp.concatenate([v_ref[:, h * ML_V:(h + 1) * ML_V], ones_col], axis=1)
        s = (_dot_nt(q, k) * d).astype(BF16)
        ct = ct_ref[0, h]
        num_aug = a * _dot(q, ct.astype(BF16)) + _dot(s, v_aug)
        den = num_aug[:, ML_V:ML_V + 1]
        hh = num_aug[:, :ML_V] / jnp.maximum(jnp.abs(den), jnp.exp(-m_t))

        m_new = m_t[L - 1:L, :]
        a_end = a[L - 1:L, :]
        w_row = jnp.exp(b_row[:, L - 1:L] - b_row + ig_row - m_new)
        ktw = (k.astype(F32).T * w_row).astype(BF16)
        ct_new = a_end * ct + _dot(ktw, v_aug)
        ct_ref[0, h] = ct_new
        m_ref[0, h] = jnp.broadcast_to(m_new, (8, LANES))

        ms = jnp.mean(hh * hh, axis=1, keepdims=True)
        hn = hh * lax.rsqrt(ms + RMS_EPS) * nw_ref[:, h * ML_V:(h + 1) * ML_V]
        gate = _sigmoid(mo_ref[:, h * ML_V:(h + 1) * ML_V].astype(F32)) * _silu(mz_ref[:, h * ML_V:(h + 1) * ML_V].astype(F32))
        out_ref[:, h * ML_V:(h + 1) * ML_V] = (hn * gate).astype(out_ref.dtype)


def _mlstm_prompt(p_main, misc, b_i, b_f, ml_norm_w, batch, seq):
    L = ML_CHUNK
    nc = seq // L
    row = lambda b, c: b * nc + c
    return pl.pallas_call(
        _mlstm_kernel,
        out_shape=(jax.ShapeDtypeStruct((batch * seq, ML_WIDTH), BF16),
                   jax.ShapeDtypeStruct((batch, ML_HEADS, ML_QK, ML_AUG), F32),
                   jax.ShapeDtypeStruct((batch, ML_HEADS, 8, LANES), F32)),
        grid=(batch, nc),
        in_specs=[pl.BlockSpec(memory_space=pltpu.SMEM),
                  pl.BlockSpec(memory_space=pltpu.SMEM),
                  pl.BlockSpec((L, 512), lambda b, c: (row(b, c), 12)),
                  pl.BlockSpec((L, 512), lambda b, c: (row(b, c), 13)),
                  pl.BlockSpec((L, 1024), lambda b, c: (row(b, c), 3)),
                  pl.BlockSpec((L, 1024), lambda b, c: (row(b, c), 4)),
                  pl.BlockSpec((L, 1024), lambda b, c: (row(b, c), 5)),
                  pl.BlockSpec((L, LANES), lambda b, c: (row(b, c), 0)),
                  pl.BlockSpec((1, ML_WIDTH), lambda b, c: (0, 0))],
        out_specs=(pl.BlockSpec((L, ML_WIDTH), lambda b, c: (row(b, c), 0)),
                   pl.BlockSpec((1, ML_HEADS, ML_QK, ML_AUG), lambda b, c: (b, 0, 0, 0)),
                   pl.BlockSpec((1, ML_HEADS, 8, LANES), lambda b, c: (b, 0, 0, 0))),
        compiler_params=pltpu.CompilerParams(dimension_semantics=("parallel", "arbitrary"),
                                             vmem_limit_bytes=VMEM_LIMIT),
        name="mlstm_prompt",
    )(b_i, b_f, p_main, p_main, p_main, p_main, p_main, misc, ml_norm_w.reshape(1, ML_WIDTH))


INT_MIN = -2 ** 31


def _float_to_key(x):
    bits = lax.bitcast_convert_type(x, I32)
    return jnp.where(bits >= 0, bits, bits ^ jnp.int32(0x7FFFFFFF))


def _key_to_float(key):
    bits = jnp.where(key >= 0, key, key ^ jnp.int32(0x7FFFFFFF))
    return lax.bitcast_convert_type(bits, F32)


KEY_NEG_INF = INT_MIN + 0x7FFFFF


def _kth_largest_key(count_ge, shape, k):
    def bit_body(it, prefix):
        cand = prefix + lax.shift_left(jnp.int32(1), 31 - it)
        cand_f = _key_to_float(jnp.maximum(cand, KEY_NEG_INF))
        return jnp.where(count_ge(cand_f) >= float(k), cand, prefix)

    return lax.fori_loop(0, 32, bit_body, jnp.full(shape, INT_MIN, I32))


def _tie_cutoff(count_tie_le, need, shape, index_bits):
    def bit_body(it, lo):
        cand = lo + lax.shift_left(jnp.int32(1), index_bits - 1 - it).astype(F32)
        return jnp.where(count_tie_le(cand) < need, cand, lo)

    return lax.fori_loop(0, index_bits, bit_body, jnp.full(shape, -1.0, F32)) + 1.0


DSA_QB = 128
DSA_TK = 512
DSA_TS = 256
DSA_TA = 256
HEAD_PAIR = 2


def _dsa_kernel(q_ref, qi_ref, az_ref, miscq_ref, k_ref, v_ref, misck_ref, out_ref,
                k_bf, vt_scr, ki_lo, ki_hi, sc_scr, acc_scr, j_scr, *head_scr, topk):
    i = pl.program_id(1)
    seq = misck_ref.shape[0]
    QB, TK, TS, TA = DSA_QB, DSA_TK, DSA_TS, DSA_TA
    nt = (i * QB + QB + TK - 1) // TK

    @pl.when(i == 0)
    def _():
        for g in range(KV_HEADS):
            k_bf[:, g * HEAD_DIM:(g + 1) * HEAD_DIM] = k_ref[pl.ds(g, seq, stride=KV_HEADS), :].astype(BF16)

        def v_tile(a, carry):
            for g in range(KV_HEADS):
                rows = pl.ds(pl.multiple_of(a * (TA * KV_HEADS), TA * KV_HEADS) + g, TA, stride=KV_HEADS)
                vt_scr[a, g * HEAD_DIM:(g + 1) * HEAD_DIM, :] = v_ref[rows, :].T.astype(BF16)
            return carry

        lax.fori_loop(0, seq // TA, v_tile, 0)
        lane = lax.broadcasted_iota(I32, (seq, LANES), 1)
        lo = jnp.where(lane < IDX_DIM, misck_ref[...], 0.0)
        ki_lo[...] = lo.astype(BF16)
        ki_hi[...] = pltpu.roll(lo, IDX_DIM, axis=1).astype(BF16)

    k_iota = lax.broadcasted_iota(I32, (TK, QB), 0)
    q_pos = i * QB + lax.broadcasted_iota(I32, (TK, QB), 1)
    w_t = miscq_ref[...].T

    def tile_rows(t):
        return pl.ds(pl.multiple_of(t * TK, TK), TK)

    def score_tile(t, carry):
        rows = pl.ds(pl.multiple_of(t * TS, TS), TS)
        klo = ki_lo[rows, :]
        khi = ki_hi[rows, :]
        acc = jnp.zeros((TS, QB), F32)
        for p in range(IDX_HEADS // 2):
            qp = qi_ref[:, p * LANES:(p + 1) * LANES]
            w0 = w_t[MISC_WI + 2 * p:MISC_WI + 2 * p + 1, :]
            w1 = w_t[MISC_WI + 2 * p + 1:MISC_WI + 2 * p + 2, :]
            acc = acc + w0 * jnp.maximum(_dot_nt(klo, qp), 0.0) + w1 * jnp.maximum(_dot_nt(khi, qp), 0.0)
        k_pos = t * TS + lax.broadcasted_iota(I32, (TS, QB), 0)
        q_pos_s = i * QB + lax.broadcasted_iota(I32, (TS, QB), 1)
        score = jnp.where(k_pos <= q_pos_s, acc * IDX_SCALE + 0.0, -jnp.inf)
        sc_scr[rows, :] = _float_to_key(score)
        return carry

    def score_tiles(t, carry):
        for sub in range(TK // TS):
            score_tile(t * (TK // TS) + sub, carry)
        return carry

    lax.fori_loop(0, nt, score_tiles, 0)

    def count_key_ge(cand):
        def body(t, cnt):
            return cnt + _fold_rows(jnp.add, jnp.where(sc_scr[tile_rows(t), :] >= cand, 1, 0))

        cnt = lax.fori_loop(0, nt, body, jnp.zeros((8, QB), I32))
        return jnp.sum(cnt, axis=0, keepdims=True)

    def bit_body(it, carry):
        prefix, cnt_at = carry
        cand = prefix + lax.shift_left(jnp.int32(1), 31 - it)
        cnt = count_key_ge(cand)
        ok = cnt >= topk
        return jnp.where(ok, cand, prefix), jnp.where(ok, cnt, cnt_at)

    thr, cnt_thr = lax.fori_loop(0, 32, bit_body, (jnp.full((1, QB), INT_MIN, I32), jnp.full((1, QB), TK, I32) * nt))

    j_scr[...] = jnp.full((1, QB), float(seq), F32)

    @pl.when(jnp.max(cnt_thr) > topk)
    def _():
        def count_keys(indicator):
            def body(t, cnt):
                k_pos = (t * TK + k_iota).astype(F32)
                return cnt + _fold_rows(jnp.add, indicator(sc_scr[tile_rows(t), :], k_pos))

            return jnp.sum(lax.fori_loop(0, nt, body, jnp.zeros((8, QB), F32)), axis=0, keepdims=True)

        need = float(topk) - count_keys(lambda key, k_pos: jnp.where(key > thr, 1.0, 0.0))
        count_tie_le = lambda cut: count_keys(
            lambda key, k_pos: jnp.where(key == thr, jnp.where(k_pos <= cut, 1.0, 0.0), 0.0))
        j_scr[...] = _tie_cutoff(count_tie_le, need, (1, QB), int(np.log2(seq)))

    cut = j_scr[...]

    def bias_tile(t, carry):
        key = sc_scr[tile_rows(t), :]
        k_pos = t * TK + k_iota
        taken = jnp.where(key > thr, 0.0,
                          jnp.where(key == thr, jnp.where(k_pos.astype(F32) <= cut, 0.0, NEG_BIG), NEG_BIG))
        sc_scr[tile_rows(t), :] = lax.bitcast_convert_type(jnp.where(k_pos <= q_pos, taken, NEG_BIG), I32)
        return carry

    lax.fori_loop(0, nt, bias_tile, 0)

    assert TK == 2 * TA and GROUP % HEAD_PAIR == 0
    az = az_ref[...].astype(F32)
    n_pairs = ATT_HEADS // HEAD_PAIR
    PQ = HEAD_PAIR * QB
    lg_scr, p_scr = head_scr[:n_pairs], head_scr[n_pairs:]
    acc_scr[...] = jnp.zeros_like(acc_scr)
    last = 2 * nt - 1

    def sub_rows(a):
        return pl.ds(pl.multiple_of(a * TA, TA), TA)

    def kv_cols(j):
        g = (j * HEAD_PAIR) // GROUP
        return slice(g * HEAD_DIM, (g + 1) * HEAD_DIM)

    def logits_stage(j, a, slot):
        q_pair = jnp.concatenate([q_ref[:, h * HEAD_DIM:(h + 1) * HEAD_DIM]
                                  for h in range(j * HEAD_PAIR, (j + 1) * HEAD_PAIR)], axis=0)
        lg_scr[j][slot] = _dot_nt(k_bf[sub_rows(a), kv_cols(j)], q_pair)

    CH = 64

    def softmax_stage(j, a, slot, state):
        m, l8, _ = state
        maxes = []
        for r in range(0, TA, CH):
            bias = lax.bitcast_convert_type(sc_scr[pl.ds(pl.multiple_of(a * TA, TA) + r, CH), :], F32)
            lg = lg_scr[j][slot, r:r + CH, :] + jnp.concatenate([bias] * HEAD_PAIR, axis=1)
            lg_scr[j][slot, r:r + CH, :] = lg
            maxes.append(_fold_rows(jnp.maximum, lg))
        m_new = jnp.maximum(m, jnp.max(_tree_reduce(jnp.maximum, maxes), axis=0, keepdims=True))
        alpha = jnp.exp2(m - m_new)
        sums = []
        for r in range(0, TA, CH):
            p = jnp.exp2(lg_scr[j][slot, r:r + CH, :] - m_new)
            p_scr[j][slot, r:r + CH, :] = p.astype(BF16)
            sums.append(_fold_rows(jnp.add, p))
        return m_new, alpha * l8 + _tree_reduce(jnp.add, sums), alpha

    def value_stage(j, a, slot, state):
        vt = vt_scr[a][kv_cols(j), :]
        acc_scr[j] = state[2] * acc_scr[j] + _dot(vt, p_scr[j][slot])

    def att_tile(t, state):
        for slot in range(2):
            a = 2 * t + slot
            out = []
            for j in range(n_pairs):
                value_stage(j, jnp.maximum(a - 1, 0), 1 - slot, state[j])
                out.append(softmax_stage(j, a, slot, state[j]))
                logits_stage(j, jnp.minimum(a + 1, last), 1 - slot)
            state = tuple(out)
        return state

    init = tuple((jnp.full((1, PQ), NEG_BIG, F32), jnp.zeros((8, PQ), F32), jnp.ones((1, PQ), F32))
                 for _ in range(n_pairs))
    for j in range(n_pairs):
        p_scr[j][1] = jnp.zeros((TA, PQ), BF16)
        logits_stage(j, 0, 0)
    fin = lax.fori_loop(0, nt, att_tile, init)
    for j in range(n_pairs):
        value_stage(j, last, 1, fin[j])
    for j in range(n_pairs):
        o_t = acc_scr[j] / jnp.sum(fin[j][1], axis=0, keepdims=True)
        for hh in range(HEAD_PAIR):
            c0 = (j * HEAD_PAIR + hh) * HEAD_DIM
            o = o_t[:, hh * QB:(hh + 1) * QB].T
            out_ref[:, c0:c0 + HEAD_DIM] = (o * _silu(az[:, c0:c0 + HEAD_DIM])).astype(out_ref.dtype)


def _dsa_prompt(p_main, k_rows, v_rows, misc, batch, seq):
    QB, TK = DSA_QB, DSA_TK
    nb = seq // QB
    topk = min(TOPK_MAX, seq // 4)
    TA = DSA_TA
    assert TK >= topk and seq % TK == 0 and seq % QB == 0 and TK % DSA_TS == 0
    qrow = lambda b, i: b * nb + i
    return pl.pallas_call(
        functools.partial(_dsa_kernel, topk=topk),
        out_shape=jax.ShapeDtypeStruct((batch * seq, ATT_WIDTH), BF16),
        grid=(batch, nb),
        in_specs=[pl.BlockSpec((QB, ATT_WIDTH), lambda b, i: (qrow(b, i), 0)),
                  pl.BlockSpec((QB, 1024), lambda b, i: (qrow(b, i), 1)),
                  pl.BlockSpec((QB, ATT_WIDTH), lambda b, i: (qrow(b, i), 2)),
                  pl.BlockSpec((QB, LANES), lambda b, i: (qrow(b, i), 0)),
                  pl.BlockSpec((seq * KV_HEADS, HEAD_DIM), lambda b, i: (b, 0)),
                  pl.BlockSpec((seq * KV_HEADS, HEAD_DIM), lambda b, i: (b, 0)),
                  pl.BlockSpec((seq, LANES), lambda b, i: (b, 0))],
        out_specs=pl.BlockSpec((QB, ATT_WIDTH), lambda b, i: (qrow(b, i), 0)),
        scratch_shapes=[pltpu.VMEM((seq, KV_WIDTH), BF16),
                        pltpu.VMEM((seq // TA, KV_WIDTH, TA), BF16),
                        pltpu.VMEM((seq, LANES), BF16),
                        pltpu.VMEM((seq, LANES), BF16),
                        pltpu.VMEM((seq, QB), I32),
                        pltpu.VMEM((ATT_HEADS // HEAD_PAIR, HEAD_DIM, HEAD_PAIR * QB), F32),
                        pltpu.VMEM((1, QB), F32)]
                       + [pltpu.VMEM((2, TA, HEAD_PAIR * QB), F32)] * (ATT_HEADS // HEAD_PAIR)
                       + [pltpu.VMEM((2, TA, HEAD_PAIR * QB), BF16)] * (ATT_HEADS // HEAD_PAIR),
        compiler_params=pltpu.CompilerParams(dimension_semantics=("parallel", "arbitrary"),
                                             vmem_limit_bytes=VMEM_LIMIT),
        name="dsa_prompt",
    )(p_main, p_main, p_main, misc, k_rows, v_rows, misc)


SCORE_ROWS = 128


def _sample_scores_kernel(pt_ref, qi_ref, wb_ref, kin_ref, cache_ref, sc_ref, snew_ref, page_buf, sems):
    b = pl.program_id(0)
    n_rows = pl.num_programs(0)
    n_pages = sc_ref.shape[1]
    slot = lax.rem(b, 2)

    def page_copy(row, p, to_slot):
        return pltpu.make_async_copy(cache_ref.at[pt_ref[row, p]], page_buf.at[to_slot, p], sems.at[to_slot])

    def start_fetch(row, to_slot):
        def body(p, carry):
            page_copy(row, p, to_slot).start()
            return carry

        lax.fori_loop(0, n_pages, body, 0, unroll=8)

    @pl.when(b == 0)
    def _():
        start_fetch(0, 0)

    @pl.when(b + 1 < n_rows)
    def _():
        start_fetch(b + 1, 1 - slot)

    def wait_page(p, carry):
        page_copy(b, p, slot).wait()
        return carry

    lax.fori_loop(0, n_pages, wait_page, 0, unroll=8)

    qi = qi_ref[0]
    wb = wb_ref[0]

    def score_rows(g, carry):
        rows = []
        for j in range(SCORE_ROWS):
            page = page_buf[slot, g * SCORE_ROWS + j]
            s = _dot(qi, page.astype(BF16))
            rows.append(jnp.sum(jnp.maximum(s, 0.0) * wb, axis=0, keepdims=True) * IDX_SCALE)
        sc_ref[0, pl.ds(pl.multiple_of(g * SCORE_ROWS, SCORE_ROWS), SCORE_ROWS), :] = jnp.concatenate(rows, axis=0)
        return carry

    lax.fori_loop(0, n_pages // SCORE_ROWS, score_rows, 0)

    ki_new = kin_ref[0].astype(BF16).astype(F32)
    s = jnp.sum(qi.astype(F32) * ki_new, axis=1, keepdims=True)
    s_new = jnp.sum(jnp.maximum(s, 0.0) * wb[:, 0:1], axis=0, keepdims=True) * IDX_SCALE
    snew_ref[0] = jnp.broadcast_to(s_new, (8, LANES))


def _sample_scores(page_table, qi16, wb, ki_new, cache_ik_t):
    bd, n_pages = page_table.shape
    assert n_pages % SCORE_ROWS == 0
    bmap = lambda b, pt: (b, 0, 0)
    return pl.pallas_call(
        _sample_scores_kernel,
        out_shape=(jax.ShapeDtypeStruct((bd, n_pages, PAGE), F32),
                   jax.ShapeDtypeStruct((bd, 8, LANES), F32)),
        grid_spec=pltpu.PrefetchScalarGridSpec(
            num_scalar_prefetch=1,
            grid=(bd,),
            in_specs=[pl.BlockSpec((1, IDX_HEADS, IDX_DIM), bmap),
                      pl.BlockSpec((1, IDX_HEADS, LANES), bmap),
                      pl.BlockSpec((1, 1, IDX_DIM), bmap),
                      pl.BlockSpec(memory_space=pl.ANY)],
            out_specs=(pl.BlockSpec((1, n_pages, PAGE), bmap),
                       pl.BlockSpec((1, 8, LANES), bmap)),
            scratch_shapes=[pltpu.VMEM((2, n_pages, IDX_DIM, PAGE), F32),
                            pltpu.SemaphoreType.DMA((2,))]),
        compiler_params=pltpu.CompilerParams(dimension_semantics=("arbitrary",),
                                             vmem_limit_bytes=VMEM_LIMIT),
        name="sample_scores",
    )(page_table, qi16, wb, ki_new, cache_ik_t)


SEL_THR, SEL_NEXT, SEL_CUT = 0, 1, 2


def _sample_select_kernel(sc_ref, snew_ref, sel_ref, *, topk):
    x = sc_ref[...]
    bd, n_pages, _ = x.shape
    past = n_pages * PAGE
    s_new = snew_ref[:, 0:1, 0:1]

    def total(v):
        return jnp.sum(jnp.sum(v, axis=1, keepdims=True), axis=2, keepdims=True)

    def count_ge(cand):
        return total(jnp.where(x >= cand, 1.0, 0.0)) + jnp.where(s_new >= cand, 1.0, 0.0)

    key = _kth_largest_key(count_ge, (bd, 1, 1), topk)
    thr = _key_to_float(key)
    thr_next = _key_to_float(key + 1)
    need = float(topk) - count_ge(thr_next)
    pos = (lax.broadcasted_iota(I32, (1, n_pages, PAGE), 1) * PAGE
           + lax.broadcasted_iota(I32, (1, n_pages, PAGE), 2)).astype(F32)

    def count_tie_le(cut):
        tie = jnp.where(x >= thr_next, 0.0, jnp.where(x >= thr, jnp.where(pos <= cut, 1.0, 0.0), 0.0))
        tie_new = jnp.where(s_new >= thr_next, 0.0, jnp.where(s_new >= thr, jnp.where(float(past) <= cut, 1.0, 0.0), 0.0))
        return total(tie) + tie_new

    row = lax.broadcasted_iota(I32, (bd, 8, LANES), 1)
    sel_ref[...] = jnp.where(row == SEL_THR, thr, jnp.where(row == SEL_NEXT, thr_next, float(past + 1)))

    @pl.when(jnp.max(count_ge(thr)) > float(topk))
    def _():
        cut = _tie_cutoff(count_tie_le, need, (bd, 1, 1), int(np.log2(past)) + 1)
        sel_ref[...] = jnp.where(row == SEL_THR, thr, jnp.where(row == SEL_NEXT, thr_next, cut))


def _sample_select(scores, s_new, topk):
    bd = scores.shape[0]
    return pl.pallas_call(
        functools.partial(_sample_select_kernel, topk=topk),
        out_shape=jax.ShapeDtypeStruct((bd, 8, LANES), F32),
        compiler_params=pltpu.CompilerParams(vmem_limit_bytes=VMEM_LIMIT),
        name="sample_select",
    )(scores, s_new)


def _taken_bias(score, pos, sel):
    thr, thr_next, cut = sel[SEL_THR:SEL_THR + 1, 0:1], sel[SEL_NEXT:SEL_NEXT + 1, 0:1], sel[SEL_CUT:SEL_CUT + 1, 0:1]
    return jnp.where(score >= thr_next, 0.0,
                     jnp.where(score >= thr, jnp.where(pos <= cut, 0.0, NEG_BIG), NEG_BIG))


def _sample_compact_kernel(pt_ref, sc_ref, sel_ref, rows_ref, nsel_ref, *, slots):
    n_pages = sc_ref.shape[1]
    pos = (lax.broadcasted_iota(I32, (n_pages, PAGE), 0) * PAGE
           + lax.broadcasted_iota(I32, (n_pages, PAGE), 1)).astype(F32)
    taken = jnp.where(_taken_bias(sc_ref[0], pos, sel_ref[0]) == 0.0, 1.0, 0.0)
    before = lax.broadcasted_iota(I32, (PAGE, PAGE), 0) < lax.broadcasted_iota(I32, (PAGE, PAGE), 1)
    in_page = _dot(taken, jnp.where(before, 1.0, 0.0))
    taken_t = taken.T
    rank_t = jnp.where(taken_t > 0.0, in_page.T, -1.0)
    page_tot = jnp.sum(taken_t, axis=0, keepdims=True)
    earlier = lax.broadcasted_iota(I32, (n_pages, n_pages), 0) < lax.broadcasted_iota(I32, (n_pages, n_pages), 1)
    first_slot = _dot(jnp.broadcast_to(page_tot, (8, n_pages)),
                      jnp.where(earlier, 1.0, 0.0))[0:1]
    phys_t = (pt_ref[0] * PAGE + lax.broadcasted_iota(I32, (PAGE, n_pages), 0)).astype(F32)
    slot = lax.broadcasted_iota(I32, (slots, n_pages), 0).astype(F32)

    def rank_body(r, acc):
        r = r.astype(F32)
        row = jnp.sum(jnp.where(rank_t == r, phys_t, 0.0), axis=0, keepdims=True)
        target = jnp.where(r < page_tot, first_slot + r, -1.0)
        return acc + jnp.where(slot == target, row, 0.0)

    most = jnp.max(page_tot).astype(I32)
    acc = lax.fori_loop(0, most, rank_body, jnp.zeros((slots, n_pages), F32))
    rows_ref[0] = jnp.broadcast_to(jnp.sum(acc, axis=1, keepdims=True), (slots, LANES))
    nsel_ref[0] = jnp.broadcast_to(jnp.sum(page_tot, axis=1, keepdims=True), (8, LANES))


def _sample_compact(page_table, scores, sel, slots):
    bd, n_pages = page_table.shape
    assert n_pages % LANES == 0
    bmap = lambda b: (b, 0, 0)
    return pl.pallas_call(
        functools.partial(_sample_compact_kernel, slots=slots),
        out_shape=(jax.ShapeDtypeStruct((bd, slots, LANES), F32), jax.ShapeDtypeStruct((bd, 8, LANES), F32)),
        grid=(bd,),
        in_specs=[pl.BlockSpec((1, 1, n_pages), bmap), pl.BlockSpec((1, n_pages, PAGE), bmap),
                  pl.BlockSpec((1, 8, LANES), bmap)],
        out_specs=(pl.BlockSpec((1, slots, LANES), bmap), pl.BlockSpec((1, 8, LANES), bmap)),
        compiler_params=pltpu.CompilerParams(dimension_semantics=("parallel",), vmem_limit_bytes=VMEM_LIMIT),
        name="sample_compact",
    )(page_table.reshape(bd, 1, n_pages), scores, sel)


SC_GATHER_CHUNK = 128


def _gather_rows(table_k, table_v, idx):
    info = plsc.get_sparse_core_info()
    n_workers = info.num_cores * info.num_subcores
    n_idx = idx.shape[0]
    per_worker = n_idx // n_workers
    assert per_worker * n_workers == n_idx and per_worker % SC_GATHER_CHUNK == 0
    mesh = plsc.VectorSubcoreMesh(core_axis_name="c", subcore_axis_name="s")
    out = jax.ShapeDtypeStruct((n_idx, table_k.shape[1]), table_k.dtype)

    @functools.partial(
        pl.kernel, mesh=mesh, out_type=(out, out),
        scratch_types=[pltpu.VMEM((SC_GATHER_CHUNK,), I32),
                       pltpu.VMEM((SC_GATHER_CHUNK, table_k.shape[1]), table_k.dtype),
                       pltpu.VMEM((SC_GATHER_CHUNK, table_v.shape[1]), table_v.dtype),
                       pltpu.SemaphoreType.DMA, pltpu.SemaphoreType.DMA])
    def gather(tk_hbm, tv_hbm, idx_hbm, ok_hbm, ov_hbm, idx_v, rk_v, rv_v, sem_k, sem_v):
        worker = lax.axis_index("s") * info.num_cores + lax.axis_index("c")

        @pl.loop(0, per_worker // SC_GATHER_CHUNK)
        def _(j):
            base = worker * per_worker + j * SC_GATHER_CHUNK
            pltpu.sync_copy(idx_hbm.at[pl.ds(base, SC_GATHER_CHUNK)], idx_v)
            copy_k = pltpu.async_copy(tk_hbm.at[idx_v], rk_v, sem_k)
            copy_v = pltpu.async_copy(tv_hbm.at[idx_v], rv_v, sem_v)
            copy_k.wait()
            copy_v.wait()
            pltpu.sync_copy(rk_v, ok_hbm.at[pl.ds(base, SC_GATHER_CHUNK)])
            pltpu.sync_copy(rv_v, ov_hbm.at[pl.ds(base, SC_GATHER_CHUNK)])

    return gather(table_k, table_v, idx)


def _sample_attend_kernel(q_ref, az_ref, kn_ref, vn_ref, snew_ref, sel_ref, nsel_ref, k_ref, v_ref, out_ref, *, past):
    width = k_ref.shape[1]
    q = q_ref[0]
    col = lax.broadcasted_iota(I32, (ATT_HEADS, width), 1)
    head = lax.broadcasted_iota(I32, (ATT_HEADS, width), 0)
    own_head = col % KV_HEADS == head // GROUP
    filled = (col // KV_HEADS).astype(F32) < nsel_ref[0][0:1, 0:1]
    lg = _dot_nt(q, k_ref[0].astype(BF16))
    lg = jnp.where(own_head, jnp.where(filled, lg, NEG_BIG), NEG_BIG)
    lg_new = (jnp.sum(q.astype(F32) * kn_ref[0].astype(BF16).astype(F32), axis=1, keepdims=True)
              + _taken_bias(snew_ref[0][0:1, 0:1], float(past), sel_ref[0]))
    m = jnp.maximum(jnp.max(lg, axis=1, keepdims=True), lg_new)
    p = jnp.exp2(lg - m)
    p_new = jnp.exp2(lg_new - m)
    l = jnp.sum(p, axis=1, keepdims=True) + p_new
    acc = _dot(p.astype(BF16), v_ref[0].astype(BF16)) + p_new * vn_ref[0].astype(BF16).astype(F32)
    out_ref[0] = (acc / l) * _silu(az_ref[0])


def _sample_attend(q8, az8, k_new8, v_new8, s_new, sel, n_sel, k_sel, v_sel, past):
    bd, width, _ = k_sel.shape
    bmap = lambda b: (b, 0, 0)
    head_tile = pl.BlockSpec((1, ATT_HEADS, HEAD_DIM), bmap)
    par_tile = pl.BlockSpec((1, 8, LANES), bmap)
    rows_tile = pl.BlockSpec((1, width, HEAD_DIM), bmap)
    return pl.pallas_call(
        functools.partial(_sample_attend_kernel, past=past),
        out_shape=jax.ShapeDtypeStruct((bd, ATT_HEADS, HEAD_DIM), F32),
        grid=(bd,),
        in_specs=[head_tile, head_tile, head_tile, head_tile, par_tile, par_tile, par_tile, rows_tile, rows_tile],
        out_specs=head_tile,
        compiler_params=pltpu.CompilerParams(dimension_semantics=("parallel",), vmem_limit_bytes=VMEM_LIMIT),
        name="sample_attend",
    )(q8, az8, k_new8, v_new8, s_new, sel, n_sel, k_sel, v_sel)


def _mlstm_step_kernel(bi_ref, bf_ref, q_ref, k_ref, v_ref, mo_ref, mz_ref, misc_ref, nw_ref,
                       c_ref, n_ref, m_ref, out_ref, c_out, n_out, m_out):
    misc = misc_ref[0]
    eye = lax.broadcasted_iota(I32, (ML_V, ML_V), 0) == lax.broadcasted_iota(I32, (ML_V, ML_V), 1)
    for h in range(ML_HEADS):
        ig = misc[:, MISC_IG + h:MISC_IG + h + 1] + bi_ref[h]
        lf = _log_sigmoid(misc[:, MISC_FG + h:MISC_FG + h + 1] + bf_ref[h])
        m_prev = m_ref[0, h][:, 0:1]
        log_a = lf + m_prev
        m_t = jnp.maximum(log_a, ig)
        d = jnp.exp(ig - m_t)
        a = jnp.exp(log_a - m_t)
        q = q_ref[0][:, h * ML_QK:(h + 1) * ML_QK]
        k = k_ref[0][:, h * ML_QK:(h + 1) * ML_QK]
        v = v_ref[0][:, h * ML_V:(h + 1) * ML_V]
        v_col = jnp.sum(jnp.where(eye, v, 0.0), axis=1, keepdims=True)
        c = c_ref[0, h]
        n = n_ref[0, h]
        s = jnp.sum(q * k, axis=1, keepdims=True) * d
        num = a * jnp.sum(c * q, axis=1, keepdims=True) + s * v_col
        den = a * jnp.sum(n * q, axis=1, keepdims=True) + s
        h_col = num / jnp.maximum(jnp.abs(den), jnp.exp(-m_t))
        c_out[0, h] = a * c + (d * v_col) * k
        n_out[0, h] = a * n + d * k
        m_out[0, h] = jnp.broadcast_to(m_t, (1, LANES))

        h_row = jnp.sum(jnp.where(eye, h_col, 0.0), axis=0, keepdims=True)
        ms = jnp.mean(h_row * h_row, axis=1, keepdims=True)
        hn = h_row * lax.rsqrt(ms + RMS_EPS) * nw_ref[:, h * ML_V:(h + 1) * ML_V]
        gate = _sigmoid(mo_ref[0][:, h * ML_V:(h + 1) * ML_V]) * _silu(mz_ref[0][:, h * ML_V:(h + 1) * ML_V])
        out_ref[0, :, h * ML_V:(h + 1) * ML_V] = (hn * gate).astype(out_ref.dtype)


def _mlstm_step(ps_main, misc, b_i, b_f, ml_norm_w, state_c, state_n, state_m):
    bd = ps_main.shape[0]
    col = lambda j: (lambda b: (b, 0, j))
    st4 = lambda b: (b, 0, 0, 0)
    return pl.pallas_call(
        _mlstm_step_kernel,
        out_shape=(jax.ShapeDtypeStruct((bd, 1, ML_WIDTH), BF16),
                   jax.ShapeDtypeStruct(state_c.shape, F32),
                   jax.ShapeDtypeStruct(state_n.shape, F32),
                   jax.ShapeDtypeStruct(state_m.shape, F32)),
        grid=(bd,),
        in_specs=[pl.BlockSpec(memory_space=pltpu.SMEM),
                  pl.BlockSpec(memory_space=pltpu.SMEM),
                  pl.BlockSpec((1, 1, 512), col(12)),
                  pl.BlockSpec((1, 1, 512), col(13)),
                  pl.BlockSpec((1, 1, 1024), col(3)),
                  pl.BlockSpec((1, 1, 1024), col(4)),
                  pl.BlockSpec((1, 1, 1024), col(5)),
                  pl.BlockSpec((1, 1, LANES), col(0)),
                  pl.BlockSpec((1, ML_WIDTH), lambda b: (0, 0)),
                  pl.BlockSpec((1, ML_HEADS, ML_V, ML_QK), st4),
                  pl.BlockSpec((1, ML_HEADS, 1, ML_QK), st4),
                  pl.BlockSpec((1, ML_HEADS, 1, LANES), st4)],
        out_specs=(pl.BlockSpec((1, 1, ML_WIDTH), lambda b: (b, 0, 0)),
                   pl.BlockSpec((1, ML_HEADS, ML_V, ML_QK), st4),
                   pl.BlockSpec((1, ML_HEADS, 1, ML_QK), st4),
                   pl.BlockSpec((1, ML_HEADS, 1, LANES), st4)),
        compiler_params=pltpu.CompilerParams(dimension_semantics=("parallel",),
                                             vmem_limit_bytes=VMEM_LIMIT),
        name="mlstm_step",
    )(b_i, b_f, ps_main, ps_main, ps_main, ps_main, ps_main, misc, ml_norm_w.reshape(1, ML_WIDTH),
      state_c, state_n, state_m)


def _small_weight(w_t):
    offs = np.cumsum((0,) + IN_SIZES)
    ak, av, ik, iw, mi, mf = (w_t[offs[j]:offs[j + 1]] for j in (1, 2, 4, 5, 10, 11))
    pad = jnp.zeros((LANES - IDX_DIM - IDX_HEADS - 2 * ML_HEADS, w_t.shape[1]), w_t.dtype)
    w_small = jnp.concatenate([ak, av, ik, iw, mi, mf, pad], axis=0)
    assert w_small.shape[0] == SMALL_W
    return w_small


def kernel(x_prompt, x_sample, cache_k, cache_v, cache_idx_k, state_C, state_n, state_m, page_table,
           norm_w, w_in, b_igate, b_fgate, ml_norm_w, w_out, final_norm_w):
    depth = w_in.shape[0]
    batch, seq, d = x_prompt.shape
    bd, dec_seq, _ = x_sample.shape
    assert depth == 1 and dec_seq == 1 and d == D_MODEL
    n_pages = page_table.shape[1]

    w_t = jnp.swapaxes(w_in[0], 0, 1)
    w_small = _small_weight(w_t)

    xp = x_prompt.reshape(batch * seq, d)
    p_main = _project_main(xp, norm_w[0], w_t, BF16, 1024, "proj_main")
    k_rows, v_rows, misc = _project_small(xp, norm_w[0], w_small, 1024, "proj_small")
    att = _dsa_prompt(p_main, k_rows, v_rows, misc, batch, seq)
    ml, ct, m_p = _mlstm_prompt(p_main, misc, b_igate[0], b_fgate[0], ml_norm_w[0], batch, seq)
    y_prompt = _out_project(xp, att, ml, w_out[0], final_norm_w, 256, "out_prompt").reshape(batch, seq, d)
    k_prompt = k_rows.reshape(1, batch, seq, KV_HEADS, HEAD_DIM)
    v_prompt = v_rows.reshape(1, batch, seq, KV_HEADS, HEAD_DIM)
    ik_prompt = misc[:, :IDX_DIM].reshape(1, batch, seq, IDX_DIM)
    c_prompt = jnp.swapaxes(ct[..., :ML_V], -1, -2)[None]
    n_prompt = ct[..., ML_V][None]
    m_prompt = m_p[:, :, 0, 0][None]

    xs = x_sample.reshape(bd, d)
    ps_main = _project_main(xs, norm_w[0], w_t, F32, bd, "proj_main_s")
    ks_rows, vs_rows, misc_s = _project_small(xs, norm_w[0], w_small, bd, "proj_small_s")
    q8 = ps_main[:, :ATT_WIDTH].reshape(bd, ATT_HEADS, HEAD_DIM).astype(BF16)
    qi16 = ps_main[:, 1024:2048].reshape(bd, IDX_HEADS, IDX_DIM).astype(BF16)
    az8 = ps_main[:, 2048:3072].reshape(bd, ATT_HEADS, HEAD_DIM)
    ki_new = misc_s[:, :IDX_DIM].reshape(bd, 1, IDX_DIM)
    wb = jnp.broadcast_to(misc_s[:, MISC_WI:MISC_WI + IDX_HEADS, None], (bd, IDX_HEADS, LANES))
    k_new8 = jnp.repeat(ks_rows.reshape(bd, KV_HEADS, HEAD_DIM), GROUP, axis=1)
    v_new8 = jnp.repeat(vs_rows.reshape(bd, KV_HEADS, HEAD_DIM), GROUP, axis=1)
    scores, s_new = _sample_scores(page_table, qi16, wb, ki_new, jnp.swapaxes(cache_idx_k[0], 1, 2))
    assert cache_k.shape[1] * PAGE < 2 ** 24
    topk_s = min(TOPK_MAX, (n_pages * PAGE + 1) // 4)
    sel = _sample_select(scores, s_new, topk_s)
    key_rows, n_sel = _sample_compact(page_table, scores, sel, topk_s)
    idx = (key_rows[:, :, :1].astype(I32) * KV_HEADS + jnp.arange(KV_HEADS, dtype=I32)).reshape(-1)
    k_sel, v_sel = _gather_rows(cache_k.reshape(-1, HEAD_DIM), cache_v.reshape(-1, HEAD_DIM), idx)
    att_s = _sample_attend(q8, az8, k_new8, v_new8, s_new, sel, n_sel,
                           k_sel.reshape(bd, topk_s * KV_HEADS, HEAD_DIM),
                           v_sel.reshape(bd, topk_s * KV_HEADS, HEAD_DIM), n_pages * PAGE)
    ml_s, c_s, n_s, m_s = _mlstm_step(
        ps_main.reshape(bd, 1, MAIN_W), misc_s.reshape(bd, 1, LANES), b_igate[0], b_fgate[0], ml_norm_w[0],
        state_C[0], state_n[0].reshape(bd, ML_HEADS, 1, ML_QK),
        jnp.broadcast_to(state_m[0][:, :, None, None], (bd, ML_HEADS, 1, LANES)))
    y_sample = _out_project(xs, att_s.reshape(bd, ATT_WIDTH).astype(BF16), ml_s.reshape(bd, ML_WIDTH),
                            w_out[0], final_norm_w, bd, "out_sample").reshape(bd, 1, d)
    k_sample = ks_rows.reshape(1, bd, 1, KV_HEADS, HEAD_DIM)
    v_sample = vs_rows.reshape(1, bd, 1, KV_HEADS, HEAD_DIM)
    ik_sample = misc_s[:, :IDX_DIM].reshape(1, bd, 1, IDX_DIM)

    return (y_prompt, y_sample, k_prompt, v_prompt, ik_prompt, c_prompt, n_prompt, m_prompt,
            k_sample, v_sample, ik_sample, c_s[None], n_s.reshape(1, bd, ML_HEADS, ML_QK), m_s[:, :, 0, 0][None])
```
